```python
import jax
import jax.numpy as jnp
from jax import lax
import numpy as np

D_MODEL = 1024
BATCH = 8
SEQ = 2048
DEPTH = 1

GRID_W = 64
CTX_LEN = 256
EPS = 1e-6
HG_HEADS = 4
HG_DIM = 128
HG_WIDTH = HG_HEADS * HG_DIM
CHUNK = 32
ATT_HEADS = 8
ATT_KV_HEADS = 2
HEAD_DIM = 64
ATT_WIDTH = ATT_HEADS * HEAD_DIM
KV_WIDTH = ATT_KV_HEADS * HEAD_DIM
WINDOW = 128
BLOCK = 128
ROPE_THETA = 10000.0
D_FF = ((8 * D_MODEL + 3 * 256 - 1) // (3 * 256)) * 256
CTX_COLS = 3 * HG_WIDTH + 2 * KV_WIDTH
IN_COLS = CTX_COLS + 2 * HG_WIDTH + ATT_WIDTH + 2 * D_MODEL

kernel_name = 'hybrid_hgrn2_swa_prefix_dit_block'


def _rmsnorm(x, w):
    xf = x.astype(jnp.float32)
    y = xf * lax.rsqrt(jnp.mean(xf * xf, axis=-1, keepdims=True) + EPS)
    return (y * w.astype(jnp.float32)).astype(x.dtype)


def _modulate(x, w, shift, scale):
    return _rmsnorm(x, w) * (1.0 + scale) + shift


def _heads(t, n_heads, head_dim):
    return t.reshape(t.shape[0], t.shape[1], n_heads, head_dim)


def _split_ctx_side(p):
    a, b, c_, d = HG_WIDTH, 2 * HG_WIDTH, 3 * HG_WIDTH, 3 * HG_WIDTH + KV_WIDTH
    return p[..., :a], p[..., a:b], p[..., b:c_], p[..., c_:d], p[..., d:CTX_COLS]


def _split_query_side(p):
    a = CTX_COLS + HG_WIDTH
    b = a + HG_WIDTH
    c_ = b + ATT_WIDTH
    return p[..., CTX_COLS:a], p[..., a:b], p[..., b:c_], p[..., c_:IN_COLS]


def _gla_chunkwise(q, k, v, log_f, s0):
    b_, s_, h_, dk = k.shape
    dv = v.shape[-1]
    n = s_ // CHUNK

    def chunks(t):
        return t.astype(jnp.float32).reshape(b_, n, CHUNK, h_, t.shape[-1]).transpose(0, 3, 1, 2, 4)

    k, v, log_f = chunks(k), chunks(v), chunks(log_f)
    cum = jnp.cumsum(log_f, axis=3)
    cum_last = cum[:, :, :, -1:, :]
    u = jnp.einsum('bhnck,bhncv->bhnkv', k * jnp.exp(cum_last - cum), v)
    decay = jnp.exp(cum_last[:, :, :, 0, :])

    def step(state, inp):
        d, un = inp
        return d[..., None] * state + un, state

    s_final, s_start = lax.scan(step, s0.astype(jnp.float32),
                                (jnp.moveaxis(decay, 2, 0), jnp.moveaxis(u, 2, 0)))
    if q is None:
        return None, s_final
    s_start = jnp.moveaxis(s_start, 0, 2)
    qd = chunks(q) * jnp.exp(cum)
    scores = jnp.einsum('bhnck,bhnsk->bhncs', qd, k * jnp.exp(-cum))
    lower_tri = jnp.tril(jnp.ones((CHUNK, CHUNK), dtype=bool))
    scores = jnp.where(lower_tri, scores, 0.0)
    o = (jnp.einsum('bhncs,bhnsv->bhncv', scores, v)
         + jnp.einsum('bhnck,bhnkv->bhncv', qd, s_start))
    o = o.transpose(0, 2, 3, 1, 4).reshape(b_, s_, h_, dv)
    return o, s_final


def _hgrn_query(q_raw):
    return _heads(jax.nn.silu(q_raw.astype(jnp.float32)) * HG_DIM ** -0.5, HG_HEADS, HG_DIM)


def _hgrn_direction(q, f_logit, inp, lb, s0, reverse):
    f = lb + (1.0 - lb) * jax.nn.sigmoid(f_logit.astype(jnp.float32))
    log_f = _heads(jnp.log(f), HG_HEADS, HG_DIM)
    k = _heads(1.0 - f, HG_HEADS, HG_DIM)
    v = _heads(inp, HG_HEADS, HG_DIM)
    if reverse:
        q = None if q is None else jnp.flip(q, 1)
        k, v, log_f = jnp.flip(k, 1), jnp.flip(v, 1), jnp.flip(log_f, 1)
    o, s = _gla_chunkwise(q, k, v, log_f, s0)
    if reverse and o is not None:
        o = jnp.flip(o, 1)
    return o, s


def _hgrn_readout(o, g_raw, norm_w, dtype):
    g = _heads(g_raw, HG_HEADS, HG_DIM).astype(jnp.float32)
    y = _rmsnorm(o, norm_w) * jax.nn.silu(g)
    return y.reshape(o.shape[0], o.shape[1], HG_WIDTH).astype(dtype)


def _rope_1d(t, pos):
    d = t.shape[-1]
    inv_freq = ROPE_THETA ** (-jnp.arange(0, d, 2, dtype=jnp.float32) / d)
    ang = pos.astype(jnp.float32)[:, None] * inv_freq[None, :]
    cos = jnp.cos(ang)[None, :, None, :]
    sin = jnp.sin(ang)[None, :, None, :]
    tf = t.astype(jnp.float32)
    t1, t2 = tf[..., : d // 2], tf[..., d // 2:]
    return jnp.concatenate([t1 * cos - t2 * sin, t1 * sin + t2 * cos], axis=-1)


def _axial_rope(t, rows, cols):
    half = t.shape[-1] // 2
    return jnp.concatenate([_rope_1d(t[..., :half], rows), _rope_1d(t[..., half:], cols)],
                           axis=-1).astype(t.dtype)


def _window_attention(q, k, v, kc, vc, sinks):
    b_, s_, hq, dh = q.shape
    hkv = k.shape[2]
    grp = hq // hkv
    nb = s_ // BLOCK
    f32 = jnp.float32
    qb = q.astype(f32).reshape(b_, nb, BLOCK, hkv, grp, dh) * dh ** -0.5

    def band(t):
        tp = jnp.pad(t.astype(f32), ((0, 0), (BLOCK, BLOCK), (0, 0), (0, 0)))
        tp = tp.reshape(b_, nb + 2, BLOCK, hkv, dh)
        return jnp.concatenate([tp[:, :-2], tp[:, 1:-1], tp[:, 2:]], axis=2)

    kw, vw = band(k), band(v)
    s_loc = jnp.einsum('bnqhgd,bnkhd->bnhgqk', qb, kw)
    qi = jnp.arange(BLOCK)[:, None]
    kj = jnp.arange(3 * BLOCK)[None, :]
    k_pos = (jnp.arange(nb)[:, None, None] - 1) * BLOCK + kj[None]
    valid = (jnp.abs(kj - BLOCK - qi) <= WINDOW)[None] & (k_pos >= 0) & (k_pos < s_)
    s_loc = jnp.where(valid[None, :, None, None], s_loc, -jnp.inf)
    s_ctx = jnp.einsum('bnqhgd,blhd->bnhgql', qb, kc.astype(f32))
    sink = jnp.broadcast_to(sinks.astype(f32).reshape(1, 1, hkv, grp, 1, 1), s_loc.shape[:-1] + (1,))
    p = jax.nn.softmax(jnp.concatenate([s_loc, s_ctx, sink], axis=-1), axis=-1)
    n_loc = 3 * BLOCK
    n_ctx = kc.shape[1]
    o = (jnp.einsum('bnhgqk,bnkhd->bnqhgd', p[..., :n_loc], vw)
         + jnp.einsum('bnhgql,blhd->bnqhgd', p[..., n_loc:n_loc + n_ctx], vc.astype(f32)))
    return o.reshape(b_, s_, hq * dh).astype(v.dtype)


def _context_attention(qc, kc, vc, sinks):
    b_, l_, hq, dh = qc.shape
    hkv = kc.shape[2]
    grp = hq // hkv
    f32 = jnp.float32
    qg = qc.astype(f32).reshape(b_, l_, hkv, grp, dh) * dh ** -0.5
    s = jnp.einsum('blhgd,bmhd->bhglm', qg, kc.astype(f32))
    sink = jnp.broadcast_to(sinks.astype(f32).reshape(1, hkv, grp, 1, 1), s.shape[:-1] + (1,))
    p = jax.nn.softmax(jnp.concatenate([s, sink], axis=-1), axis=-1)
    o = jnp.einsum('bhglm,bmhd->blhgd', p[..., :l_], vc.astype(f32))
    return o.reshape(b_, l_, hq * dh).astype(vc.dtype)


def _merge(y_hg, y_at, gates, w_bh, w_ba, w_o):
    g_hg, g_at = jnp.split(gates, 2, axis=-1)
    mixed = jax.nn.sigmoid(g_hg) * (y_hg @ w_bh) + jax.nn.sigmoid(g_at) * (y_at @ w_ba)
    return mixed @ w_o


def _swiglu(h, w_gate, w_up, w_down):
    return (jax.nn.silu(h @ w_gate) * (h @ w_up)) @ w_down


def setup_inputs(seed: int = 0) -> dict:
    key = jax.random.key(seed)
    ks = jax.random.split(key, 20)
    f32 = jnp.float32

    def nrm(k, shape, scale):
        return jax.random.normal(k, shape, f32) * scale

    return {
        'x': nrm(ks[0], (BATCH, SEQ, D_MODEL), 1.0),
        'c': nrm(ks[1], (BATCH, D_MODEL), 1.0),
        'ctx': nrm(ks[2], (BATCH, CTX_LEN, D_MODEL), 1.0),
        'c_ctx': nrm(ks[3], (D_MODEL,), 1.0),
        'w_ada': nrm(ks[4], (DEPTH, D_MODEL, 6 * D_MODEL), 0.5 * D_MODEL ** -0.5),
        'b_ada': nrm(ks[5], (DEPTH, 6 * D_MODEL), 0.02),
        'norm_mix_w': 1.0 + nrm(ks[6], (DEPTH, D_MODEL), 0.02),
        'norm_ffn_w': 1.0 + nrm(ks[7], (DEPTH, D_MODEL), 0.02),
        'w_in': nrm(ks[8], (DEPTH, D_MODEL, IN_COLS), D_MODEL ** -0.5),
        'hgrn_lb_logits': nrm(ks[9], (2, DEPTH + 1, HG_WIDTH), 0.5),
        'hgrn_norm_w': 1.0 + nrm(ks[10], (DEPTH, HG_DIM), 0.02),
        'q_norm_w': 1.0 + nrm(ks[11], (DEPTH, HEAD_DIM), 0.02),
        'k_norm_w': 1.0 + nrm(ks[12], (DEPTH, HEAD_DIM), 0.02),
        'attn_sinks': nrm(ks[13], (DEPTH, ATT_HEADS), 0.5),
        'w_branch_hgrn': nrm(ks[14], (DEPTH, HG_WIDTH, D_MODEL), HG_WIDTH ** -0.5),
        'w_branch_attn': nrm(ks[15], (DEPTH, ATT_WIDTH, D_MODEL), ATT_WIDTH ** -0.5),
        'w_out': nrm(ks[16], (DEPTH, D_MODEL, D_MODEL), D_MODEL ** -0.5),
        'w_ffn_gate': nrm(ks[17], (DEPTH, D_MODEL, D_FF), D_MODEL ** -0.5),
        'w_ffn_up': nrm(ks[18], (DEPTH, D_MODEL, D_FF), D_MODEL ** -0.5),
        'w_ffn_down': nrm(ks[19], (DEPTH, D_FF, D_MODEL), D_FF ** -0.5),
    }


def reference(x, c, ctx, c_ctx, w_ada, b_ada, norm_mix_w, norm_ffn_w, w_in, hgrn_lb_logits,
              hgrn_norm_w, q_norm_w, k_norm_w, attn_sinks, w_branch_hgrn, w_branch_attn,
              w_out, w_ffn_gate, w_ffn_up, w_ffn_down):
    n_lat = x.shape[1]
    ROWS = n_lat // GRID_W
    rows = jnp.repeat(jnp.arange(ROWS), GRID_W)
    cols = jnp.tile(jnp.arange(GRID_W), ROWS)
    lower_bounds = jnp.cumsum(jax.nn.softmax(hgrn_lb_logits.astype(jnp.float32), axis=1), axis=1)

    for layer in range(DEPTH):
        last = layer == DEPTH - 1
        mod = jax.nn.silu(c) @ w_ada[layer] + b_ada[layer]
        mod_c = jax.nn.silu(c_ctx) @ w_ada[layer] + b_ada[layer]
        sh1, sc1, g1, sh2, sc2, g2 = [m[:, None, :] for m in jnp.split(mod, 6, axis=-1)]
        csh1, csc1, cg1, csh2, csc2, cg2 = jnp.split(mod_c, 6, axis=-1)
        lb_fwd, lb_bwd = lower_bounds[0, layer], lower_bounds[1, layer]

        h = _modulate(x, norm_mix_w[layer], sh1, sc1)
        hc = _modulate(ctx, norm_mix_w[layer], csh1, csc1)
        p = h @ w_in[layer]
        pc = hc @ (w_in[layer, :, :CTX_COLS] if last else w_in[layer])

        cf_fwd, cf_bwd, c_inp, c_k, c_v = _split_ctx_side(pc)
        s0 = jnp.zeros((ctx.shape[0], HG_HEADS, HG_DIM, HG_DIM), jnp.float32)
        c_q_hg = None if last else _hgrn_query(_split_query_side(pc)[0])
        co_fwd, cs_fwd = _hgrn_direction(c_q_hg, cf_fwd, c_inp, lb_fwd, s0, False)
        co_bwd, cs_bwd = _hgrn_direction(c_q_hg, cf_bwd, c_inp, lb_bwd, s0, True)
        ck = _rmsnorm(_heads(c_k, ATT_KV_HEADS, HEAD_DIM), k_norm_w[layer])
        cv = _heads(c_v, ATT_KV_HEADS, HEAD_DIM)

        f_fwd, f_bwd, inp, k_raw, v_raw = _split_ctx_side(p)
        q_hg_raw, g_hg, q_raw, gates = _split_query_side(p)
        q_hg = _hgrn_query(q_hg_raw)
        o_fwd, _ = _hgrn_direction(q_hg, f_fwd, inp, lb_fwd, cs_fwd, False)
        o_bwd, _ = _hgrn_direction(q_hg, f_bwd, inp, lb_bwd, cs_bwd, True)
        y_hg = _hgrn_readout(o_fwd + o_bwd, g_hg, hgrn_norm_w[layer], x.dtype)
        q = _axial_rope(_rmsnorm(_heads(q_raw, ATT_HEADS, HEAD_DIM), q_norm_w[layer]), rows, cols)
        k = _axial_rope(_rmsnorm(_heads(k_raw, ATT_KV_HEADS, HEAD_DIM), k_norm_w[layer]), rows, cols)
        y_at = _window_attention(q, k, _heads(v_raw, ATT_KV_HEADS, HEAD_DIM), ck, cv, attn_sinks[layer])
        x_new = x + g1 * _merge(y_hg, y_at, gates, w_branch_hgrn[layer], w_branch_attn[layer], w_out[layer])
        x_new = x_new + g2 * _swiglu(_modulate(x_new, norm_ffn_w[layer], sh2, sc2),
                                     w_ffn_gate[layer], w_ffn_up[layer], w_ffn_down[layer])

        if not last:
            _, cg_hg, cq_raw, c_gates = _split_query_side(pc)
            cy_hg = _hgrn_readout(co_fwd + co_bwd, cg_hg, hgrn_norm_w[layer], ctx.dtype)
            cq = _rmsnorm(_heads(cq_raw, ATT_HEADS, HEAD_DIM), q_norm_w[layer])
            cy_at = _context_attention(cq, ck, cv, attn_sinks[layer])
            ctx = ctx + cg1 * _merge(cy_hg, cy_at, c_gates, w_branch_hgrn[layer],
                                     w_branch_attn[layer], w_out[layer])
            ctx = ctx + cg2 * _swiglu(_modulate(ctx, norm_ffn_w[layer], csh2, csc2),
                                      w_ffn_gate[layer], w_ffn_up[layer], w_ffn_down[layer])
        x = x_new
    return x
```

```python
import functools

import jax
import jax.numpy as jnp
from jax import lax
from jax.experimental import pallas as pl
from jax.experimental.pallas import tpu as pltpu

F32 = jnp.float32
BF16 = jnp.bfloat16

D_MODEL = 1024
GRID_W = 64
EPS = 1e-6
HG_HEADS = 4
HG_DIM = 128
HG_WIDTH = HG_HEADS * HG_DIM
CHUNK = 32
ATT_HEADS = 8
ATT_KV_HEADS = 2
HEAD_DIM = 64
ATT_WIDTH = ATT_HEADS * HEAD_DIM
KV_WIDTH = ATT_KV_HEADS * HEAD_DIM
WINDOW = 128
ROPE_THETA = 10000.0
D_FF = 2816
CTX_COLS = 3 * HG_WIDTH + 2 * KV_WIDTH
IN_COLS = CTX_COLS + 2 * HG_WIDTH + ATT_WIDTH + 2 * D_MODEL

C_FF, C_FB, C_INP, C_K, C_V = 0, HG_WIDTH, 2 * HG_WIDTH, 3 * HG_WIDTH, 3 * HG_WIDTH + KV_WIDTH
C_QHG = CTX_COLS
C_GHG = C_QHG + HG_WIDTH
C_Q = C_GHG + HG_WIDTH
C_GATES = C_Q + ATT_WIDTH

LANES = 128
TOK_TILE = 256
HG_TILE = 256
ATT_BLOCK = 128
VMEM_LIMIT = 56 * 1024 * 1024
NEG = -1e30


def _dot(a, b):
    return jnp.dot(a, b, preferred_element_type=F32)


def _dot_nt(a, b):
    return lax.dot_general(a, b, (((1,), (1,)), ((), ())), preferred_element_type=F32)


def _sigmoid(x):
    return 1.0 / (1.0 + jnp.exp(-x))


def _const_spec(shape):
    n = len(shape)
    return pl.BlockSpec(shape, lambda *_: (0,) * n, pipeline_mode=pl.Buffered(1))


def _ada_kernel(c_ref, w_ref, b_ref, o_ref):
    c = c_ref[...]
    s = (c * _sigmoid(c)).astype(BF16)
    o_ref[...] = _dot(s, w_ref[...].astype(BF16)) + b_ref[...]


def _ada_call(cc, w_ada, b_ada):
    rows = cc.shape[0]
    n_out = w_ada.shape[1]
    bn = 1024
    return pl.pallas_call(
        _ada_kernel,
        grid=(n_out // bn,),
        in_specs=[pl.BlockSpec((rows, D_MODEL), lambda j: (0, 0)),
                  pl.BlockSpec((D_MODEL, bn), lambda j: (0, j)),
                  pl.BlockSpec((1, bn), lambda j: (0, j))],
        out_specs=pl.BlockSpec((rows, bn), lambda j: (0, j)),
        out_shape=jax.ShapeDtypeStruct((rows, n_out), F32),
        compiler_params=pltpu.CompilerParams(dimension_semantics=("arbitrary",),
                                             vmem_limit_bytes=VMEM_LIMIT),
        name="ada_mod",
    )(cc, w_ada, b_ada)


def _chunk_cumsum(x, reverse):
    n = x.shape[0]
    r = lax.broadcasted_iota(jnp.int32, x.shape, 0) & (CHUNK - 1)
    s = 1
    while s < CHUNK:
        if reverse:
            x = x + jnp.where(r < CHUNK - s, pltpu.roll(x, n - s, 0), 0.0)
        else:
            x = x + jnp.where(r >= s, pltpu.roll(x, s, 0), 0.0)
        s *= 2
    return x


def _head_sumsq(t, ones_blk):
    sq = t * t
    hi = sq.astype(BF16)
    lo = (sq - hi.astype(F32)).astype(BF16)
    return _dot(hi, ones_blk) + _dot(lo, ones_blk)


def _rope(t, cos, sin_signed, lane):
    partner = jnp.where((lane & 31) < 16, pltpu.roll(t, LANES - 16, 1), pltpu.roll(t, 16, 1))
    return t * cos + partner * sin_signed


def _inproj_kernel(x_ref, mod_ref, nw_ref, w_ref, lbl_ref, qw_ref, kw_ref, cos_ref, sin_ref,
                   *out_refs, latent):
    if latent:
        (qdf_ref, kdf_ref, kd2f_ref, qdb_ref, kdb_ref, kd2b_ref, dec_ref, vt_ref, v_ref,
         g_ref, q_ref, katt_ref, vatt_ref, gates_ref) = out_refs
    else:
        kd2f_ref, kd2b_ref, dec_ref, vt_ref, katt_ref, vatt_ref = out_refs

    tm = x_ref.shape[1]
    nchunk = tm // CHUNK
    x = x_ref[0]
    sh = mod_ref[0, :, 0:D_MODEL]
    sc = mod_ref[0, :, D_MODEL:2 * D_MODEL]
    ms = jnp.mean(x * x, axis=-1, keepdims=True)
    h = (x * lax.rsqrt(ms + EPS) * nw_ref[...]) * (1.0 + sc) + sh
    hb = h.astype(BF16)

    def mm(a, b):
        return _dot(hb, w_ref[:, a:b])

    l0f, l1f = lbl_ref[0:1, :], lbl_ref[1:2, :]
    l0b, l1b = lbl_ref[2:3, :], lbl_ref[3:4, :]
    lb_f = 1.0 / (1.0 + jnp.exp(l1f - l0f))
    lb_b = 1.0 / (1.0 + jnp.exp(l1b - l0b))

    if latent:
        q_hg = mm(C_QHG, C_QHG + HG_WIDTH)
        q_hg = q_hg * _sigmoid(q_hg) * (HG_DIM ** -0.5)

    for direction, (col, lb, reverse) in enumerate(((C_FF, lb_f, False), (C_FB, lb_b, True))):
        z = mm(col, col + HG_WIDTH)
        f = lb + (1.0 - lb) * _sigmoid(z)
        k = 1.0 - f
        cum = _chunk_cumsum(jnp.log(f), reverse)
        cum3 = cum.reshape(nchunk, CHUNK, HG_WIDTH)
        tot3 = cum3[:, 0:1, :] if reverse else cum3[:, CHUNK - 1:CHUNK, :]
        k3 = k.reshape(nchunk, CHUNK, HG_WIDTH)
        kd2 = (k3 * jnp.exp(tot3 - cum3)).reshape(tm, HG_WIDTH).astype(BF16)
        dec = jnp.exp(tot3).reshape(nchunk, HG_WIDTH)
        dec_ref[0, :, direction * HG_WIDTH:(direction + 1) * HG_WIDTH] = dec
        if direction == 0:
            kd2f_ref[0] = kd2
        else:
            kd2b_ref[0] = kd2
        if latent:
            qd = (q_hg * jnp.exp(cum)).astype(BF16)
            kd = (k * jnp.exp(-cum)).astype(BF16)
            if direction == 0:
                qdf_ref[0], kdf_ref[0] = qd, kd
            else:
                qdb_ref[0], kdb_ref[0] = qd, kd

    v = mm(C_INP, C_INP + HG_WIDTH)
    for s in range(tm // LANES):
        vt_ref[0, s] = v[s * LANES:(s + 1) * LANES, :].T.astype(BF16)
    if latent:
        v_ref[0] = v.astype(BF16)
        g = mm(C_GHG, C_GHG + HG_WIDTH)
        g_ref[0] = (g * _sigmoid(g)).astype(BF16)
        gates_ref[0] = _sigmoid(mm(C_GATES, C_GATES + 2 * D_MODEL)).astype(BF16)

    lane = lax.broadcasted_iota(jnp.int32, (tm, LANES), 1)
    bi = lax.broadcasted_iota(jnp.int32, (LANES, LANES), 0) // HEAD_DIM
    bj = lax.broadcasted_iota(jnp.int32, (LANES, LANES), 1) // HEAD_DIM
    ones_blk = jnp.where(bi == bj, 1.0, 0.0).astype(BF16)
    left = lane < HEAD_DIM

    kraw = mm(C_K, C_K + KV_WIDTH)
    kn = kraw * lax.rsqrt(_head_sumsq(kraw, ones_blk) * (1.0 / HEAD_DIM) + EPS) * kw_ref[...]
    if latent:
        kn = _rope(kn, cos_ref[...], sin_ref[...], lane)
    vraw = mm(C_V, C_V + KV_WIDTH)
    for src, dst in ((kn, katt_ref), (vraw, vatt_ref)):
        swapped = pltpu.roll(src, HEAD_DIM, 1)
        dst[0, :, 0 * LANES:1 * LANES] = jnp.where(left, src, 0.0).astype(BF16)
        dst[0, :, 1 * LANES:2 * LANES] = jnp.where(left, 0.0, swapped).astype(BF16)
        dst[0, :, 2 * LANES:3 * LANES] = jnp.where(left, swapped, 0.0).astype(BF16)
        dst[0, :, 3 * LANES:4 * LANES] = jnp.where(left, 0.0, src).astype(BF16)

    if latent:
        for s in range(ATT_WIDTH // LANES):
            qraw = mm(C_Q + s * LANES, C_Q + (s + 1) * LANES)
            qn = qraw * lax.rsqrt(_head_sumsq(qraw, ones_blk) * (1.0 / HEAD_DIM) + EPS) * qw_ref[...]
            qn = _rope(qn, cos_ref[...], sin_ref[...], lane) * (HEAD_DIM ** -0.5)
            q_ref[0, :, s * LANES:(s + 1) * LANES] = qn.astype(BF16)


def _inproj_call(x, mod3, nw, w_in, lbl, qw, kw, cos_t, sin_t, latent):
    b_, s_, _ = x.shape
    tm = TOK_TILE
    grid = (b_, s_ // tm)
    tok = lambda w: pl.BlockSpec((1, tm, w), lambda b, i: (b, i, 0))
    mod_map = (lambda b, i: (b, 0, 0)) if latent else (lambda b, i: (0, 0, 0))
    ncols = IN_COLS if latent else CTX_COLS
    in_specs = [tok(D_MODEL),
                pl.BlockSpec((1, 1, mod3.shape[2]), mod_map),
                _const_spec((1, D_MODEL)),
                pl.BlockSpec((D_MODEL, ncols), lambda b, i: (0, 0), pipeline_mode=pl.Buffered(1)),
                _const_spec((4, HG_WIDTH)),
                _const_spec((1, LANES)),
                _const_spec((1, LANES)),
                pl.BlockSpec((tm, LANES), lambda b, i: (i, 0)),
                pl.BlockSpec((tm, LANES), lambda b, i: (i, 0))]
    bf = lambda w: jax.ShapeDtypeStruct((b_, s_, w), BF16)
    dec_shape = jax.ShapeDtypeStruct((b_, s_ // CHUNK, 2 * HG_WIDTH), F32)
    dec_spec = pl.BlockSpec((1, tm // CHUNK, 2 * HG_WIDTH), lambda b, i: (b, i, 0))
    vt_shape = jax.ShapeDtypeStruct((b_, s_ // LANES, HG_WIDTH, LANES), BF16)
    vt_spec = pl.BlockSpec((1, tm // LANES, HG_WIDTH, LANES), lambda b, i: (b, i, 0, 0))
    if latent:
        out_shape = [bf(HG_WIDTH)] * 6 + [dec_shape, vt_shape, bf(HG_WIDTH), bf(HG_WIDTH),
                                         bf(ATT_WIDTH), bf(4 * LANES), bf(4 * LANES), bf(2 * D_MODEL)]
        out_specs = [tok(HG_WIDTH)] * 6 + [dec_spec, vt_spec, tok(HG_WIDTH), tok(HG_WIDTH),
                                          tok(ATT_WIDTH), tok(4 * LANES), tok(4 * LANES), tok(2 * D_MODEL)]
    else:
        out_shape = [bf(HG_WIDTH)] * 2 + [dec_shape, vt_shape, bf(4 * LANES), bf(4 * LANES)]
        out_specs = [tok(HG_WIDTH)] * 2 + [dec_spec, vt_spec, tok(4 * LANES), tok(4 * LANES)]
    return pl.pallas_call(
        functools.partial(_inproj_kernel, latent=latent),
        grid=grid, in_specs=in_specs, out_specs=out_specs, out_shape=out_shape,
        compiler_params=pltpu.CompilerParams(dimension_semantics=("parallel", "parallel"),
                                             vmem_limit_bytes=VMEM_LIMIT),
        name="inproj_latent" if latent else "inproj_ctx",
    )(x, mod3, nw, w_in, lbl, qw, kw, cos_t, sin_t)


def _hgrn_kernel(qdf_ref, kdf_ref, kd2f_ref, qdb_ref, kdb_ref, kd2b_ref, v_ref, vt_ref,
                 decf_ref, decb_ref, ckd2f_ref, ckd2b_ref, cvt_ref, cdecf_ref, cdecb_ref,
                 g_ref, nw_ref, y_ref, o_acc):
    seq = v_ref.shape[1]
    ctx_len = ckd2f_ref.shape[1]
    tile = HG_TILE
    cpt = tile // CHUNK
    cps = LANES // CHUNK
    n_tiles = seq // tile

    row_chunk = lax.broadcasted_iota(jnp.int32, (LANES, HG_DIM), 0) // CHUNK
    ti = lax.broadcasted_iota(jnp.int32, (tile, tile), 0)
    tj = lax.broadcasted_iota(jnp.int32, (tile, tile), 1)
    same = (ti // CHUNK) == (tj // CHUNK)
    mask_f = same & (tj <= ti)
    mask_b = same & (tj >= ti)

    def state_step(st, dec_row, vt_slab, kd2_slab, n_in_slab):
        kd2_m = jnp.where(row_chunk == n_in_slab, kd2_slab, jnp.zeros_like(kd2_slab))
        return dec_row * st + _dot(vt_slab, kd2_m)

    def ctx_state(ckd2_ref, cdec_ref, reverse):
        st = jnp.zeros((HG_DIM, HG_DIM), F32)
        order = range(ctx_len // CHUNK)
        for n in (reversed(order) if reverse else order):
            slab = n // cps
            st = state_step(st, cdec_ref[0, n:n + 1, :], cvt_ref[0, slab],
                            ckd2_ref[0, slab * LANES:(slab + 1) * LANES, :], n % cps)
        return st

    def run_direction(qd_ref, kd_ref, kd2_ref, dec_ref, st0, reverse, finalize):
        mask = mask_b if reverse else mask_f

        def tile_body(it, st):
            t = (n_tiles - 1 - it) if reverse else it
            base = pl.multiple_of(t * tile, tile)
            rows = pl.ds(base, tile)
            qd = qd_ref[0, rows, :]
            kd = kd_ref[0, rows, :]
            kd2 = kd2_ref[0, rows, :]
            v = v_ref[0, rows, :]
            dec = dec_ref[0, pl.ds(pl.multiple_of(t * cpt, cpt), cpt), :]
            sc = jnp.where(mask, _dot_nt(qd, kd), 0.0).astype(BF16)
            o = _dot(sc, v)
            order = range(cpt)
            outs = [None] * cpt
            for n in (reversed(order) if reverse else order):
                r0 = n * CHUNK
                outs[n] = o[r0:r0 + CHUNK, :] + _dot_nt(qd[r0:r0 + CHUNK, :], st.astype(BF16))
                slab = n // cps
                st = state_step(st, dec[n:n + 1, :], vt_ref[0, t * (tile // LANES) + slab],
                                kd2[slab * LANES:(slab + 1) * LANES, :], n % cps)
            o_t = jnp.concatenate(outs, axis=0)
            if finalize:
                o_t = o_t + o_acc[rows, :]
                ms = jnp.mean(o_t * o_t, axis=-1, keepdims=True)
                y = o_t * lax.rsqrt(ms + EPS) * nw_ref[...] * g_ref[0, rows, :].astype(F32)
                y_ref[0, rows, :] = y.astype(BF16)
            else:
                o_acc[rows, :] = o_t
            return st

        lax.fori_loop(0, n_tiles, tile_body, st0)

    run_direction(qdf_ref, kdf_ref, kd2f_ref, decf_ref, ctx_state(ckd2f_ref, cdecf_ref, False),
                  False, False)
    run_direction(qdb_ref, kdb_ref, kd2b_ref, decb_ref, ctx_state(ckd2b_ref, cdecb_ref, True),
                  True, True)


def _hgrn_call(qdf, kdf, kd2f, qdb, kdb, kd2b, v, vt, dec, ckd2f, ckd2b, cvt, cdec, g, nw):
    b_, s_, _ = v.shape
    l_ = ckd2f.shape[1]
    seq = lambda n: pl.BlockSpec((1, n, HG_DIM), lambda b, h: (b, 0, h))
    vt_spec = lambda n: pl.BlockSpec((1, n // LANES, HG_DIM, LANES), lambda b, h: (b, 0, h, 0))
    dec_f = lambda n: pl.BlockSpec((1, n // CHUNK, HG_DIM), lambda b, h: (b, 0, h))
    dec_b = lambda n: pl.BlockSpec((1, n // CHUNK, HG_DIM), lambda b, h: (b, 0, HG_HEADS + h))
    return pl.pallas_call(
        _hgrn_kernel,
        grid=(b_, HG_HEADS),
        in_specs=[seq(s_)] * 7 + [vt_spec(s_), dec_f(s_), dec_b(s_),
                                  seq(l_), seq(l_), vt_spec(l_), dec_f(l_), dec_b(l_),
                                  seq(s_), _const_spec((1, HG_DIM))],
        out_specs=seq(s_),
        out_shape=jax.ShapeDtypeStruct((b_, s_, HG_WIDTH), BF16),
        scratch_shapes=[pltpu.VMEM((s_, HG_DIM), F32)],
        compiler_params=pltpu.CompilerParams(dimension_semantics=("parallel", "parallel"),
                                             vmem_limit_bytes=VMEM_LIMIT),
        name="hgrn_scan",
    )(qdf, kdf, kd2f, qdb, kdb, kd2b, v, vt, dec, dec, ckd2f, ckd2b, cvt, cdec, cdec, g, nw)


def _attn_kernel(sink_ref, q_ref, kp_ref, ko_ref, kn_ref, vp_ref, vo_ref, vn_ref, kc_ref, vc_ref,
                 y_ref):
    i = pl.program_id(1)
    nb = pl.num_programs(1)
    blk = ATT_BLOCK
    qi = lax.broadcasted_iota(jnp.int32, (blk, blk), 0)
    kj = lax.broadcasted_iota(jnp.int32, (blk, blk), 1)
    mask_prev = (kj >= qi) & (i > 0)
    mask_next = (kj <= qi) & (i < nb - 1)
    lane = lax.broadcasted_iota(jnp.int32, (blk, LANES), 1)

    for slab in range(ATT_WIDTH // LANES):
        q = q_ref[0, :, slab * LANES:(slab + 1) * LANES]
        kv_head = slab // 2
        probs, values, inv_den = [], [], []
        for parity in range(2):
            cols = slice((2 * kv_head + parity) * LANES, (2 * kv_head + parity + 1) * LANES)
            sink = sink_ref[2 * slab + parity]
            s_p = jnp.where(mask_prev, _dot_nt(q, kp_ref[0, :, cols]), NEG)
            s_o = _dot_nt(q, ko_ref[0, :, cols])
            s_n = jnp.where(mask_next, _dot_nt(q, kn_ref[0, :, cols]), NEG)
            s_c = _dot_nt(q, kc_ref[0, :, cols])
            s = jnp.concatenate([s_p, s_o, s_n, s_c], axis=1)
            mx = jnp.maximum(jnp.max(s, axis=1, keepdims=True), sink)
            p = jnp.exp(s - mx)
            den = jnp.sum(p, axis=1, keepdims=True) + jnp.exp(sink - mx)
            inv_den.append(1.0 / den)
            probs.append(p.astype(BF16))
            values += [vp_ref[0, :, cols], vo_ref[0, :, cols], vn_ref[0, :, cols], vc_ref[0, :, cols]]
        o = _dot(jnp.concatenate(probs, axis=1), jnp.concatenate(values, axis=0))
        o = o * jnp.where(lane < HEAD_DIM, inv_den[0], inv_den[1])
        y_ref[0, :, slab * LANES:(slab + 1) * LANES] = o.astype(BF16)


def _attn_call(sinks, q, katt, vatt, ckatt, cvatt):
    b_, s_, _ = q.shape
    l_ = ckatt.shape[1]
    nb = s_ // ATT_BLOCK
    w = katt.shape[2]
    own = lambda width: pl.BlockSpec((1, ATT_BLOCK, width), lambda b, i: (b, i, 0))
    prev = pl.BlockSpec((1, ATT_BLOCK, w), lambda b, i: (b, jnp.maximum(i - 1, 0), 0))
    nxt = pl.BlockSpec((1, ATT_BLOCK, w), lambda b, i: (b, jnp.minimum(i + 1, nb - 1), 0))
    ctx = pl.BlockSpec((1, l_, w), lambda b, i: (b, 0, 0))
    return pl.pallas_call(
        _attn_kernel,
        grid=(b_, nb),
        in_specs=[pl.BlockSpec(memory_space=pltpu.SMEM), own(ATT_WIDTH),
                  prev, own(w), nxt, prev, own(w), nxt, ctx, ctx],
        out_specs=own(ATT_WIDTH),
        out_shape=jax.ShapeDtypeStruct((b_, s_, ATT_WIDTH), BF16),
        compiler_params=pltpu.CompilerParams(dimension_semantics=("parallel", "parallel"),
                                             vmem_limit_bytes=VMEM_LIMIT),
        name="window_attn",
    )(sinks, q, katt, katt, katt, vatt, vatt, vatt, ckatt, cvatt)


def _merge_ffn_kernel(x_ref, yh_ref, ya_ref, gates_ref, mod_ref, nw_ref, wbh_ref, wba_ref, wo_ref,
                      wg_ref, wu_ref, wd_ref, o_ref):
    x = x_ref[0]
    g1 = mod_ref[0, :, 2 * D_MODEL:3 * D_MODEL]
    sh2 = mod_ref[0, :, 3 * D_MODEL:4 * D_MODEL]
    sc2 = mod_ref[0, :, 4 * D_MODEL:5 * D_MODEL]
    g2 = mod_ref[0, :, 5 * D_MODEL:6 * D_MODEL]
    a = _dot(yh_ref[0], wbh_ref[...])
    b = _dot(ya_ref[0], wba_ref[...])
    mixed = (gates_ref[0, :, 0:D_MODEL].astype(F32) * a
             + gates_ref[0, :, D_MODEL:2 * D_MODEL].astype(F32) * b)
    x1 = x + g1 * _dot(mixed.astype(BF16), wo_ref[...])
    ms = jnp.mean(x1 * x1, axis=-1, keepdims=True)
    h2 = ((x1 * lax.rsqrt(ms + EPS) * nw_ref[...]) * (1.0 + sc2) + sh2).astype(BF16)
    gate = _dot(h2, wg_ref[...])
    up = _dot(h2, wu_ref[...])
    act = (gate * _sigmoid(gate) * up).astype(BF16)
    o_ref[0] = x1 + g2 * _dot(act, wd_ref[...])


def _merge_ffn_call(x, yh, ya, gates, mod3, nw, wbh, wba, wo, wg, wu, wd):
    b_, s_, _ = x.shape
    tm = TOK_TILE
    tok = lambda w: pl.BlockSpec((1, tm, w), lambda b, i: (b, i, 0))
    return pl.pallas_call(
        _merge_ffn_kernel,
        grid=(b_, s_ // tm),
        in_specs=[tok(D_MODEL), tok(HG_WIDTH), tok(ATT_WIDTH), tok(2 * D_MODEL),
                  pl.BlockSpec((1, 1, mod3.shape[2]), lambda b, i: (b, 0, 0)),
                  _const_spec((1, D_MODEL)),
                  _const_spec(wbh.shape), _const_spec(wba.shape), _const_spec(wo.shape),
                  _const_spec(wg.shape), _const_spec(wu.shape), _const_spec(wd.shape)],
        out_specs=tok(D_MODEL),
        out_shape=jax.ShapeDtypeStruct(x.shape, F32),
        compiler_params=pltpu.CompilerParams(dimension_semantics=("parallel", "parallel"),
                                             vmem_limit_bytes=VMEM_LIMIT),
        name="merge_ffn",
    )(x, yh, ya, gates, mod3, nw, wbh, wba, wo, wg, wu, wd)


def _rope_tables(n_tok):
    t = jnp.arange(n_tok)
    rows = (t // GRID_W).astype(F32)
    cols = (t % GRID_W).astype(F32)
    half = HEAD_DIM // 2
    inv_freq = ROPE_THETA ** (-jnp.arange(0, half, 2, dtype=F32) / half)
    d = jnp.arange(LANES) % HEAD_DIM
    pos = jnp.where((d < half)[None, :], rows[:, None], cols[:, None])
    ang = pos * inv_freq[(d % half) % (half // 2)][None, :]
    sign = jnp.where((d % half) < half // 2, -1.0, 1.0)[None, :]
    return jnp.cos(ang), jnp.sin(ang) * sign


def kernel(x, c, ctx, c_ctx, w_ada, b_ada, norm_mix_w, norm_ffn_w, w_in, hgrn_lb_logits, hgrn_norm_w,
           q_norm_w, k_norm_w, attn_sinks, w_branch_hgrn, w_branch_attn, w_out, w_ffn_gate, w_ffn_up,
           w_ffn_down):
    b_, s_, _ = x.shape
    layer = 0
    pad = 16 - (b_ + 1)
    cc = jnp.concatenate([c, c_ctx[None, :], jnp.zeros((pad, D_MODEL), F32)], axis=0)
    mod = _ada_call(cc, w_ada[layer], b_ada[layer][None, :])
    mod_lat = mod[:b_].reshape(b_, 1, 6 * D_MODEL)
    mod_ctx = mod[b_:b_ + 1].reshape(1, 1, 6 * D_MODEL)

    w_in_b = w_in[layer].astype(BF16)
    lbl = hgrn_lb_logits[:, 0:2, :].reshape(4, HG_WIDTH)
    qw = jnp.tile(q_norm_w[layer], 2)[None, :]
    kw = jnp.tile(k_norm_w[layer], 2)[None, :]
    nw_mix = norm_mix_w[layer][None, :]
    cos_t, sin_t = _rope_tables(s_)

    (qdf, kdf, kd2f, qdb, kdb, kd2b, dec, vt, v, g, q, katt, vatt, gates) = _inproj_call(
        x, mod_lat, nw_mix, w_in_b, lbl, qw, kw, cos_t, sin_t, latent=True)
    ckd2f, ckd2b, cdec, cvt, ckatt, cvatt = _inproj_call(
        ctx, mod_ctx, nw_mix, w_in_b, lbl, qw, kw, cos_t, sin_t, latent=False)

    y_hg = _hgrn_call(qdf, kdf, kd2f, qdb, kdb, kd2b, v, vt, dec, ckd2f, ckd2b, cvt, cdec, g,
                      hgrn_norm_w[layer][None, :])
    y_at = _attn_call(attn_sinks[layer], q, katt, vatt, ckatt, cvatt)

    return _merge_ffn_call(x, y_hg, y_at, gates, mod_lat, norm_ffn_w[layer][None, :],
                           w_branch_hgrn[layer].astype(BF16), w_branch_attn[layer].astype(BF16),
                           w_out[layer].astype(BF16), w_ffn_gate[layer].astype(BF16),
                           w_ffn_up[layer].astype(BF16), w_ffn_down[layer].astype(BF16))
```

```python
import functools

import jax
import jax.numpy as jnp
import numpy as np
from jax import lax
from jax.experimental import pallas as pl
from jax.experimental.pallas import tpu as pltpu

F32 = jnp.float32
BF16 = jnp.bfloat16

D_MODEL = 1024
GRID_W = 64
EPS = 1e-6
HG_HEADS = 4
HG_DIM = 128
HG_WIDTH = HG_HEADS * HG_DIM
CHUNK = 32
ATT_HEADS = 8
ATT_KV_HEADS = 2
HEAD_DIM = 64
ATT_WIDTH = ATT_HEADS * HEAD_DIM
KV_WIDTH = ATT_KV_HEADS * HEAD_DIM
WINDOW = 128
ROPE_THETA = 10000.0
D_FF = 2816
CTX_COLS = 3 * HG_WIDTH + 2 * KV_WIDTH
IN_COLS = CTX_COLS + 2 * HG_WIDTH + ATT_WIDTH + 2 * D_MODEL

C_FF, C_FB, C_INP, C_K, C_V = 0, HG_WIDTH, 2 * HG_WIDTH, 3 * HG_WIDTH, 3 * HG_WIDTH + KV_WIDTH
C_QHG = CTX_COLS
C_GHG = C_QHG + HG_WIDTH
C_Q = C_GHG + HG_WIDTH
C_GATES = C_Q + ATT_WIDTH

LANES = 128
TOK_TILE = 256
INPROJ_TILE = 256
HG_TILE = 256
HG_HEADS_PER_STEP = 2
ATT_BLOCK = 128
VMEM_LIMIT = 56 * 1024 * 1024
NEG = -1e30
LOG2E = 1.4426950408889634


def _dot(a, b):
    return jnp.dot(a, b, preferred_element_type=F32)


def _dot_nt(a, b):
    return lax.dot_general(a, b, (((1,), (1,)), ((), ())), preferred_element_type=F32)


def _sigmoid(x):
    return 0.5 * jnp.tanh(0.5 * x) + 0.5


def _const_spec(shape):
    n = len(shape)
    return pl.BlockSpec(shape, lambda *_: (0,) * n, pipeline_mode=pl.Buffered(1))


def _ada_kernel(c_ref, w_ref, b_ref, o_ref):
    c = c_ref[...]
    s = (c * _sigmoid(c)).astype(BF16)
    o_ref[...] = _dot(s, w_ref[...].astype(BF16)) + b_ref[...]


def _ada_call(cc, w_ada, b_ada):
    rows = cc.shape[0]
    n_out = w_ada.shape[1]
    bn = 1024
    return pl.pallas_call(
        _ada_kernel,
        grid=(n_out // bn,),
        in_specs=[pl.BlockSpec((rows, D_MODEL), lambda j: (0, 0)),
                  pl.BlockSpec((D_MODEL, bn), lambda j: (0, j)),
                  pl.BlockSpec((1, bn), lambda j: (0, j))],
        out_specs=pl.BlockSpec((rows, bn), lambda j: (0, j)),
        out_shape=jax.ShapeDtypeStruct((rows, n_out), F32),
        compiler_params=pltpu.CompilerParams(dimension_semantics=("arbitrary",),
                                             vmem_limit_bytes=VMEM_LIMIT),
        name="ada_mod",
    )(cc, w_ada, b_ada)


def _chunk_cumsum(x, reverse):
    n = x.shape[0]
    r = lax.broadcasted_iota(jnp.int32, x.shape, 0) & (CHUNK - 1)
    s = 1
    while s < CHUNK:
        if reverse:
            x = x + jnp.where(r < CHUNK - s, pltpu.roll(x, n - s, 0), 0.0)
        else:
            x = x + jnp.where(r >= s, pltpu.roll(x, s, 0), 0.0)
        s *= 2
    return x


def _rope(t, cos, sin_signed, lane):
    partner = jnp.where((lane & 31) < 16, pltpu.roll(t, LANES - 16, 1), pltpu.roll(t, 16, 1))
    return t * cos + partner * sin_signed


def _inproj_kernel(x_ref, mod_ref, nw_ref, w_ref, lbl_ref, qw_ref, kw_ref, cos_ref, sin_ref,
                   *out_refs, latent):
    if latent:
        (qdf_ref, kdf_ref, kd2f_ref, qdb_ref, kdb_ref, kd2b_ref, dec_ref, vt_ref, v_ref,
         g_ref, q_ref, katt_ref, vatt_ref, gates_ref) = out_refs
    else:
        kd2f_ref, kd2b_ref, dec_ref, vt_ref, katt_ref, vatt_ref = out_refs

    tm = x_ref.shape[1]
    nchunk = tm // CHUNK
    x = x_ref[0]
    sh = mod_ref[0, :, 0:D_MODEL]
    sc = mod_ref[0, :, D_MODEL:2 * D_MODEL]
    ms = jnp.mean(x * x, axis=-1, keepdims=True)
    h = (x * lax.rsqrt(ms + EPS) * nw_ref[...]) * (1.0 + sc) + sh
    hb = h.astype(BF16)

    def mm(a, b):
        return _dot(hb, w_ref[:, a:b])

    l0f, l1f = lbl_ref[0:1, :], lbl_ref[1:2, :]
    l0b, l1b = lbl_ref[2:3, :], lbl_ref[3:4, :]
    lb_f = 1.0 / (1.0 + jnp.exp(l1f - l0f))
    lb_b = 1.0 / (1.0 + jnp.exp(l1b - l0b))

    gw = HG_WIDTH // 2
    sw = D_MODEL // 2
    lane = lax.broadcasted_iota(jnp.int32, (tm, LANES), 1)
    left = lane < HEAD_DIM
    bi = lax.broadcasted_iota(jnp.int32, (2 * LANES, LANES), 0) // HEAD_DIM
    bj = lax.broadcasted_iota(jnp.int32, (2 * LANES, LANES), 1) // HEAD_DIM
    ones_blk2 = jnp.where((bi % 2) == bj, 1.0, 0.0).astype(BF16)
    keep = {}

    def qhg_finish(raw):
        keep["q_hg"] = raw * _sigmoid(raw) * (HG_DIM ** -0.5)

    def gate_finish(raw, direction, part):
        reverse = direction == 1
        lo = part * gw
        lb = (lb_f, lb_b)[direction][:, lo:lo + gw]
        f = lb + (1.0 - lb) * _sigmoid(raw)
        k = 1.0 - f
        cum = _chunk_cumsum(jnp.log(f), reverse)
        cum3 = cum.reshape(nchunk, CHUNK, gw)
        tot3 = cum3[:, 0:1, :] if reverse else cum3[:, CHUNK - 1:CHUNK, :]
        k3 = k.reshape(nchunk, CHUNK, gw)
        kd2 = (k3 * jnp.exp(tot3 - cum3)).reshape(tm, gw).astype(BF16)
        dec_ref[0, :, direction * HG_WIDTH + lo:direction * HG_WIDTH + lo + gw] = (
            jnp.exp(tot3).reshape(nchunk, gw))
        (kd2f_ref, kd2b_ref)[direction][0, :, lo:lo + gw] = kd2
        if latent:
            qd = (keep["q_hg"][:, lo:lo + gw] * jnp.exp(cum)).astype(BF16)
            kd = (k * jnp.exp(-cum)).astype(BF16)
            (qdf_ref, qdb_ref)[direction][0, :, lo:lo + gw] = qd
            (kdf_ref, kdb_ref)[direction][0, :, lo:lo + gw] = kd

    def mgate_finish(raw, j):
        gates_ref[0, :, j * sw:(j + 1) * sw] = _sigmoid(raw).astype(BF16)

    def v_finish(raw):
        for s in range(tm // LANES):
            vt_ref[0, s] = raw[s * LANES:(s + 1) * LANES, :].T.astype(BF16)
        if latent:
            v_ref[0] = raw.astype(BF16)

    def g_finish(raw):
        g_ref[0] = (raw * _sigmoid(raw)).astype(BF16)

    def split_sq(t):
        sq = t * t
        hi = sq.astype(BF16)
        lo = (sq - hi.astype(F32)).astype(BF16)
        return jnp.concatenate([hi, lo], axis=1)

    def spread_heads(src, dst):
        swapped = pltpu.roll(src, HEAD_DIM, 1)
        dst[0, :, 0 * LANES:1 * LANES] = jnp.where(left, src, 0.0).astype(BF16)
        dst[0, :, 1 * LANES:2 * LANES] = jnp.where(left, 0.0, swapped).astype(BF16)
        dst[0, :, 2 * LANES:3 * LANES] = jnp.where(left, swapped, 0.0).astype(BF16)
        dst[0, :, 3 * LANES:4 * LANES] = jnp.where(left, 0.0, src).astype(BF16)

    def kv_finish(raw):
        keep["k_raw"] = raw[:, 0:LANES]
        keep["k_split"] = split_sq(keep["k_raw"])
        for s in range(tm // LANES):
            vatt_ref[0, s] = raw[s * LANES:(s + 1) * LANES, LANES:2 * LANES].T.astype(BF16)

    def knorm_finish(ss):
        kn = keep["k_raw"] * lax.rsqrt(ss * (1.0 / HEAD_DIM) + EPS) * kw_ref[...]
        if latent:
            kn = _rope(kn, cos_ref[...], sin_ref[...], lane)
        spread_heads(kn, katt_ref)

    def q_finish(raw):
        keep["q_raw"] = raw
        keep["q_split"] = [split_sq(raw[:, s * LANES:(s + 1) * LANES]) for s in range(ATT_WIDTH // LANES)]

    def qnorm_finish(ss_list):
        for s, ss in enumerate(ss_list):
            qraw = keep["q_raw"][:, s * LANES:(s + 1) * LANES]
            qn = qraw * lax.rsqrt(ss * (1.0 / HEAD_DIM) + EPS) * qw_ref[...]
            qn = _rope(qn, cos_ref[...], sin_ref[...], lane) * (HEAD_DIM ** -0.5 * LOG2E)
            q_ref[0, :, s * LANES:(s + 1) * LANES] = qn.astype(BF16)

    def gate_stage(direction, part):
        col = (C_FF, C_FB)[direction] + part * gw
        return (lambda: mm(col, col + gw),
                functools.partial(gate_finish, direction=direction, part=part))

    def mgate_stage(j):
        c0 = C_GATES + j * sw
        return (lambda: mm(c0, c0 + sw), functools.partial(mgate_finish, j=j))

    v_stage = (lambda: mm(C_INP, C_INP + HG_WIDTH), v_finish)
    kv_stage = (lambda: mm(C_K, C_K + 2 * KV_WIDTH), kv_finish)
    knorm_stage = (lambda: _dot(keep["k_split"], ones_blk2), knorm_finish)
    if latent:
        stages = [(lambda: mm(C_QHG, C_QHG + HG_WIDTH), qhg_finish)]
        for j in range(4):
            stages += [gate_stage(j // 2, j % 2), mgate_stage(j)]
        stages += [(lambda: mm(C_Q, C_Q + ATT_WIDTH), q_finish),
                   v_stage,
                   (lambda: [_dot(t, ones_blk2) for t in keep["q_split"]], qnorm_finish),
                   kv_stage,
                   (lambda: mm(C_GHG, C_GHG + HG_WIDTH), g_finish),
                   knorm_stage]
    else:
        stages = [gate_stage(0, 0), gate_stage(0, 1), gate_stage(1, 0), kv_stage, gate_stage(1, 1),
                  knorm_stage, v_stage]

    raw = stages[0][0]()
    for i, (_, finish) in enumerate(stages):
        nxt = stages[i + 1][0]() if i + 1 < len(stages) else None
        finish(raw)
        raw = nxt


def _inproj_call(x, mod3, nw, w_in, lbl, qw, kw, cos_t, sin_t, latent):
    b_, s_, _ = x.shape
    tm = min(INPROJ_TILE, s_)
    grid = (b_, s_ // tm)
    tok = lambda w: pl.BlockSpec((1, tm, w), lambda b, i: (b, i, 0))
    mod_map = (lambda b, i: (b, 0, 0)) if latent else (lambda b, i: (0, 0, 0))
    ncols = IN_COLS if latent else CTX_COLS
    in_specs = [tok(D_MODEL),
                pl.BlockSpec((1, 1, mod3.shape[2]), mod_map),
                _const_spec((1, D_MODEL)),
                pl.BlockSpec((D_MODEL, ncols), lambda b, i: (0, 0), pipeline_mode=pl.Buffered(1)),
                _const_spec((4, HG_WIDTH)),
                _const_spec((1, LANES)),
                _const_spec((1, LANES)),
                pl.BlockSpec((tm, LANES), lambda b, i: (i, 0)),
                pl.BlockSpec((tm, LANES), lambda b, i: (i, 0))]
    bf = lambda w: jax.ShapeDtypeStruct((b_, s_, w), BF16)
    dec_shape = jax.ShapeDtypeStruct((b_, s_ // CHUNK, 2 * HG_WIDTH), F32)
    dec_spec = pl.BlockSpec((1, tm // CHUNK, 2 * HG_WIDTH), lambda b, i: (b, i, 0))
    vt_shape = jax.ShapeDtypeStruct((b_, s_ // LANES, HG_WIDTH, LANES), BF16)
    vt_spec = pl.BlockSpec((1, tm // LANES, HG_WIDTH, LANES), lambda b, i: (b, i, 0, 0))
    vatt_shape = jax.ShapeDtypeStruct((b_, s_ // LANES, KV_WIDTH, LANES), BF16)
    vatt_spec = pl.BlockSpec((1, tm // LANES, KV_WIDTH, LANES), lambda b, i: (b, i, 0, 0))
    if latent:
        out_shape = [bf(HG_WIDTH)] * 6 + [dec_shape, vt_shape, bf(HG_WIDTH), bf(HG_WIDTH),
                                         bf(ATT_WIDTH), bf(4 * LANES), vatt_shape, bf(2 * D_MODEL)]
        out_specs = [tok(HG_WIDTH)] * 6 + [dec_spec, vt_spec, tok(HG_WIDTH), tok(HG_WIDTH),
                                          tok(ATT_WIDTH), tok(4 * LANES), vatt_spec, tok(2 * D_MODEL)]
    else:
        out_shape = [bf(HG_WIDTH)] * 2 + [dec_shape, vt_shape, bf(4 * LANES), vatt_shape]
        out_specs = [tok(HG_WIDTH)] * 2 + [dec_spec, vt_spec, tok(4 * LANES), vatt_spec]
    return pl.pallas_call(
        functools.partial(_inproj_kernel, latent=latent),
        grid=grid, in_specs=in_specs, out_specs=out_specs, out_shape=out_shape,
        compiler_params=pltpu.CompilerParams(dimension_semantics=("parallel", "parallel"),
                                             vmem_limit_bytes=VMEM_LIMIT),
        name="inproj_latent" if latent else "inproj_ctx",
    )(x, mod3, nw, w_in, lbl, qw, kw, cos_t, sin_t)


def _hgrn_kernel(qdf_ref, kdf_ref, kd2f_ref, qdb_ref, kdb_ref, kd2b_ref, v_ref, vt_ref,
                 decf_ref, decb_ref, ckd2f_ref, ckd2b_ref, cvt_ref, cdecf_ref, cdecb_ref,
                 g_ref, nw_ref, y_ref, o_acc):
    seq = v_ref.shape[1]
    ctx_len = ckd2f_ref.shape[1]
    heads = v_ref.shape[2] // HG_DIM
    tile = HG_TILE
    cpt = tile // CHUNK
    cps = LANES // CHUNK
    spt = tile // LANES
    n_tiles = seq // tile
    half = n_tiles // 2

    row_chunk = lax.broadcasted_iota(jnp.int32, (LANES, HG_DIM), 0) // CHUNK
    ti = lax.broadcasted_iota(jnp.int32, (tile, tile), 0)
    tj = lax.broadcasted_iota(jnp.int32, (tile, tile), 1)
    same = (ti // CHUNK) == (tj // CHUNK)
    mask_f = same & (tj <= ti)
    mask_b = same & (tj >= ti)

    def chunk_update(vt_slab, kd2_slab, n_in_slab):
        kd2_m = jnp.where(row_chunk == n_in_slab, kd2_slab, jnp.zeros_like(kd2_slab))
        return _dot(vt_slab, kd2_m)

    def ctx_state(ckd2_ref, cdec_ref, cols, reverse):
        st = jnp.zeros((HG_DIM, HG_DIM), F32)
        order = range(ctx_len // CHUNK)
        for n in (reversed(order) if reverse else order):
            slab = n // cps
            st = cdec_ref[0, n:n + 1, cols] * st + chunk_update(
                cvt_ref[0, slab, cols, :], ckd2_ref[0, slab * LANES:(slab + 1) * LANES, cols], n % cps)
        return st

    def emit(o_t, cols, rows, finalize):
        if finalize:
            o_t = o_t + o_acc[rows, cols]
            ms = jnp.mean(o_t * o_t, axis=-1, keepdims=True)
            y = o_t * lax.rsqrt(ms + EPS) * nw_ref[...] * g_ref[0, rows, cols].astype(F32)
            y_ref[0, rows, cols] = y.astype(BF16)
        else:
            o_acc[rows, cols] = o_t

    def body(it, states, finalize):
        chains = []
        for h in range(heads):
            cols = slice(h * HG_DIM, (h + 1) * HG_DIM)
            chains.append((qdf_ref, kdf_ref, kd2f_ref, decf_ref, cols, it, False))
            chains.append((qdb_ref, kdb_ref, kd2b_ref, decb_ref, cols, n_tiles - 1 - it, True))
        n_ch = len(chains)
        rows = [pl.ds(pl.multiple_of(c[5] * tile, tile), tile) for c in chains]
        orders = [list(reversed(range(cpt))) if c[6] else list(range(cpt)) for c in chains]
        qd, sc, ups = [], [], []
        for (qd_ref, kd_ref, kd2_ref, _, cols, t, reverse), r in zip(chains, rows):
            qd.append(qd_ref[0, r, cols])
            sc.append(_dot_nt(qd[-1], kd_ref[0, r, cols]))
            kd2 = kd2_ref[0, r, cols]
            ups.append([chunk_update(vt_ref[0, t * spt + n // cps, cols, :],
                                     kd2[(n // cps) * LANES:(n // cps + 1) * LANES, :], n % cps)
                        for n in range(cpt)])
        o = []
        for c, r, s in zip(chains, rows, sc):
            p = jnp.where(mask_b if c[6] else mask_f, s, 0.0).astype(BF16)
            o.append(_dot(p, v_ref[0, r, c[4]]))
        starts, new_states = [], []
        for i, c in enumerate(chains):
            dec = c[3][0, pl.ds(pl.multiple_of(c[5] * cpt, cpt), cpt), c[4]]
            st = states[i]
            start = [None] * cpt
            for n in orders[i]:
                start[n] = st.T.astype(BF16)
                st = dec[n:n + 1, :] * st + ups[i][n]
            starts.append(start)
            new_states.append(st)
        for i, c in enumerate(chains):
            outs = [o[i][n * CHUNK:(n + 1) * CHUNK, :]
                    + _dot(qd[i][n * CHUNK:(n + 1) * CHUNK, :], starts[i][n]) for n in range(cpt)]
            emit(jnp.concatenate(outs, axis=0), c[4], rows[i], finalize)
        return tuple(new_states)

    states = []
    for h in range(heads):
        cols = slice(h * HG_DIM, (h + 1) * HG_DIM)
        states += [ctx_state(ckd2f_ref, cdecf_ref, cols, False), ctx_state(ckd2b_ref, cdecb_ref, cols, True)]
    states = lax.fori_loop(0, half, functools.partial(body, finalize=False), tuple(states))
    lax.fori_loop(half, n_tiles, functools.partial(body, finalize=True), states)


def _hgrn_call(qdf, kdf, kd2f, qdb, kdb, kd2b, v, vt, dec, ckd2f, ckd2b, cvt, cdec, g, nw):
    b_, s_, _ = v.shape
    l_ = ckd2f.shape[1]
    w = HG_HEADS_PER_STEP * HG_DIM
    groups = HG_HEADS // HG_HEADS_PER_STEP
    seq = lambda n: pl.BlockSpec((1, n, w), lambda b, h: (b, 0, h))
    vt_spec = lambda n: pl.BlockSpec((1, n // LANES, w, LANES), lambda b, h: (b, 0, h, 0))
    dec_f = lambda n: pl.BlockSpec((1, n // CHUNK, w), lambda b, h: (b, 0, h))
    dec_b = lambda n: pl.BlockSpec((1, n // CHUNK, w), lambda b, h: (b, 0, groups + h))
    return pl.pallas_call(
        _hgrn_kernel,
        grid=(b_, groups),
        in_specs=[seq(s_)] * 7 + [vt_spec(s_), dec_f(s_), dec_b(s_),
                                  seq(l_), seq(l_), vt_spec(l_), dec_f(l_), dec_b(l_),
                                  seq(s_), _const_spec((1, HG_DIM))],
        out_specs=seq(s_),
        out_shape=jax.ShapeDtypeStruct((b_, s_, HG_WIDTH), BF16),
        scratch_shapes=[pltpu.VMEM((s_, w), F32)],
        compiler_params=pltpu.CompilerParams(dimension_semantics=("parallel", "parallel"),
                                             vmem_limit_bytes=VMEM_LIMIT),
        name="hgrn_scan",
    )(qdf, kdf, kd2f, qdb, kdb, kd2b, v, vt, dec, dec, ckd2f, ckd2b, cvt, cdec, cdec, g, nw)


def _attn_kernel(sink_ref, q_ref, kp_ref, ko_ref, kn_ref, vp_ref, vo_ref, vn_ref, kc_ref, vc_ref,
                 y_ref):
    i = pl.program_id(1)
    nb = pl.num_programs(1)
    blk = ATT_BLOCK
    n_ctx_blk = vc_ref.shape[1]
    key_r = lax.broadcasted_iota(jnp.int32, (blk, blk), 0)
    qry_c = lax.broadcasted_iota(jnp.int32, (blk, blk), 1)
    mask_prev = (key_r >= qry_c) & (i > 0)
    mask_next = (key_r <= qry_c) & (i < nb - 1)
    heads = ATT_HEADS
    keep = {}

    n_keys = 3 * blk + kc_ref.shape[1]

    def score_issue(slab):
        kv_head = slab // 2
        q = q_ref[0, :, slab * LANES:(slab + 1) * LANES]
        keys = []
        for parity in range(2):
            cols = slice((2 * kv_head + parity) * LANES, (2 * kv_head + parity + 1) * LANES)
            keys += [k_ref[0, :, cols] for k_ref in (kp_ref, ko_ref, kn_ref, kc_ref)]
        return _dot_nt(jnp.concatenate(keys, axis=0), q)

    def score_finish(raw, slab):
        for parity in range(2):
            h = 2 * slab + parity
            s = raw[parity * n_keys:(parity + 1) * n_keys, :]
            sink = sink_ref[h] * LOG2E
            s = jnp.concatenate([jnp.where(mask_prev, s[0:blk], NEG), s[blk:2 * blk],
                                 jnp.where(mask_next, s[2 * blk:3 * blk], NEG), s[3 * blk:]], axis=0)
            mx = jnp.maximum(jnp.max(s, axis=0, keepdims=True), sink)
            p = jnp.exp2(s - mx)
            den = jnp.sum(p, axis=0, keepdims=True) + jnp.exp2(sink - mx)
            keep["p", h] = p.astype(BF16)
            keep["inv", h] = 1.0 / den

    def value_issue(h):
        kv_head = h // (ATT_HEADS // ATT_KV_HEADS)
        rows = slice(kv_head * HEAD_DIM, (kv_head + 1) * HEAD_DIM)
        vt = jnp.concatenate([vp_ref[0, 0, rows, :], vo_ref[0, 0, rows, :], vn_ref[0, 0, rows, :]]
                             + [vc_ref[0, c, rows, :] for c in range(n_ctx_blk)], axis=1)
        return _dot(vt, keep["p", h])

    def value_finish(raw, h):
        keep["o", h] = raw * keep["inv", h]
        if h % 2 == 1:
            slab = h // 2
            o_t = jnp.concatenate([keep["o", h - 1], keep["o", h]], axis=0)
            y_ref[0, :, slab * LANES:(slab + 1) * LANES] = o_t.T.astype(BF16)

    n_slabs = heads // 2
    stages = [(functools.partial(score_issue, 0), functools.partial(score_finish, slab=0))]
    for slab in range(n_slabs):
        if slab + 1 < n_slabs:
            stages.append((functools.partial(score_issue, slab + 1),
                           functools.partial(score_finish, slab=slab + 1)))
        for h in (2 * slab, 2 * slab + 1):
            stages.append((functools.partial(value_issue, h), functools.partial(value_finish, h=h)))
    raw = stages[0][0]()
    for n, (_, finish) in enumerate(stages):
        nxt = stages[n + 1][0]() if n + 1 < len(stages) else None
        finish(raw)
        raw = nxt


def _attn_call(sinks, q, katt, vatt, ckatt, cvatt):
    b_, s_, _ = q.shape
    l_ = ckatt.shape[1]
    nb = s_ // ATT_BLOCK
    w = katt.shape[2]
    own = lambda width: pl.BlockSpec((1, ATT_BLOCK, width), lambda b, i: (b, i, 0))
    prev = pl.BlockSpec((1, ATT_BLOCK, w), lambda b, i: (b, jnp.maximum(i - 1, 0), 0))
    nxt = pl.BlockSpec((1, ATT_BLOCK, w), lambda b, i: (b, jnp.minimum(i + 1, nb - 1), 0))
    ctx = pl.BlockSpec((1, l_, w), lambda b, i: (b, 0, 0))
    vblk = lambda f: pl.BlockSpec((1, 1, KV_WIDTH, LANES), lambda b, i: (b, f(i), 0, 0))
    v_prev = vblk(lambda i: jnp.maximum(i - 1, 0))
    v_own = vblk(lambda i: i)
    v_next = vblk(lambda i: jnp.minimum(i + 1, nb - 1))
    v_ctx = pl.BlockSpec((1, l_ // LANES, KV_WIDTH, LANES), lambda b, i: (b, 0, 0, 0))
    return pl.pallas_call(
        _attn_kernel,
        grid=(b_, nb),
        in_specs=[pl.BlockSpec(memory_space=pltpu.SMEM), own(ATT_WIDTH),
                  prev, own(w), nxt, v_prev, v_own, v_next, ctx, v_ctx],
        out_specs=own(ATT_WIDTH),
        out_shape=jax.ShapeDtypeStruct((b_, s_, ATT_WIDTH), BF16),
        compiler_params=pltpu.CompilerParams(dimension_semantics=("parallel", "parallel"),
                                             vmem_limit_bytes=VMEM_LIMIT),
        name="window_attn",
    )(sinks, q, katt, katt, katt, vatt, vatt, vatt, ckatt, cvatt)


def _merge_ffn_kernel(x_ref, yh_ref, ya_ref, gates_ref, mod_ref, nw_ref, wbh_ref, wba_ref, wo_ref,
                      wg_ref, wu_ref, wd_ref, o_ref):
    x = x_ref[0]
    g1 = mod_ref[0, :, 2 * D_MODEL:3 * D_MODEL]
    sh2 = mod_ref[0, :, 3 * D_MODEL:4 * D_MODEL]
    sc2 = mod_ref[0, :, 4 * D_MODEL:5 * D_MODEL]
    g2 = mod_ref[0, :, 5 * D_MODEL:6 * D_MODEL]
    a = _dot(yh_ref[0], wbh_ref[...])
    b = _dot(ya_ref[0], wba_ref[...])
    mixed = (gates_ref[0, :, 0:D_MODEL].astype(F32) * a
             + gates_ref[0, :, D_MODEL:2 * D_MODEL].astype(F32) * b)
    x1 = x + g1 * _dot(mixed.astype(BF16), wo_ref[...])
    ms = jnp.mean(x1 * x1, axis=-1, keepdims=True)
    h2 = ((x1 * lax.rsqrt(ms + EPS) * nw_ref[...]) * (1.0 + sc2) + sh2).astype(BF16)
    gate = _dot(h2, wg_ref[...])
    up = _dot(h2, wu_ref[...])
    act = (gate * _sigmoid(gate) * up).astype(BF16)
    o_ref[0] = x1 + g2 * _dot(act, wd_ref[...])


def _merge_ffn_call(x, yh, ya, gates, mod3, nw, wbh, wba, wo, wg, wu, wd):
    b_, s_, _ = x.shape
    tm = TOK_TILE
    tok = lambda w: pl.BlockSpec((1, tm, w), lambda b, i: (b, i, 0))
    return pl.pallas_call(
        _merge_ffn_kernel,
        grid=(b_, s_ // tm),
        in_specs=[tok(D_MODEL), tok(HG_WIDTH), tok(ATT_WIDTH), tok(2 * D_MODEL),
                  pl.BlockSpec((1, 1, mod3.shape[2]), lambda b, i: (b, 0, 0)),
                  _const_spec((1, D_MODEL)),
                  _const_spec(wbh.shape), _const_spec(wba.shape), _const_spec(wo.shape),
                  _const_spec(wg.shape), _const_spec(wu.shape), _const_spec(wd.shape)],
        out_specs=tok(D_MODEL),
        out_shape=jax.ShapeDtypeStruct(x.shape, F32),
        compiler_params=pltpu.CompilerParams(dimension_semantics=("parallel", "parallel"),
                                             vmem_limit_bytes=VMEM_LIMIT),
        name="merge_ffn",
    )(x, yh, ya, gates, mod3, nw, wbh, wba, wo, wg, wu, wd)


def _rope_tables(n_tok):
    t = np.arange(n_tok)
    rows = (t // GRID_W).astype(np.float64)
    cols = (t % GRID_W).astype(np.float64)
    half = HEAD_DIM // 2
    inv_freq = ROPE_THETA ** (-np.arange(0, half, 2, dtype=np.float64) / half)
    d = np.arange(LANES) % HEAD_DIM
    pos = np.where((d < half)[None, :], rows[:, None], cols[:, None])
    ang = pos * inv_freq[(d % half) % (half // 2)][None, :]
    sign = np.where((d % half) < half // 2, -1.0, 1.0)[None, :]
    return jnp.asarray(np.cos(ang), F32), jnp.asarray(np.sin(ang) * sign, F32)


def kernel(x, c, ctx, c_ctx, w_ada, b_ada, norm_mix_w, norm_ffn_w, w_in, hgrn_lb_logits, hgrn_norm_w,
           q_norm_w, k_norm_w, attn_sinks, w_branch_hgrn, w_branch_attn, w_out, w_ffn_gate, w_ffn_up,
           w_ffn_down):
    b_, s_, _ = x.shape
    layer = 0
    pad = 16 - (b_ + 1)
    cc = jnp.concatenate([c, c_ctx[None, :], jnp.zeros((pad, D_MODEL), F32)], axis=0)
    mod = _ada_call(cc, w_ada[layer], b_ada[layer][None, :])
    mod_lat = mod[:b_].reshape(b_, 1, 6 * D_MODEL)
    mod_ctx = mod[b_:b_ + 1].reshape(1, 1, 6 * D_MODEL)

    w_in_b = w_in[layer].astype(BF16)
    lbl = hgrn_lb_logits[:, 0:2, :].reshape(4, HG_WIDTH)
    qw = jnp.tile(q_norm_w[layer], 2)[None, :]
    kw = jnp.tile(k_norm_w[layer], 2)[None, :]
    nw_mix = norm_mix_w[layer][None, :]
    cos_t, sin_t = _rope_tables(s_)

    (qdf, kdf, kd2f, qdb, kdb, kd2b, dec, vt, v, g, q, katt, vatt, gates) = _inproj_call(
        x, mod_lat, nw_mix, w_in_b, lbl, qw, kw, cos_t, sin_t, latent=True)
    ckd2f, ckd2b, cdec, cvt, ckatt, cvatt = _inproj_call(
        ctx, mod_ctx, nw_mix, w_in_b, lbl, qw, kw, cos_t, sin_t, latent=False)

    y_hg = _hgrn_call(qdf, kdf, kd2f, qdb, kdb, kd2b, v, vt, dec, ckd2f, ckd2b, cvt, cdec, g,
                      hgrn_norm_w[layer][None, :])
    y_at = _attn_call(attn_sinks[layer], q, katt, vatt, ckatt, cvatt)

    return _merge_ffn_call(x, y_hg, y_at, gates, mod_lat, norm_ffn_w[layer][None, :],
                           w_branch_hgrn[layer].astype(BF16), w_branch_attn[layer].astype(BF16),
                           w_out[layer].astype(BF16), w_ffn_gate[layer].astype(BF16),
                           w_ffn_up[layer].astype(BF16), w_ffn_down[layer].astype(BF16))
```

```python
import functools

import jax
import jax.numpy as jnp
import numpy as np
from jax import lax
from jax.experimental import pallas as pl
from jax.experimental.pallas import tpu as pltpu

F32 = jnp.float32
BF16 = jnp.bfloat16

D_MODEL = 1024
GRID_W = 64
EPS = 1e-6
HG_HEADS = 4
HG_DIM = 128
HG_WIDTH = HG_HEADS * HG_DIM
CHUNK = 32
ATT_HEADS = 8
ATT_KV_HEADS = 2
HEAD_DIM = 64
ATT_WIDTH = ATT_HEADS * HEAD_DIM
KV_WIDTH = ATT_KV_HEADS * HEAD_DIM
WINDOW = 128
ROPE_THETA = 10000.0
D_FF = 2816
CTX_COLS = 3 * HG_WIDTH + 2 * KV_WIDTH
IN_COLS = CTX_COLS + 2 * HG_WIDTH + ATT_WIDTH + 2 * D_MODEL

C_FF, C_FB, C_INP, C_K, C_V = 0, HG_WIDTH, 2 * HG_WIDTH, 3 * HG_WIDTH, 3 * HG_WIDTH + KV_WIDTH
C_QHG = CTX_COLS
C_GHG = C_QHG + HG_WIDTH
C_Q = C_GHG + HG_WIDTH
C_GATES = C_Q + ATT_WIDTH

LANES = 128
TOK_TILE = 512
FFN_SUBTILES = 2
MXU_DEPTH = 256
FFN_CHUNK_BOUNDS = (0, 6 * MXU_DEPTH, D_FF)
INPROJ_TILE = 256
HG_TILE = 256
HG_HEADS_PER_STEP = 2
ATT_BLOCK = 128
ATT_GROUP_HEADS = 2
ATT_STEP_BLOCKS = 4
VMEM_LIMIT = 56 * 1024 * 1024
NEG = -1e30
LOG2E = 1.4426950408889634


def _dot(a, b):
    return jnp.dot(a, b, preferred_element_type=F32)


def _dot_nt(a, b):
    return lax.dot_general(a, b, (((1,), (1,)), ((), ())), preferred_element_type=F32)


def _sigmoid(x):
    return 0.5 * jnp.tanh(0.5 * x) + 0.5


def _const_spec(shape):
    n = len(shape)
    return pl.BlockSpec(shape, lambda *_: (0,) * n, pipeline_mode=pl.Buffered(1))


def _ada_kernel(c_ref, w_ref, b_ref, o_ref):
    c = c_ref[...]
    s = (c * _sigmoid(c)).astype(BF16)
    o_ref[...] = _dot(s, w_ref[...].astype(BF16)) + b_ref[...]


def _ada_call(cc, w_ada, b_ada):
    rows = cc.shape[0]
    n_out = w_ada.shape[1]
    bn = 1024
    return pl.pallas_call(
        _ada_kernel,
        grid=(n_out // bn,),
        in_specs=[pl.BlockSpec((rows, D_MODEL), lambda j: (0, 0)),
                  pl.BlockSpec((D_MODEL, bn), lambda j: (0, j)),
                  pl.BlockSpec((1, bn), lambda j: (0, j))],
        out_specs=pl.BlockSpec((rows, bn), lambda j: (0, j)),
        out_shape=jax.ShapeDtypeStruct((rows, n_out), F32),
        compiler_params=pltpu.CompilerParams(dimension_semantics=("arbitrary",),
                                             vmem_limit_bytes=VMEM_LIMIT),
        name="ada_mod",
    )(cc, w_ada, b_ada)


def _chunk_cumsum(x, reverse):
    n = x.shape[0]
    r = lax.broadcasted_iota(jnp.int32, x.shape, 0) & (CHUNK - 1)
    s = 1
    while s < CHUNK:
        if reverse:
            x = x + jnp.where(r < CHUNK - s, pltpu.roll(x, n - s, 0), 0.0)
        else:
            x = x + jnp.where(r >= s, pltpu.roll(x, s, 0), 0.0)
        s *= 2
    return x


def _rope(t, cos, sin_signed, lane):
    partner = jnp.where((lane & 31) < 16, pltpu.roll(t, LANES - 16, 1), pltpu.roll(t, 16, 1))
    return t * cos + partner * sin_signed


def _inproj_kernel(x_ref, mod_ref, nw_ref, w_ref, lbl_ref, qw_ref, kw_ref, cos_ref, sin_ref,
                   *out_refs, latent):
    if latent:
        (qdf_ref, kdf_ref, kd2f_ref, qdb_ref, kdb_ref, kd2b_ref, dec_ref, vt_ref, v_ref,
         g_ref, q_ref, katt_ref, vatt_ref, gates_ref) = out_refs
    else:
        kd2f_ref, kd2b_ref, dec_ref, vt_ref, katt_ref, vatt_ref = out_refs

    tm = x_ref.shape[1]
    nchunk = tm // CHUNK
    x = x_ref[0]
    sh = mod_ref[0, :, 0:D_MODEL]
    sc = mod_ref[0, :, D_MODEL:2 * D_MODEL]
    ms = jnp.mean(x * x, axis=-1, keepdims=True)
    h = (x * lax.rsqrt(ms + EPS) * nw_ref[...]) * (1.0 + sc) + sh
    hb = h.astype(BF16)

    def mm(a, b):
        return _dot(hb, w_ref[:, a:b])

    l0f, l1f = lbl_ref[0:1, :], lbl_ref[1:2, :]
    l0b, l1b = lbl_ref[2:3, :], lbl_ref[3:4, :]
    lb_f = 1.0 / (1.0 + jnp.exp(l1f - l0f))
    lb_b = 1.0 / (1.0 + jnp.exp(l1b - l0b))

    gw = HG_WIDTH // 2
    sw = D_MODEL // 2
    lane = lax.broadcasted_iota(jnp.int32, (tm, LANES), 1)
    left = lane < HEAD_DIM
    bi = lax.broadcasted_iota(jnp.int32, (2 * LANES, LANES), 0) // HEAD_DIM
    bj = lax.broadcasted_iota(jnp.int32, (2 * LANES, LANES), 1) // HEAD_DIM
    ones_blk2 = jnp.where((bi % 2) == bj, 1.0, 0.0).astype(BF16)
    keep = {}

    def qhg_finish(raw):
        keep["q_hg"] = raw * _sigmoid(raw) * (HG_DIM ** -0.5)

    def gate_finish(raw, direction, part):
        reverse = direction == 1
        lo = part * gw
        lb = (lb_f, lb_b)[direction][:, lo:lo + gw]
        f = lb + (1.0 - lb) * _sigmoid(raw)
        k = 1.0 - f
        cum = _chunk_cumsum(jnp.log(f), reverse)
        cum3 = cum.reshape(nchunk, CHUNK, gw)
        tot3 = cum3[:, 0:1, :] if reverse else cum3[:, CHUNK - 1:CHUNK, :]
        k3 = k.reshape(nchunk, CHUNK, gw)
        kd2 = (k3 * jnp.exp(tot3 - cum3)).reshape(tm, gw).astype(BF16)
        dec_ref[0, :, direction * HG_WIDTH + lo:direction * HG_WIDTH + lo + gw] = (
            jnp.exp(tot3).reshape(nchunk, gw))
        (kd2f_ref, kd2b_ref)[direction][0, :, lo:lo + gw] = kd2
        if latent:
            qd = (keep["q_hg"][:, lo:lo + gw] * jnp.exp(cum)).astype(BF16)
            kd = (k * jnp.exp(-cum)).astype(BF16)
            (qdf_ref, qdb_ref)[direction][0, :, lo:lo + gw] = qd
            (kdf_ref, kdb_ref)[direction][0, :, lo:lo + gw] = kd

    def mgate_finish(raw, j):
        gates_ref[0, :, j * sw:(j + 1) * sw] = _sigmoid(raw).astype(BF16)

    def v_finish(raw):
        for s in range(tm // LANES):
            vt_ref[0, s] = raw[s * LANES:(s + 1) * LANES, :].T.astype(BF16)
        if latent:
            v_ref[0] = raw.astype(BF16)

    def g_finish(raw):
        g_ref[0] = (raw * _sigmoid(raw)).astype(BF16)

    def split_sq(t):
        sq = t * t
        hi = sq.astype(BF16)
        lo = (sq - hi.astype(F32)).astype(BF16)
        return jnp.concatenate([hi, lo], axis=1)

    def kv_finish(raw):
        keep["k_raw"] = raw[:, 0:LANES]
        keep["k_split"] = split_sq(keep["k_raw"])
        vatt_ref[0] = raw[:, LANES:2 * LANES].astype(BF16)

    def knorm_finish(ss):
        kn = keep["k_raw"] * lax.rsqrt(ss * (1.0 / HEAD_DIM) + EPS) * kw_ref[...]
        if latent:
            kn = _rope(kn, cos_ref[...], sin_ref[...], lane)
        for s in range(tm // LANES):
            katt_ref[0, s] = kn[s * LANES:(s + 1) * LANES, :].T.astype(BF16)

    def q_finish(raw):
        keep["q_raw"] = raw
        keep["q_split"] = [split_sq(raw[:, s * LANES:(s + 1) * LANES]) for s in range(ATT_WIDTH // LANES)]

    def qnorm_finish(ss_list):
        for s, ss in enumerate(ss_list):
            qraw = keep["q_raw"][:, s * LANES:(s + 1) * LANES]
            qn = qraw * lax.rsqrt(ss * (1.0 / HEAD_DIM) + EPS) * qw_ref[...]
            qn = _rope(qn, cos_ref[...], sin_ref[...], lane) * (HEAD_DIM ** -0.5 * LOG2E)
            swapped = pltpu.roll(qn, HEAD_DIM, 1)
            if s < ATT_WIDTH // LANES // 2:
                even, odd = jnp.where(left, qn, 0.0), jnp.where(left, swapped, 0.0)
            else:
                even, odd = jnp.where(left, 0.0, swapped), jnp.where(left, 0.0, qn)
            q_ref[0, :, (2 * s) * LANES:(2 * s + 1) * LANES] = even.astype(BF16)
            q_ref[0, :, (2 * s + 1) * LANES:(2 * s + 2) * LANES] = odd.astype(BF16)

    def gate_stage(direction, part):
        col = (C_FF, C_FB)[direction] + part * gw
        return (lambda: mm(col, col + gw),
                functools.partial(gate_finish, direction=direction, part=part))

    def mgate_stage(j):
        c0 = C_GATES + j * sw
        return (lambda: mm(c0, c0 + sw), functools.partial(mgate_finish, j=j))

    v_stage = (lambda: mm(C_INP, C_INP + HG_WIDTH), v_finish)
    kv_stage = (lambda: mm(C_K, C_K + 2 * KV_WIDTH), kv_finish)
    knorm_stage = (lambda: _dot(keep["k_split"], ones_blk2), knorm_finish)
    if latent:
        stages = [(lambda: mm(C_QHG, C_QHG + HG_WIDTH), qhg_finish)]
        for j in range(4):
            stages += [gate_stage(j // 2, j % 2), mgate_stage(j)]
        stages += [(lambda: mm(C_Q, C_Q + ATT_WIDTH), q_finish),
                   v_stage,
                   (lambda: [_dot(t, ones_blk2) for t in keep["q_split"]], qnorm_finish),
                   kv_stage,
                   (lambda: mm(C_GHG, C_GHG + HG_WIDTH), g_finish),
                   knorm_stage]
    else:
        stages = [gate_stage(0, 0), gate_stage(0, 1), gate_stage(1, 0), kv_stage, gate_stage(1, 1),
                  knorm_stage, v_stage]

    raw = stages[0][0]()
    for i, (_, finish) in enumerate(stages):
        nxt = stages[i + 1][0]() if i + 1 < len(stages) else None
        finish(raw)
        raw = nxt


def _inproj_call(x, mod3, nw, w_in, lbl, qw, kw, cos_t, sin_t, latent):
    b_, s_, _ = x.shape
    tm = min(INPROJ_TILE, s_)
    grid = (b_, s_ // tm)
    tok = lambda w: pl.BlockSpec((1, tm, w), lambda b, i: (b, i, 0))
    mod_map = (lambda b, i: (b, 0, 0)) if latent else (lambda b, i: (0, 0, 0))
    ncols = IN_COLS if latent else CTX_COLS
    in_specs = [tok(D_MODEL),
                pl.BlockSpec((1, 1, mod3.shape[2]), mod_map),
                _const_spec((1, D_MODEL)),
                pl.BlockSpec((D_MODEL, ncols), lambda b, i: (0, 0), pipeline_mode=pl.Buffered(1)),
                _const_spec((4, HG_WIDTH)),
                _const_spec((1, LANES)),
                _const_spec((1, LANES)),
                pl.BlockSpec((tm, LANES), lambda b, i: (i, 0)),
                pl.BlockSpec((tm, LANES), lambda b, i: (i, 0))]
    bf = lambda w: jax.ShapeDtypeStruct((b_, s_, w), BF16)
    dec_shape = jax.ShapeDtypeStruct((b_, s_ // CHUNK, 2 * HG_WIDTH), F32)
    dec_spec = pl.BlockSpec((1, tm // CHUNK, 2 * HG_WIDTH), lambda b, i: (b, i, 0))
    vt_shape = jax.ShapeDtypeStruct((b_, s_ // LANES, HG_WIDTH, LANES), BF16)
    vt_spec = pl.BlockSpec((1, tm // LANES, HG_WIDTH, LANES), lambda b, i: (b, i, 0, 0))
    katt_shape = jax.ShapeDtypeStruct((b_, s_ // LANES, KV_WIDTH, LANES), BF16)
    katt_spec = pl.BlockSpec((1, tm // LANES, KV_WIDTH, LANES), lambda b, i: (b, i, 0, 0))
    if latent:
        out_shape = [bf(HG_WIDTH)] * 6 + [dec_shape, vt_shape, bf(HG_WIDTH), bf(HG_WIDTH),
                                         bf(ATT_HEADS * LANES), katt_shape, bf(KV_WIDTH), bf(2 * D_MODEL)]
        out_specs = [tok(HG_WIDTH)] * 6 + [dec_spec, vt_spec, tok(HG_WIDTH), tok(HG_WIDTH),
                                          tok(ATT_HEADS * LANES), katt_spec, tok(KV_WIDTH), tok(2 * D_MODEL)]
    else:
        out_shape = [bf(HG_WIDTH)] * 2 + [dec_shape, vt_shape, katt_shape, bf(KV_WIDTH)]
        out_specs = [tok(HG_WIDTH)] * 2 + [dec_spec, vt_spec, katt_spec, tok(KV_WIDTH)]
    return pl.pallas_call(
        functools.partial(_inproj_kernel, latent=latent),
        grid=grid, in_specs=in_specs, out_specs=out_specs, out_shape=out_shape,
        compiler_params=pltpu.CompilerParams(dimension_semantics=("parallel", "parallel"),
                                             vmem_limit_bytes=VMEM_LIMIT),
        name="inproj_latent" if latent else "inproj_ctx",
    )(x, mod3, nw, w_in, lbl, qw, kw, cos_t, sin_t)


def _hgrn_kernel(qdf_ref, kdf_ref, kd2f_ref, qdb_ref, kdb_ref, kd2b_ref, v_ref, vt_ref,
                 decf_ref, decb_ref, ckd2f_ref, ckd2b_ref, cvt_ref, cdecf_ref, cdecb_ref,
                 g_ref, nw_ref, y_ref, o_acc):
    seq = v_ref.shape[1]
    ctx_len = ckd2f_ref.shape[1]
    heads = v_ref.shape[2] // HG_DIM
    tile = HG_TILE
    cpt = tile // CHUNK
    cps = LANES // CHUNK
    spt = tile // LANES
    n_tiles = seq // tile
    half = n_tiles // 2

    row_chunk = lax.broadcasted_iota(jnp.int32, (LANES, HG_DIM), 0) // CHUNK
    ti = lax.broadcasted_iota(jnp.int32, (tile, tile), 0)
    tj = lax.broadcasted_iota(jnp.int32, (tile, tile), 1)
    same = (ti // CHUNK) == (tj // CHUNK)
    mask_f = same & (tj <= ti)
    mask_b = same & (tj >= ti)

    def chunk_update(vt_slab, kd2_slab, n_in_slab):
        kd2_m = jnp.where(row_chunk == n_in_slab, kd2_slab, jnp.zeros_like(kd2_slab))
        return _dot(vt_slab, kd2_m)

    def ctx_state(ckd2_ref, cdec_ref, cols, reverse):
        st = jnp.zeros((HG_DIM, HG_DIM), F32)
        order = range(ctx_len // CHUNK)
        for n in (reversed(order) if reverse else order):
            slab = n // cps
            st = cdec_ref[0, n:n + 1, cols] * st + chunk_update(
                cvt_ref[0, slab, cols, :], ckd2_ref[0, slab * LANES:(slab + 1) * LANES, cols], n % cps)
        return st

    def emit(o_t, cols, rows, finalize):
        if finalize:
            o_t = o_t + o_acc[rows, cols]
            ms = jnp.mean(o_t * o_t, axis=-1, keepdims=True)
            y = o_t * lax.rsqrt(ms + EPS) * nw_ref[...] * g_ref[0, rows, cols].astype(F32)
            y_ref[0, rows, cols] = y.astype(BF16)
        else:
            o_acc[rows, cols] = o_t

    def body(it, states, finalize):
        chains = []
        for h in range(heads):
            cols = slice(h * HG_DIM, (h + 1) * HG_DIM)
            chains.append((qdf_ref, kdf_ref, kd2f_ref, decf_ref, cols, it, False))
            chains.append((qdb_ref, kdb_ref, kd2b_ref, decb_ref, cols, n_tiles - 1 - it, True))
        n_ch = len(chains)
        rows = [pl.ds(pl.multiple_of(c[5] * tile, tile), tile) for c in chains]
        orders = [list(reversed(range(cpt))) if c[6] else list(range(cpt)) for c in chains]
        qd, sc, ups = [], [], []
        for (qd_ref, kd_ref, kd2_ref, _, cols, t, reverse), r in zip(chains, rows):
            qd.append(qd_ref[0, r, cols])
            sc.append(_dot_nt(qd[-1], kd_ref[0, r, cols]))
            kd2 = kd2_ref[0, r, cols]
            ups.append([chunk_update(vt_ref[0, t * spt + n // cps, cols, :],
                                     kd2[(n // cps) * LANES:(n // cps + 1) * LANES, :], n % cps)
                        for n in range(cpt)])
        o = []
        for c, r, s in zip(chains, rows, sc):
            p = jnp.where(mask_b if c[6] else mask_f, s, 0.0).astype(BF16)
            o.append(_dot(p, v_ref[0, r, c[4]]))
        starts, new_states = [], []
        for i, c in enumerate(chains):
            dec = c[3][0, pl.ds(pl.multiple_of(c[5] * cpt, cpt), cpt), c[4]]
            st = states[i]
            start = [None] * cpt
            for n in orders[i]:
                start[n] = st.T.astype(BF16)
                st = dec[n:n + 1, :] * st + ups[i][n]
            starts.append(start)
            new_states.append(st)
        for i, c in enumerate(chains):
            outs = [o[i][n * CHUNK:(n + 1) * CHUNK, :]
                    + _dot(qd[i][n * CHUNK:(n + 1) * CHUNK, :], starts[i][n]) for n in range(cpt)]
            emit(jnp.concatenate(outs, axis=0), c[4], rows[i], finalize)
        return tuple(new_states)

    states = []
    for h in range(heads):
        cols = slice(h * HG_DIM, (h + 1) * HG_DIM)
        states += [ctx_state(ckd2f_ref, cdecf_ref, cols, False), ctx_state(ckd2b_ref, cdecb_ref, cols, True)]
    states = lax.fori_loop(0, half, functools.partial(body, finalize=False), tuple(states))
    lax.fori_loop(half, n_tiles, functools.partial(body, finalize=True), states)


def _hgrn_call(qdf, kdf, kd2f, qdb, kdb, kd2b, v, vt, dec, ckd2f, ckd2b, cvt, cdec, g, nw):
    b_, s_, _ = v.shape
    l_ = ckd2f.shape[1]
    w = HG_HEADS_PER_STEP * HG_DIM
    groups = HG_HEADS // HG_HEADS_PER_STEP
    seq = lambda n: pl.BlockSpec((1, n, w), lambda b, h: (b, 0, h))
    vt_spec = lambda n: pl.BlockSpec((1, n // LANES, w, LANES), lambda b, h: (b, 0, h, 0))
    dec_f = lambda n: pl.BlockSpec((1, n // CHUNK, w), lambda b, h: (b, 0, h))
    dec_b = lambda n: pl.BlockSpec((1, n // CHUNK, w), lambda b, h: (b, 0, groups + h))
    return pl.pallas_call(
        _hgrn_kernel,
        grid=(b_, groups),
        in_specs=[seq(s_)] * 7 + [vt_spec(s_), dec_f(s_), dec_b(s_),
                                  seq(l_), seq(l_), vt_spec(l_), dec_f(l_), dec_b(l_),
                                  seq(s_), _const_spec((1, HG_DIM))],
        out_specs=seq(s_),
        out_shape=jax.ShapeDtypeStruct((b_, s_, HG_WIDTH), BF16),
        scratch_shapes=[pltpu.VMEM((s_, w), F32)],
        compiler_params=pltpu.CompilerParams(dimension_semantics=("parallel", "parallel"),
                                             vmem_limit_bytes=VMEM_LIMIT),
        name="hgrn_scan",
    )(qdf, kdf, kd2f, qdb, kdb, kd2b, v, vt, dec, dec, ckd2f, ckd2b, cvt, cdec, cdec, g, nw)


def _attn_kernel(sink_ref, q_ref, kt_ref, kc_ref, v_ref, vc_ref, y_ref):
    nb = kt_ref.shape[1]
    blk = ATT_BLOCK
    hpg = ATT_GROUP_HEADS
    n_groups = ATT_HEADS // hpg
    qry_r = lax.broadcasted_iota(jnp.int32, (blk, blk), 0)
    key_c = lax.broadcasted_iota(jnp.int32, (blk, blk), 1)
    left = lax.broadcasted_iota(jnp.int32, (blk, LANES), 1) < HEAD_DIM
    ctx_keys = [kc_ref[0, c] for c in range(kc_ref.shape[1])]
    keep = {}

    def band(u):
        qb = pl.program_id(1) * ATT_STEP_BLOCKS + u
        ids = (jnp.maximum(qb - 1, 0), qb, jnp.minimum(qb + 1, nb - 1))
        keys_t = jnp.concatenate([kt_ref[0, j] for j in ids] + ctx_keys, axis=1)
        values = jnp.concatenate([v_ref[0, pl.ds(pl.multiple_of(j * blk, blk), blk), :] for j in ids]
                                 + [vc_ref[0]], axis=0)
        return qb, keys_t, values

    def score_issue(u, g):
        if g == 0:
            keep["band", u] = band(u)
        q = jnp.concatenate([q_ref[0, u * blk:(u + 1) * blk, h * LANES:(h + 1) * LANES]
                             for h in range(g * hpg, (g + 1) * hpg)], axis=0)
        return _dot(q, keep["band", u][1])

    def score_finish(raw, u, g):
        qb = keep["band", u][0]
        mask_prev = (key_c >= qry_r) & (qb > 0)
        mask_next = (key_c <= qry_r) & (qb < nb - 1)
        probs, inv = [], []
        for n in range(hpg):
            s = raw[n * blk:(n + 1) * blk, :]
            sink = sink_ref[g * hpg + n] * LOG2E
            s = jnp.concatenate([jnp.where(mask_prev, s[:, 0:blk], NEG), s[:, blk:2 * blk],
                                 jnp.where(mask_next, s[:, 2 * blk:3 * blk], NEG), s[:, 3 * blk:]], axis=1)
            mx = jnp.maximum(jnp.max(s, axis=1, keepdims=True), sink)
            p = jnp.exp2(s - mx)
            den = jnp.sum(p, axis=1, keepdims=True) + jnp.exp2(sink - mx)
            probs.append(p.astype(BF16))
            inv.append(1.0 / den)
        keep["p", u, g] = jnp.concatenate(probs, axis=0)
        keep["inv", u, g] = inv

    def value_issue(u, g):
        return _dot(keep["p", u, g], keep["band", u][2])

    def value_finish(raw, u, g):
        for n in range(0, hpg, 2):
            h = g * hpg + n
            even = raw[n * blk:(n + 1) * blk, :] * keep["inv", u, g][n]
            odd = raw[(n + 1) * blk:(n + 2) * blk, :] * keep["inv", u, g][n + 1]
            if h < ATT_HEADS // ATT_KV_HEADS:
                y = jnp.where(left, even, pltpu.roll(odd, HEAD_DIM, 1))
            else:
                y = jnp.where(left, pltpu.roll(even, HEAD_DIM, 1), odd)
            y_ref[0, u * blk:(u + 1) * blk, (h // 2) * LANES:(h // 2 + 1) * LANES] = y.astype(BF16)

    part = functools.partial
    work = [(u, g) for u in range(ATT_STEP_BLOCKS) for g in range(n_groups)]
    stages = [(part(score_issue, *work[0]), part(score_finish, u=work[0][0], g=work[0][1]))]
    for n, (u, g) in enumerate(work):
        if n + 1 < len(work):
            un, gn = work[n + 1]
            stages.append((part(score_issue, un, gn), part(score_finish, u=un, g=gn)))
        stages.append((part(value_issue, u, g), part(value_finish, u=u, g=g)))
    raw = stages[0][0]()
    for n, (_, finish) in enumerate(stages):
        nxt = stages[n + 1][0]() if n + 1 < len(stages) else None
        finish(raw)
        raw = nxt


def _attn_call(sinks, q, katt, vatt, ckatt, cvatt):
    b_, s_, qw = q.shape
    l_ = cvatt.shape[1]
    rows = ATT_STEP_BLOCKS * ATT_BLOCK
    batch4 = lambda n: pl.BlockSpec((1, n, KV_WIDTH, LANES), lambda b, i: (b, 0, 0, 0))
    batch3 = lambda n: pl.BlockSpec((1, n, KV_WIDTH), lambda b, i: (b, 0, 0))
    return pl.pallas_call(
        _attn_kernel,
        grid=(b_, s_ // rows),
        in_specs=[pl.BlockSpec(memory_space=pltpu.SMEM),
                  pl.BlockSpec((1, rows, qw), lambda b, i: (b, i, 0)),
                  batch4(s_ // LANES), batch4(l_ // LANES), batch3(s_), batch3(l_)],
        out_specs=pl.BlockSpec((1, rows, ATT_WIDTH), lambda b, i: (b, i, 0)),
        out_shape=jax.ShapeDtypeStruct((b_, s_, ATT_WIDTH), BF16),
        compiler_params=pltpu.CompilerParams(dimension_semantics=("parallel", "arbitrary"),
                                             vmem_limit_bytes=VMEM_LIMIT),
        name="window_attn",
    )(sinks, q, katt, ckatt, vatt, cvatt)


def _merge_ffn_kernel(x_ref, yh_ref, ya_ref, gates_ref, mod_ref, nw_ref, wbh_ref, wba_ref, wo_ref,
                      wg_ref, wu_ref, wd_ref, o_ref):
    tm = x_ref.shape[1]
    n_sub = FFN_SUBTILES
    sub = tm // n_sub
    bounds = FFN_CHUNK_BOUNDS
    n_chunks = len(bounds) - 1
    g1 = mod_ref[0, :, 2 * D_MODEL:3 * D_MODEL]
    sh2 = mod_ref[0, :, 3 * D_MODEL:4 * D_MODEL]
    sc2 = mod_ref[0, :, 4 * D_MODEL:5 * D_MODEL]
    g2 = mod_ref[0, :, 5 * D_MODEL:6 * D_MODEL]
    keep = {}

    def branch_issue(r):
        rows = slice(r * sub, (r + 1) * sub)
        return _dot(yh_ref[0, rows, :], wbh_ref[...]), _dot(ya_ref[0, rows, :], wba_ref[...])

    def branch_finish(raw, r):
        rows = slice(r * sub, (r + 1) * sub)
        a, b = raw
        mixed = (gates_ref[0, rows, 0:D_MODEL].astype(F32) * a
                 + gates_ref[0, rows, D_MODEL:2 * D_MODEL].astype(F32) * b)
        keep["mixed", r] = mixed.astype(BF16)

    def out_issue(r):
        return _dot(keep["mixed", r], wo_ref[...])

    def out_finish(raw, r):
        x1 = x_ref[0, r * sub:(r + 1) * sub, :] + g1 * raw
        ms = jnp.mean(x1 * x1, axis=-1, keepdims=True)
        keep["x1", r] = x1
        keep["h2", r] = ((x1 * lax.rsqrt(ms + EPS) * nw_ref[...]) * (1.0 + sc2) + sh2).astype(BF16)

    def hidden_issue(r, c):
        h2 = keep["h2", r]
        cols = slice(bounds[c], bounds[c + 1])
        return _dot(h2, wg_ref[:, cols]), _dot(h2, wu_ref[:, cols])

    def hidden_finish(raw, r, c):
        gate, up = raw
        keep["act", r, c] = (gate * _sigmoid(gate) * up).astype(BF16)

    def down_issue(r, c):
        return _dot(keep["act", r, c], wd_ref[bounds[c]:bounds[c + 1], :])

    def down_finish(raw, r, c):
        dn = raw if c == 0 else keep["dn", r] + raw
        if c + 1 < n_chunks:
            keep["dn", r] = dn
        else:
            o_ref[0, r * sub:(r + 1) * sub, :] = keep["x1", r] + g2 * dn

    part = functools.partial
    stages = []
    for issue, finish in ((branch_issue, branch_finish), (out_issue, out_finish)):
        stages += [(part(issue, r), part(finish, r=r)) for r in range(n_sub)]
    for c in range(n_chunks):
        stages += [(part(hidden_issue, r, c), part(hidden_finish, r=r, c=c)) for r in range(n_sub)]
        stages += [(part(down_issue, r, c), part(down_finish, r=r, c=c)) for r in range(n_sub)]
    raw = stages[0][0]()
    for n, (_, finish) in enumerate(stages):
        nxt = stages[n + 1][0]() if n + 1 < len(stages) else None
        finish(raw)
        raw = nxt


def _merge_ffn_call(x, yh, ya, gates, mod3, nw, wbh, wba, wo, wg, wu, wd):
    b_, s_, _ = x.shape
    tm = TOK_TILE
    tok = lambda w: pl.BlockSpec((1, tm, w), lambda b, i: (b, i, 0))
    return pl.pallas_call(
        _merge_ffn_kernel,
        grid=(b_, s_ // tm),
        in_specs=[tok(D_MODEL), tok(HG_WIDTH), tok(ATT_WIDTH), tok(2 * D_MODEL),
                  pl.BlockSpec((1, 1, mod3.shape[2]), lambda b, i: (b, 0, 0)),
                  _const_spec((1, D_MODEL)),
                  _const_spec(wbh.shape), _const_spec(wba.shape), _const_spec(wo.shape),
                  _const_spec(wg.shape), _const_spec(wu.shape), _const_spec(wd.shape)],
        out_specs=tok(D_MODEL),
        out_shape=jax.ShapeDtypeStruct(x.shape, F32),
        compiler_params=pltpu.CompilerParams(dimension_semantics=("parallel", "parallel"),
                                             vmem_limit_bytes=VMEM_LIMIT),
        name="merge_ffn",
    )(x, yh, ya, gates, mod3, nw, wbh, wba, wo, wg, wu, wd)


def _rope_tables(n_tok):
    t = np.arange(n_tok)
    rows = (t // GRID_W).astype(np.float64)
    cols = (t % GRID_W).astype(np.float64)
    half = HEAD_DIM // 2
    inv_freq = ROPE_THETA ** (-np.arange(0, half, 2, dtype=np.float64) / half)
    d = np.arange(LANES) % HEAD_DIM
    pos = np.where((d < half)[None, :], rows[:, None], cols[:, None])
    ang = pos * inv_freq[(d % half) % (half // 2)][None, :]
    sign = np.where((d % half) < half // 2, -1.0, 1.0)[None, :]
    return jnp.asarray(np.cos(ang), F32), jnp.asarray(np.sin(ang) * sign, F32)


def kernel(x, c, ctx, c_ctx, w_ada, b_ada, norm_mix_w, norm_ffn_w, w_in, hgrn_lb_logits, hgrn_norm_w,
           q_norm_w, k_norm_w, attn_sinks, w_branch_hgrn, w_branch_attn, w_out, w_ffn_gate, w_ffn_up,
           w_ffn_down):
    b_, s_, _ = x.shape
    layer = 0
    pad = 16 - (b_ + 1)
    cc = jnp.concatenate([c, c_ctx[None, :], jnp.zeros((pad, D_MODEL), F32)], axis=0)
    mod = _ada_call(cc, w_ada[layer], b_ada[layer][None, :])
    mod_lat = mod[:b_].reshape(b_, 1, 6 * D_MODEL)
    mod_ctx = mod[b_:b_ + 1].reshape(1, 1, 6 * D_MODEL)

    w_in_b = w_in[layer].astype(BF16)
    lbl = hgrn_lb_logits[:, 0:2, :].reshape(4, HG_WIDTH)
    qw = jnp.tile(q_norm_w[layer], 2)[None, :]
    kw = jnp.tile(k_norm_w[layer], 2)[None, :]
    nw_mix = norm_mix_w[layer][None, :]
    cos_t, sin_t = _rope_tables(s_)

    (qdf, kdf, kd2f, qdb, kdb, kd2b, dec, vt, v, g, q, katt, vatt, gates) = _inproj_call(
        x, mod_lat, nw_mix, w_in_b, lbl, qw, kw, cos_t, sin_t, latent=True)
    ckd2f, ckd2b, cdec, cvt, ckatt, cvatt = _inproj_call(
        ctx, mod_ctx, nw_mix, w_in_b, lbl, qw, kw, cos_t, sin_t, latent=False)

    y_hg = _hgrn_call(qdf, kdf, kd2f, qdb, kdb, kd2b, v, vt, dec, ckd2f, ckd2b, cvt, cdec, g,
                      hgrn_norm_w[layer][None, :])
    y_at = _attn_call(attn_sinks[layer], q, katt, vatt, ckatt, cvatt)

    return _merge_ffn_call(x, y_hg, y_at, gates, mod_lat, norm_ffn_w[layer][None, :],
                           w_branch_hgrn[layer].astype(BF16), w_branch_attn[layer].astype(BF16),
                           w_out[layer].astype(BF16), w_ffn_gate[layer].astype(BF16),
                           w_ffn_up[layer].astype(BF16), w_ffn_down[layer].astype(BF16))
```

```python
import functools

import jax
import jax.numpy as jnp
import numpy as np
from jax import lax
from jax.experimental import pallas as pl
from jax.experimental.pallas import tpu as pltpu

F32 = jnp.float32
BF16 = jnp.bfloat16

D_MODEL = 1024
GRID_W = 64
EPS = 1e-6
HG_HEADS = 4
HG_DIM = 128
HG_WIDTH = HG_HEADS * HG_DIM
CHUNK = 32
ATT_HEADS = 8
ATT_KV_HEADS = 2
HEAD_DIM = 64
ATT_WIDTH = ATT_HEADS * HEAD_DIM
KV_WIDTH = ATT_KV_HEADS * HEAD_DIM
WINDOW = 128
ROPE_THETA = 10000.0
D_FF = 2816
CTX_COLS = 3 * HG_WIDTH + 2 * KV_WIDTH
IN_COLS = CTX_COLS + 2 * HG_WIDTH + ATT_WIDTH + 2 * D_MODEL

C_FF, C_FB, C_INP, C_K, C_V = 0, HG_WIDTH, 2 * HG_WIDTH, 3 * HG_WIDTH, 3 * HG_WIDTH + KV_WIDTH
C_QHG = CTX_COLS
C_GHG = C_QHG + HG_WIDTH
C_Q = C_GHG + HG_WIDTH
C_GATES = C_Q + ATT_WIDTH

LANES = 128
TOK_TILE = 512
FFN_SUBTILES = 2
MXU_DEPTH = 256
FFN_CHUNK_BOUNDS = (0, 6 * MXU_DEPTH, D_FF)
INPROJ_TILE = 256
HG_TILE = 256
HG_HEADS_PER_STEP = 2
ATT_BLOCK = 128
ATT_GROUP_HEADS = 2
ATT_STEP_BLOCKS = 4
VMEM_LIMIT = 56 * 1024 * 1024
NEG = -1e30
LOG2E = 1.4426950408889634


def _dot(a, b):
    return jnp.dot(a, b, preferred_element_type=F32)


def _dot_nt(a, b):
    return lax.dot_general(a, b, (((1,), (1,)), ((), ())), preferred_element_type=F32)


def _sigmoid(x):
    return 0.5 * jnp.tanh(0.5 * x) + 0.5


def _const_spec(shape):
    n = len(shape)
    return pl.BlockSpec(shape, lambda *_: (0,) * n, pipeline_mode=pl.Buffered(1))


def _ada_kernel(c_ref, w_ref, b_ref, o_ref):
    c = c_ref[...]
    s = (c * _sigmoid(c)).astype(BF16)
    o_ref[...] = _dot(s, w_ref[...].astype(BF16)) + b_ref[...]


def _ada_call(cc, w_ada, b_ada):
    rows = cc.shape[0]
    n_out = w_ada.shape[1]
    bn = 1024
    return pl.pallas_call(
        _ada_kernel,
        grid=(n_out // bn,),
        in_specs=[pl.BlockSpec((rows, D_MODEL), lambda j: (0, 0)),
                  pl.BlockSpec((D_MODEL, bn), lambda j: (0, j)),
                  pl.BlockSpec((1, bn), lambda j: (0, j))],
        out_specs=pl.BlockSpec((rows, bn), lambda j: (0, j)),
        out_shape=jax.ShapeDtypeStruct((rows, n_out), F32),
        compiler_params=pltpu.CompilerParams(dimension_semantics=("arbitrary",),
                                             vmem_limit_bytes=VMEM_LIMIT),
        name="ada_mod",
    )(cc, w_ada, b_ada)


def _chunk_cumsum(x, reverse):
    n, c = x.shape
    sub = 8
    r = lax.broadcasted_iota(jnp.int32, x.shape, 0) & (CHUNK - 1)
    s = 1
    while s < sub:
        if reverse:
            x = x + jnp.where(r < CHUNK - s, pltpu.roll(x, n - s, 0), 0.0)
        else:
            x = x + jnp.where(r >= s, pltpu.roll(x, s, 0), 0.0)
        s *= 2
    per = CHUNK // sub
    x4 = x.reshape(n // CHUNK, per, sub, c)
    s = 1
    while s < per:
        if reverse:
            x4 = jnp.concatenate([x4[:, :per - s] + x4[:, s:], x4[:, per - s:]], axis=1)
        else:
            x4 = jnp.concatenate([x4[:, :s], x4[:, s:] + x4[:, :per - s]], axis=1)
        s *= 2
    return x4.reshape(n, c)


def _rope(t, cos, sin_signed, lane):
    partner = jnp.where((lane & 31) < 16, pltpu.roll(t, LANES - 16, 1), pltpu.roll(t, 16, 1))
    return t * cos + partner * sin_signed


def _inproj_kernel(x_ref, mod_ref, nw_ref, w_ref, lbl_ref, qw_ref, kw_ref, cos_ref, sin_ref,
                   *out_refs, latent):
    if latent:
        (qdf_ref, kdf_ref, kd2f_ref, qdb_ref, kdb_ref, kd2b_ref, dec_ref, vt_ref, v_ref,
         g_ref, q_ref, katt_ref, vatt_ref, gates_ref) = out_refs
    else:
        kd2f_ref, kd2b_ref, dec_ref, vt_ref, katt_ref, vatt_ref = out_refs

    tm = x_ref.shape[1]
    nchunk = tm // CHUNK
    x = x_ref[0]
    sh = mod_ref[0, :, 0:D_MODEL]
    sc = mod_ref[0, :, D_MODEL:2 * D_MODEL]
    ms = jnp.mean(x * x, axis=-1, keepdims=True)
    h = x * lax.rsqrt(ms + EPS) * (nw_ref[...] * (1.0 + sc)) + sh
    hb = h.astype(BF16)

    def mm(a, b):
        return _dot(hb, w_ref[:, a:b])

    l0f, l1f = lbl_ref[0:1, :], lbl_ref[1:2, :]
    l0b, l1b = lbl_ref[2:3, :], lbl_ref[3:4, :]
    lb_f = 1.0 / (1.0 + jnp.exp(l1f - l0f))
    lb_b = 1.0 / (1.0 + jnp.exp(l1b - l0b))

    gw = HG_WIDTH // 2
    sw = D_MODEL // 2
    lane = lax.broadcasted_iota(jnp.int32, (tm, LANES), 1)
    left = lane < HEAD_DIM
    bi = lax.broadcasted_iota(jnp.int32, (2 * LANES, LANES), 0) // HEAD_DIM
    bj = lax.broadcasted_iota(jnp.int32, (2 * LANES, LANES), 1) // HEAD_DIM
    ones_blk2 = jnp.where((bi % 2) == bj, 1.0, 0.0).astype(BF16)
    keep = {}

    def qhg_finish(raw):
        keep["q_hg"] = raw * (jnp.tanh(raw) + 1.0) * (HG_DIM ** -0.5)

    def gate_finish(raw, direction, part):
        reverse = direction == 1
        lo = part * gw
        lb = (lb_f, lb_b)[direction][:, lo:lo + gw]
        c_half = 0.5 - 0.5 * lb
        ct = c_half * jnp.tanh(raw)
        f = (0.5 + 0.5 * lb) + ct
        k = c_half - ct
        cum = _chunk_cumsum(jnp.log(f), reverse)
        cum3 = cum.reshape(nchunk, CHUNK, gw)
        tot3 = cum3[:, 0:1, :] if reverse else cum3[:, CHUNK - 1:CHUNK, :]
        dec3 = jnp.exp(tot3)
        dec_ref[0, :, direction * HG_WIDTH + lo:direction * HG_WIDTH + lo + gw] = dec3.reshape(nchunk, gw)
        if latent:
            e_pos = jnp.exp(cum)
            kd = k * (1.0 / e_pos)
            kd2 = kd.reshape(nchunk, CHUNK, gw) * dec3
            (qdf_ref, qdb_ref)[direction][0, :, lo:lo + gw] = (keep["q_hg"][:, lo:lo + gw] * e_pos).astype(BF16)
            (kdf_ref, kdb_ref)[direction][0, :, lo:lo + gw] = kd.astype(BF16)
        else:
            kd2 = k.reshape(nchunk, CHUNK, gw) * jnp.exp(tot3 - cum3)
        (kd2f_ref, kd2b_ref)[direction][0, :, lo:lo + gw] = kd2.reshape(tm, gw).astype(BF16)

    def mgate_finish(raw, j):
        gates_ref[0, :, j * sw:(j + 1) * sw] = (jnp.tanh(raw) + 1.0).astype(BF16)

    def v_finish(raw):
        for s in range(tm // LANES):
            vt_ref[0, s] = raw[s * LANES:(s + 1) * LANES, :].T.astype(BF16)
        if latent:
            v_ref[0] = raw.astype(BF16)

    def g_finish(raw):
        g_ref[0] = (raw * (jnp.tanh(raw) + 1.0)).astype(BF16)

    def split_sq(t):
        sq = t * t
        hi = sq.astype(BF16)
        lo = (sq - hi.astype(F32)).astype(BF16)
        return jnp.concatenate([hi, lo], axis=1)

    def kv_finish(raw):
        keep["k_raw"] = raw[:, 0:LANES]
        keep["k_split"] = split_sq(keep["k_raw"])
        vatt_ref[0] = raw[:, LANES:2 * LANES].astype(BF16)

    def knorm_finish(ss):
        kn = keep["k_raw"] * lax.rsqrt(ss * (1.0 / HEAD_DIM) + EPS) * kw_ref[...]
        if latent:
            kn = _rope(kn, cos_ref[...], sin_ref[...], lane)
        for s in range(tm // LANES):
            katt_ref[0, s] = kn[s * LANES:(s + 1) * LANES, :].T.astype(BF16)

    def q_finish(raw):
        keep["q_raw"] = raw
        keep["q_split"] = [split_sq(raw[:, s * LANES:(s + 1) * LANES]) for s in range(ATT_WIDTH // LANES)]

    def qnorm_finish(ss_list):
        for s, ss in enumerate(ss_list):
            qraw = keep["q_raw"][:, s * LANES:(s + 1) * LANES]
            qn = qraw * lax.rsqrt(ss * (1.0 / HEAD_DIM) + EPS) * qw_ref[...]
            qn = _rope(qn, cos_ref[...], sin_ref[...], lane) * (HEAD_DIM ** -0.5 * LOG2E)
            swapped = pltpu.roll(qn, HEAD_DIM, 1)
            if s < ATT_WIDTH // LANES // 2:
                even, odd = jnp.where(left, qn, 0.0), jnp.where(left, swapped, 0.0)
            else:
                even, odd = jnp.where(left, 0.0, swapped), jnp.where(left, 0.0, qn)
            q_ref[0, :, (2 * s) * LANES:(2 * s + 1) * LANES] = even.astype(BF16)
            q_ref[0, :, (2 * s + 1) * LANES:(2 * s + 2) * LANES] = odd.astype(BF16)

    def gate_stage(direction, part):
        col = (C_FF, C_FB)[direction] + part * gw
        return (lambda: mm(col, col + gw),
                functools.partial(gate_finish, direction=direction, part=part))

    def mgate_stage(j):
        c0 = C_GATES + j * sw
        return (lambda: mm(c0, c0 + sw), functools.partial(mgate_finish, j=j))

    v_stage = (lambda: mm(C_INP, C_INP + HG_WIDTH), v_finish)
    kv_stage = (lambda: mm(C_K, C_K + 2 * KV_WIDTH), kv_finish)
    knorm_stage = (lambda: _dot(keep["k_split"], ones_blk2), knorm_finish)
    if latent:
        stages = [(lambda: mm(C_QHG, C_QHG + HG_WIDTH), qhg_finish)]
        for j in range(4):
            stages += [gate_stage(j // 2, j % 2), mgate_stage(j)]
        stages += [(lambda: mm(C_Q, C_Q + ATT_WIDTH), q_finish),
                   v_stage,
                   (lambda: [_dot(t, ones_blk2) for t in keep["q_split"]], qnorm_finish),
                   kv_stage,
                   (lambda: mm(C_GHG, C_GHG + HG_WIDTH), g_finish),
                   knorm_stage]
    else:
        stages = [gate_stage(0, 0), gate_stage(0, 1), gate_stage(1, 0), kv_stage, gate_stage(1, 1),
                  knorm_stage, v_stage]

    raw = stages[0][0]()
    for i, (_, finish) in enumerate(stages):
        nxt = stages[i + 1][0]() if i + 1 < len(stages) else None
        finish(raw)
        raw = nxt


def _inproj_call(x, mod3, nw, w_in, lbl, qw, kw, cos_t, sin_t, latent):
    b_, s_, _ = x.shape
    tm = min(INPROJ_TILE, s_)
    grid = (b_, s_ // tm)
    tok = lambda w: pl.BlockSpec((1, tm, w), lambda b, i: (b, i, 0))
    mod_map = (lambda b, i: (b, 0, 0)) if latent else (lambda b, i: (0, 0, 0))
    ncols = IN_COLS if latent else CTX_COLS
    in_specs = [tok(D_MODEL),
                pl.BlockSpec((1, 1, mod3.shape[2]), mod_map),
                _const_spec((1, D_MODEL)),
                pl.BlockSpec((D_MODEL, ncols), lambda b, i: (0, 0), pipeline_mode=pl.Buffered(1)),
                _const_spec((4, HG_WIDTH)),
                _const_spec((1, LANES)),
                _const_spec((1, LANES)),
                pl.BlockSpec((tm, LANES), lambda b, i: (i, 0)),
                pl.BlockSpec((tm, LANES), lambda b, i: (i, 0))]
    bf = lambda w: jax.ShapeDtypeStruct((b_, s_, w), BF16)
    dec_shape = jax.ShapeDtypeStruct((b_, s_ // CHUNK, 2 * HG_WIDTH), F32)
    dec_spec = pl.BlockSpec((1, tm // CHUNK, 2 * HG_WIDTH), lambda b, i: (b, i, 0))
    vt_shape = jax.ShapeDtypeStruct((b_, s_ // LANES, HG_WIDTH, LANES), BF16)
    vt_spec = pl.BlockSpec((1, tm // LANES, HG_WIDTH, LANES), lambda b, i: (b, i, 0, 0))
    katt_shape = jax.ShapeDtypeStruct((b_, s_ // LANES, KV_WIDTH, LANES), BF16)
    katt_spec = pl.BlockSpec((1, tm // LANES, KV_WIDTH, LANES), lambda b, i: (b, i, 0, 0))
    if latent:
        out_shape = [bf(HG_WIDTH)] * 6 + [dec_shape, vt_shape, bf(HG_WIDTH), bf(HG_WIDTH),
                                         bf(ATT_HEADS * LANES), katt_shape, bf(KV_WIDTH), bf(2 * D_MODEL)]
        out_specs = [tok(HG_WIDTH)] * 6 + [dec_spec, vt_spec, tok(HG_WIDTH), tok(HG_WIDTH),
                                          tok(ATT_HEADS * LANES), katt_spec, tok(KV_WIDTH), tok(2 * D_MODEL)]
    else:
        out_shape = [bf(HG_WIDTH)] * 2 + [dec_shape, vt_shape, katt_shape, bf(KV_WIDTH)]
        out_specs = [tok(HG_WIDTH)] * 2 + [dec_spec, vt_spec, katt_spec, tok(KV_WIDTH)]
    return pl.pallas_call(
        functools.partial(_inproj_kernel, latent=latent),
        grid=grid, in_specs=in_specs, out_specs=out_specs, out_shape=out_shape,
        compiler_params=pltpu.CompilerParams(dimension_semantics=("parallel", "parallel"),
                                             vmem_limit_bytes=VMEM_LIMIT),
        name="inproj_latent" if latent else "inproj_ctx",
    )(x, mod3, nw, w_in, lbl, qw, kw, cos_t, sin_t)


def _hgrn_kernel(qdf_ref, kdf_ref, kd2f_ref, qdb_ref, kdb_ref, kd2b_ref, v_ref, vt_ref,
                 decf_ref, decb_ref, ckd2f_ref, ckd2b_ref, cvt_ref, cdecf_ref, cdecb_ref,
                 g_ref, nw_ref, *rest):
    n_w = (len(rest) - 2) // 2
    w_refs, y_ref, wb_refs, o_acc = rest[:n_w], rest[n_w], rest[n_w + 1:2 * n_w + 1], rest[-1]
    for w_ref, wb_ref in zip(w_refs, wb_refs):
        wb_ref[...] = w_ref[...].astype(BF16)

    seq = v_ref.shape[1]
    ctx_len = ckd2f_ref.shape[1]
    heads = v_ref.shape[2] // HG_DIM
    tile = HG_TILE
    cpt = tile // CHUNK
    cps = LANES // CHUNK
    spt = tile // LANES
    n_tiles = seq // tile
    half = n_tiles // 2

    row_chunk = lax.broadcasted_iota(jnp.int32, (LANES, HG_DIM), 0) // CHUNK
    ti = lax.broadcasted_iota(jnp.int32, (tile, tile), 0)
    tj = lax.broadcasted_iota(jnp.int32, (tile, tile), 1)
    same = (ti // CHUNK) == (tj // CHUNK)
    mask_f = same & (tj <= ti)
    mask_b = same & (tj >= ti)

    def chunk_update(vt_slab, kd2_slab, n_in_slab):
        kd2_m = jnp.where(row_chunk == n_in_slab, kd2_slab, jnp.zeros_like(kd2_slab))
        return _dot(vt_slab, kd2_m)

    def ctx_state(ckd2_ref, cdec_ref, cols, reverse):
        st = jnp.zeros((HG_DIM, HG_DIM), F32)
        order = range(ctx_len // CHUNK)
        for n in (reversed(order) if reverse else order):
            slab = n // cps
            st = cdec_ref[0, n:n + 1, cols] * st + chunk_update(
                cvt_ref[0, slab, cols, :], ckd2_ref[0, slab * LANES:(slab + 1) * LANES, cols], n % cps)
        return st

    def emit(o_t, cols, rows, finalize):
        if finalize:
            o_t = o_t + o_acc[rows, cols]
            ms = jnp.mean(o_t * o_t, axis=-1, keepdims=True)
            y = o_t * lax.rsqrt(ms + EPS) * nw_ref[...] * g_ref[0, rows, cols].astype(F32)
            y_ref[0, rows, cols] = y.astype(BF16)
        else:
            o_acc[rows, cols] = o_t

    def body(it, states, finalize):
        chains = []
        for h in range(heads):
            cols = slice(h * HG_DIM, (h + 1) * HG_DIM)
            chains.append((qdf_ref, kdf_ref, kd2f_ref, decf_ref, cols, it, False))
            chains.append((qdb_ref, kdb_ref, kd2b_ref, decb_ref, cols, n_tiles - 1 - it, True))
        n_ch = len(chains)
        rows = [pl.ds(pl.multiple_of(c[5] * tile, tile), tile) for c in chains]
        orders = [list(reversed(range(cpt))) if c[6] else list(range(cpt)) for c in chains]
        qd, sc, ups = [], [], []
        for (qd_ref, kd_ref, kd2_ref, _, cols, t, reverse), r in zip(chains, rows):
            qd.append(qd_ref[0, r, cols])
            sc.append(_dot_nt(qd[-1], kd_ref[0, r, cols]))
            kd2 = kd2_ref[0, r, cols]
            ups.append([chunk_update(vt_ref[0, t * spt + n // cps, cols, :],
                                     kd2[(n // cps) * LANES:(n // cps + 1) * LANES, :], n % cps)
                        for n in range(cpt)])
        o = []
        for c, r, s in zip(chains, rows, sc):
            p = jnp.where(mask_b if c[6] else mask_f, s, 0.0).astype(BF16)
            o.append(_dot(p, v_ref[0, r, c[4]]))
        starts, new_states = [], []
        for i, c in enumerate(chains):
            dec = c[3][0, pl.ds(pl.multiple_of(c[5] * cpt, cpt), cpt), c[4]]
            st = states[i]
            start = [None] * cpt
            for n in orders[i]:
                start[n] = st.T.astype(BF16)
                st = dec[n:n + 1, :] * st + ups[i][n]
            starts.append(start)
            new_states.append(st)
        for i, c in enumerate(chains):
            outs = [o[i][n * CHUNK:(n + 1) * CHUNK, :]
                    + _dot(qd[i][n * CHUNK:(n + 1) * CHUNK, :], starts[i][n]) for n in range(cpt)]
            emit(jnp.concatenate(outs, axis=0), c[4], rows[i], finalize)
        return tuple(new_states)

    states = []
    for h in range(heads):
        cols = slice(h * HG_DIM, (h + 1) * HG_DIM)
        states += [ctx_state(ckd2f_ref, cdecf_ref, cols, False), ctx_state(ckd2b_ref, cdecb_ref, cols, True)]
    states = lax.fori_loop(0, half, functools.partial(body, finalize=False), tuple(states))
    lax.fori_loop(half, n_tiles, functools.partial(body, finalize=True), states)


def _hgrn_call(qdf, kdf, kd2f, qdb, kdb, kd2b, v, vt, dec, ckd2f, ckd2b, cvt, cdec, g, nw, weights):
    b_, s_, _ = v.shape
    l_ = ckd2f.shape[1]
    w = HG_HEADS_PER_STEP * HG_DIM
    groups = HG_HEADS // HG_HEADS_PER_STEP
    seq = lambda n: pl.BlockSpec((1, n, w), lambda b, h: (b, 0, h))
    vt_spec = lambda n: pl.BlockSpec((1, n // LANES, w, LANES), lambda b, h: (b, 0, h, 0))
    dec_f = lambda n: pl.BlockSpec((1, n // CHUNK, w), lambda b, h: (b, 0, h))
    dec_b = lambda n: pl.BlockSpec((1, n // CHUNK, w), lambda b, h: (b, 0, groups + h))
    n_steps = b_ * groups
    w_specs = [pl.BlockSpec((wt.shape[0] // n_steps, wt.shape[1]), lambda b, h: (b * groups + h, 0))
               for wt in weights]
    outs = pl.pallas_call(
        _hgrn_kernel,
        grid=(b_, groups),
        in_specs=[seq(s_)] * 7 + [vt_spec(s_), dec_f(s_), dec_b(s_),
                                  seq(l_), seq(l_), vt_spec(l_), dec_f(l_), dec_b(l_),
                                  seq(s_), _const_spec((1, HG_DIM))] + w_specs,
        out_specs=[seq(s_)] + w_specs,
        out_shape=[jax.ShapeDtypeStruct((b_, s_, HG_WIDTH), BF16)]
        + [jax.ShapeDtypeStruct(wt.shape, BF16) for wt in weights],
        scratch_shapes=[pltpu.VMEM((s_, w), F32)],
        compiler_params=pltpu.CompilerParams(dimension_semantics=("parallel", "parallel"),
                                             vmem_limit_bytes=VMEM_LIMIT),
        name="hgrn_scan",
    )(qdf, kdf, kd2f, qdb, kdb, kd2b, v, vt, dec, dec, ckd2f, ckd2b, cvt, cdec, cdec, g, nw, *weights)
    return outs[0], outs[1:]


def _attn_kernel(sink_ref, q_ref, kt_ref, kc_ref, v_ref, vc_ref, y_ref):
    nb = kt_ref.shape[1]
    blk = ATT_BLOCK
    hpg = ATT_GROUP_HEADS
    n_groups = ATT_HEADS // hpg
    qry_r = lax.broadcasted_iota(jnp.int32, (blk, blk), 0)
    key_c = lax.broadcasted_iota(jnp.int32, (blk, blk), 1)
    left = lax.broadcasted_iota(jnp.int32, (blk, LANES), 1) < HEAD_DIM
    ctx_keys = [kc_ref[0, c] for c in range(kc_ref.shape[1])]
    keep = {}

    def band(u):
        qb = pl.program_id(1) * ATT_STEP_BLOCKS + u
        ids = (jnp.maximum(qb - 1, 0), qb, jnp.minimum(qb + 1, nb - 1))
        keys_t = jnp.concatenate([kt_ref[0, j] for j in ids] + ctx_keys, axis=1)
        values = jnp.concatenate([v_ref[0, pl.ds(pl.multiple_of(j * blk, blk), blk), :] for j in ids]
                                 + [vc_ref[0]], axis=0)
        return qb, keys_t, values

    def score_issue(u, g):
        if g == 0:
            keep["band", u] = band(u)
        q = jnp.concatenate([q_ref[0, u * blk:(u + 1) * blk, h * LANES:(h + 1) * LANES]
                             for h in range(g * hpg, (g + 1) * hpg)], axis=0)
        return _dot(q, keep["band", u][1])

    def score_finish(raw, u, g):
        qb = keep["band", u][0]
        mask_prev = (key_c >= qry_r) & (qb > 0)
        mask_next = (key_c <= qry_r) & (qb < nb - 1)
        probs, inv = [], []
        for n in range(hpg):
            s = raw[n * blk:(n + 1) * blk, :]
            sink = sink_ref[g * hpg + n] * LOG2E
            s = jnp.concatenate([jnp.where(mask_prev, s[:, 0:blk], NEG), s[:, blk:2 * blk],
                                 jnp.where(mask_next, s[:, 2 * blk:3 * blk], NEG), s[:, 3 * blk:]], axis=1)
            mx = jnp.maximum(jnp.max(s, axis=1, keepdims=True), sink)
            p = jnp.exp2(s - mx)
            den = jnp.sum(p, axis=1, keepdims=True) + jnp.exp2(sink - mx)
            probs.append(p.astype(BF16))
            inv.append(1.0 / den)
        keep["p", u, g] = jnp.concatenate(probs, axis=0)
        keep["inv", u, g] = inv

    def value_issue(u, g):
        return _dot(keep["p", u, g], keep["band", u][2])

    def value_finish(raw, u, g):
        for n in range(0, hpg, 2):
            h = g * hpg + n
            even = raw[n * blk:(n + 1) * blk, :] * keep["inv", u, g][n]
            odd = raw[(n + 1) * blk:(n + 2) * blk, :] * keep["inv", u, g][n + 1]
            if h < ATT_HEADS // ATT_KV_HEADS:
                y = jnp.where(left, even, pltpu.roll(odd, HEAD_DIM, 1))
            else:
                y = jnp.where(left, pltpu.roll(even, HEAD_DIM, 1), odd)
            y_ref[0, u * blk:(u + 1) * blk, (h // 2) * LANES:(h // 2 + 1) * LANES] = y.astype(BF16)

    part = functools.partial
    work = [(u, g) for u in range(ATT_STEP_BLOCKS) for g in range(n_groups)]
    stages = [(part(score_issue, *work[0]), part(score_finish, u=work[0][0], g=work[0][1]))]
    for n, (u, g) in enumerate(work):
        if n + 1 < len(work):
            un, gn = work[n + 1]
            stages.append((part(score_issue, un, gn), part(score_finish, u=un, g=gn)))
        stages.append((part(value_issue, u, g), part(value_finish, u=u, g=g)))
    raw = stages[0][0]()
    for n, (_, finish) in enumerate(stages):
        nxt = stages[n + 1][0]() if n + 1 < len(stages) else None
        finish(raw)
        raw = nxt


def _attn_call(sinks, q, katt, vatt, ckatt, cvatt):
    b_, s_, qw = q.shape
    l_ = cvatt.shape[1]
    rows = ATT_STEP_BLOCKS * ATT_BLOCK
    batch4 = lambda n: pl.BlockSpec((1, n, KV_WIDTH, LANES), lambda b, i: (b, 0, 0, 0))
    batch3 = lambda n: pl.BlockSpec((1, n, KV_WIDTH), lambda b, i: (b, 0, 0))
    return pl.pallas_call(
        _attn_kernel,
        grid=(b_, s_ // rows),
        in_specs=[pl.BlockSpec(memory_space=pltpu.SMEM),
                  pl.BlockSpec((1, rows, qw), lambda b, i: (b, i, 0)),
                  batch4(s_ // LANES), batch4(l_ // LANES), batch3(s_), batch3(l_)],
        out_specs=pl.BlockSpec((1, rows, ATT_WIDTH), lambda b, i: (b, i, 0)),
        out_shape=jax.ShapeDtypeStruct((b_, s_, ATT_WIDTH), BF16),
        compiler_params=pltpu.CompilerParams(dimension_semantics=("parallel", "arbitrary"),
                                             vmem_limit_bytes=VMEM_LIMIT),
        name="window_attn",
    )(sinks, q, katt, ckatt, vatt, cvatt)


def _merge_ffn_kernel(x_ref, yh_ref, ya_ref, gates_ref, mod_ref, nw_ref, wbh_ref, wba_ref, wo_ref,
                      wg_ref, wu_ref, wd_ref, o_ref):
    tm = x_ref.shape[1]
    n_sub = FFN_SUBTILES
    sub = tm // n_sub
    bounds = FFN_CHUNK_BOUNDS
    n_chunks = len(bounds) - 1
    g1 = mod_ref[0, :, 2 * D_MODEL:3 * D_MODEL]
    sh2 = mod_ref[0, :, 3 * D_MODEL:4 * D_MODEL]
    sc2 = mod_ref[0, :, 4 * D_MODEL:5 * D_MODEL]
    g2 = mod_ref[0, :, 5 * D_MODEL:6 * D_MODEL]
    keep = {}

    def branch_issue(r):
        rows = slice(r * sub, (r + 1) * sub)
        return _dot(yh_ref[0, rows, :], wbh_ref[...]), _dot(ya_ref[0, rows, :], wba_ref[...])

    def branch_finish(raw, r):
        rows = slice(r * sub, (r + 1) * sub)
        a, b = raw
        mixed = (gates_ref[0, rows, 0:D_MODEL].astype(F32) * a
                 + gates_ref[0, rows, D_MODEL:2 * D_MODEL].astype(F32) * b)
        keep["mixed", r] = (0.5 * mixed).astype(BF16)

    def out_issue(r):
        return _dot(keep["mixed", r], wo_ref[...])

    def out_finish(raw, r):
        x1 = x_ref[0, r * sub:(r + 1) * sub, :] + g1 * raw
        ms = jnp.mean(x1 * x1, axis=-1, keepdims=True)
        keep["x1", r] = x1
        keep["h2", r] = ((x1 * lax.rsqrt(ms + EPS) * nw_ref[...]) * (1.0 + sc2) + sh2).astype(BF16)

    def hidden_issue(r, c):
        h2 = keep["h2", r]
        cols = slice(bounds[c], bounds[c + 1])
        return _dot(h2, wg_ref[:, cols]), _dot(h2, wu_ref[:, cols])

    def hidden_finish(raw, r, c):
        gate, up = raw
        keep["act", r, c] = (gate * _sigmoid(gate) * up).astype(BF16)

    def down_issue(r, c):
        return _dot(keep["act", r, c], wd_ref[bounds[c]:bounds[c + 1], :])

    def down_finish(raw, r, c):
        dn = raw if c == 0 else keep["dn", r] + raw
        if c + 1 < n_chunks:
            keep["dn", r] = dn
        else:
            o_ref[0, r * sub:(r + 1) * sub, :] = keep["x1", r] + g2 * dn

    part = functools.partial
    stages = []
    for issue, finish in ((branch_issue, branch_finish), (out_issue, out_finish)):
        stages += [(part(issue, r), part(finish, r=r)) for r in range(n_sub)]
    for c in range(n_chunks):
        stages += [(part(hidden_issue, r, c), part(hidden_finish, r=r, c=c)) for r in range(n_sub)]
        stages += [(part(down_issue, r, c), part(down_finish, r=r, c=c)) for r in range(n_sub)]
    raw = stages[0][0]()
    for n, (_, finish) in enumerate(stages):
        nxt = stages[n + 1][0]() if n + 1 < len(stages) else None
        finish(raw)
        raw = nxt


def _merge_ffn_call(x, yh, ya, gates, mod3, nw, wbh, wba, wo, wg, wu, wd):
    b_, s_, _ = x.shape
    tm = TOK_TILE
    tok = lambda w: pl.BlockSpec((1, tm, w), lambda b, i: (b, i, 0))
    return pl.pallas_call(
        _merge_ffn_kernel,
        grid=(b_, s_ // tm),
        in_specs=[tok(D_MODEL), tok(HG_WIDTH), tok(ATT_WIDTH), tok(2 * D_MODEL),
                  pl.BlockSpec((1, 1, mod3.shape[2]), lambda b, i: (b, 0, 0)),
                  _const_spec((1, D_MODEL)),
                  _const_spec(wbh.shape), _const_spec(wba.shape), _const_spec(wo.shape),
                  _const_spec(wg.shape), _const_spec(wu.shape), _const_spec(wd.shape)],
        out_specs=tok(D_MODEL),
        out_shape=jax.ShapeDtypeStruct(x.shape, F32),
        compiler_params=pltpu.CompilerParams(dimension_semantics=("parallel", "parallel"),
                                             vmem_limit_bytes=VMEM_LIMIT),
        name="merge_ffn",
    )(x, yh, ya, gates, mod3, nw, wbh, wba, wo, wg, wu, wd)


def _rope_tables(n_tok):
    t = np.arange(n_tok)
    rows = (t // GRID_W).astype(np.float64)
    cols = (t % GRID_W).astype(np.float64)
    half = HEAD_DIM // 2
    inv_freq = ROPE_THETA ** (-np.arange(0, half, 2, dtype=np.float64) / half)
    d = np.arange(LANES) % HEAD_DIM
    pos = np.where((d < half)[None, :], rows[:, None], cols[:, None])
    ang = pos * inv_freq[(d % half) % (half // 2)][None, :]
    sign = np.where((d % half) < half // 2, -1.0, 1.0)[None, :]
    return jnp.asarray(np.cos(ang), F32), jnp.asarray(np.sin(ang) * sign, F32)


def kernel(x, c, ctx, c_ctx, w_ada, b_ada, norm_mix_w, norm_ffn_w, w_in, hgrn_lb_logits, hgrn_norm_w,
           q_norm_w, k_norm_w, attn_sinks, w_branch_hgrn, w_branch_attn, w_out, w_ffn_gate, w_ffn_up,
           w_ffn_down):
    b_, s_, _ = x.shape
    layer = 0
    pad = 16 - (b_ + 1)
    cc = jnp.concatenate([c, c_ctx[None, :], jnp.zeros((pad, D_MODEL), F32)], axis=0)
    mod = _ada_call(cc, w_ada[layer], b_ada[layer][None, :])
    mod_lat = mod[:b_].reshape(b_, 1, 6 * D_MODEL)
    mod_ctx = mod[b_:b_ + 1].reshape(1, 1, 6 * D_MODEL)

    col = np.ones((IN_COLS,), np.float32)
    for lo, width in ((C_FF, 2 * HG_WIDTH), (C_QHG, 2 * HG_WIDTH), (C_GATES, 2 * D_MODEL)):
        col[lo:lo + width] = 0.5
    w_in_b = (w_in[layer] * col[None, :]).astype(BF16)
    lbl = hgrn_lb_logits[:, 0:2, :].reshape(4, HG_WIDTH)
    qw = jnp.tile(q_norm_w[layer], 2)[None, :]
    kw = jnp.tile(k_norm_w[layer], 2)[None, :]
    nw_mix = norm_mix_w[layer][None, :]
    cos_t, sin_t = _rope_tables(s_)

    (qdf, kdf, kd2f, qdb, kdb, kd2b, dec, vt, v, g, q, katt, vatt, gates) = _inproj_call(
        x, mod_lat, nw_mix, w_in_b, lbl, qw, kw, cos_t, sin_t, latent=True)
    ckd2f, ckd2b, cdec, cvt, ckatt, cvatt = _inproj_call(
        ctx, mod_ctx, nw_mix, w_in_b, lbl, qw, kw, cos_t, sin_t, latent=False)

    y_hg, w_bf16 = _hgrn_call(
        qdf, kdf, kd2f, qdb, kdb, kd2b, v, vt, dec, ckd2f, ckd2b, cvt, cdec, g, hgrn_norm_w[layer][None, :],
        (w_branch_hgrn[layer], w_branch_attn[layer], w_out[layer], w_ffn_gate[layer], w_ffn_up[layer],
         w_ffn_down[layer]))
    y_at = _attn_call(attn_sinks[layer], q, katt, vatt, ckatt, cvatt)

    return _merge_ffn_call(x, y_hg, y_at, gates, mod_lat, norm_ffn_w[layer][None, :], *w_bf16)
```

```python
import functools

import jax
import jax.numpy as jnp
import numpy as np
from jax import lax
from jax.experimental import pallas as pl
from jax.experimental.pallas import tpu as pltpu

F32 = jnp.float32
BF16 = jnp.bfloat16

D_MODEL = 1024
GRID_W = 64
EPS = 1e-6
HG_HEADS = 4
HG_DIM = 128
HG_WIDTH = HG_HEADS * HG_DIM
CHUNK = 32
ATT_HEADS = 8
ATT_KV_HEADS = 2
HEAD_DIM = 64
ATT_WIDTH = ATT_HEADS * HEAD_DIM
KV_WIDTH = ATT_KV_HEADS * HEAD_DIM
WINDOW = 128
ROPE_THETA = 10000.0
D_FF = 2816
CTX_COLS = 3 * HG_WIDTH + 2 * KV_WIDTH
IN_COLS = CTX_COLS + 2 * HG_WIDTH + ATT_WIDTH + 2 * D_MODEL

C_FF, C_FB, C_INP, C_K, C_V = 0, HG_WIDTH, 2 * HG_WIDTH, 3 * HG_WIDTH, 3 * HG_WIDTH + KV_WIDTH
C_QHG = CTX_COLS
C_GHG = C_QHG + HG_WIDTH
C_Q = C_GHG + HG_WIDTH
C_GATES = C_Q + ATT_WIDTH

LANES = 128
TOK_TILE = 512
FFN_SUBTILES = 2
MXU_DEPTH = 256
FFN_CHUNK_BOUNDS = (0, 6 * MXU_DEPTH, D_FF)
INPROJ_TILE = 512
HG_TILE = 256
HG_HEADS_PER_STEP = 4
ATT_BLOCK = 128
ATT_GROUP_HEADS = 2
ATT_STEP_BLOCKS = 8
VMEM_LIMIT = 56 * 1024 * 1024
NEG = -1e30
LOG2E = 1.4426950408889634


def _dot(a, b):
    return jnp.dot(a, b, preferred_element_type=F32)


def _dot_nt(a, b):
    return lax.dot_general(a, b, (((1,), (1,)), ((), ())), preferred_element_type=F32)


def _sigmoid(x):
    return 0.5 * jnp.tanh(0.5 * x) + 0.5


def _const_spec(shape):
    n = len(shape)
    return pl.BlockSpec(shape, lambda *_: (0,) * n, pipeline_mode=pl.Buffered(1))


def _ada_kernel(c_ref, w_ref, b_ref, o_ref):
    c = c_ref[...]
    s = (c * _sigmoid(c)).astype(BF16)
    o_ref[...] = _dot(s, w_ref[...].astype(BF16)) + b_ref[...]


def _ada_call(cc, w_ada, b_ada):
    rows = cc.shape[0]
    n_out = w_ada.shape[1]
    bn = 1024
    return pl.pallas_call(
        _ada_kernel,
        grid=(n_out // bn,),
        in_specs=[pl.BlockSpec((rows, D_MODEL), lambda j: (0, 0)),
                  pl.BlockSpec((D_MODEL, bn), lambda j: (0, j)),
                  pl.BlockSpec((1, bn), lambda j: (0, j))],
        out_specs=pl.BlockSpec((rows, bn), lambda j: (0, j)),
        out_shape=jax.ShapeDtypeStruct((rows, n_out), F32),
        compiler_params=pltpu.CompilerParams(dimension_semantics=("arbitrary",),
                                             vmem_limit_bytes=VMEM_LIMIT),
        name="ada_mod",
    )(cc, w_ada, b_ada)


def _chunk_cumsum(x, reverse):
    n, c = x.shape
    sub = 8
    r = lax.broadcasted_iota(jnp.int32, x.shape, 0) & (CHUNK - 1)
    s = 1
    while s < sub:
        if reverse:
            x = x + jnp.where(r < CHUNK - s, pltpu.roll(x, n - s, 0), 0.0)
        else:
            x = x + jnp.where(r >= s, pltpu.roll(x, s, 0), 0.0)
        s *= 2
    per = CHUNK // sub
    x4 = x.reshape(n // CHUNK, per, sub, c)
    s = 1
    while s < per:
        if reverse:
            x4 = jnp.concatenate([x4[:, :per - s] + x4[:, s:], x4[:, per - s:]], axis=1)
        else:
            x4 = jnp.concatenate([x4[:, :s], x4[:, s:] + x4[:, :per - s]], axis=1)
        s *= 2
    return x4.reshape(n, c)


def _rope(t, cos, sin_signed, lane):
    partner = jnp.where((lane & 31) < 16, pltpu.roll(t, LANES - 16, 1), pltpu.roll(t, 16, 1))
    return t * cos + partner * sin_signed


def _inproj_kernel(x_ref, mod_ref, nw_ref, w_ref, lbl_ref, qw_ref, kw_ref, cos_ref, sin_ref,
                   *out_refs, latent):
    if latent:
        (qdf_ref, kdf_ref, qdb_ref, kdb_ref, dec_ref, vt_ref, v_ref,
         g_ref, q_ref, katt_ref, vatt_ref, gates_ref) = out_refs
    else:
        kd2f_ref, kd2b_ref, dec_ref, vt_ref, katt_ref, vatt_ref = out_refs

    tm = x_ref.shape[1]
    nchunk = tm // CHUNK
    x = x_ref[0]
    sh = mod_ref[0, :, 0:D_MODEL]
    sc = mod_ref[0, :, D_MODEL:2 * D_MODEL]
    ms = jnp.mean(x * x, axis=-1, keepdims=True)
    h = x * lax.rsqrt(ms + EPS) * (nw_ref[...] * (1.0 + sc)) + sh
    hb = h.astype(BF16)

    def mm(a, b):
        return _dot(hb, w_ref[:, a:b])

    l0f, l1f = lbl_ref[0:1, :], lbl_ref[1:2, :]
    l0b, l1b = lbl_ref[2:3, :], lbl_ref[3:4, :]
    lb_f = 1.0 / (1.0 + jnp.exp(l1f - l0f))
    lb_b = 1.0 / (1.0 + jnp.exp(l1b - l0b))

    gw = HG_WIDTH // 2
    sw = D_MODEL // 2
    lane = lax.broadcasted_iota(jnp.int32, (tm, LANES), 1)
    left = lane < HEAD_DIM
    bi = lax.broadcasted_iota(jnp.int32, (2 * LANES, LANES), 0) // HEAD_DIM
    bj = lax.broadcasted_iota(jnp.int32, (2 * LANES, LANES), 1) // HEAD_DIM
    ones_blk2 = jnp.where((bi % 2) == bj, 1.0, 0.0).astype(BF16)
    keep = {}

    def qhg_finish(raw):
        keep["q_hg"] = raw * (jnp.tanh(raw) + 1.0) * (HG_DIM ** -0.5)

    def gate_finish(raw, direction, part):
        reverse = direction == 1
        lo = part * gw
        lb = (lb_f, lb_b)[direction][:, lo:lo + gw]
        c_half = 0.5 - 0.5 * lb
        ct = c_half * jnp.tanh(raw)
        f = (0.5 + 0.5 * lb) + ct
        k = c_half - ct
        cum = _chunk_cumsum(jnp.log(f), reverse)
        cum3 = cum.reshape(nchunk, CHUNK, gw)
        tot3 = cum3[:, 0:1, :] if reverse else cum3[:, CHUNK - 1:CHUNK, :]
        dec3 = jnp.exp(tot3)
        dec_ref[0, :, direction * HG_WIDTH + lo:direction * HG_WIDTH + lo + gw] = dec3.reshape(nchunk, gw)
        if latent:
            e_pos = jnp.exp(cum)
            (qdf_ref, qdb_ref)[direction][0, :, lo:lo + gw] = (keep["q_hg"][:, lo:lo + gw] * e_pos).astype(BF16)
            (kdf_ref, kdb_ref)[direction][0, :, lo:lo + gw] = (k * (1.0 / e_pos)).astype(BF16)
        else:
            kd2 = k.reshape(nchunk, CHUNK, gw) * jnp.exp(tot3 - cum3)
            (kd2f_ref, kd2b_ref)[direction][0, :, lo:lo + gw] = kd2.reshape(tm, gw).astype(BF16)

    def mgate_finish(raw, j):
        gates_ref[0, :, j * sw:(j + 1) * sw] = (jnp.tanh(raw) + 1.0).astype(BF16)

    def v_finish(raw):
        for s in range(tm // LANES):
            vt_ref[0, s] = raw[s * LANES:(s + 1) * LANES, :].T.astype(BF16)
        if latent:
            v_ref[0] = raw.astype(BF16)

    def g_finish(raw):
        g_ref[0] = (raw * (jnp.tanh(raw) + 1.0)).astype(BF16)

    def split_sq(t):
        sq = t * t
        hi = sq.astype(BF16)
        lo = (sq - hi.astype(F32)).astype(BF16)
        return jnp.concatenate([hi, lo], axis=1)

    def kv_finish(raw):
        keep["k_raw"] = raw[:, 0:LANES]
        keep["k_split"] = split_sq(keep["k_raw"])
        vatt_ref[0] = raw[:, LANES:2 * LANES].astype(BF16)

    def knorm_finish(ss):
        kn = keep["k_raw"] * lax.rsqrt(ss * (1.0 / HEAD_DIM) + EPS) * kw_ref[...]
        if latent:
            kn = _rope(kn, cos_ref[...], sin_ref[...], lane)
        for s in range(tm // LANES):
            katt_ref[0, s] = kn[s * LANES:(s + 1) * LANES, :].T.astype(BF16)

    def q_finish(raw):
        keep["q_raw"] = raw
        keep["q_split"] = [split_sq(raw[:, s * LANES:(s + 1) * LANES]) for s in range(ATT_WIDTH // LANES)]

    def qnorm_finish(ss_list):
        for s, ss in enumerate(ss_list):
            qraw = keep["q_raw"][:, s * LANES:(s + 1) * LANES]
            qn = qraw * lax.rsqrt(ss * (1.0 / HEAD_DIM) + EPS) * qw_ref[...]
            qn = _rope(qn, cos_ref[...], sin_ref[...], lane) * (HEAD_DIM ** -0.5 * LOG2E)
            swapped = pltpu.roll(qn, HEAD_DIM, 1)
            if s < ATT_WIDTH // LANES // 2:
                even, odd = jnp.where(left, qn, 0.0), jnp.where(left, swapped, 0.0)
            else:
                even, odd = jnp.where(left, 0.0, swapped), jnp.where(left, 0.0, qn)
            q_ref[0, :, (2 * s) * LANES:(2 * s + 1) * LANES] = even.astype(BF16)
            q_ref[0, :, (2 * s + 1) * LANES:(2 * s + 2) * LANES] = odd.astype(BF16)

    def gate_stage(direction, part):
        col = (C_FF, C_FB)[direction] + part * gw
        return (lambda: mm(col, col + gw),
                functools.partial(gate_finish, direction=direction, part=part))

    def mgate_stage(j):
        c0 = C_GATES + j * sw
        return (lambda: mm(c0, c0 + sw), functools.partial(mgate_finish, j=j))

    v_stage = (lambda: mm(C_INP, C_INP + HG_WIDTH), v_finish)
    kv_stage = (lambda: mm(C_K, C_K + 2 * KV_WIDTH), kv_finish)
    knorm_stage = (lambda: _dot(keep["k_split"], ones_blk2), knorm_finish)
    if latent:
        stages = [(lambda: mm(C_QHG, C_QHG + HG_WIDTH), qhg_finish)]
        for j in range(4):
            stages += [gate_stage(j // 2, j % 2), mgate_stage(j)]
        stages += [(lambda: mm(C_Q, C_Q + ATT_WIDTH), q_finish),
                   v_stage,
                   (lambda: [_dot(t, ones_blk2) for t in keep["q_split"]], qnorm_finish),
                   kv_stage,
                   (lambda: mm(C_GHG, C_GHG + HG_WIDTH), g_finish),
                   knorm_stage]
    else:
        stages = [gate_stage(0, 0), gate_stage(0, 1), gate_stage(1, 0), kv_stage, gate_stage(1, 1),
                  knorm_stage, v_stage]

    raw = stages[0][0]()
    for i, (_, finish) in enumerate(stages):
        nxt = stages[i + 1][0]() if i + 1 < len(stages) else None
        finish(raw)
        raw = nxt


def _inproj_call(x, mod3, nw, w_in, lbl, qw, kw, cos_t, sin_t, latent):
    b_, s_, _ = x.shape
    tm = min(INPROJ_TILE, s_)
    grid = (b_, s_ // tm)
    tok = lambda w: pl.BlockSpec((1, tm, w), lambda b, i: (b, i, 0))
    mod_map = (lambda b, i: (b, 0, 0)) if latent else (lambda b, i: (0, 0, 0))
    ncols = IN_COLS if latent else CTX_COLS
    in_specs = [tok(D_MODEL),
                pl.BlockSpec((1, 1, mod3.shape[2]), mod_map),
                _const_spec((1, D_MODEL)),
                pl.BlockSpec((D_MODEL, ncols), lambda b, i: (0, 0), pipeline_mode=pl.Buffered(1)),
                _const_spec((4, HG_WIDTH)),
                _const_spec((1, LANES)),
                _const_spec((1, LANES)),
                pl.BlockSpec((tm, LANES), lambda b, i: (i, 0)),
                pl.BlockSpec((tm, LANES), lambda b, i: (i, 0))]
    bf = lambda w: jax.ShapeDtypeStruct((b_, s_, w), BF16)
    dec_shape = jax.ShapeDtypeStruct((b_, s_ // CHUNK, 2 * HG_WIDTH), F32)
    dec_spec = pl.BlockSpec((1, tm // CHUNK, 2 * HG_WIDTH), lambda b, i: (b, i, 0))
    vt_shape = jax.ShapeDtypeStruct((b_, s_ // LANES, HG_WIDTH, LANES), BF16)
    vt_spec = pl.BlockSpec((1, tm // LANES, HG_WIDTH, LANES), lambda b, i: (b, i, 0, 0))
    katt_shape = jax.ShapeDtypeStruct((b_, s_ // LANES, KV_WIDTH, LANES), BF16)
    katt_spec = pl.BlockSpec((1, tm // LANES, KV_WIDTH, LANES), lambda b, i: (b, i, 0, 0))
    if latent:
        out_shape = [bf(HG_WIDTH)] * 4 + [dec_shape, vt_shape, bf(HG_WIDTH), bf(HG_WIDTH),
                                         bf(ATT_HEADS * LANES), katt_shape, bf(KV_WIDTH), bf(2 * D_MODEL)]
        out_specs = [tok(HG_WIDTH)] * 4 + [dec_spec, vt_spec, tok(HG_WIDTH), tok(HG_WIDTH),
                                          tok(ATT_HEADS * LANES), katt_spec, tok(KV_WIDTH), tok(2 * D_MODEL)]
    else:
        out_shape = [bf(HG_WIDTH)] * 2 + [dec_shape, vt_shape, katt_shape, bf(KV_WIDTH)]
        out_specs = [tok(HG_WIDTH)] * 2 + [dec_spec, vt_spec, katt_spec, tok(KV_WIDTH)]
    return pl.pallas_call(
        functools.partial(_inproj_kernel, latent=latent),
        grid=grid, in_specs=in_specs, out_specs=out_specs, out_shape=out_shape,
        compiler_params=pltpu.CompilerParams(dimension_semantics=("parallel", "parallel"),
                                             vmem_limit_bytes=VMEM_LIMIT),
        name="inproj_latent" if latent else "inproj_ctx",
    )(x, mod3, nw, w_in, lbl, qw, kw, cos_t, sin_t)


def _cast_side_job(w_refs, wb_refs):
    for w_ref, wb_ref in zip(w_refs, wb_refs):
        wb_ref[...] = w_ref[...].astype(BF16)


def _side_job_specs(weights, n_steps, step_of):
    specs = [pl.BlockSpec((wt.shape[0] // n_steps, wt.shape[1]), lambda *ids: (step_of(*ids), 0))
             for wt in weights]
    shapes = [jax.ShapeDtypeStruct(wt.shape, BF16) for wt in weights]
    return specs, shapes


def _hgrn_kernel(qdf_ref, kdf_ref, qdb_ref, kdb_ref, v_ref, vt_ref,
                 decf_ref, decb_ref, ckd2f_ref, ckd2b_ref, cvt_ref, cdecf_ref, cdecb_ref,
                 g_ref, nw_ref, *rest):
    n_w = (len(rest) - 2) // 2
    w_refs, y_ref, wb_refs, o_acc = rest[:n_w], rest[n_w], rest[n_w + 1:2 * n_w + 1], rest[-1]
    _cast_side_job(w_refs, wb_refs)

    seq = v_ref.shape[1]
    ctx_len = ckd2f_ref.shape[1]
    heads = v_ref.shape[2] // HG_DIM
    tile = HG_TILE
    cpt = tile // CHUNK
    cps = LANES // CHUNK
    spt = tile // LANES
    n_tiles = seq // tile
    half = n_tiles // 2

    row_chunk = lax.broadcasted_iota(jnp.int32, (LANES, HG_DIM), 0) // CHUNK
    ti = lax.broadcasted_iota(jnp.int32, (tile, tile), 0)
    tj = lax.broadcasted_iota(jnp.int32, (tile, tile), 1)
    same = (ti // CHUNK) == (tj // CHUNK)
    mask_f = same & (tj <= ti)
    mask_b = same & (tj >= ti)

    def chunk_update(vt_slab, k_slab, n_in_slab):
        k_m = jnp.where(row_chunk == n_in_slab, k_slab, jnp.zeros_like(k_slab))
        return _dot(vt_slab, k_m)

    def ctx_state(ckd2_ref, cdec_ref, cols, reverse):
        st = jnp.zeros((HG_DIM, HG_DIM), F32)
        order = range(ctx_len // CHUNK)
        for n in (reversed(order) if reverse else order):
            slab = n // cps
            st = cdec_ref[0, n:n + 1, cols] * st + chunk_update(
                cvt_ref[0, slab, cols, :], ckd2_ref[0, slab * LANES:(slab + 1) * LANES, cols], n % cps)
        return st

    def emit(o_t, cols, rows, finalize):
        if finalize:
            o_t = o_t + o_acc[rows, cols]
            ms = jnp.mean(o_t * o_t, axis=-1, keepdims=True)
            y = o_t * lax.rsqrt(ms + EPS) * nw_ref[...] * g_ref[0, rows, cols].astype(F32)
            y_ref[0, rows, cols] = y.astype(BF16)
        else:
            o_acc[rows, cols] = o_t

    def body(it, states, finalize):
        chains = []
        for h in range(heads):
            cols = slice(h * HG_DIM, (h + 1) * HG_DIM)
            chains.append((qdf_ref, kdf_ref, decf_ref, cols, it, False))
            chains.append((qdb_ref, kdb_ref, decb_ref, cols, n_tiles - 1 - it, True))
        n_ch = len(chains)
        rows = [pl.ds(pl.multiple_of(c[4] * tile, tile), tile) for c in chains]
        orders = [list(reversed(range(cpt))) if c[5] else list(range(cpt)) for c in chains]
        qd, sc, ups = [], [], []
        for (qd_ref, kd_ref, _, cols, t, reverse), r in zip(chains, rows):
            qd.append(qd_ref[0, r, cols])
            kd = kd_ref[0, r, cols]
            sc.append(_dot_nt(qd[-1], kd))
            ups.append([chunk_update(vt_ref[0, t * spt + n // cps, cols, :],
                                     kd[(n // cps) * LANES:(n // cps + 1) * LANES, :], n % cps)
                        for n in range(cpt)])
        o = []
        for c, r, s in zip(chains, rows, sc):
            p = jnp.where(mask_b if c[5] else mask_f, s, 0.0).astype(BF16)
            o.append(_dot(p, v_ref[0, r, c[3]]))
        starts, new_states = [], []
        for i, c in enumerate(chains):
            dec = c[2][0, pl.ds(pl.multiple_of(c[4] * cpt, cpt), cpt), c[3]]
            st = states[i]
            start = [None] * cpt
            for n in orders[i]:
                start[n] = st.T.astype(BF16)
                st = dec[n:n + 1, :] * (st + ups[i][n])
            starts.append(start)
            new_states.append(st)
        for i, c in enumerate(chains):
            outs = [o[i][n * CHUNK:(n + 1) * CHUNK, :]
                    + _dot(qd[i][n * CHUNK:(n + 1) * CHUNK, :], starts[i][n]) for n in range(cpt)]
            emit(jnp.concatenate(outs, axis=0), c[3], rows[i], finalize)
        return tuple(new_states)

    states = []
    for h in range(heads):
        cols = slice(h * HG_DIM, (h + 1) * HG_DIM)
        states += [ctx_state(ckd2f_ref, cdecf_ref, cols, False), ctx_state(ckd2b_ref, cdecb_ref, cols, True)]
    states = lax.fori_loop(0, half, functools.partial(body, finalize=False), tuple(states))
    lax.fori_loop(half, n_tiles, functools.partial(body, finalize=True), states)


def _hgrn_call(qdf, kdf, qdb, kdb, v, vt, dec, ckd2f, ckd2b, cvt, cdec, g, nw, weights):
    b_, s_, _ = v.shape
    l_ = ckd2f.shape[1]
    w = HG_HEADS_PER_STEP * HG_DIM
    groups = HG_HEADS // HG_HEADS_PER_STEP
    seq = lambda n: pl.BlockSpec((1, n, w), lambda b, h: (b, 0, h))
    vt_spec = lambda n: pl.BlockSpec((1, n // LANES, w, LANES), lambda b, h: (b, 0, h, 0))
    dec_f = lambda n: pl.BlockSpec((1, n // CHUNK, w), lambda b, h: (b, 0, h))
    dec_b = lambda n: pl.BlockSpec((1, n // CHUNK, w), lambda b, h: (b, 0, groups + h))
    w_specs, w_shapes = _side_job_specs(weights, b_ * groups, lambda b, h: b * groups + h)
    outs = pl.pallas_call(
        _hgrn_kernel,
        grid=(b_, groups),
        in_specs=[seq(s_)] * 5 + [vt_spec(s_), dec_f(s_), dec_b(s_),
                                  seq(l_), seq(l_), vt_spec(l_), dec_f(l_), dec_b(l_),
                                  seq(s_), _const_spec((1, HG_DIM))] + w_specs,
        out_specs=[seq(s_)] + w_specs,
        out_shape=[jax.ShapeDtypeStruct((b_, s_, HG_WIDTH), BF16)] + w_shapes,
        scratch_shapes=[pltpu.VMEM((s_, w), F32)],
        compiler_params=pltpu.CompilerParams(dimension_semantics=("parallel", "parallel"),
                                             vmem_limit_bytes=VMEM_LIMIT),
        name="hgrn_scan",
    )(qdf, kdf, qdb, kdb, v, vt, dec, dec, ckd2f, ckd2b, cvt, cdec, cdec, g, nw, *weights)
    return outs[0], outs[1:]


def _attn_kernel(sink_ref, q_ref, kt_ref, kc_ref, v_ref, vc_ref, *rest):
    n_w = (len(rest) - 1) // 2
    y_ref = rest[n_w]
    _cast_side_job(rest[:n_w], rest[n_w + 1:])
    nb = kt_ref.shape[1]
    blk = ATT_BLOCK
    hpg = ATT_GROUP_HEADS
    n_groups = ATT_HEADS // hpg
    qry_r = lax.broadcasted_iota(jnp.int32, (blk, blk), 0)
    key_c = lax.broadcasted_iota(jnp.int32, (blk, blk), 1)
    left = lax.broadcasted_iota(jnp.int32, (blk, LANES), 1) < HEAD_DIM
    ctx_keys = [kc_ref[0, c] for c in range(kc_ref.shape[1])]
    keep = {}

    def band(u):
        qb = pl.program_id(1) * ATT_STEP_BLOCKS + u
        ids = (jnp.maximum(qb - 1, 0), qb, jnp.minimum(qb + 1, nb - 1))
        keys_t = jnp.concatenate([kt_ref[0, j] for j in ids] + ctx_keys, axis=1)
        values = jnp.concatenate([v_ref[0, pl.ds(pl.multiple_of(j * blk, blk), blk), :] for j in ids]
                                 + [vc_ref[0]], axis=0)
        return qb, keys_t, values

    def score_issue(u, g):
        if g == 0:
            keep["band", u] = band(u)
        q = jnp.concatenate([q_ref[0, u * blk:(u + 1) * blk, h * LANES:(h + 1) * LANES]
                             for h in range(g * hpg, (g + 1) * hpg)], axis=0)
        return _dot(q, keep["band", u][1])

    def score_finish(raw, u, g):
        qb = keep["band", u][0]
        mask_prev = (key_c >= qry_r) & (qb > 0)
        mask_next = (key_c <= qry_r) & (qb < nb - 1)
        probs, inv = [], []
        for n in range(hpg):
            s = raw[n * blk:(n + 1) * blk, :]
            sink = sink_ref[g * hpg + n] * LOG2E
            s = jnp.concatenate([jnp.where(mask_prev, s[:, 0:blk], NEG), s[:, blk:2 * blk],
                                 jnp.where(mask_next, s[:, 2 * blk:3 * blk], NEG), s[:, 3 * blk:]], axis=1)
            mx = jnp.maximum(jnp.max(s, axis=1, keepdims=True), sink)
            p = jnp.exp2(s - mx)
            den = jnp.sum(p, axis=1, keepdims=True) + jnp.exp2(sink - mx)
            probs.append(p.astype(BF16))
            inv.append(1.0 / den)
        keep["p", u, g] = jnp.concatenate(probs, axis=0)
        keep["inv", u, g] = inv

    def value_issue(u, g):
        return _dot(keep["p", u, g], keep["band", u][2])

    def value_finish(raw, u, g):
        for n in range(0, hpg, 2):
            h = g * hpg + n
            even = raw[n * blk:(n + 1) * blk, :] * keep["inv", u, g][n]
            odd = raw[(n + 1) * blk:(n + 2) * blk, :] * keep["inv", u, g][n + 1]
            if h < ATT_HEADS // ATT_KV_HEADS:
                y = jnp.where(left, even, pltpu.roll(odd, HEAD_DIM, 1))
            else:
                y = jnp.where(left, pltpu.roll(even, HEAD_DIM, 1), odd)
            y_ref[0, u * blk:(u + 1) * blk, (h // 2) * LANES:(h // 2 + 1) * LANES] = y.astype(BF16)

    part = functools.partial
    work = [(u, g) for u in range(ATT_STEP_BLOCKS) for g in range(n_groups)]
    stages = [(part(score_issue, *work[0]), part(score_finish, u=work[0][0], g=work[0][1]))]
    for n, (u, g) in enumerate(work):
        if n + 1 < len(work):
            un, gn = work[n + 1]
            stages.append((part(score_issue, un, gn), part(score_finish, u=un, g=gn)))
        stages.append((part(value_issue, u, g), part(value_finish, u=u, g=g)))
    raw = stages[0][0]()
    for n, (_, finish) in enumerate(stages):
        nxt = stages[n + 1][0]() if n + 1 < len(stages) else None
        finish(raw)
        raw = nxt


def _attn_call(sinks, q, katt, vatt, ckatt, cvatt, weights):
    b_, s_, qw = q.shape
    l_ = cvatt.shape[1]
    rows = ATT_STEP_BLOCKS * ATT_BLOCK
    batch4 = lambda n: pl.BlockSpec((1, n, KV_WIDTH, LANES), lambda b, i: (b, 0, 0, 0))
    batch3 = lambda n: pl.BlockSpec((1, n, KV_WIDTH), lambda b, i: (b, 0, 0))
    steps = s_ // rows
    w_specs, w_shapes = _side_job_specs(weights, b_ * steps, lambda b, i: b * steps + i)
    outs = pl.pallas_call(
        _attn_kernel,
        grid=(b_, steps),
        in_specs=[pl.BlockSpec(memory_space=pltpu.SMEM),
                  pl.BlockSpec((1, rows, qw), lambda b, i: (b, i, 0)),
                  batch4(s_ // LANES), batch4(l_ // LANES), batch3(s_), batch3(l_)] + w_specs,
        out_specs=[pl.BlockSpec((1, rows, ATT_WIDTH), lambda b, i: (b, i, 0))] + w_specs,
        out_shape=[jax.ShapeDtypeStruct((b_, s_, ATT_WIDTH), BF16)] + w_shapes,
        compiler_params=pltpu.CompilerParams(dimension_semantics=("parallel", "arbitrary"),
                                             vmem_limit_bytes=VMEM_LIMIT),
        name="window_attn",
    )(sinks, q, katt, ckatt, vatt, cvatt, *weights)
    return outs[0], outs[1:]


def _merge_ffn_kernel(x_ref, yh_ref, ya_ref, gates_ref, mod_ref, nw_ref, wbh_ref, wba_ref, wo_ref,
                      wg_ref, wu_ref, wd_ref, o_ref):
    tm = x_ref.shape[1]
    n_sub = FFN_SUBTILES
    sub = tm // n_sub
    bounds = FFN_CHUNK_BOUNDS
    n_chunks = len(bounds) - 1
    g1 = mod_ref[0, :, 2 * D_MODEL:3 * D_MODEL]
    sh2 = mod_ref[0, :, 3 * D_MODEL:4 * D_MODEL]
    sc2 = mod_ref[0, :, 4 * D_MODEL:5 * D_MODEL]
    g2 = mod_ref[0, :, 5 * D_MODEL:6 * D_MODEL]
    keep = {}

    def branch_issue(r):
        rows = slice(r * sub, (r + 1) * sub)
        return _dot(yh_ref[0, rows, :], wbh_ref[...]), _dot(ya_ref[0, rows, :], wba_ref[...])

    def branch_finish(raw, r):
        rows = slice(r * sub, (r + 1) * sub)
        a, b = raw
        mixed = (gates_ref[0, rows, 0:D_MODEL].astype(F32) * a
                 + gates_ref[0, rows, D_MODEL:2 * D_MODEL].astype(F32) * b)
        keep["mixed", r] = (0.5 * mixed).astype(BF16)

    def out_issue(r):
        return _dot(keep["mixed", r], wo_ref[...])

    def out_finish(raw, r):
        x1 = x_ref[0, r * sub:(r + 1) * sub, :] + g1 * raw
        ms = jnp.mean(x1 * x1, axis=-1, keepdims=True)
        keep["x1", r] = x1
        keep["h2", r] = ((x1 * lax.rsqrt(ms + EPS) * nw_ref[...]) * (1.0 + sc2) + sh2).astype(BF16)

    def hidden_issue(r, c):
        h2 = keep["h2", r]
        cols = slice(bounds[c], bounds[c + 1])
        return _dot(h2, wg_ref[:, cols]), _dot(h2, wu_ref[:, cols])

    def hidden_finish(raw, r, c):
        gate, up = raw
        keep["act", r, c] = (gate * _sigmoid(gate) * up).astype(BF16)

    def down_issue(r, c):
        return _dot(keep["act", r, c], wd_ref[bounds[c]:bounds[c + 1], :])

    def down_finish(raw, r, c):
        dn = raw if c == 0 else keep["dn", r] + raw
        if c + 1 < n_chunks:
            keep["dn", r] = dn
        else:
            o_ref[0, r * sub:(r + 1) * sub, :] = keep["x1", r] + g2 * dn

    part = functools.partial
    stages = []
    for issue, finish in ((branch_issue, branch_finish), (out_issue, out_finish)):
        stages += [(part(issue, r), part(finish, r=r)) for r in range(n_sub)]
    for c in range(n_chunks):
        stages += [(part(hidden_issue, r, c), part(hidden_finish, r=r, c=c)) for r in range(n_sub)]
        stages += [(part(down_issue, r, c), part(down_finish, r=r, c=c)) for r in range(n_sub)]
    raw = stages[0][0]()
    for n, (_, finish) in enumerate(stages):
        nxt = stages[n + 1][0]() if n + 1 < len(stages) else None
        finish(raw)
        raw = nxt


def _merge_ffn_call(x, yh, ya, gates, mod3, nw, wbh, wba, wo, wg, wu, wd):
    b_, s_, _ = x.shape
    tm = TOK_TILE
    tok = lambda w: pl.BlockSpec((1, tm, w), lambda b, i: (b, i, 0))
    return pl.pallas_call(
        _merge_ffn_kernel,
        grid=(b_, s_ // tm),
        in_specs=[tok(D_MODEL), tok(HG_WIDTH), tok(ATT_WIDTH), tok(2 * D_MODEL),
                  pl.BlockSpec((1, 1, mod3.shape[2]), lambda b, i: (b, 0, 0)),
                  _const_spec((1, D_MODEL)),
                  _const_spec(wbh.shape), _const_spec(wba.shape), _const_spec(wo.shape),
                  _const_spec(wg.shape), _const_spec(wu.shape), _const_spec(wd.shape)],
        out_specs=tok(D_MODEL),
        out_shape=jax.ShapeDtypeStruct(x.shape, F32),
        compiler_params=pltpu.CompilerParams(dimension_semantics=("parallel", "parallel"),
                                             vmem_limit_bytes=VMEM_LIMIT),
        name="merge_ffn",
    )(x, yh, ya, gates, mod3, nw, wbh, wba, wo, wg, wu, wd)


def _rope_tables(n_tok):
    t = np.arange(n_tok)
    rows = (t // GRID_W).astype(np.float64)
    cols = (t % GRID_W).astype(np.float64)
    half = HEAD_DIM // 2
    inv_freq = ROPE_THETA ** (-np.arange(0, half, 2, dtype=np.float64) / half)
    d = np.arange(LANES) % HEAD_DIM
    pos = np.where((d < half)[None, :], rows[:, None], cols[:, None])
    ang = pos * inv_freq[(d % half) % (half // 2)][None, :]
    sign = np.where((d % half) < half // 2, -1.0, 1.0)[None, :]
    return jnp.asarray(np.cos(ang), F32), jnp.asarray(np.sin(ang) * sign, F32)


def kernel(x, c, ctx, c_ctx, w_ada, b_ada, norm_mix_w, norm_ffn_w, w_in, hgrn_lb_logits, hgrn_norm_w,
           q_norm_w, k_norm_w, attn_sinks, w_branch_hgrn, w_branch_attn, w_out, w_ffn_gate, w_ffn_up,
           w_ffn_down):
    b_, s_, _ = x.shape
    layer = 0
    pad = 16 - (b_ + 1)
    cc = jnp.concatenate([c, c_ctx[None, :], jnp.zeros((pad, D_MODEL), F32)], axis=0)
    mod = _ada_call(cc, w_ada[layer], b_ada[layer][None, :])
    mod_lat = mod[:b_].reshape(b_, 1, 6 * D_MODEL)
    mod_ctx = mod[b_:b_ + 1].reshape(1, 1, 6 * D_MODEL)

    col = np.ones((IN_COLS,), np.float32)
    for lo, width in ((C_FF, 2 * HG_WIDTH), (C_QHG, 2 * HG_WIDTH), (C_GATES, 2 * D_MODEL)):
        col[lo:lo + width] = 0.5
    w_in_b = (w_in[layer] * col[None, :]).astype(BF16)
    lbl = hgrn_lb_logits[:, 0:2, :].reshape(4, HG_WIDTH)
    qw = jnp.tile(q_norm_w[layer], 2)[None, :]
    kw = jnp.tile(k_norm_w[layer], 2)[None, :]
    nw_mix = norm_mix_w[layer][None, :]
    cos_t, sin_t = _rope_tables(s_)

    (qdf, kdf, qdb, kdb, dec, vt, v, g, q, katt, vatt, gates) = _inproj_call(
        x, mod_lat, nw_mix, w_in_b, lbl, qw, kw, cos_t, sin_t, latent=True)
    ckd2f, ckd2b, cdec, cvt, ckatt, cvatt = _inproj_call(
        ctx, mod_ctx, nw_mix, w_in_b, lbl, qw, kw, cos_t, sin_t, latent=False)

    y_hg, (w_bh, w_ba, w_o, w_d) = _hgrn_call(
        qdf, kdf, qdb, kdb, v, vt, dec, ckd2f, ckd2b, cvt, cdec, g, hgrn_norm_w[layer][None, :],
        (w_branch_hgrn[layer], w_branch_attn[layer], w_out[layer], w_ffn_down[layer]))
    y_at, (w_g, w_u) = _attn_call(attn_sinks[layer], q, katt, vatt, ckatt, cvatt,
                                  (w_ffn_gate[layer], w_ffn_up[layer]))

    return _merge_ffn_call(x, y_hg, y_at, gates, mod_lat, norm_ffn_w[layer][None, :],
                           w_bh, w_ba, w_o, w_g, w_u, w_d)
```

```python
import functools

import jax
import jax.numpy as jnp
import numpy as np
from jax import lax
from jax.experimental import pallas as pl
from jax.experimental.pallas import tpu as pltpu

F32 = jnp.float32
BF16 = jnp.bfloat16

D_MODEL = 1024
GRID_W = 64
EPS = 1e-6
HG_HEADS = 4
HG_DIM = 128
HG_WIDTH = HG_HEADS * HG_DIM
CHUNK = 32
ATT_HEADS = 8
ATT_KV_HEADS = 2
HEAD_DIM = 64
ATT_WIDTH = ATT_HEADS * HEAD_DIM
KV_WIDTH = ATT_KV_HEADS * HEAD_DIM
WINDOW = 128
ROPE_THETA = 10000.0
D_FF = 2816
CTX_COLS = 3 * HG_WIDTH + 2 * KV_WIDTH
IN_COLS = CTX_COLS + 2 * HG_WIDTH + ATT_WIDTH + 2 * D_MODEL

C_FF, C_FB, C_INP, C_K, C_V = 0, HG_WIDTH, 2 * HG_WIDTH, 3 * HG_WIDTH, 3 * HG_WIDTH + KV_WIDTH
C_QHG = CTX_COLS
C_GHG = C_QHG + HG_WIDTH
C_Q = C_GHG + HG_WIDTH
C_GATES = C_Q + ATT_WIDTH

LANES = 128
TOK_TILE = 512
FFN_SUBTILES = 2
MXU_DEPTH = 256
FFN_CHUNK_BOUNDS = (0, 6 * MXU_DEPTH, D_FF)
INPROJ_TILE = 512
HG_TILE = 256
HG_HEADS_PER_STEP = 4
ATT_BLOCK = 128
ATT_GROUP_HEADS = 2
ATT_STEP_BLOCKS = 8
VMEM_LIMIT = 56 * 1024 * 1024
NEG = -1e30
LOG2E = 1.4426950408889634


def _dot(a, b):
    return jnp.dot(a, b, preferred_element_type=F32)


def _dot_nt(a, b):
    return lax.dot_general(a, b, (((1,), (1,)), ((), ())), preferred_element_type=F32)


def _sigmoid(x):
    return 0.5 * jnp.tanh(0.5 * x) + 0.5


def _const_spec(shape):
    n = len(shape)
    return pl.BlockSpec(shape, lambda *_: (0,) * n, pipeline_mode=pl.Buffered(1))


def _ada_kernel(c_ref, w_ref, b_ref, o_ref):
    c = c_ref[...]
    s = (c * _sigmoid(c)).astype(BF16)
    o_ref[...] = _dot(s, w_ref[...].astype(BF16)) + b_ref[...]


def _ada_call(cc, w_ada, b_ada):
    rows = cc.shape[0]
    n_out = w_ada.shape[1]
    bn = 1024
    return pl.pallas_call(
        _ada_kernel,
        grid=(n_out // bn,),
        in_specs=[pl.BlockSpec((rows, D_MODEL), lambda j: (0, 0)),
                  pl.BlockSpec((D_MODEL, bn), lambda j: (0, j)),
                  pl.BlockSpec((1, bn), lambda j: (0, j))],
        out_specs=pl.BlockSpec((rows, bn), lambda j: (0, j)),
        out_shape=jax.ShapeDtypeStruct((rows, n_out), F32),
        compiler_params=pltpu.CompilerParams(dimension_semantics=("arbitrary",),
                                             vmem_limit_bytes=VMEM_LIMIT),
        name="ada_mod",
    )(cc, w_ada, b_ada)


def _chunk_cumsum(x, reverse):
    n, c = x.shape
    sub = 8
    r = lax.broadcasted_iota(jnp.int32, x.shape, 0) & (CHUNK - 1)
    s = 1
    while s < sub:
        if reverse:
            x = x + jnp.where(r < CHUNK - s, pltpu.roll(x, n - s, 0), 0.0)
        else:
            x = x + jnp.where(r >= s, pltpu.roll(x, s, 0), 0.0)
        s *= 2
    per = CHUNK // sub
    x4 = x.reshape(n // CHUNK, per, sub, c)
    s = 1
    while s < per:
        if reverse:
            x4 = jnp.concatenate([x4[:, :per - s] + x4[:, s:], x4[:, per - s:]], axis=1)
        else:
            x4 = jnp.concatenate([x4[:, :s], x4[:, s:] + x4[:, :per - s]], axis=1)
        s *= 2
    return x4.reshape(n, c)


def _rope(t, cos, sin_signed, lane):
    partner = jnp.where((lane & 31) < 16, pltpu.roll(t, LANES - 16, 1), pltpu.roll(t, 16, 1))
    return t * cos + partner * sin_signed


def _inproj_kernel(x_ref, mod_ref, nw_ref, w_ref, lbl_ref, qw_ref, kw_ref, cos_ref, sin_ref,
                   *out_refs, latent):
    if latent:
        (qdf_ref, kdf_ref, qdb_ref, kdb_ref, dec_ref, vt_ref, v_ref,
         g_ref, q_ref, katt_ref, vatt_ref, gates_ref) = out_refs
    else:
        kd2f_ref, kd2b_ref, dec_ref, vt_ref, katt_ref, vatt_ref = out_refs

    tm = x_ref.shape[1]
    nchunk = tm // CHUNK
    x = x_ref[0]
    sh = mod_ref[0, :, 0:D_MODEL]
    sc = mod_ref[0, :, D_MODEL:2 * D_MODEL]
    ms = jnp.mean(x * x, axis=-1, keepdims=True)
    h = x * lax.rsqrt(ms + EPS) * (nw_ref[...] * (1.0 + sc)) + sh
    hb = h.astype(BF16)

    def mm(a, b):
        return _dot(hb, w_ref[:, a:b])

    l0f, l1f = lbl_ref[0:1, :], lbl_ref[1:2, :]
    l0b, l1b = lbl_ref[2:3, :], lbl_ref[3:4, :]
    lb_f = 1.0 / (1.0 + jnp.exp(l1f - l0f))
    lb_b = 1.0 / (1.0 + jnp.exp(l1b - l0b))

    gw = HG_WIDTH // 2
    sw = D_MODEL // 2
    lane = lax.broadcasted_iota(jnp.int32, (tm, LANES), 1)
    left = lane < HEAD_DIM
    bi = lax.broadcasted_iota(jnp.int32, (2 * LANES, LANES), 0) // HEAD_DIM
    bj = lax.broadcasted_iota(jnp.int32, (2 * LANES, LANES), 1) // HEAD_DIM
    ones_blk2 = jnp.where((bi % 2) == bj, 1.0, 0.0).astype(BF16)
    keep = {}

    def qhg_finish(raw):
        keep["q_hg"] = raw * (jnp.tanh(raw) + 1.0) * (HG_DIM ** -0.5)

    def gate_finish(raw, direction, part):
        reverse = direction == 1
        lo = part * gw
        lb = (lb_f, lb_b)[direction][:, lo:lo + gw]
        c_half = 0.5 - 0.5 * lb
        ct = c_half * jnp.tanh(raw)
        f = (0.5 + 0.5 * lb) + ct
        k = c_half - ct
        cum = _chunk_cumsum(jnp.log(f), reverse)
        cum3 = cum.reshape(nchunk, CHUNK, gw)
        tot3 = cum3[:, 0:1, :] if reverse else cum3[:, CHUNK - 1:CHUNK, :]
        dec3 = jnp.exp(tot3)
        dec_ref[0, :, direction * HG_WIDTH + lo:direction * HG_WIDTH + lo + gw] = dec3.reshape(nchunk, gw)
        if latent:
            e_pos = jnp.exp(cum)
            (qdf_ref, qdb_ref)[direction][0, :, lo:lo + gw] = (keep["q_hg"][:, lo:lo + gw] * e_pos).astype(BF16)
            (kdf_ref, kdb_ref)[direction][0, :, lo:lo + gw] = (k * (1.0 / e_pos)).astype(BF16)
        else:
            kd2 = k.reshape(nchunk, CHUNK, gw) * jnp.exp(tot3 - cum3)
            (kd2f_ref, kd2b_ref)[direction][0, :, lo:lo + gw] = kd2.reshape(tm, gw).astype(BF16)

    def mgate_finish(raw, j):
        gates_ref[0, :, j * sw:(j + 1) * sw] = (jnp.tanh(raw) + 1.0).astype(BF16)

    def v_finish(raw):
        for s in range(tm // LANES):
            vt_ref[0, s] = raw[s * LANES:(s + 1) * LANES, :].T.astype(BF16)
        if latent:
            v_ref[0] = raw.astype(BF16)

    def g_finish(raw):
        g_ref[0] = (raw * (jnp.tanh(raw) + 1.0)).astype(BF16)

    def split_sq(t):
        sq = t * t
        hi = sq.astype(BF16)
        lo = (sq - hi.astype(F32)).astype(BF16)
        return jnp.concatenate([hi, lo], axis=1)

    def kv_finish(raw):
        keep["k_raw"] = raw[:, 0:LANES]
        keep["k_split"] = split_sq(keep["k_raw"])
        vatt_ref[0] = raw[:, LANES:2 * LANES].astype(BF16)

    def knorm_finish(ss):
        kn = keep["k_raw"] * lax.rsqrt(ss * (1.0 / HEAD_DIM) + EPS) * kw_ref[...]
        if latent:
            kn = _rope(kn, cos_ref[...], sin_ref[...], lane)
        for s in range(tm // LANES):
            katt_ref[0, s] = kn[s * LANES:(s + 1) * LANES, :].T.astype(BF16)

    def q_finish(raw):
        keep["q_raw"] = raw
        keep["q_split"] = [split_sq(raw[:, s * LANES:(s + 1) * LANES]) for s in range(ATT_WIDTH // LANES)]

    def qnorm_finish(ss_list):
        for s, ss in enumerate(ss_list):
            qraw = keep["q_raw"][:, s * LANES:(s + 1) * LANES]
            qn = qraw * lax.rsqrt(ss * (1.0 / HEAD_DIM) + EPS) * qw_ref[...]
            qn = _rope(qn, cos_ref[...], sin_ref[...], lane) * (HEAD_DIM ** -0.5 * LOG2E)
            swapped = pltpu.roll(qn, HEAD_DIM, 1)
            if s < ATT_WIDTH // LANES // 2:
                even, odd = jnp.where(left, qn, 0.0), jnp.where(left, swapped, 0.0)
            else:
                even, odd = jnp.where(left, 0.0, swapped), jnp.where(left, 0.0, qn)
            q_ref[0, :, (2 * s) * LANES:(2 * s + 1) * LANES] = even.astype(BF16)
            q_ref[0, :, (2 * s + 1) * LANES:(2 * s + 2) * LANES] = odd.astype(BF16)

    def gate_stage(direction, part):
        col = (C_FF, C_FB)[direction] + part * gw
        return (lambda: mm(col, col + gw),
                functools.partial(gate_finish, direction=direction, part=part))

    def mgate_stage(j):
        c0 = C_GATES + j * sw
        return (lambda: mm(c0, c0 + sw), functools.partial(mgate_finish, j=j))

    v_stage = (lambda: mm(C_INP, C_INP + HG_WIDTH), v_finish)
    kv_stage = (lambda: mm(C_K, C_K + 2 * KV_WIDTH), kv_finish)
    knorm_stage = (lambda: _dot(keep["k_split"], ones_blk2), knorm_finish)
    if latent:
        stages = [(lambda: mm(C_QHG, C_QHG + HG_WIDTH), qhg_finish)]
        for j in range(4):
            stages += [gate_stage(j // 2, j % 2), mgate_stage(j)]
        stages += [(lambda: mm(C_Q, C_Q + ATT_WIDTH), q_finish),
                   v_stage,
                   (lambda: [_dot(t, ones_blk2) for t in keep["q_split"]], qnorm_finish),
                   kv_stage,
                   (lambda: mm(C_GHG, C_GHG + HG_WIDTH), g_finish),
                   knorm_stage]
    else:
        stages = [gate_stage(0, 0), gate_stage(0, 1), gate_stage(1, 0), kv_stage, gate_stage(1, 1),
                  knorm_stage, v_stage]

    raw = stages[0][0]()
    for i, (_, finish) in enumerate(stages):
        nxt = stages[i + 1][0]() if i + 1 < len(stages) else None
        finish(raw)
        raw = nxt


def _inproj_call(x, mod3, nw, w_in, lbl, qw, kw, cos_t, sin_t, latent):
    b_, s_, _ = x.shape
    tm = min(INPROJ_TILE, s_)
    grid = (b_, s_ // tm)
    tok = lambda w: pl.BlockSpec((1, tm, w), lambda b, i: (b, i, 0))
    mod_map = (lambda b, i: (b, 0, 0)) if latent else (lambda b, i: (0, 0, 0))
    ncols = IN_COLS if latent else CTX_COLS
    in_specs = [tok(D_MODEL),
                pl.BlockSpec((1, 1, mod3.shape[2]), mod_map),
                _const_spec((1, D_MODEL)),
                pl.BlockSpec((D_MODEL, ncols), lambda b, i: (0, 0), pipeline_mode=pl.Buffered(1)),
                _const_spec((4, HG_WIDTH)),
                _const_spec((1, LANES)),
                _const_spec((1, LANES)),
                pl.BlockSpec((tm, LANES), lambda b, i: (i, 0)),
                pl.BlockSpec((tm, LANES), lambda b, i: (i, 0))]
    bf = lambda w: jax.ShapeDtypeStruct((b_, s_, w), BF16)
    dec_shape = jax.ShapeDtypeStruct((b_, s_ // CHUNK, 2 * HG_WIDTH), F32)
    dec_spec = pl.BlockSpec((1, tm // CHUNK, 2 * HG_WIDTH), lambda b, i: (b, i, 0))
    vt_shape = jax.ShapeDtypeStruct((b_, s_ // LANES, HG_WIDTH, LANES), BF16)
    vt_spec = pl.BlockSpec((1, tm // LANES, HG_WIDTH, LANES), lambda b, i: (b, i, 0, 0))
    katt_shape = jax.ShapeDtypeStruct((b_, s_ // LANES, KV_WIDTH, LANES), BF16)
    katt_spec = pl.BlockSpec((1, tm // LANES, KV_WIDTH, LANES), lambda b, i: (b, i, 0, 0))
    if latent:
        out_shape = [bf(HG_WIDTH)] * 4 + [dec_shape, vt_shape, bf(HG_WIDTH), bf(HG_WIDTH),
                                         bf(ATT_HEADS * LANES), katt_shape, bf(KV_WIDTH), bf(2 * D_MODEL)]
        out_specs = [tok(HG_WIDTH)] * 4 + [dec_spec, vt_spec, tok(HG_WIDTH), tok(HG_WIDTH),
                                          tok(ATT_HEADS * LANES), katt_spec, tok(KV_WIDTH), tok(2 * D_MODEL)]
    else:
        out_shape = [bf(HG_WIDTH)] * 2 + [dec_shape, vt_shape, katt_shape, bf(KV_WIDTH)]
        out_specs = [tok(HG_WIDTH)] * 2 + [dec_spec, vt_spec, katt_spec, tok(KV_WIDTH)]
    return pl.pallas_call(
        functools.partial(_inproj_kernel, latent=latent),
        grid=grid, in_specs=in_specs, out_specs=out_specs, out_shape=out_shape,
        compiler_params=pltpu.CompilerParams(dimension_semantics=("parallel", "parallel"),
                                             vmem_limit_bytes=VMEM_LIMIT),
        name="inproj_latent" if latent else "inproj_ctx",
    )(x, mod3, nw, w_in, lbl, qw, kw, cos_t, sin_t)


def _cast_side_job(w_refs, wb_refs):
    for w_ref, wb_ref in zip(w_refs, wb_refs):
        wb_ref[...] = w_ref[...].astype(BF16)


def _side_job_specs(weights, n_steps, step_of):
    specs = [pl.BlockSpec((wt.shape[0] // n_steps, wt.shape[1]), lambda *ids: (step_of(*ids), 0))
             for wt in weights]
    shapes = [jax.ShapeDtypeStruct(wt.shape, BF16) for wt in weights]
    return specs, shapes


def _hgrn_kernel(qdf_ref, kdf_ref, qdb_ref, kdb_ref, v_ref, vt_ref,
                 decf_ref, decb_ref, ckd2f_ref, ckd2b_ref, cvt_ref, cdecf_ref, cdecb_ref,
                 g_ref, nw_ref, *rest):
    n_w = (len(rest) - 2) // 2
    w_refs, y_ref, wb_refs, o_acc = rest[:n_w], rest[n_w], rest[n_w + 1:2 * n_w + 1], rest[-1]
    _cast_side_job(w_refs, wb_refs)

    seq = v_ref.shape[1]
    ctx_len = ckd2f_ref.shape[1]
    heads = v_ref.shape[2] // HG_DIM
    tile = HG_TILE
    cpt = tile // CHUNK
    cps = LANES // CHUNK
    spt = tile // LANES
    n_tiles = seq // tile
    half = n_tiles // 2

    row_chunk = lax.broadcasted_iota(jnp.int32, (LANES, HG_DIM), 0) // CHUNK
    ti = lax.broadcasted_iota(jnp.int32, (LANES, LANES), 0)
    tj = lax.broadcasted_iota(jnp.int32, (LANES, LANES), 1)
    same = (ti // CHUNK) == (tj // CHUNK)
    mask_f = same & (tj <= ti)
    mask_b = same & (tj >= ti)

    def chunk_update(vt_slab, k_slab, n_in_slab):
        k_m = jnp.where(row_chunk == n_in_slab, k_slab, jnp.zeros_like(k_slab))
        return _dot(vt_slab, k_m)

    def ctx_state(ckd2_ref, cdec_ref, cols, reverse):
        st = jnp.zeros((HG_DIM, HG_DIM), F32)
        order = range(ctx_len // CHUNK)
        for n in (reversed(order) if reverse else order):
            slab = n // cps
            st = cdec_ref[0, n:n + 1, cols] * st + chunk_update(
                cvt_ref[0, slab, cols, :], ckd2_ref[0, slab * LANES:(slab + 1) * LANES, cols], n % cps)
        return st

    def emit(o_t, cols, rows, finalize):
        if finalize:
            o_t = o_t + o_acc[rows, cols]
            ms = jnp.mean(o_t * o_t, axis=-1, keepdims=True)
            y = o_t * lax.rsqrt(ms + EPS) * nw_ref[...] * g_ref[0, rows, cols].astype(F32)
            y_ref[0, rows, cols] = y.astype(BF16)
        else:
            o_acc[rows, cols] = o_t

    def body(it, states, finalize):
        chains = []
        for h in range(heads):
            cols = slice(h * HG_DIM, (h + 1) * HG_DIM)
            chains.append((qdf_ref, kdf_ref, decf_ref, cols, it, False))
            chains.append((qdb_ref, kdb_ref, decb_ref, cols, n_tiles - 1 - it, True))
        n_ch = len(chains)
        rows = [pl.ds(pl.multiple_of(c[4] * tile, tile), tile) for c in chains]
        orders = [list(reversed(range(cpt))) if c[5] else list(range(cpt)) for c in chains]
        qd, sc, ups = [], [], []
        for (qd_ref, kd_ref, _, cols, t, reverse), r in zip(chains, rows):
            qd.append(qd_ref[0, r, cols])
            kd = kd_ref[0, r, cols]
            sc.append([_dot_nt(qd[-1][s * LANES:(s + 1) * LANES, :], kd[s * LANES:(s + 1) * LANES, :])
                       for s in range(spt)])
            ups.append([chunk_update(vt_ref[0, t * spt + n // cps, cols, :],
                                     kd[(n // cps) * LANES:(n // cps + 1) * LANES, :], n % cps)
                        for n in range(cpt)])
        o = []
        for c, r, s in zip(chains, rows, sc):
            v = v_ref[0, r, c[3]]
            o.append(jnp.concatenate(
                [_dot(jnp.where(mask_b if c[5] else mask_f, s[j], 0.0).astype(BF16),
                      v[j * LANES:(j + 1) * LANES, :]) for j in range(spt)], axis=0))
        starts, new_states = [], []
        for i, c in enumerate(chains):
            dec = c[2][0, pl.ds(pl.multiple_of(c[4] * cpt, cpt), cpt), c[3]]
            st = states[i]
            start = [None] * cpt
            for n in orders[i]:
                start[n] = st.T.astype(BF16)
                st = dec[n:n + 1, :] * (st + ups[i][n])
            starts.append(start)
            new_states.append(st)
        for i, c in enumerate(chains):
            outs = [o[i][n * CHUNK:(n + 1) * CHUNK, :]
                    + _dot(qd[i][n * CHUNK:(n + 1) * CHUNK, :], starts[i][n]) for n in range(cpt)]
            emit(jnp.concatenate(outs, axis=0), c[3], rows[i], finalize)
        return tuple(new_states)

    states = []
    for h in range(heads):
        cols = slice(h * HG_DIM, (h + 1) * HG_DIM)
        states += [ctx_state(ckd2f_ref, cdecf_ref, cols, False), ctx_state(ckd2b_ref, cdecb_ref, cols, True)]
    states = lax.fori_loop(0, half, functools.partial(body, finalize=False), tuple(states))
    lax.fori_loop(half, n_tiles, functools.partial(body, finalize=True), states)


def _hgrn_call(qdf, kdf, qdb, kdb, v, vt, dec, ckd2f, ckd2b, cvt, cdec, g, nw, weights):
    b_, s_, _ = v.shape
    l_ = ckd2f.shape[1]
    w = HG_HEADS_PER_STEP * HG_DIM
    groups = HG_HEADS // HG_HEADS_PER_STEP
    seq = lambda n: pl.BlockSpec((1, n, w), lambda b, h: (b, 0, h))
    vt_spec = lambda n: pl.BlockSpec((1, n // LANES, w, LANES), lambda b, h: (b, 0, h, 0))
    dec_f = lambda n: pl.BlockSpec((1, n // CHUNK, w), lambda b, h: (b, 0, h))
    dec_b = lambda n: pl.BlockSpec((1, n // CHUNK, w), lambda b, h: (b, 0, groups + h))
    w_specs, w_shapes = _side_job_specs(weights, b_ * groups, lambda b, h: b * groups + h)
    outs = pl.pallas_call(
        _hgrn_kernel,
        grid=(b_, groups),
        in_specs=[seq(s_)] * 5 + [vt_spec(s_), dec_f(s_), dec_b(s_),
                                  seq(l_), seq(l_), vt_spec(l_), dec_f(l_), dec_b(l_),
                                  seq(s_), _const_spec((1, HG_DIM))] + w_specs,
        out_specs=[seq(s_)] + w_specs,
        out_shape=[jax.ShapeDtypeStruct((b_, s_, HG_WIDTH), BF16)] + w_shapes,
        scratch_shapes=[pltpu.VMEM((s_, w), F32)],
        compiler_params=pltpu.CompilerParams(dimension_semantics=("parallel", "parallel"),
                                             vmem_limit_bytes=VMEM_LIMIT),
        name="hgrn_scan",
    )(qdf, kdf, qdb, kdb, v, vt, dec, dec, ckd2f, ckd2b, cvt, cdec, cdec, g, nw, *weights)
    return outs[0], outs[1:]


def _attn_kernel(sink_ref, q_ref, kt_ref, kc_ref, v_ref, vc_ref, *rest):
    n_w = (len(rest) - 1) // 2
    y_ref = rest[n_w]
    _cast_side_job(rest[:n_w], rest[n_w + 1:])
    nb = kt_ref.shape[1]
    blk = ATT_BLOCK
    hpg = ATT_GROUP_HEADS
    n_groups = ATT_HEADS // hpg
    qry_r = lax.broadcasted_iota(jnp.int32, (blk, blk), 0)
    key_c = lax.broadcasted_iota(jnp.int32, (blk, blk), 1)
    left = lax.broadcasted_iota(jnp.int32, (blk, LANES), 1) < HEAD_DIM
    ctx_keys = [kc_ref[0, c] for c in range(kc_ref.shape[1])]
    keep = {}

    def band(u):
        qb = pl.program_id(1) * ATT_STEP_BLOCKS + u
        ids = (jnp.maximum(qb - 1, 0), qb, jnp.minimum(qb + 1, nb - 1))
        keys_t = jnp.concatenate([kt_ref[0, j] for j in ids] + ctx_keys, axis=1)
        values = jnp.concatenate([v_ref[0, pl.ds(pl.multiple_of(j * blk, blk), blk), :] for j in ids]
                                 + [vc_ref[0]], axis=0)
        values = jnp.concatenate([values, jnp.ones_like(values)], axis=1)
        return qb, keys_t, values

    def score_issue(u, g):
        if g == 0:
            keep["band", u] = band(u)
        q = jnp.concatenate([q_ref[0, u * blk:(u + 1) * blk, h * LANES:(h + 1) * LANES]
                             for h in range(g * hpg, (g + 1) * hpg)], axis=0)
        return _dot(q, keep["band", u][1])

    def score_finish(raw, u, g):
        qb = keep["band", u][0]
        mask_prev = (key_c >= qry_r) & (qb > 0)
        mask_next = (key_c <= qry_r) & (qb < nb - 1)
        probs, sink_p = [], []
        for n in range(hpg):
            s = raw[n * blk:(n + 1) * blk, :]
            sink = sink_ref[g * hpg + n] * LOG2E
            s = jnp.concatenate([jnp.where(mask_prev, s[:, 0:blk], NEG), s[:, blk:2 * blk],
                                 jnp.where(mask_next, s[:, 2 * blk:3 * blk], NEG), s[:, 3 * blk:]], axis=1)
            mx = jnp.maximum(jnp.max(s, axis=1, keepdims=True), sink)
            probs.append(jnp.exp2(s - mx).astype(BF16))
            sink_p.append(jnp.exp2(sink - mx))
        keep["p", u, g] = jnp.concatenate(probs, axis=0)
        keep["sink_p", u, g] = sink_p

    def value_issue(u, g):
        return _dot(keep["p", u, g], keep["band", u][2])

    def value_finish(raw, u, g):
        for n in range(0, hpg, 2):
            h = g * hpg + n
            even, odd = [raw[m * blk:(m + 1) * blk, 0:LANES]
                         / (raw[m * blk:(m + 1) * blk, LANES:2 * LANES] + keep["sink_p", u, g][m])
                         for m in (n, n + 1)]
            if h < ATT_HEADS // ATT_KV_HEADS:
                y = jnp.where(left, even, pltpu.roll(odd, HEAD_DIM, 1))
            else:
                y = jnp.where(left, pltpu.roll(even, HEAD_DIM, 1), odd)
            y_ref[0, u * blk:(u + 1) * blk, (h // 2) * LANES:(h // 2 + 1) * LANES] = y.astype(BF16)

    part = functools.partial
    work = [(u, g) for u in range(ATT_STEP_BLOCKS) for g in range(n_groups)]
    stages = [(part(score_issue, *work[0]), part(score_finish, u=work[0][0], g=work[0][1]))]
    for n, (u, g) in enumerate(work):
        if n + 1 < len(work):
            un, gn = work[n + 1]
            stages.append((part(score_issue, un, gn), part(score_finish, u=un, g=gn)))
        stages.append((part(value_issue, u, g), part(value_finish, u=u, g=g)))
    raw = stages[0][0]()
    for n, (_, finish) in enumerate(stages):
        nxt = stages[n + 1][0]() if n + 1 < len(stages) else None
        finish(raw)
        raw = nxt


def _attn_call(sinks, q, katt, vatt, ckatt, cvatt, weights):
    b_, s_, qw = q.shape
    l_ = cvatt.shape[1]
    rows = ATT_STEP_BLOCKS * ATT_BLOCK
    batch4 = lambda n: pl.BlockSpec((1, n, KV_WIDTH, LANES), lambda b, i: (b, 0, 0, 0))
    batch3 = lambda n: pl.BlockSpec((1, n, KV_WIDTH), lambda b, i: (b, 0, 0))
    steps = s_ // rows
    w_specs, w_shapes = _side_job_specs(weights, b_ * steps, lambda b, i: b * steps + i)
    outs = pl.pallas_call(
        _attn_kernel,
        grid=(b_, steps),
        in_specs=[pl.BlockSpec(memory_space=pltpu.SMEM),
                  pl.BlockSpec((1, rows, qw), lambda b, i: (b, i, 0)),
                  batch4(s_ // LANES), batch4(l_ // LANES), batch3(s_), batch3(l_)] + w_specs,
        out_specs=[pl.BlockSpec((1, rows, ATT_WIDTH), lambda b, i: (b, i, 0))] + w_specs,
        out_shape=[jax.ShapeDtypeStruct((b_, s_, ATT_WIDTH), BF16)] + w_shapes,
        compiler_params=pltpu.CompilerParams(dimension_semantics=("parallel", "arbitrary"),
                                             vmem_limit_bytes=VMEM_LIMIT),
        name="window_attn",
    )(sinks, q, katt, ckatt, vatt, cvatt, *weights)
    return outs[0], outs[1:]


def _merge_ffn_kernel(x_ref, yh_ref, ya_ref, gates_ref, mod_ref, nw_ref, wbh_ref, wba_ref, wo_ref,
                      wg_ref, wu_ref, wd_ref, o_ref):
    tm = x_ref.shape[1]
    n_sub = FFN_SUBTILES
    sub = tm // n_sub
    bounds = FFN_CHUNK_BOUNDS
    n_chunks = len(bounds) - 1
    g1 = mod_ref[0, :, 2 * D_MODEL:3 * D_MODEL]
    sh2 = mod_ref[0, :, 3 * D_MODEL:4 * D_MODEL]
    sc2 = mod_ref[0, :, 4 * D_MODEL:5 * D_MODEL]
    g2 = mod_ref[0, :, 5 * D_MODEL:6 * D_MODEL]
    keep = {}

    def branch_issue(r):
        rows = slice(r * sub, (r + 1) * sub)
        return _dot(yh_ref[0, rows, :], wbh_ref[...]), _dot(ya_ref[0, rows, :], wba_ref[...])

    def branch_finish(raw, r):
        rows = slice(r * sub, (r + 1) * sub)
        a, b = raw
        mixed = (gates_ref[0, rows, 0:D_MODEL].astype(F32) * a
                 + gates_ref[0, rows, D_MODEL:2 * D_MODEL].astype(F32) * b)
        keep["mixed", r] = (0.5 * mixed).astype(BF16)

    def out_issue(r):
        return _dot(keep["mixed", r], wo_ref[...])

    def out_finish(raw, r):
        x1 = x_ref[0, r * sub:(r + 1) * sub, :] + g1 * raw
        ms = jnp.mean(x1 * x1, axis=-1, keepdims=True)
        keep["x1", r] = x1
        keep["h2", r] = ((x1 * lax.rsqrt(ms + EPS) * nw_ref[...]) * (1.0 + sc2) + sh2).astype(BF16)

    def hidden_issue(r, c):
        h2 = keep["h2", r]
        cols = slice(bounds[c], bounds[c + 1])
        return _dot(h2, wg_ref[:, cols]), _dot(h2, wu_ref[:, cols])

    def hidden_finish(raw, r, c):
        gate, up = raw
        keep["act", r, c] = (gate * _sigmoid(gate) * up).astype(BF16)

    def down_issue(r, c):
        return _dot(keep["act", r, c], wd_ref[bounds[c]:bounds[c + 1], :])

    def down_finish(raw, r, c):
        dn = raw if c == 0 else keep["dn", r] + raw
        if c + 1 < n_chunks:
            keep["dn", r] = dn
        else:
            o_ref[0, r * sub:(r + 1) * sub, :] = keep["x1", r] + g2 * dn

    part = functools.partial
    stages = []
    for issue, finish in ((branch_issue, branch_finish), (out_issue, out_finish)):
        stages += [(part(issue, r), part(finish, r=r)) for r in range(n_sub)]
    for c in range(n_chunks):
        stages += [(part(hidden_issue, r, c), part(hidden_finish, r=r, c=c)) for r in range(n_sub)]
        stages += [(part(down_issue, r, c), part(down_finish, r=r, c=c)) for r in range(n_sub)]
    raw = stages[0][0]()
    for n, (_, finish) in enumerate(stages):
        nxt = stages[n + 1][0]() if n + 1 < len(stages) else None
        finish(raw)
        raw = nxt


def _merge_ffn_call(x, yh, ya, gates, mod3, nw, wbh, wba, wo, wg, wu, wd):
    b_, s_, _ = x.shape
    tm = TOK_TILE
    tok = lambda w: pl.BlockSpec((1, tm, w), lambda b, i: (b, i, 0))
    return pl.pallas_call(
        _merge_ffn_kernel,
        grid=(b_, s_ // tm),
        in_specs=[tok(D_MODEL), tok(HG_WIDTH), tok(ATT_WIDTH), tok(2 * D_MODEL),
                  pl.BlockSpec((1, 1, mod3.shape[2]), lambda b, i: (b, 0, 0)),
                  _const_spec((1, D_MODEL)),
                  _const_spec(wbh.shape), _const_spec(wba.shape), _const_spec(wo.shape),
                  _const_spec(wg.shape), _const_spec(wu.shape), _const_spec(wd.shape)],
        out_specs=tok(D_MODEL),
        out_shape=jax.ShapeDtypeStruct(x.shape, F32),
        compiler_params=pltpu.CompilerParams(dimension_semantics=("parallel", "parallel"),
                                             vmem_limit_bytes=VMEM_LIMIT),
        name="merge_ffn",
    )(x, yh, ya, gates, mod3, nw, wbh, wba, wo, wg, wu, wd)


def _rope_tables(n_tok):
    t = np.arange(n_tok)
    rows = (t // GRID_W).astype(np.float64)
    cols = (t % GRID_W).astype(np.float64)
    half = HEAD_DIM // 2
    inv_freq = ROPE_THETA ** (-np.arange(0, half, 2, dtype=np.float64) / half)
    d = np.arange(LANES) % HEAD_DIM
    pos = np.where((d < half)[None, :], rows[:, None], cols[:, None])
    ang = pos * inv_freq[(d % half) % (half // 2)][None, :]
    sign = np.where((d % half) < half // 2, -1.0, 1.0)[None, :]
    return jnp.asarray(np.cos(ang), F32), jnp.asarray(np.sin(ang) * sign, F32)


def kernel(x, c, ctx, c_ctx, w_ada, b_ada, norm_mix_w, norm_ffn_w, w_in, hgrn_lb_logits, hgrn_norm_w,
           q_norm_w, k_norm_w, attn_sinks, w_branch_hgrn, w_branch_attn, w_out, w_ffn_gate, w_ffn_up,
           w_ffn_down):
    b_, s_, _ = x.shape
    layer = 0
    pad = 16 - (b_ + 1)
    cc = jnp.concatenate([c, c_ctx[None, :], jnp.zeros((pad, D_MODEL), F32)], axis=0)
    mod = _ada_call(cc, w_ada[layer], b_ada[layer][None, :])
    mod_lat = mod[:b_].reshape(b_, 1, 6 * D_MODEL)
    mod_ctx = mod[b_:b_ + 1].reshape(1, 1, 6 * D_MODEL)

    col = np.ones((IN_COLS,), np.float32)
    for lo, width in ((C_FF, 2 * HG_WIDTH), (C_QHG, 2 * HG_WIDTH), (C_GATES, 2 * D_MODEL)):
        col[lo:lo + width] = 0.5
    w_in_b = (w_in[layer] * col[None, :]).astype(BF16)
    lbl = hgrn_lb_logits[:, 0:2, :].reshape(4, HG_WIDTH)
    qw = jnp.tile(q_norm_w[layer], 2)[None, :]
    kw = jnp.tile(k_norm_w[layer], 2)[None, :]
    nw_mix = norm_mix_w[layer][None, :]
    cos_t, sin_t = _rope_tables(s_)

    (qdf, kdf, qdb, kdb, dec, vt, v, g, q, katt, vatt, gates) = _inproj_call(
        x, mod_lat, nw_mix, w_in_b, lbl, qw, kw, cos_t, sin_t, latent=True)
    ckd2f, ckd2b, cdec, cvt, ckatt, cvatt = _inproj_call(
        ctx, mod_ctx, nw_mix, w_in_b, lbl, qw, kw, cos_t, sin_t, latent=False)

    y_hg, (w_bh, w_ba, w_o, w_d) = _hgrn_call(
        qdf, kdf, qdb, kdb, v, vt, dec, ckd2f, ckd2b, cvt, cdec, g, hgrn_norm_w[layer][None, :],
        (w_branch_hgrn[layer], w_branch_attn[layer], w_out[layer], w_ffn_down[layer]))
    y_at, (w_g, w_u) = _attn_call(attn_sinks[layer], q, katt, vatt, ckatt, cvatt,
                                  (w_ffn_gate[layer], w_ffn_up[layer]))

    return _merge_ffn_call(x, y_hg, y_at, gates, mod_lat, norm_ffn_w[layer][None, :],
                           w_bh, w_ba, w_o, w_g, w_u, w_d)
```

```python
import functools

import jax
import jax.numpy as jnp
import numpy as np
from jax import lax
from jax.experimental import pallas as pl
from jax.experimental.pallas import tpu as pltpu

F32 = jnp.float32
BF16 = jnp.bfloat16

D_MODEL = 1024
GRID_W = 64
EPS = 1e-6
HG_HEADS = 4
HG_DIM = 128
HG_WIDTH = HG_HEADS * HG_DIM
CHUNK = 32
ATT_HEADS = 8
ATT_KV_HEADS = 2
HEAD_DIM = 64
ATT_WIDTH = ATT_HEADS * HEAD_DIM
KV_WIDTH = ATT_KV_HEADS * HEAD_DIM
WINDOW = 128
ROPE_THETA = 10000.0
D_FF = 2816
CTX_COLS = 3 * HG_WIDTH + 2 * KV_WIDTH
IN_COLS = CTX_COLS + 2 * HG_WIDTH + ATT_WIDTH + 2 * D_MODEL

C_FF, C_FB, C_INP, C_K, C_V = 0, HG_WIDTH, 2 * HG_WIDTH, 3 * HG_WIDTH, 3 * HG_WIDTH + KV_WIDTH
C_QHG = CTX_COLS
C_GHG = C_QHG + HG_WIDTH
C_Q = C_GHG + HG_WIDTH
C_GATES = C_Q + ATT_WIDTH

LANES = 128
TOK_TILE = 512
FFN_SUBTILES = 2
MXU_DEPTH = 256
FFN_CHUNK_BOUNDS = (0, 6 * MXU_DEPTH, D_FF)
INPROJ_TILE = 512
INPROJ_DEPTH = 2
HG_TILE = 256
HG_HEADS_PER_STEP = 4
ATT_BLOCK = 128
ATT_GROUP_HEADS = 2
ATT_DEPTH = 1
ATT_STEP_BLOCKS = 8
ADA_ROWS = 16
ADA_STEPS = 6
VMEM_LIMIT = 56 * 1024 * 1024
NEG = -1e30
LOG2E = 1.4426950408889634


def _dot(a, b):
    return jnp.dot(a, b, preferred_element_type=F32)


def _dot_nt(a, b):
    return lax.dot_general(a, b, (((1,), (1,)), ((), ())), preferred_element_type=F32)


def _sigmoid(x):
    return 0.5 * jnp.tanh(0.5 * x) + 0.5


def _emit_pipelined(stages, depth):
    raws = [issue() for issue, _ in stages[:depth]]
    for i, (_, finish) in enumerate(stages):
        if i + depth < len(stages):
            raws.append(stages[i + depth][0]())
        finish(raws[i])
        raws[i] = None


def _const_spec(shape):
    n = len(shape)
    return pl.BlockSpec(shape, lambda *_: (0,) * n, pipeline_mode=pl.Buffered(1))


def _ada_kernel(c_ref, cctx_ref, w_ref, b_ref, win_ref, col_ref, o_ref, winb_ref):
    rows = o_ref.shape[0]
    n_cond = c_ref.shape[0] + 1
    c = jnp.concatenate([c_ref[...], cctx_ref[...], jnp.zeros((rows - n_cond, D_MODEL), F32)], axis=0)
    s = (c * _sigmoid(c)).astype(BF16)
    mod = _dot(s, w_ref[...].astype(BF16)) + b_ref[...]
    for r in range(mod.shape[0]):
        o_ref[r] = mod[r:r + 1, :]
    winb_ref[...] = (win_ref[...] * col_ref[...]).astype(BF16)


def _ada_call(c, c_ctx, w_ada, b_ada, w_in, col):
    rows = ADA_ROWS
    n_out = w_ada.shape[1]
    steps = ADA_STEPS
    bn = n_out // steps
    bw = w_in.shape[1] // steps
    return pl.pallas_call(
        _ada_kernel,
        grid=(steps,),
        in_specs=[pl.BlockSpec(c.shape, lambda j: (0, 0)),
                  pl.BlockSpec((1, D_MODEL), lambda j: (0, 0)),
                  pl.BlockSpec((D_MODEL, bn), lambda j: (0, j)),
                  pl.BlockSpec((1, bn), lambda j: (0, j)),
                  pl.BlockSpec((D_MODEL, bw), lambda j: (0, j)),
                  pl.BlockSpec((1, bw), lambda j: (0, j))],
        out_specs=[pl.BlockSpec((rows, 1, bn), lambda j: (0, 0, j)),
                   pl.BlockSpec((D_MODEL, bw), lambda j: (0, j))],
        out_shape=[jax.ShapeDtypeStruct((rows, 1, n_out), F32),
                   jax.ShapeDtypeStruct(w_in.shape, BF16)],
        compiler_params=pltpu.CompilerParams(dimension_semantics=("arbitrary",),
                                             vmem_limit_bytes=VMEM_LIMIT),
        name="ada_mod",
    )(c, c_ctx, w_ada, b_ada, w_in, col)


def _chunk_cumsum(x, reverse):
    n, c = x.shape
    sub = 8
    r = lax.broadcasted_iota(jnp.int32, x.shape, 0) & (CHUNK - 1)
    s = 1
    while s < sub:
        if reverse:
            x = x + jnp.where(r < CHUNK - s, pltpu.roll(x, n - s, 0), 0.0)
        else:
            x = x + jnp.where(r >= s, pltpu.roll(x, s, 0), 0.0)
        s *= 2
    per = CHUNK // sub
    x4 = x.reshape(n // CHUNK, per, sub, c)
    s = 1
    while s < per:
        if reverse:
            x4 = jnp.concatenate([x4[:, :per - s] + x4[:, s:], x4[:, per - s:]], axis=1)
        else:
            x4 = jnp.concatenate([x4[:, :s], x4[:, s:] + x4[:, :per - s]], axis=1)
        s *= 2
    return x4.reshape(n, c)


def _rope(t, cos, sin_signed, lane):
    partner = jnp.where((lane & 31) < 16, pltpu.roll(t, LANES - 16, 1), pltpu.roll(t, 16, 1))
    return t * cos + partner * sin_signed


def _inproj_kernel(x_ref, mod_ref, nw_ref, w_ref, lbl_ref, qw_ref, kw_ref, cos_ref, sin_ref,
                   *out_refs, latent):
    if latent:
        (qdf_ref, kdf_ref, qdb_ref, kdb_ref, dec_ref, vt_ref, v_ref,
         g_ref, q_ref, katt_ref, vatt_ref, gates_ref) = out_refs
    else:
        kd2f_ref, kd2b_ref, dec_ref, vt_ref, katt_ref, vatt_ref = out_refs

    tm = x_ref.shape[1]
    nchunk = tm // CHUNK
    x = x_ref[0]
    sh = mod_ref[0, :, 0:D_MODEL]
    sc = mod_ref[0, :, D_MODEL:2 * D_MODEL]
    ms = jnp.mean(x * x, axis=-1, keepdims=True)
    h = x * lax.rsqrt(ms + EPS) * (nw_ref[...] * (1.0 + sc)) + sh
    hb = h.astype(BF16)

    def mm(a, b):
        return _dot(hb, w_ref[:, a:b])

    l0f, l1f = lbl_ref[0, 0:1, :], lbl_ref[0, 1:2, :]
    l0b, l1b = lbl_ref[1, 0:1, :], lbl_ref[1, 1:2, :]
    lb_f = 1.0 / (1.0 + jnp.exp(l1f - l0f))
    lb_b = 1.0 / (1.0 + jnp.exp(l1b - l0b))

    qw = jnp.concatenate([qw_ref[...]] * (LANES // HEAD_DIM), axis=1)
    kw = jnp.concatenate([kw_ref[...]] * (LANES // HEAD_DIM), axis=1)
    gw = HG_WIDTH // 2
    sw = D_MODEL // 2
    lane = lax.broadcasted_iota(jnp.int32, (tm, LANES), 1)
    left = lane < HEAD_DIM
    bi = lax.broadcasted_iota(jnp.int32, (2 * LANES, LANES), 0) // HEAD_DIM
    bj = lax.broadcasted_iota(jnp.int32, (2 * LANES, LANES), 1) // HEAD_DIM
    ones_blk2 = jnp.where((bi % 2) == bj, 1.0, 0.0).astype(BF16)
    keep = {}

    def qhg_finish(raw):
        keep["q_hg"] = raw * (jnp.tanh(raw) + 1.0) * (HG_DIM ** -0.5)

    def gate_finish(raw, direction, part):
        reverse = direction == 1
        lo = part * gw
        lb = (lb_f, lb_b)[direction][:, lo:lo + gw]
        c_half = 0.5 - 0.5 * lb
        ct = c_half * jnp.tanh(raw)
        f = (0.5 + 0.5 * lb) + ct
        k = c_half - ct
        cum = _chunk_cumsum(jnp.log(f), reverse)
        cum3 = cum.reshape(nchunk, CHUNK, gw)
        tot3 = cum3[:, 0:1, :] if reverse else cum3[:, CHUNK - 1:CHUNK, :]
        dec3 = jnp.exp(tot3)
        dec_ref[0, :, direction * HG_WIDTH + lo:direction * HG_WIDTH + lo + gw] = dec3.reshape(nchunk, gw)
        if latent:
            e_pos = jnp.exp(cum)
            (qdf_ref, qdb_ref)[direction][0, :, lo:lo + gw] = (keep["q_hg"][:, lo:lo + gw] * e_pos).astype(BF16)
            (kdf_ref, kdb_ref)[direction][0, :, lo:lo + gw] = (k * jnp.exp(-cum)).astype(BF16)
        else:
            kd2 = k.reshape(nchunk, CHUNK, gw) * jnp.exp(tot3 - cum3)
            (kd2f_ref, kd2b_ref)[direction][0, :, lo:lo + gw] = kd2.reshape(tm, gw).astype(BF16)

    def mgate_finish(raw, j):
        gates_ref[0, :, j * sw:(j + 1) * sw] = (jnp.tanh(raw) + 1.0).astype(BF16)

    def v_finish(raw):
        for s in range(tm // LANES):
            vt_ref[0, s] = raw[s * LANES:(s + 1) * LANES, :].T.astype(BF16)
        if latent:
            v_ref[0] = raw.astype(BF16)

    def g_finish(raw):
        g_ref[0] = (raw * (jnp.tanh(raw) + 1.0)).astype(BF16)

    def split_sq(t):
        sq = t * t
        hi = sq.astype(BF16)
        lo = (sq - hi.astype(F32)).astype(BF16)
        return jnp.concatenate([hi, lo], axis=1)

    def kv_finish(raw):
        keep["k_raw"] = raw[:, 0:LANES]
        keep["k_split"] = split_sq(keep["k_raw"])
        vatt_ref[0] = raw[:, LANES:2 * LANES].astype(BF16)

    def knorm_finish(ss):
        kn = keep["k_raw"] * lax.rsqrt(ss * (1.0 / HEAD_DIM) + EPS) * kw
        if latent:
            kn = _rope(kn, cos_ref[...], sin_ref[...], lane)
        for s in range(tm // LANES):
            katt_ref[0, s] = kn[s * LANES:(s + 1) * LANES, :].T.astype(BF16)

    def q_finish(raw):
        keep["q_raw"] = raw
        keep["q_split"] = [split_sq(raw[:, s * LANES:(s + 1) * LANES]) for s in range(ATT_WIDTH // LANES)]

    def qnorm_finish(ss_list):
        for s, ss in enumerate(ss_list):
            qraw = keep["q_raw"][:, s * LANES:(s + 1) * LANES]
            qn = qraw * lax.rsqrt(ss * (1.0 / HEAD_DIM) + EPS) * qw
            qn = _rope(qn, cos_ref[...], sin_ref[...], lane) * (HEAD_DIM ** -0.5 * LOG2E)
            swapped = pltpu.roll(qn, HEAD_DIM, 1)
            if s < ATT_WIDTH // LANES // 2:
                even, odd = jnp.where(left, qn, 0.0), jnp.where(left, swapped, 0.0)
            else:
                even, odd = jnp.where(left, 0.0, swapped), jnp.where(left, 0.0, qn)
            q_ref[0, :, (2 * s) * LANES:(2 * s + 1) * LANES] = even.astype(BF16)
            q_ref[0, :, (2 * s + 1) * LANES:(2 * s + 2) * LANES] = odd.astype(BF16)

    def gate_stage(direction, part):
        col = (C_FF, C_FB)[direction] + part * gw
        return (lambda: mm(col, col + gw),
                functools.partial(gate_finish, direction=direction, part=part))

    def mgate_stage(j):
        c0 = C_GATES + j * sw
        return (lambda: mm(c0, c0 + sw), functools.partial(mgate_finish, j=j))

    v_stage = (lambda: mm(C_INP, C_INP + HG_WIDTH), v_finish)
    kv_stage = (lambda: mm(C_K, C_K + 2 * KV_WIDTH), kv_finish)
    knorm_stage = (lambda: _dot(keep["k_split"], ones_blk2), knorm_finish)
    if latent:
        stages = [(lambda: mm(C_QHG, C_QHG + HG_WIDTH), qhg_finish)]
        for j in range(4):
            stages += [gate_stage(j // 2, j % 2), mgate_stage(j)]
        stages += [(lambda: mm(C_Q, C_Q + ATT_WIDTH), q_finish),
                   v_stage,
                   kv_stage,
                   (lambda: [_dot(t, ones_blk2) for t in keep["q_split"]], qnorm_finish),
                   (lambda: mm(C_GHG, C_GHG + HG_WIDTH), g_finish),
                   knorm_stage]
    else:
        stages = [gate_stage(0, 0), gate_stage(0, 1), kv_stage, gate_stage(1, 0), gate_stage(1, 1),
                  knorm_stage, v_stage]
    _emit_pipelined(stages, INPROJ_DEPTH)


def _inproj_call(x, mod3, mod_row0, nw, w_in, lbl, qw, kw, cos_t, sin_t, latent):
    b_, s_, _ = x.shape
    tm = min(INPROJ_TILE, s_)
    grid = (b_, s_ // tm)
    tok = lambda w: pl.BlockSpec((1, tm, w), lambda b, i: (b, i, 0))
    mod_map = (lambda b, i: (mod_row0 + b, 0, 0)) if latent else (lambda b, i: (mod_row0, 0, 0))
    ncols = IN_COLS if latent else CTX_COLS
    in_specs = [tok(D_MODEL),
                pl.BlockSpec((1, 1, mod3.shape[2]), mod_map),
                _const_spec((1, D_MODEL)),
                pl.BlockSpec((D_MODEL, ncols), lambda b, i: (0, 0), pipeline_mode=pl.Buffered(1)),
                _const_spec(lbl.shape),
                _const_spec((1, HEAD_DIM)),
                _const_spec((1, HEAD_DIM)),
                pl.BlockSpec((tm, LANES), lambda b, i: (i, 0)),
                pl.BlockSpec((tm, LANES), lambda b, i: (i, 0))]
    bf = lambda w: jax.ShapeDtypeStruct((b_, s_, w), BF16)
    dec_shape = jax.ShapeDtypeStruct((b_, s_ // CHUNK, 2 * HG_WIDTH), F32)
    dec_spec = pl.BlockSpec((1, tm // CHUNK, 2 * HG_WIDTH), lambda b, i: (b, i, 0))
    vt_shape = jax.ShapeDtypeStruct((b_, s_ // LANES, HG_WIDTH, LANES), BF16)
    vt_spec = pl.BlockSpec((1, tm // LANES, HG_WIDTH, LANES), lambda b, i: (b, i, 0, 0))
    katt_shape = jax.ShapeDtypeStruct((b_, s_ // LANES, KV_WIDTH, LANES), BF16)
    katt_spec = pl.BlockSpec((1, tm // LANES, KV_WIDTH, LANES), lambda b, i: (b, i, 0, 0))
    if latent:
        out_shape = [bf(HG_WIDTH)] * 4 + [dec_shape, vt_shape, bf(HG_WIDTH), bf(HG_WIDTH),
                                         bf(ATT_HEADS * LANES), katt_shape, bf(KV_WIDTH), bf(2 * D_MODEL)]
        out_specs = [tok(HG_WIDTH)] * 4 + [dec_spec, vt_spec, tok(HG_WIDTH), tok(HG_WIDTH),
                                          tok(ATT_HEADS * LANES), katt_spec, tok(KV_WIDTH), tok(2 * D_MODEL)]
    else:
        out_shape = [bf(HG_WIDTH)] * 2 + [dec_shape, vt_shape, katt_shape, bf(KV_WIDTH)]
        out_specs = [tok(HG_WIDTH)] * 2 + [dec_spec, vt_spec, katt_spec, tok(KV_WIDTH)]
    return pl.pallas_call(
        functools.partial(_inproj_kernel, latent=latent),
        grid=grid, in_specs=in_specs, out_specs=out_specs, out_shape=out_shape,
        compiler_params=pltpu.CompilerParams(dimension_semantics=("parallel", "parallel"),
                                             vmem_limit_bytes=VMEM_LIMIT),
        name="inproj_latent" if latent else "inproj_ctx",
    )(x, mod3, nw, w_in, lbl, qw, kw, cos_t, sin_t)


def _cast_side_job(w_refs, wb_refs):
    for w_ref, wb_ref in zip(w_refs, wb_refs):
        wb_ref[...] = w_ref[...].astype(BF16)


def _side_job_specs(weights, n_steps, step_of):
    specs = [pl.BlockSpec((wt.shape[0] // n_steps, wt.shape[1]), lambda *ids: (step_of(*ids), 0))
             for wt in weights]
    shapes = [jax.ShapeDtypeStruct(wt.shape, BF16) for wt in weights]
    return specs, shapes


def _hgrn_kernel(qdf_ref, kdf_ref, qdb_ref, kdb_ref, v_ref, vt_ref,
                 decf_ref, decb_ref, ckd2f_ref, ckd2b_ref, cvt_ref, cdecf_ref, cdecb_ref,
                 g_ref, nw_ref, *rest):
    n_w = (len(rest) - 2) // 2
    w_refs, y_ref, wb_refs, o_acc = rest[:n_w], rest[n_w], rest[n_w + 1:2 * n_w + 1], rest[-1]
    _cast_side_job(w_refs, wb_refs)

    seq = v_ref.shape[1]
    ctx_len = ckd2f_ref.shape[1]
    heads = v_ref.shape[2] // HG_DIM
    tile = HG_TILE
    cpt = tile // CHUNK
    cps = LANES // CHUNK
    spt = tile // LANES
    n_tiles = seq // tile
    half = n_tiles // 2

    row_chunk = lax.broadcasted_iota(jnp.int32, (LANES, HG_DIM), 0) // CHUNK
    ti = lax.broadcasted_iota(jnp.int32, (LANES, LANES), 0)
    tj = lax.broadcasted_iota(jnp.int32, (LANES, LANES), 1)
    same = (ti // CHUNK) == (tj // CHUNK)
    mask_f = same & (tj <= ti)
    mask_b = same & (tj >= ti)

    def chunk_update(vt_slab, k_slab, n_in_slab):
        k_m = jnp.where(row_chunk == n_in_slab, k_slab, jnp.zeros_like(k_slab))
        return _dot(vt_slab, k_m)

    def ctx_state(ckd2_ref, cdec_ref, cols, reverse):
        st = jnp.zeros((HG_DIM, HG_DIM), F32)
        order = range(ctx_len // CHUNK)
        for n in (reversed(order) if reverse else order):
            slab = n // cps
            st = cdec_ref[0, n:n + 1, cols] * st + chunk_update(
                cvt_ref[0, slab, cols, :], ckd2_ref[0, slab * LANES:(slab + 1) * LANES, cols], n % cps)
        return st

    def emit(o_t, cols, rows, finalize):
        if finalize:
            o_t = o_t + o_acc[rows, cols]
            ms = jnp.mean(o_t * o_t, axis=-1, keepdims=True)
            y = o_t * lax.rsqrt(ms + EPS) * nw_ref[...] * g_ref[0, rows, cols].astype(F32)
            y_ref[0, rows, cols] = y.astype(BF16)
        else:
            o_acc[rows, cols] = o_t

    def body(it, states, finalize):
        chains = []
        for h in range(heads):
            cols = slice(h * HG_DIM, (h + 1) * HG_DIM)
            chains.append((qdf_ref, kdf_ref, decf_ref, cols, it, False))
            chains.append((qdb_ref, kdb_ref, decb_ref, cols, n_tiles - 1 - it, True))
        n_ch = len(chains)
        rows = [pl.ds(pl.multiple_of(c[4] * tile, tile), tile) for c in chains]
        orders = [list(reversed(range(cpt))) if c[5] else list(range(cpt)) for c in chains]
        qd, sc, ups = [], [], []
        for (qd_ref, kd_ref, _, cols, t, reverse), r in zip(chains, rows):
            qd.append(qd_ref[0, r, cols])
            kd = kd_ref[0, r, cols]
            sc.append([_dot_nt(qd[-1][s * LANES:(s + 1) * LANES, :], kd[s * LANES:(s + 1) * LANES, :])
                       for s in range(spt)])
            ups.append([chunk_update(vt_ref[0, t * spt + n // cps, cols, :],
                                     kd[(n // cps) * LANES:(n // cps + 1) * LANES, :], n % cps)
                        for n in range(cpt)])
        o = []
        for c, r, s in zip(chains, rows, sc):
            v = v_ref[0, r, c[3]]
            o.append(jnp.concatenate(
                [_dot(jnp.where(mask_b if c[5] else mask_f, s[j], 0.0).astype(BF16),
                      v[j * LANES:(j + 1) * LANES, :]) for j in range(spt)], axis=0))
        starts, new_states = [], []
        for i, c in enumerate(chains):
            dec = c[2][0, pl.ds(pl.multiple_of(c[4] * cpt, cpt), cpt), c[3]]
            st = states[i]
            start = [None] * cpt
            for n in orders[i]:
                start[n] = st.T.astype(BF16)
                st = dec[n:n + 1, :] * (st + ups[i][n])
            starts.append(start)
            new_states.append(st)
        for i, c in enumerate(chains):
            outs = [o[i][n * CHUNK:(n + 1) * CHUNK, :]
                    + _dot(qd[i][n * CHUNK:(n + 1) * CHUNK, :], starts[i][n]) for n in range(cpt)]
            emit(jnp.concatenate(outs, axis=0), c[3], rows[i], finalize)
        return tuple(new_states)

    states = []
    for h in range(heads):
        cols = slice(h * HG_DIM, (h + 1) * HG_DIM)
        states += [ctx_state(ckd2f_ref, cdecf_ref, cols, False), ctx_state(ckd2b_ref, cdecb_ref, cols, True)]
    states = lax.fori_loop(0, half, functools.partial(body, finalize=False), tuple(states))
    lax.fori_loop(half, n_tiles, functools.partial(body, finalize=True), states)


def _hgrn_call(qdf, kdf, qdb, kdb, v, vt, dec, ckd2f, ckd2b, cvt, cdec, g, nw, weights):
    b_, s_, _ = v.shape
    l_ = ckd2f.shape[1]
    w = HG_HEADS_PER_STEP * HG_DIM
    groups = HG_HEADS // HG_HEADS_PER_STEP
    seq = lambda n: pl.BlockSpec((1, n, w), lambda b, h: (b, 0, h))
    vt_spec = lambda n: pl.BlockSpec((1, n // LANES, w, LANES), lambda b, h: (b, 0, h, 0))
    dec_f = lambda n: pl.BlockSpec((1, n // CHUNK, w), lambda b, h: (b, 0, h))
    dec_b = lambda n: pl.BlockSpec((1, n // CHUNK, w), lambda b, h: (b, 0, groups + h))
    w_specs, w_shapes = _side_job_specs(weights, b_ * groups, lambda b, h: b * groups + h)
    outs = pl.pallas_call(
        _hgrn_kernel,
        grid=(b_, groups),
        in_specs=[seq(s_)] * 5 + [vt_spec(s_), dec_f(s_), dec_b(s_),
                                  seq(l_), seq(l_), vt_spec(l_), dec_f(l_), dec_b(l_),
                                  seq(s_), _const_spec((1, HG_DIM))] + w_specs,
        out_specs=[seq(s_)] + w_specs,
        out_shape=[jax.ShapeDtypeStruct((b_, s_, HG_WIDTH), BF16)] + w_shapes,
        scratch_shapes=[pltpu.VMEM((s_, w), F32)],
        compiler_params=pltpu.CompilerParams(dimension_semantics=("parallel", "parallel"),
                                             vmem_limit_bytes=VMEM_LIMIT),
        name="hgrn_scan",
    )(qdf, kdf, qdb, kdb, v, vt, dec, dec, ckd2f, ckd2b, cvt, cdec, cdec, g, nw, *weights)
    return outs[0], outs[1:]


def _attn_kernel(sink_ref, q_ref, kt_ref, kc_ref, v_ref, vc_ref, *rest):
    n_w = (len(rest) - 1) // 2
    y_ref = rest[n_w]
    _cast_side_job(rest[:n_w], rest[n_w + 1:])
    nb = kt_ref.shape[1]
    blk = ATT_BLOCK
    hpg = ATT_GROUP_HEADS
    n_groups = ATT_HEADS // hpg
    qry_r = lax.broadcasted_iota(jnp.int32, (blk, blk), 0)
    key_c = lax.broadcasted_iota(jnp.int32, (blk, blk), 1)
    left = lax.broadcasted_iota(jnp.int32, (blk, LANES), 1) < HEAD_DIM
    ctx_keys = [kc_ref[0, c] for c in range(kc_ref.shape[1])]
    keep = {}

    def band(u):
        qb = pl.program_id(1) * ATT_STEP_BLOCKS + u
        ids = (jnp.maximum(qb - 1, 0), qb, jnp.minimum(qb + 1, nb - 1))
        keys_t = jnp.concatenate([kt_ref[0, j] for j in ids] + ctx_keys, axis=1)
        values = jnp.concatenate([v_ref[0, pl.ds(pl.multiple_of(j * blk, blk), blk), :] for j in ids]
                                 + [vc_ref[0]], axis=0)
        values = jnp.concatenate([values, jnp.ones_like(values)], axis=1)
        return qb, keys_t, values

    def score_issue(u, g):
        if g == 0:
            keep["band", u] = band(u)
        q = jnp.concatenate([q_ref[0, u * blk:(u + 1) * blk, h * LANES:(h + 1) * LANES]
                             for h in range(g * hpg, (g + 1) * hpg)], axis=0)
        return _dot(q, keep["band", u][1])

    def score_finish(raw, u, g):
        qb = keep["band", u][0]
        mask_prev = (key_c >= qry_r) & (qb > 0)
        mask_next = (key_c <= qry_r) & (qb < nb - 1)
        probs, sink_p = [], []
        for n in range(hpg):
            s = raw[n * blk:(n + 1) * blk, :]
            sink = sink_ref[g * hpg + n] * LOG2E
            s = jnp.concatenate([jnp.where(mask_prev, s[:, 0:blk], NEG), s[:, blk:2 * blk],
                                 jnp.where(mask_next, s[:, 2 * blk:3 * blk], NEG), s[:, 3 * blk:]], axis=1)
            mx = jnp.maximum(jnp.max(s, axis=1, keepdims=True), sink)
            probs.append(jnp.exp2(s - mx).astype(BF16))
            sink_p.append(jnp.exp2(sink - mx))
        keep["p", u, g] = jnp.concatenate(probs, axis=0)
        keep["sink_p", u, g] = sink_p

    def value_issue(u, g):
        return _dot(keep["p", u, g], keep["band", u][2])

    def value_finish(raw, u, g):
        for n in range(0, hpg, 2):
            h = g * hpg + n
            even, odd = [raw[m * blk:(m + 1) * blk, 0:LANES]
                         / (raw[m * blk:(m + 1) * blk, LANES:2 * LANES] + keep["sink_p", u, g][m])
                         for m in (n, n + 1)]
            if h < ATT_HEADS // ATT_KV_HEADS:
                y = jnp.where(left, even, pltpu.roll(odd, HEAD_DIM, 1))
            else:
                y = jnp.where(left, pltpu.roll(even, HEAD_DIM, 1), odd)
            y_ref[0, u * blk:(u + 1) * blk, (h // 2) * LANES:(h // 2 + 1) * LANES] = y.astype(BF16)

    part = functools.partial
    work = [(u, g) for u in range(ATT_STEP_BLOCKS) for g in range(n_groups)]
    score = lambda u, g: (part(score_issue, u, g), part(score_finish, u=u, g=g))
    stages = [score(*w) for w in work[:ATT_DEPTH + 1]]
    for n, (u, g) in enumerate(work):
        stages.append((part(value_issue, u, g), part(value_finish, u=u, g=g)))
        if n + ATT_DEPTH + 1 < len(work):
            stages.append(score(*work[n + ATT_DEPTH + 1]))
    _emit_pipelined(stages, ATT_DEPTH)


def _attn_call(sinks, q, katt, vatt, ckatt, cvatt, weights):
    b_, s_, qw = q.shape
    l_ = cvatt.shape[1]
    rows = ATT_STEP_BLOCKS * ATT_BLOCK
    batch4 = lambda n: pl.BlockSpec((1, n, KV_WIDTH, LANES), lambda b, i: (b, 0, 0, 0))
    batch3 = lambda n: pl.BlockSpec((1, n, KV_WIDTH), lambda b, i: (b, 0, 0))
    steps = s_ // rows
    w_specs, w_shapes = _side_job_specs(weights, b_ * steps, lambda b, i: b * steps + i)
    outs = pl.pallas_call(
        _attn_kernel,
        grid=(b_, steps),
        in_specs=[pl.BlockSpec(memory_space=pltpu.SMEM),
                  pl.BlockSpec((1, rows, qw), lambda b, i: (b, i, 0)),
                  batch4(s_ // LANES), batch4(l_ // LANES), batch3(s_), batch3(l_)] + w_specs,
        out_specs=[pl.BlockSpec((1, rows, ATT_WIDTH), lambda b, i: (b, i, 0))] + w_specs,
        out_shape=[jax.ShapeDtypeStruct((b_, s_, ATT_WIDTH), BF16)] + w_shapes,
        compiler_params=pltpu.CompilerParams(dimension_semantics=("parallel", "arbitrary"),
                                             vmem_limit_bytes=VMEM_LIMIT),
        name="window_attn",
    )(sinks, q, katt, ckatt, vatt, cvatt, *weights)
    return outs[0], outs[1:]


def _merge_ffn_kernel(x_ref, yh_ref, ya_ref, gates_ref, mod_ref, nw_ref, wbh_ref, wba_ref, wo_ref,
                      wg_ref, wu_ref, wd_ref, o_ref):
    tm = x_ref.shape[1]
    n_sub = FFN_SUBTILES
    sub = tm // n_sub
    bounds = FFN_CHUNK_BOUNDS
    n_chunks = len(bounds) - 1
    g1 = mod_ref[0, :, 2 * D_MODEL:3 * D_MODEL]
    sh2 = mod_ref[0, :, 3 * D_MODEL:4 * D_MODEL]
    sc2 = mod_ref[0, :, 4 * D_MODEL:5 * D_MODEL]
    g2 = mod_ref[0, :, 5 * D_MODEL:6 * D_MODEL]
    keep = {}

    def branch_issue(r):
        rows = slice(r * sub, (r + 1) * sub)
        return _dot(yh_ref[0, rows, :], wbh_ref[...]), _dot(ya_ref[0, rows, :], wba_ref[...])

    def branch_finish(raw, r):
        rows = slice(r * sub, (r + 1) * sub)
        a, b = raw
        mixed = (gates_ref[0, rows, 0:D_MODEL].astype(F32) * a
                 + gates_ref[0, rows, D_MODEL:2 * D_MODEL].astype(F32) * b)
        keep["mixed", r] = (0.5 * mixed).astype(BF16)

    def out_issue(r):
        return _dot(keep["mixed", r], wo_ref[...])

    def out_finish(raw, r):
        x1 = x_ref[0, r * sub:(r + 1) * sub, :] + g1 * raw
        ms = jnp.mean(x1 * x1, axis=-1, keepdims=True)
        keep["x1", r] = x1
        keep["h2", r] = ((x1 * lax.rsqrt(ms + EPS) * nw_ref[...]) * (1.0 + sc2) + sh2).astype(BF16)

    def hidden_issue(r, c):
        h2 = keep["h2", r]
        cols = slice(bounds[c], bounds[c + 1])
        return _dot(h2, wg_ref[:, cols]), _dot(h2, wu_ref[:, cols])

    def hidden_finish(raw, r, c):
        gate, up = raw
        keep["act", r, c] = (gate * _sigmoid(gate) * up).astype(BF16)

    def down_issue(r, c):
        return _dot(keep["act", r, c], wd_ref[bounds[c]:bounds[c + 1], :])

    def down_finish(raw, r, c):
        dn = raw if c == 0 else keep["dn", r] + raw
        if c + 1 < n_chunks:
            keep["dn", r] = dn
        else:
            o_ref[0, r * sub:(r + 1) * sub, :] = keep["x1", r] + g2 * dn

    part = functools.partial
    stages = []
    for issue, finish in ((branch_issue, branch_finish), (out_issue, out_finish)):
        stages += [(part(issue, r), part(finish, r=r)) for r in range(n_sub)]
    for c in range(n_chunks):
        stages += [(part(hidden_issue, r, c), part(hidden_finish, r=r, c=c)) for r in range(n_sub)]
        stages += [(part(down_issue, r, c), part(down_finish, r=r, c=c)) for r in range(n_sub)]
    _emit_pipelined(stages, n_sub - 1)


def _merge_ffn_call(x, yh, ya, gates, mod3, nw, wbh, wba, wo, wg, wu, wd):
    b_, s_, _ = x.shape
    tm = TOK_TILE
    tok = lambda w: pl.BlockSpec((1, tm, w), lambda b, i: (b, i, 0))
    return pl.pallas_call(
        _merge_ffn_kernel,
        grid=(b_, s_ // tm),
        in_specs=[tok(D_MODEL), tok(HG_WIDTH), tok(ATT_WIDTH), tok(2 * D_MODEL),
                  pl.BlockSpec((1, 1, mod3.shape[2]), lambda b, i: (b, 0, 0)),
                  _const_spec((1, D_MODEL)),
                  _const_spec(wbh.shape), _const_spec(wba.shape), _const_spec(wo.shape),
                  _const_spec(wg.shape), _const_spec(wu.shape), _const_spec(wd.shape)],
        out_specs=tok(D_MODEL),
        out_shape=jax.ShapeDtypeStruct(x.shape, F32),
        compiler_params=pltpu.CompilerParams(dimension_semantics=("parallel", "parallel"),
                                             vmem_limit_bytes=VMEM_LIMIT),
        name="merge_ffn",
    )(x, yh, ya, gates, mod3, nw, wbh, wba, wo, wg, wu, wd)


def _rope_tables(n_tok):
    t = np.arange(n_tok)
    rows = (t // GRID_W).astype(np.float64)
    cols = (t % GRID_W).astype(np.float64)
    half = HEAD_DIM // 2
    inv_freq = ROPE_THETA ** (-np.arange(0, half, 2, dtype=np.float64) / half)
    d = np.arange(LANES) % HEAD_DIM
    pos = np.where((d < half)[None, :], rows[:, None], cols[:, None])
    ang = pos * inv_freq[(d % half) % (half // 2)][None, :]
    sign = np.where((d % half) < half // 2, -1.0, 1.0)[None, :]
    return jnp.asarray(np.cos(ang), F32), jnp.asarray(np.sin(ang) * sign, F32)


def kernel(x, c, ctx, c_ctx, w_ada, b_ada, norm_mix_w, norm_ffn_w, w_in, hgrn_lb_logits, hgrn_norm_w,
           q_norm_w, k_norm_w, attn_sinks, w_branch_hgrn, w_branch_attn, w_out, w_ffn_gate, w_ffn_up,
           w_ffn_down):
    b_, s_, _ = x.shape
    layer = 0
    col = np.ones((1, IN_COLS), np.float32)
    for lo, width in ((C_FF, 2 * HG_WIDTH), (C_QHG, 2 * HG_WIDTH), (C_GATES, 2 * D_MODEL)):
        col[:, lo:lo + width] = 0.5
    mod, w_in_b = _ada_call(c, c_ctx[None, :], w_ada[layer], b_ada[layer][None, :], w_in[layer],
                            jnp.asarray(col))

    lbl = hgrn_lb_logits[:, 0:2, :]
    qw = q_norm_w[layer][None, :]
    kw = k_norm_w[layer][None, :]
    nw_mix = norm_mix_w[layer][None, :]
    cos_t, sin_t = _rope_tables(s_)

    (qdf, kdf, qdb, kdb, dec, vt, v, g, q, katt, vatt, gates) = _inproj_call(
        x, mod, 0, nw_mix, w_in_b, lbl, qw, kw, cos_t, sin_t, latent=True)
    l_ = ctx.shape[1]
    pack = max(1, min(INPROJ_TILE // l_, b_))
    ctx_out = _inproj_call(ctx.reshape(b_ // pack, pack * l_, D_MODEL), mod, b_, nw_mix, w_in_b, lbl, qw, kw,
                           cos_t, sin_t, latent=False)
    ckd2f, ckd2b, cdec, cvt, ckatt, cvatt = [t.reshape((b_, t.shape[1] // pack) + t.shape[2:])
                                             for t in ctx_out]

    y_hg, (w_bh, w_ba, w_o, w_d) = _hgrn_call(
        qdf, kdf, qdb, kdb, v, vt, dec, ckd2f, ckd2b, cvt, cdec, g, hgrn_norm_w[layer][None, :],
        (w_branch_hgrn[layer], w_branch_attn[layer], w_out[layer], w_ffn_down[layer]))
    y_at, (w_g, w_u) = _attn_call(attn_sinks[layer], q, katt, vatt, ckatt, cvatt,
                                  (w_ffn_gate[layer], w_ffn_up[layer]))

    return _merge_ffn_call(x, y_hg, y_at, gates, mod, norm_ffn_w[layer][None, :],
                           w_bh, w_ba, w_o, w_g, w_u, w_d)
```

```python
import functools

import jax
import jax.numpy as jnp
import numpy as np
from jax import lax
from jax.experimental import pallas as pl
from jax.experimental.pallas import tpu as pltpu

F32 = jnp.float32
BF16 = jnp.bfloat16

D_MODEL = 1024
GRID_W = 64
EPS = 1e-6
HG_HEADS = 4
HG_DIM = 128
HG_WIDTH = HG_HEADS * HG_DIM
CHUNK = 32
ATT_HEADS = 8
ATT_KV_HEADS = 2
HEAD_DIM = 64
ATT_WIDTH = ATT_HEADS * HEAD_DIM
KV_WIDTH = ATT_KV_HEADS * HEAD_DIM
WINDOW = 128
ROPE_THETA = 10000.0
D_FF = 2816
CTX_COLS = 3 * HG_WIDTH + 2 * KV_WIDTH
IN_COLS = CTX_COLS + 2 * HG_WIDTH + ATT_WIDTH + 2 * D_MODEL

C_FF, C_FB, C_INP, C_K, C_V = 0, HG_WIDTH, 2 * HG_WIDTH, 3 * HG_WIDTH, 3 * HG_WIDTH + KV_WIDTH
C_QHG = CTX_COLS
C_GHG = C_QHG + HG_WIDTH
C_Q = C_GHG + HG_WIDTH
C_GATES = C_Q + ATT_WIDTH

LANES = 128
TOK_TILE = 512
FFN_SUBTILES = 2
MXU_DEPTH = 256
FFN_CHUNK_BOUNDS = (0, 6 * MXU_DEPTH, D_FF)
INPROJ_TILE = 512
INPROJ_DEPTH = 2
HG_TILE = 256
HG_HEADS_PER_STEP = 4
ATT_BLOCK = 128
ATT_GROUP_HEADS = 2
ATT_DEPTH = 1
ATT_STEP_BLOCKS = 8
ADA_ROWS = 16
ADA_STEPS = 6
VMEM_LIMIT = 56 * 1024 * 1024
NEG = -1e30
LOG2E = 1.4426950408889634


def _dot(a, b):
    return jnp.dot(a, b, preferred_element_type=F32)


def _dot_nt(a, b):
    return lax.dot_general(a, b, (((1,), (1,)), ((), ())), preferred_element_type=F32)


def _sigmoid(x):
    return 0.5 * jnp.tanh(0.5 * x) + 0.5


def _emit_pipelined(stages, depth):
    raws = [issue() for issue, _ in stages[:depth]]
    for i, (_, finish) in enumerate(stages):
        if i + depth < len(stages):
            raws.append(stages[i + depth][0]())
        finish(raws[i])
        raws[i] = None


def _const_spec(shape):
    n = len(shape)
    return pl.BlockSpec(shape, lambda *_: (0,) * n, pipeline_mode=pl.Buffered(1))


def _ada_kernel(c_ref, cctx_ref, w_ref, b_ref, win_ref, col_ref, o_ref, winb_ref):
    rows = o_ref.shape[0]
    n_cond = c_ref.shape[0] + 1
    c = jnp.concatenate([c_ref[...], cctx_ref[...], jnp.zeros((rows - n_cond, D_MODEL), F32)], axis=0)
    s = (c * _sigmoid(c)).astype(BF16)
    mod = _dot(s, w_ref[...].astype(BF16)) + b_ref[...]
    for r in range(mod.shape[0]):
        o_ref[r] = mod[r:r + 1, :]
    winb_ref[...] = (win_ref[...] * col_ref[...]).astype(BF16)


def _ada_call(c, c_ctx, w_ada, b_ada, w_in, col):
    rows = ADA_ROWS
    n_out = w_ada.shape[1]
    steps = ADA_STEPS
    bn = n_out // steps
    bw = w_in.shape[1] // steps
    return pl.pallas_call(
        _ada_kernel,
        grid=(steps,),
        in_specs=[pl.BlockSpec(c.shape, lambda j: (0, 0)),
                  pl.BlockSpec((1, D_MODEL), lambda j: (0, 0)),
                  pl.BlockSpec((D_MODEL, bn), lambda j: (0, j)),
                  pl.BlockSpec((1, bn), lambda j: (0, j)),
                  pl.BlockSpec((D_MODEL, bw), lambda j: (0, j)),
                  pl.BlockSpec((1, bw), lambda j: (0, j))],
        out_specs=[pl.BlockSpec((rows, 1, bn), lambda j: (0, 0, j)),
                   pl.BlockSpec((D_MODEL, bw), lambda j: (0, j))],
        out_shape=[jax.ShapeDtypeStruct((rows, 1, n_out), F32),
                   jax.ShapeDtypeStruct(w_in.shape, BF16)],
        compiler_params=pltpu.CompilerParams(dimension_semantics=("arbitrary",),
                                             vmem_limit_bytes=VMEM_LIMIT),
        name="ada_mod",
    )(c, c_ctx, w_ada, b_ada, w_in, col)


def _chunk_cumsum(x, reverse):
    n, c = x.shape
    sub = 8
    r = lax.broadcasted_iota(jnp.int32, x.shape, 0) & (CHUNK - 1)
    s = 1
    while s < sub:
        if reverse:
            x = x + jnp.where(r < CHUNK - s, pltpu.roll(x, n - s, 0), 0.0)
        else:
            x = x + jnp.where(r >= s, pltpu.roll(x, s, 0), 0.0)
        s *= 2
    per = CHUNK // sub
    x4 = x.reshape(n // CHUNK, per, sub, c)
    s = 1
    while s < per:
        if reverse:
            x4 = jnp.concatenate([x4[:, :per - s] + x4[:, s:], x4[:, per - s:]], axis=1)
        else:
            x4 = jnp.concatenate([x4[:, :s], x4[:, s:] + x4[:, :per - s]], axis=1)
        s *= 2
    return x4.reshape(n, c)


def _rope(t, cos, sin_signed, lane):
    partner = jnp.where((lane & 31) < 16, pltpu.roll(t, LANES - 16, 1), pltpu.roll(t, 16, 1))
    return t * cos + partner * sin_signed


def _inproj_kernel(x_ref, mod_ref, nw_ref, w_ref, lbl_ref, qw_ref, kw_ref, cos_ref, sin_ref,
                   *out_refs, latent):
    if latent:
        hg_ref, dec_ref, vt_ref, v_ref, g_ref, q_ref, katt_ref, vatt_ref, gates_ref = out_refs
    else:
        kd2f_ref, kd2b_ref, dec_ref, vt_ref, katt_ref, vatt_ref = out_refs

    tm = x_ref.shape[1]
    nchunk = tm // CHUNK
    x = x_ref[0]
    sh = mod_ref[0, :, 0:D_MODEL]
    sc = mod_ref[0, :, D_MODEL:2 * D_MODEL]
    ms = jnp.mean(x * x, axis=-1, keepdims=True)
    h = x * lax.rsqrt(ms + EPS) * (nw_ref[...] * (1.0 + sc)) + sh
    hb = h.astype(BF16)

    def mm(a, b):
        return _dot(hb, w_ref[:, a:b])

    l0f, l1f = lbl_ref[0, 0:1, :], lbl_ref[0, 1:2, :]
    l0b, l1b = lbl_ref[1, 0:1, :], lbl_ref[1, 1:2, :]
    lb_f = 1.0 / (1.0 + jnp.exp(l1f - l0f))
    lb_b = 1.0 / (1.0 + jnp.exp(l1b - l0b))

    qw = jnp.concatenate([qw_ref[...]] * (LANES // HEAD_DIM), axis=1)
    kw = jnp.concatenate([kw_ref[...]] * (LANES // HEAD_DIM), axis=1)
    gw = HG_WIDTH // 2
    sw = D_MODEL // 2
    lane = lax.broadcasted_iota(jnp.int32, (tm, LANES), 1)
    left = lane < HEAD_DIM
    bi = lax.broadcasted_iota(jnp.int32, (2 * LANES, LANES), 0) // HEAD_DIM
    bj = lax.broadcasted_iota(jnp.int32, (2 * LANES, LANES), 1) // HEAD_DIM
    ones_blk2 = jnp.where((bi % 2) == bj, 1.0, 0.0).astype(BF16)
    keep = {}

    def qhg_finish(raw):
        keep["q_hg"] = raw * (jnp.tanh(raw) + 1.0) * (HG_DIM ** -0.5)

    def gate_finish(raw, direction, part):
        reverse = direction == 1
        lo = part * gw
        lb = (lb_f, lb_b)[direction][:, lo:lo + gw]
        c_half = 0.5 - 0.5 * lb
        ct = c_half * jnp.tanh(raw)
        f = (0.5 + 0.5 * lb) + ct
        k = c_half - ct
        cum = _chunk_cumsum(jnp.log(f), reverse)
        cum3 = cum.reshape(nchunk, CHUNK, gw)
        tot3 = cum3[:, 0:1, :] if reverse else cum3[:, CHUNK - 1:CHUNK, :]
        dec3 = jnp.exp(tot3)
        dec_ref[0, :, direction * HG_WIDTH + lo:direction * HG_WIDTH + lo + gw] = dec3.reshape(nchunk, gw)
        kd2 = (k.reshape(nchunk, CHUNK, gw) * jnp.exp(tot3 - cum3)).reshape(tm, gw).astype(BF16)
        if latent:
            base = direction * 3 * HG_WIDTH + lo
            hg_ref[0, :, base:base + gw] = (keep["q_hg"][:, lo:lo + gw] * jnp.exp(cum)).astype(BF16)
            hg_ref[0, :, base + HG_WIDTH:base + HG_WIDTH + gw] = (k * jnp.exp(-cum)).astype(BF16)
            hg_ref[0, :, base + 2 * HG_WIDTH:base + 2 * HG_WIDTH + gw] = kd2
        else:
            (kd2f_ref, kd2b_ref)[direction][0, :, lo:lo + gw] = kd2

    def mgate_finish(raw, j):
        gates_ref[0, :, j * sw:(j + 1) * sw] = (jnp.tanh(raw) + 1.0).astype(BF16)

    def v_finish(raw):
        for s in range(tm // LANES):
            vt_ref[0, s] = raw[s * LANES:(s + 1) * LANES, :].T.astype(BF16)
        if latent:
            v_ref[0] = raw.astype(BF16)

    def g_finish(raw):
        g_ref[0] = (raw * (jnp.tanh(raw) + 1.0)).astype(BF16)

    def split_sq(t):
        sq = t * t
        hi = sq.astype(BF16)
        lo = (sq - hi.astype(F32)).astype(BF16)
        return jnp.concatenate([hi, lo], axis=1)

    def kv_finish(raw):
        keep["k_raw"] = raw[:, 0:LANES]
        keep["k_split"] = split_sq(keep["k_raw"])
        vatt_ref[0] = raw[:, LANES:2 * LANES].astype(BF16)

    def knorm_finish(ss):
        kn = keep["k_raw"] * lax.rsqrt(ss * (1.0 / HEAD_DIM) + EPS) * kw
        if latent:
            kn = _rope(kn, cos_ref[...], sin_ref[...], lane)
        for s in range(tm // LANES):
            katt_ref[0, s] = kn[s * LANES:(s + 1) * LANES, :].T.astype(BF16)

    def q_finish(raw):
        keep["q_raw"] = raw
        keep["q_split"] = [split_sq(raw[:, s * LANES:(s + 1) * LANES]) for s in range(ATT_WIDTH // LANES)]

    def qnorm_finish(ss_list):
        for s, ss in enumerate(ss_list):
            qraw = keep["q_raw"][:, s * LANES:(s + 1) * LANES]
            qn = qraw * lax.rsqrt(ss * (1.0 / HEAD_DIM) + EPS) * qw
            qn = _rope(qn, cos_ref[...], sin_ref[...], lane) * (HEAD_DIM ** -0.5 * LOG2E)
            swapped = pltpu.roll(qn, HEAD_DIM, 1)
            if s < ATT_WIDTH // LANES // 2:
                even, odd = jnp.where(left, qn, 0.0), jnp.where(left, swapped, 0.0)
            else:
                even, odd = jnp.where(left, 0.0, swapped), jnp.where(left, 0.0, qn)
            q_ref[0, :, (2 * s) * LANES:(2 * s + 1) * LANES] = even.astype(BF16)
            q_ref[0, :, (2 * s + 1) * LANES:(2 * s + 2) * LANES] = odd.astype(BF16)

    def gate_stage(direction, part):
        col = (C_FF, C_FB)[direction] + part * gw
        return (lambda: mm(col, col + gw),
                functools.partial(gate_finish, direction=direction, part=part))

    def mgate_stage(j):
        c0 = C_GATES + j * sw
        return (lambda: mm(c0, c0 + sw), functools.partial(mgate_finish, j=j))

    v_stage = (lambda: mm(C_INP, C_INP + HG_WIDTH), v_finish)
    kv_stage = (lambda: mm(C_K, C_K + 2 * KV_WIDTH), kv_finish)
    knorm_stage = (lambda: _dot(keep["k_split"], ones_blk2), knorm_finish)
    if latent:
        stages = [(lambda: mm(C_QHG, C_QHG + HG_WIDTH), qhg_finish)]
        for j in range(4):
            stages += [gate_stage(j // 2, j % 2), mgate_stage(j)]
        stages += [(lambda: mm(C_Q, C_Q + ATT_WIDTH), q_finish),
                   v_stage,
                   kv_stage,
                   (lambda: [_dot(t, ones_blk2) for t in keep["q_split"]], qnorm_finish),
                   (lambda: mm(C_GHG, C_GHG + HG_WIDTH), g_finish),
                   knorm_stage]
    else:
        stages = [gate_stage(0, 0), gate_stage(0, 1), kv_stage, gate_stage(1, 0), gate_stage(1, 1),
                  knorm_stage, v_stage]
    _emit_pipelined(stages, INPROJ_DEPTH)


def _inproj_call(x, mod3, mod_row0, nw, w_in, lbl, qw, kw, cos_t, sin_t, latent):
    b_, s_, _ = x.shape
    tm = min(INPROJ_TILE, s_)
    grid = (b_, s_ // tm)
    tok = lambda w: pl.BlockSpec((1, tm, w), lambda b, i: (b, i, 0))
    mod_map = (lambda b, i: (mod_row0 + b, 0, 0)) if latent else (lambda b, i: (mod_row0, 0, 0))
    ncols = IN_COLS if latent else CTX_COLS
    in_specs = [tok(D_MODEL),
                pl.BlockSpec((1, 1, mod3.shape[2]), mod_map),
                _const_spec((1, D_MODEL)),
                pl.BlockSpec((D_MODEL, ncols), lambda b, i: (0, 0), pipeline_mode=pl.Buffered(1)),
                _const_spec(lbl.shape),
                _const_spec((1, HEAD_DIM)),
                _const_spec((1, HEAD_DIM)),
                pl.BlockSpec((tm, LANES), lambda b, i: (i, 0)),
                pl.BlockSpec((tm, LANES), lambda b, i: (i, 0))]
    bf = lambda w: jax.ShapeDtypeStruct((b_, s_, w), BF16)
    dec_shape = jax.ShapeDtypeStruct((b_, s_ // CHUNK, 2 * HG_WIDTH), F32)
    dec_spec = pl.BlockSpec((1, tm // CHUNK, 2 * HG_WIDTH), lambda b, i: (b, i, 0))
    vt_shape = jax.ShapeDtypeStruct((b_, s_ // LANES, HG_WIDTH, LANES), BF16)
    vt_spec = pl.BlockSpec((1, tm // LANES, HG_WIDTH, LANES), lambda b, i: (b, i, 0, 0))
    katt_shape = jax.ShapeDtypeStruct((b_, s_ // LANES, KV_WIDTH, LANES), BF16)
    katt_spec = pl.BlockSpec((1, tm // LANES, KV_WIDTH, LANES), lambda b, i: (b, i, 0, 0))
    if latent:
        out_shape = [bf(6 * HG_WIDTH), dec_shape, vt_shape, bf(HG_WIDTH), bf(HG_WIDTH),
                                         bf(ATT_HEADS * LANES), katt_shape, bf(KV_WIDTH), bf(2 * D_MODEL)]
        out_specs = [tok(6 * HG_WIDTH), dec_spec, vt_spec, tok(HG_WIDTH), tok(HG_WIDTH),
                                          tok(ATT_HEADS * LANES), katt_spec, tok(KV_WIDTH), tok(2 * D_MODEL)]
    else:
        out_shape = [bf(HG_WIDTH)] * 2 + [dec_shape, vt_shape, katt_shape, bf(KV_WIDTH)]
        out_specs = [tok(HG_WIDTH)] * 2 + [dec_spec, vt_spec, katt_spec, tok(KV_WIDTH)]
    return pl.pallas_call(
        functools.partial(_inproj_kernel, latent=latent),
        grid=grid, in_specs=in_specs, out_specs=out_specs, out_shape=out_shape,
        compiler_params=pltpu.CompilerParams(dimension_semantics=("parallel", "parallel"),
                                             vmem_limit_bytes=VMEM_LIMIT),
        name="inproj_latent" if latent else "inproj_ctx",
    )(x, mod3, nw, w_in, lbl, qw, kw, cos_t, sin_t)


def _cast_side_job(w_refs, wb_refs):
    for w_ref, wb_ref in zip(w_refs, wb_refs):
        wb_ref[...] = w_ref[...].astype(BF16)


def _side_job_specs(weights, n_steps, step_of):
    specs = [pl.BlockSpec((wt.shape[0] // n_steps, wt.shape[1]), lambda *ids: (step_of(*ids), 0))
             for wt in weights]
    shapes = [jax.ShapeDtypeStruct(wt.shape, BF16) for wt in weights]
    return specs, shapes


def _hgrn_kernel(qdf_ref, kdf_ref, kd2f_ref, qdb_ref, kdb_ref, kd2b_ref, v_ref, vt_ref,
                 decf_ref, decb_ref, ckd2f_ref, ckd2b_ref, cvt_ref, cdecf_ref, cdecb_ref,
                 g_ref, nw_ref, *rest):
    n_w = (len(rest) - 2) // 2
    w_refs, y_ref, wb_refs, o_acc = rest[:n_w], rest[n_w], rest[n_w + 1:2 * n_w + 1], rest[-1]
    _cast_side_job(w_refs, wb_refs)

    seq = v_ref.shape[1]
    ctx_len = ckd2f_ref.shape[1]
    heads = v_ref.shape[2] // HG_DIM
    tile = HG_TILE
    cpt = tile // CHUNK
    cps = LANES // CHUNK
    spt = tile // LANES
    n_tiles = seq // tile
    half = n_tiles // 2

    row_chunk = lax.broadcasted_iota(jnp.int32, (LANES, HG_DIM), 0) // CHUNK
    ti = lax.broadcasted_iota(jnp.int32, (LANES, LANES), 0)
    tj = lax.broadcasted_iota(jnp.int32, (LANES, LANES), 1)
    same = (ti // CHUNK) == (tj // CHUNK)
    mask_f = same & (tj <= ti)
    mask_b = same & (tj >= ti)

    def chunk_update(vt_slab, k_slab, n_in_slab):
        k_m = jnp.where(row_chunk == n_in_slab, k_slab, jnp.zeros_like(k_slab))
        return _dot(vt_slab, k_m)

    def ctx_state(ckd2_ref, cdec_ref, cols, reverse):
        st = jnp.zeros((HG_DIM, HG_DIM), F32)
        order = range(ctx_len // CHUNK)
        for n in (reversed(order) if reverse else order):
            slab = n // cps
            st = cdec_ref[0, n:n + 1, cols] * st + chunk_update(
                cvt_ref[0, slab, cols, :], ckd2_ref[0, slab * LANES:(slab + 1) * LANES, cols], n % cps)
        return st

    def emit(o_t, cols, rows, finalize):
        if finalize:
            o_t = o_t + o_acc[rows, cols]
            ms = jnp.mean(o_t * o_t, axis=-1, keepdims=True)
            y = o_t * lax.rsqrt(ms + EPS) * nw_ref[...] * g_ref[0, rows, cols].astype(F32)
            y_ref[0, rows, cols] = y.astype(BF16)
        else:
            o_acc[rows, cols] = o_t

    def body(it, states, finalize):
        chains = []
        for h in range(heads):
            cols = slice(h * HG_DIM, (h + 1) * HG_DIM)
            chains.append((qdf_ref, kdf_ref, decf_ref, cols, it, False, kd2f_ref))
            chains.append((qdb_ref, kdb_ref, decb_ref, cols, n_tiles - 1 - it, True, kd2b_ref))
        n_ch = len(chains)
        rows = [pl.ds(pl.multiple_of(c[4] * tile, tile), tile) for c in chains]
        orders = [list(reversed(range(cpt))) if c[5] else list(range(cpt)) for c in chains]
        qd, sc, ups = [], [], []
        for (qd_ref, kd_ref, _, cols, t, reverse, kd2_ref), r in zip(chains, rows):
            qd.append(qd_ref[0, r, cols])
            kd = kd_ref[0, r, cols]
            sc.append([_dot_nt(qd[-1][s * LANES:(s + 1) * LANES, :], kd[s * LANES:(s + 1) * LANES, :])
                       for s in range(spt)])
            kd2 = kd2_ref[0, r, cols]
            ups.append([chunk_update(vt_ref[0, t * spt + n // cps, cols, :],
                                     kd2[(n // cps) * LANES:(n // cps + 1) * LANES, :], n % cps)
                        for n in range(cpt)])
        o = []
        for c, r, s in zip(chains, rows, sc):
            v = v_ref[0, r, c[3]]
            o.append(jnp.concatenate(
                [_dot(jnp.where(mask_b if c[5] else mask_f, s[j], 0.0).astype(BF16),
                      v[j * LANES:(j + 1) * LANES, :]) for j in range(spt)], axis=0))
        starts, new_states = [], []
        for i, c in enumerate(chains):
            dec = c[2][0, pl.ds(pl.multiple_of(c[4] * cpt, cpt), cpt), c[3]]
            st = states[i]
            start = [None] * cpt
            for n in orders[i]:
                start[n] = st.T.astype(BF16)
                st = dec[n:n + 1, :] * st + ups[i][n]
            starts.append(start)
            new_states.append(st)
        for i, c in enumerate(chains):
            outs = [o[i][n * CHUNK:(n + 1) * CHUNK, :]
                    + _dot(qd[i][n * CHUNK:(n + 1) * CHUNK, :], starts[i][n]) for n in range(cpt)]
            emit(jnp.concatenate(outs, axis=0), c[3], rows[i], finalize)
        return tuple(new_states)

    states = []
    for h in range(heads):
        cols = slice(h * HG_DIM, (h + 1) * HG_DIM)
        states += [ctx_state(ckd2f_ref, cdecf_ref, cols, False), ctx_state(ckd2b_ref, cdecb_ref, cols, True)]
    states = lax.fori_loop(0, half, functools.partial(body, finalize=False), tuple(states))
    lax.fori_loop(half, n_tiles, functools.partial(body, finalize=True), states)


def _hgrn_call(hg, v, vt, dec, ckd2f, ckd2b, cvt, cdec, g, nw, weights):
    b_, s_, _ = v.shape
    l_ = ckd2f.shape[1]
    w = HG_HEADS_PER_STEP * HG_DIM
    groups = HG_HEADS // HG_HEADS_PER_STEP
    seq = lambda n: pl.BlockSpec((1, n, w), lambda b, h: (b, 0, h))
    vt_spec = lambda n: pl.BlockSpec((1, n // LANES, w, LANES), lambda b, h: (b, 0, h, 0))
    dec_f = lambda n: pl.BlockSpec((1, n // CHUNK, w), lambda b, h: (b, 0, h))
    dec_b = lambda n: pl.BlockSpec((1, n // CHUNK, w), lambda b, h: (b, 0, groups + h))
    w_specs, w_shapes = _side_job_specs(weights, b_ * groups, lambda b, h: b * groups + h)
    outs = pl.pallas_call(
        _hgrn_kernel,
        grid=(b_, groups),
        in_specs=[pl.BlockSpec((1, s_, w), lambda b, h, j=j: (b, 0, j * groups + h)) for j in range(6)]
        + [seq(s_), vt_spec(s_), dec_f(s_), dec_b(s_),
                                  seq(l_), seq(l_), vt_spec(l_), dec_f(l_), dec_b(l_),
                                  seq(s_), _const_spec((1, HG_DIM))] + w_specs,
        out_specs=[seq(s_)] + w_specs,
        out_shape=[jax.ShapeDtypeStruct((b_, s_, HG_WIDTH), BF16)] + w_shapes,
        scratch_shapes=[pltpu.VMEM((s_, w), F32)],
        compiler_params=pltpu.CompilerParams(dimension_semantics=("parallel", "parallel"),
                                             vmem_limit_bytes=VMEM_LIMIT),
        name="hgrn_scan",
    )(hg, hg, hg, hg, hg, hg, v, vt, dec, dec, ckd2f, ckd2b, cvt, cdec, cdec, g, nw, *weights)
    return outs[0], outs[1:]


def _attn_kernel(sink_ref, q_ref, kt_ref, kc_ref, v_ref, vc_ref, *rest):
    n_w = (len(rest) - 1) // 2
    y_ref = rest[n_w]
    _cast_side_job(rest[:n_w], rest[n_w + 1:])
    nb = kt_ref.shape[1]
    blk = ATT_BLOCK
    hpg = ATT_GROUP_HEADS
    n_groups = ATT_HEADS // hpg
    qry_r = lax.broadcasted_iota(jnp.int32, (blk, blk), 0)
    key_c = lax.broadcasted_iota(jnp.int32, (blk, blk), 1)
    left = lax.broadcasted_iota(jnp.int32, (blk, LANES), 1) < HEAD_DIM
    ctx_keys = [kc_ref[0, c] for c in range(kc_ref.shape[1])]
    keep = {}

    def band(u):
        qb = pl.program_id(1) * ATT_STEP_BLOCKS + u
        ids = (jnp.maximum(qb - 1, 0), qb, jnp.minimum(qb + 1, nb - 1))
        keys_t = jnp.concatenate([kt_ref[0, j] for j in ids] + ctx_keys, axis=1)
        values = jnp.concatenate([v_ref[0, pl.ds(pl.multiple_of(j * blk, blk), blk), :] for j in ids]
                                 + [vc_ref[0]], axis=0)
        values = jnp.concatenate([values, jnp.ones_like(values)], axis=1)
        return qb, keys_t, values

    def score_issue(u, g):
        if g == 0:
            keep["band", u] = band(u)
        q = jnp.concatenate([q_ref[0, u * blk:(u + 1) * blk, h * LANES:(h + 1) * LANES]
                             for h in range(g * hpg, (g + 1) * hpg)], axis=0)
        return _dot(q, keep["band", u][1])

    def score_finish(raw, u, g):
        qb = keep["band", u][0]
        mask_prev = (key_c >= qry_r) & (qb > 0)
        mask_next = (key_c <= qry_r) & (qb < nb - 1)
        probs, sink_p = [], []
        for n in range(hpg):
            s = raw[n * blk:(n + 1) * blk, :]
            sink = sink_ref[g * hpg + n] * LOG2E
            s = jnp.concatenate([jnp.where(mask_prev, s[:, 0:blk], NEG), s[:, blk:2 * blk],
                                 jnp.where(mask_next, s[:, 2 * blk:3 * blk], NEG), s[:, 3 * blk:]], axis=1)
            mx = jnp.maximum(jnp.max(s, axis=1, keepdims=True), sink)
            probs.append(jnp.exp2(s - mx).astype(BF16))
            sink_p.append(jnp.exp2(sink - mx))
        keep["p", u, g] = jnp.concatenate(probs, axis=0)
        keep["sink_p", u, g] = sink_p

    def value_issue(u, g):
        return _dot(keep["p", u, g], keep["band", u][2])

    def value_finish(raw, u, g):
        for n in range(0, hpg, 2):
            h = g * hpg + n
            even, odd = [raw[m * blk:(m + 1) * blk, 0:LANES]
                         / (raw[m * blk:(m + 1) * blk, LANES:2 * LANES] + keep["sink_p", u, g][m])
                         for m in (n, n + 1)]
            if h < ATT_HEADS // ATT_KV_HEADS:
                y = jnp.where(left, even, pltpu.roll(odd, HEAD_DIM, 1))
            else:
                y = jnp.where(left, pltpu.roll(even, HEAD_DIM, 1), odd)
            y_ref[0, u * blk:(u + 1) * blk, (h // 2) * LANES:(h // 2 + 1) * LANES] = y.astype(BF16)

    part = functools.partial
    work = [(u, g) for u in range(ATT_STEP_BLOCKS) for g in range(n_groups)]
    score = lambda u, g: (part(score_issue, u, g), part(score_finish, u=u, g=g))
    stages = [score(*w) for w in work[:ATT_DEPTH + 1]]
    for n, (u, g) in enumerate(work):
        stages.append((part(value_issue, u, g), part(value_finish, u=u, g=g)))
        if n + ATT_DEPTH + 1 < len(work):
            stages.append(score(*work[n + ATT_DEPTH + 1]))
    _emit_pipelined(stages, ATT_DEPTH)


def _attn_call(sinks, q, katt, vatt, ckatt, cvatt, weights):
    b_, s_, qw = q.shape
    l_ = cvatt.shape[1]
    rows = ATT_STEP_BLOCKS * ATT_BLOCK
    batch4 = lambda n: pl.BlockSpec((1, n, KV_WIDTH, LANES), lambda b, i: (b, 0, 0, 0))
    batch3 = lambda n: pl.BlockSpec((1, n, KV_WIDTH), lambda b, i: (b, 0, 0))
    steps = s_ // rows
    w_specs, w_shapes = _side_job_specs(weights, b_ * steps, lambda b, i: b * steps + i)
    outs = pl.pallas_call(
        _attn_kernel,
        grid=(b_, steps),
        in_specs=[pl.BlockSpec(memory_space=pltpu.SMEM),
                  pl.BlockSpec((1, rows, qw), lambda b, i: (b, i, 0)),
                  batch4(s_ // LANES), batch4(l_ // LANES), batch3(s_), batch3(l_)] + w_specs,
        out_specs=[pl.BlockSpec((1, rows, ATT_WIDTH), lambda b, i: (b, i, 0))] + w_specs,
        out_shape=[jax.ShapeDtypeStruct((b_, s_, ATT_WIDTH), BF16)] + w_shapes,
        compiler_params=pltpu.CompilerParams(dimension_semantics=("parallel", "arbitrary"),
                                             vmem_limit_bytes=VMEM_LIMIT),
        name="window_attn",
    )(sinks, q, katt, ckatt, vatt, cvatt, *weights)
    return outs[0], outs[1:]


def _merge_ffn_kernel(x_ref, yh_ref, ya_ref, gates_ref, mod_ref, nw_ref, wbh_ref, wba_ref, wo_ref,
                      wg_ref, wu_ref, wd_ref, o_ref):
    tm = x_ref.shape[1]
    n_sub = FFN_SUBTILES
    sub = tm // n_sub
    bounds = FFN_CHUNK_BOUNDS
    n_chunks = len(bounds) - 1
    g1 = mod_ref[0, :, 2 * D_MODEL:3 * D_MODEL]
    sh2 = mod_ref[0, :, 3 * D_MODEL:4 * D_MODEL]
    sc2 = mod_ref[0, :, 4 * D_MODEL:5 * D_MODEL]
    g2 = mod_ref[0, :, 5 * D_MODEL:6 * D_MODEL]
    keep = {}

    def branch_issue(r):
        rows = slice(r * sub, (r + 1) * sub)
        return _dot(yh_ref[0, rows, :], wbh_ref[...]), _dot(ya_ref[0, rows, :], wba_ref[...])

    def branch_finish(raw, r):
        rows = slice(r * sub, (r + 1) * sub)
        a, b = raw
        mixed = (gates_ref[0, rows, 0:D_MODEL].astype(F32) * a
                 + gates_ref[0, rows, D_MODEL:2 * D_MODEL].astype(F32) * b)
        keep["mixed", r] = (0.5 * mixed).astype(BF16)

    def out_issue(r):
        return _dot(keep["mixed", r], wo_ref[...])

    def out_finish(raw, r):
        x1 = x_ref[0, r * sub:(r + 1) * sub, :] + g1 * raw
        ms = jnp.mean(x1 * x1, axis=-1, keepdims=True)
        keep["x1", r] = x1
        keep["h2", r] = ((x1 * lax.rsqrt(ms + EPS) * nw_ref[...]) * (1.0 + sc2) + sh2).astype(BF16)

    def hidden_issue(r, c):
        h2 = keep["h2", r]
        cols = slice(bounds[c], bounds[c + 1])
        return _dot(h2, wg_ref[:, cols]), _dot(h2, wu_ref[:, cols])

    def hidden_finish(raw, r, c):
        gate, up = raw
        keep["act", r, c] = (gate * _sigmoid(gate) * up).astype(BF16)

    def down_issue(r, c):
        return _dot(keep["act", r, c], wd_ref[bounds[c]:bounds[c + 1], :])

    def down_finish(raw, r, c):
        dn = raw if c == 0 else keep["dn", r] + raw
        if c + 1 < n_chunks:
            keep["dn", r] = dn
        else:
            o_ref[0, r * sub:(r + 1) * sub, :] = keep["x1", r] + g2 * dn

    part = functools.partial
    stages = []
    for issue, finish in ((branch_issue, branch_finish), (out_issue, out_finish)):
        stages += [(part(issue, r), part(finish, r=r)) for r in range(n_sub)]
    for c in range(n_chunks):
        stages += [(part(hidden_issue, r, c), part(hidden_finish, r=r, c=c)) for r in range(n_sub)]
        stages += [(part(down_issue, r, c), part(down_finish, r=r, c=c)) for r in range(n_sub)]
    _emit_pipelined(stages, n_sub - 1)


def _merge_ffn_call(x, yh, ya, gates, mod3, nw, wbh, wba, wo, wg, wu, wd):
    b_, s_, _ = x.shape
    tm = TOK_TILE
    tok = lambda w: pl.BlockSpec((1, tm, w), lambda b, i: (b, i, 0))
    return pl.pallas_call(
        _merge_ffn_kernel,
        grid=(b_, s_ // tm),
        in_specs=[tok(D_MODEL), tok(HG_WIDTH), tok(ATT_WIDTH), tok(2 * D_MODEL),
                  pl.BlockSpec((1, 1, mod3.shape[2]), lambda b, i: (b, 0, 0)),
                  _const_spec((1, D_MODEL)),
                  _const_spec(wbh.shape), _const_spec(wba.shape), _const_spec(wo.shape),
                  _const_spec(wg.shape), _const_spec(wu.shape), _const_spec(wd.shape)],
        out_specs=tok(D_MODEL),
        out_shape=jax.ShapeDtypeStruct(x.shape, F32),
        compiler_params=pltpu.CompilerParams(dimension_semantics=("parallel", "parallel"),
                                             vmem_limit_bytes=VMEM_LIMIT),
        name="merge_ffn",
    )(x, yh, ya, gates, mod3, nw, wbh, wba, wo, wg, wu, wd)


def _rope_tables(n_tok):
    t = np.arange(n_tok)
    rows = (t // GRID_W).astype(np.float64)
    cols = (t % GRID_W).astype(np.float64)
    half = HEAD_DIM // 2
    inv_freq = ROPE_THETA ** (-np.arange(0, half, 2, dtype=np.float64) / half)
    d = np.arange(LANES) % HEAD_DIM
    pos = np.where((d < half)[None, :], rows[:, None], cols[:, None])
    ang = pos * inv_freq[(d % half) % (half // 2)][None, :]
    sign = np.where((d % half) < half // 2, -1.0, 1.0)[None, :]
    return jnp.asarray(np.cos(ang), F32), jnp.asarray(np.sin(ang) * sign, F32)


def kernel(x, c, ctx, c_ctx, w_ada, b_ada, norm_mix_w, norm_ffn_w, w_in, hgrn_lb_logits, hgrn_norm_w,
           q_norm_w, k_norm_w, attn_sinks, w_branch_hgrn, w_branch_attn, w_out, w_ffn_gate, w_ffn_up,
           w_ffn_down):
    b_, s_, _ = x.shape
    layer = 0
    col = np.ones((1, IN_COLS), np.float32)
    for lo, width in ((C_FF, 2 * HG_WIDTH), (C_QHG, 2 * HG_WIDTH), (C_GATES, 2 * D_MODEL)):
        col[:, lo:lo + width] = 0.5
    mod, w_in_b = _ada_call(c, c_ctx[None, :], w_ada[layer], b_ada[layer][None, :], w_in[layer],
                            jnp.asarray(col))

    lbl = hgrn_lb_logits[:, 0:2, :]
    qw = q_norm_w[layer][None, :]
    kw = k_norm_w[layer][None, :]
    nw_mix = norm_mix_w[layer][None, :]
    cos_t, sin_t = _rope_tables(s_)

    (hg, dec, vt, v, g, q, katt, vatt, gates) = _inproj_call(
        x, mod, 0, nw_mix, w_in_b, lbl, qw, kw, cos_t, sin_t, latent=True)
    l_ = ctx.shape[1]
    pack = max(1, min(INPROJ_TILE // l_, b_))
    ctx_out = _inproj_call(ctx.reshape(b_ // pack, pack * l_, D_MODEL), mod, b_, nw_mix, w_in_b, lbl, qw, kw,
                           cos_t, sin_t, latent=False)
    ckd2f, ckd2b, cdec, cvt, ckatt, cvatt = [t.reshape((b_, t.shape[1] // pack) + t.shape[2:])
                                             for t in ctx_out]

    y_hg, (w_bh, w_ba, w_o) = _hgrn_call(
        hg, v, vt, dec, ckd2f, ckd2b, cvt, cdec, g, hgrn_norm_w[layer][None, :],
        (w_branch_hgrn[layer], w_branch_attn[layer], w_out[layer]))
    y_at, (w_g, w_u, w_d) = _attn_call(attn_sinks[layer], q, katt, vatt, ckatt, cvatt,
                                       (w_ffn_gate[layer], w_ffn_up[layer], w_ffn_down[layer]))

    return _merge_ffn_call(x, y_hg, y_at, gates, mod, norm_ffn_w[layer][None, :],
                           w_bh, w_ba, w_o, w_g, w_u, w_d)
```

```python
import functools

import jax
import jax.numpy as jnp
import numpy as np
from jax import lax
from jax.experimental import pallas as pl
from jax.experimental.pallas import tpu as pltpu

F32 = jnp.float32
BF16 = jnp.bfloat16

D_MODEL = 1024
GRID_W = 64
EPS = 1e-6
HG_HEADS = 4
HG_DIM = 128
HG_WIDTH = HG_HEADS * HG_DIM
CHUNK = 32
ATT_HEADS = 8
ATT_KV_HEADS = 2
HEAD_DIM = 64
ATT_WIDTH = ATT_HEADS * HEAD_DIM
KV_WIDTH = ATT_KV_HEADS * HEAD_DIM
WINDOW = 128
ROPE_THETA = 10000.0
D_FF = 2816
CTX_COLS = 3 * HG_WIDTH + 2 * KV_WIDTH
IN_COLS = CTX_COLS + 2 * HG_WIDTH + ATT_WIDTH + 2 * D_MODEL

C_FF, C_FB, C_INP, C_K, C_V = 0, HG_WIDTH, 2 * HG_WIDTH, 3 * HG_WIDTH, 3 * HG_WIDTH + KV_WIDTH
C_QHG = CTX_COLS
C_GHG = C_QHG + HG_WIDTH
C_Q = C_GHG + HG_WIDTH
C_GATES = C_Q + ATT_WIDTH

LANES = 128
TOK_TILE = 512
FFN_SUBTILES = 2
MXU_DEPTH = 256
FFN_CHUNK_BOUNDS = (0, 6 * MXU_DEPTH, D_FF)
INPROJ_TILE = 256
INPROJ_DEPTH = 2
HG_TILE = 256
HG_HEADS_PER_STEP = 4
ATT_BLOCK = 128
ATT_GROUP_HEADS = 2
ATT_DEPTH = 1
ATT_STEP_BLOCKS = 8
ADA_ROWS = 16
ADA_STEPS = 6
VMEM_LIMIT = 56 * 1024 * 1024
NEG = -1e30
LOG2E = 1.4426950408889634


def _dot(a, b):
    return jnp.dot(a, b, preferred_element_type=F32)


def _dot_nt(a, b):
    return lax.dot_general(a, b, (((1,), (1,)), ((), ())), preferred_element_type=F32)


def _sigmoid(x):
    return 0.5 * jnp.tanh(0.5 * x) + 0.5


def _emit_pipelined(stages, depth):
    raws = [issue() for issue, _ in stages[:depth]]
    for i, (_, finish) in enumerate(stages):
        if i + depth < len(stages):
            raws.append(stages[i + depth][0]())
        finish(raws[i])
        raws[i] = None


def _const_spec(shape):
    n = len(shape)
    return pl.BlockSpec(shape, lambda *_: (0,) * n, pipeline_mode=pl.Buffered(1))


def _ada_kernel(c_ref, cctx_ref, w_ref, b_ref, win_ref, col_ref, o_ref, winb_ref):
    rows = o_ref.shape[0]
    n_cond = c_ref.shape[0] + 1
    c = jnp.concatenate([c_ref[...], cctx_ref[...], jnp.zeros((rows - n_cond, D_MODEL), F32)], axis=0)
    s = (c * _sigmoid(c)).astype(BF16)
    mod = _dot(s, w_ref[...].astype(BF16)) + b_ref[...]
    for r in range(mod.shape[0]):
        o_ref[r] = mod[r:r + 1, :]
    winb_ref[...] = (win_ref[...] * col_ref[...]).astype(BF16)


def _ada_call(c, c_ctx, w_ada, b_ada, w_in, col):
    rows = ADA_ROWS
    n_out = w_ada.shape[1]
    steps = ADA_STEPS
    bn = n_out // steps
    bw = w_in.shape[1] // steps
    return pl.pallas_call(
        _ada_kernel,
        grid=(steps,),
        in_specs=[pl.BlockSpec(c.shape, lambda j: (0, 0)),
                  pl.BlockSpec((1, D_MODEL), lambda j: (0, 0)),
                  pl.BlockSpec((D_MODEL, bn), lambda j: (0, j)),
                  pl.BlockSpec((1, bn), lambda j: (0, j)),
                  pl.BlockSpec((D_MODEL, bw), lambda j: (0, j)),
                  pl.BlockSpec((1, bw), lambda j: (0, j))],
        out_specs=[pl.BlockSpec((rows, 1, bn), lambda j: (0, 0, j)),
                   pl.BlockSpec((D_MODEL, bw), lambda j: (0, j))],
        out_shape=[jax.ShapeDtypeStruct((rows, 1, n_out), F32),
                   jax.ShapeDtypeStruct(w_in.shape, BF16)],
        compiler_params=pltpu.CompilerParams(dimension_semantics=("arbitrary",),
                                             vmem_limit_bytes=VMEM_LIMIT),
        name="ada_mod",
    )(c, c_ctx, w_ada, b_ada, w_in, col)


def _chunk_cumsum(x, reverse):
    n, c = x.shape
    sub = 8
    r = lax.broadcasted_iota(jnp.int32, x.shape, 0) & (CHUNK - 1)
    s = 1
    while s < sub:
        if reverse:
            x = x + jnp.where(r < CHUNK - s, pltpu.roll(x, n - s, 0), 0.0)
        else:
            x = x + jnp.where(r >= s, pltpu.roll(x, s, 0), 0.0)
        s *= 2
    per = CHUNK // sub
    x4 = x.reshape(n // CHUNK, per, sub, c)
    s = 1
    while s < per:
        if reverse:
            x4 = jnp.concatenate([x4[:, :per - s] + x4[:, s:], x4[:, per - s:]], axis=1)
        else:
            x4 = jnp.concatenate([x4[:, :s], x4[:, s:] + x4[:, :per - s]], axis=1)
        s *= 2
    return x4.reshape(n, c)


def _rope(t, cos, sin_signed, lane):
    partner = jnp.where((lane & 31) < 16, pltpu.roll(t, LANES - 16, 1), pltpu.roll(t, 16, 1))
    return t * cos + partner * sin_signed


def _inproj_kernel(x_ref, mod_ref, nw_ref, w_ref, lbl_ref, qw_ref, kw_ref, cos_ref, sin_ref,
                   *out_refs, latent):
    if latent:
        hg_ref, dec_ref, vt_ref, v_ref, g_ref, q_ref, katt_ref, vatt_ref, gates_ref = out_refs
    else:
        kd2f_ref, kd2b_ref, dec_ref, vt_ref, katt_ref, vatt_ref = out_refs

    tm = x_ref.shape[1]
    nchunk = tm // CHUNK
    x = x_ref[0]
    sh = mod_ref[0, :, 0:D_MODEL]
    sc = mod_ref[0, :, D_MODEL:2 * D_MODEL]
    ms = jnp.mean(x * x, axis=-1, keepdims=True)
    h = x * lax.rsqrt(ms + EPS) * (nw_ref[...] * (1.0 + sc)) + sh
    hb = h.astype(BF16)

    def mm(a, b):
        return _dot(hb, w_ref[:, a:b])

    l0f, l1f = lbl_ref[0, 0:1, :], lbl_ref[0, 1:2, :]
    l0b, l1b = lbl_ref[1, 0:1, :], lbl_ref[1, 1:2, :]
    lb_f = 1.0 / (1.0 + jnp.exp(l1f - l0f))
    lb_b = 1.0 / (1.0 + jnp.exp(l1b - l0b))

    qw = jnp.concatenate([qw_ref[...]] * (LANES // HEAD_DIM), axis=1)
    kw = jnp.concatenate([kw_ref[...]] * (LANES // HEAD_DIM), axis=1)
    gw = HG_WIDTH // 2
    sw = D_MODEL // 2
    lane = lax.broadcasted_iota(jnp.int32, (tm, LANES), 1)
    left = lane < HEAD_DIM
    bi = lax.broadcasted_iota(jnp.int32, (2 * LANES, LANES), 0) // HEAD_DIM
    bj = lax.broadcasted_iota(jnp.int32, (2 * LANES, LANES), 1) // HEAD_DIM
    ones_blk2 = jnp.where((bi % 2) == bj, 1.0, 0.0).astype(BF16)
    keep = {}

    def qhg_finish(raw):
        keep["q_hg"] = raw * (jnp.tanh(raw) + 1.0) * (HG_DIM ** -0.5)

    def gate_finish(raw, direction, part):
        reverse = direction == 1
        lo = part * gw
        lb = (lb_f, lb_b)[direction][:, lo:lo + gw]
        c_half = 0.5 - 0.5 * lb
        ct = c_half * jnp.tanh(raw)
        f = (0.5 + 0.5 * lb) + ct
        k = c_half - ct
        cum = _chunk_cumsum(jnp.log(f), reverse)
        cum3 = cum.reshape(nchunk, CHUNK, gw)
        tot3 = cum3[:, 0:1, :] if reverse else cum3[:, CHUNK - 1:CHUNK, :]
        dec3 = jnp.exp(tot3)
        dec_ref[0, :, direction * HG_WIDTH + lo:direction * HG_WIDTH + lo + gw] = dec3.reshape(nchunk, gw)
        kd2 = (k.reshape(nchunk, CHUNK, gw) * jnp.exp(tot3 - cum3)).reshape(tm, gw).astype(BF16)
        if latent:
            base = direction * 3 * HG_WIDTH + lo
            hg_ref[0, :, base:base + gw] = (keep["q_hg"][:, lo:lo + gw] * jnp.exp(cum)).astype(BF16)
            hg_ref[0, :, base + HG_WIDTH:base + HG_WIDTH + gw] = (k * jnp.exp(-cum)).astype(BF16)
            hg_ref[0, :, base + 2 * HG_WIDTH:base + 2 * HG_WIDTH + gw] = kd2
        else:
            (kd2f_ref, kd2b_ref)[direction][0, :, lo:lo + gw] = kd2

    def mgate_finish(raw, j):
        gates_ref[0, :, j * sw:(j + 1) * sw] = (jnp.tanh(raw) + 1.0).astype(BF16)

    def v_finish(raw):
        for s in range(tm // LANES):
            vt_ref[0, s] = raw[s * LANES:(s + 1) * LANES, :].T.astype(BF16)
        if latent:
            v_ref[0] = raw.astype(BF16)

    def g_finish(raw):
        g_ref[0] = (raw * (jnp.tanh(raw) + 1.0)).astype(BF16)

    def split_sq(t):
        sq = t * t
        hi = sq.astype(BF16)
        lo = (sq - hi.astype(F32)).astype(BF16)
        return jnp.concatenate([hi, lo], axis=1)

    def kv_finish(raw):
        keep["k_raw"] = raw[:, 0:LANES]
        keep["k_split"] = split_sq(keep["k_raw"])
        vatt_ref[0] = raw[:, LANES:2 * LANES].astype(BF16)

    def knorm_finish(ss):
        kn = keep["k_raw"] * lax.rsqrt(ss * (1.0 / HEAD_DIM) + EPS) * kw
        if latent:
            kn = _rope(kn, cos_ref[...], sin_ref[...], lane)
        for s in range(tm // LANES):
            katt_ref[0, s] = kn[s * LANES:(s + 1) * LANES, :].T.astype(BF16)

    def q_finish(raw):
        keep["q_raw"] = raw
        keep["q_split"] = [split_sq(raw[:, s * LANES:(s + 1) * LANES]) for s in range(ATT_WIDTH // LANES)]

    def qnorm_finish(ss_list):
        for s, ss in enumerate(ss_list):
            qraw = keep["q_raw"][:, s * LANES:(s + 1) * LANES]
            qn = qraw * lax.rsqrt(ss * (1.0 / HEAD_DIM) + EPS) * qw
            qn = _rope(qn, cos_ref[...], sin_ref[...], lane) * (HEAD_DIM ** -0.5 * LOG2E)
            swapped = pltpu.roll(qn, HEAD_DIM, 1)
            if s < ATT_WIDTH // LANES // 2:
                even, odd = jnp.where(left, qn, 0.0), jnp.where(left, swapped, 0.0)
            else:
                even, odd = jnp.where(left, 0.0, swapped), jnp.where(left, 0.0, qn)
            q_ref[0, :, (2 * s) * LANES:(2 * s + 1) * LANES] = even.astype(BF16)
            q_ref[0, :, (2 * s + 1) * LANES:(2 * s + 2) * LANES] = odd.astype(BF16)

    def gate_stage(direction, part):
        col = (C_FF, C_FB)[direction] + part * gw
        return (lambda: mm(col, col + gw),
                functools.partial(gate_finish, direction=direction, part=part))

    def mgate_stage(j):
        c0 = C_GATES + j * sw
        return (lambda: mm(c0, c0 + sw), functools.partial(mgate_finish, j=j))

    v_stage = (lambda: mm(C_INP, C_INP + HG_WIDTH), v_finish)
    kv_stage = (lambda: mm(C_K, C_K + 2 * KV_WIDTH), kv_finish)
    knorm_stage = (lambda: _dot(keep["k_split"], ones_blk2), knorm_finish)
    if latent:
        stages = [(lambda: mm(C_QHG, C_QHG + HG_WIDTH), qhg_finish)]
        for j in range(4):
            stages += [gate_stage(j // 2, j % 2), mgate_stage(j)]
        stages += [(lambda: mm(C_Q, C_Q + ATT_WIDTH), q_finish),
                   v_stage,
                   kv_stage,
                   (lambda: [_dot(t, ones_blk2) for t in keep["q_split"]], qnorm_finish),
                   (lambda: mm(C_GHG, C_GHG + HG_WIDTH), g_finish),
                   knorm_stage]
    else:
        stages = [gate_stage(0, 0), gate_stage(0, 1), kv_stage, gate_stage(1, 0), gate_stage(1, 1),
                  knorm_stage, v_stage]
    _emit_pipelined(stages, INPROJ_DEPTH)


def _inproj_call(x, mod3, mod_row0, nw, w_in, lbl, qw, kw, cos_t, sin_t, latent):
    b_, s_, _ = x.shape
    tm = min(INPROJ_TILE, s_)
    grid = (b_, s_ // tm)
    tok = lambda w: pl.BlockSpec((1, tm, w), lambda b, i: (b, i, 0))
    mod_map = (lambda b, i: (mod_row0 + b, 0, 0)) if latent else (lambda b, i: (mod_row0, 0, 0))
    ncols = IN_COLS if latent else CTX_COLS
    in_specs = [tok(D_MODEL),
                pl.BlockSpec((1, 1, mod3.shape[2]), mod_map),
                _const_spec((1, D_MODEL)),
                pl.BlockSpec((D_MODEL, ncols), lambda b, i: (0, 0), pipeline_mode=pl.Buffered(1)),
                _const_spec(lbl.shape),
                _const_spec((1, HEAD_DIM)),
                _const_spec((1, HEAD_DIM)),
                pl.BlockSpec((tm, LANES), lambda b, i: (i, 0)),
                pl.BlockSpec((tm, LANES), lambda b, i: (i, 0))]
    bf = lambda w: jax.ShapeDtypeStruct((b_, s_, w), BF16)
    dec_shape = jax.ShapeDtypeStruct((b_, s_ // CHUNK, 2 * HG_WIDTH), F32)
    dec_spec = pl.BlockSpec((1, tm // CHUNK, 2 * HG_WIDTH), lambda b, i: (b, i, 0))
    vt_shape = jax.ShapeDtypeStruct((b_, s_ // LANES, HG_WIDTH, LANES), BF16)
    vt_spec = pl.BlockSpec((1, tm // LANES, HG_WIDTH, LANES), lambda b, i: (b, i, 0, 0))
    katt_shape = jax.ShapeDtypeStruct((b_, s_ // LANES, KV_WIDTH, LANES), BF16)
    katt_spec = pl.BlockSpec((1, tm // LANES, KV_WIDTH, LANES), lambda b, i: (b, i, 0, 0))
    if latent:
        out_shape = [bf(6 * HG_WIDTH), dec_shape, vt_shape, bf(HG_WIDTH), bf(HG_WIDTH),
                                         bf(ATT_HEADS * LANES), katt_shape, bf(KV_WIDTH), bf(2 * D_MODEL)]
        out_specs = [tok(6 * HG_WIDTH), dec_spec, vt_spec, tok(HG_WIDTH), tok(HG_WIDTH),
                                          tok(ATT_HEADS * LANES), katt_spec, tok(KV_WIDTH), tok(2 * D_MODEL)]
    else:
        out_shape = [bf(HG_WIDTH)] * 2 + [dec_shape, vt_shape, katt_shape, bf(KV_WIDTH)]
        out_specs = [tok(HG_WIDTH)] * 2 + [dec_spec, vt_spec, katt_spec, tok(KV_WIDTH)]
    return pl.pallas_call(
        functools.partial(_inproj_kernel, latent=latent),
        grid=grid, in_specs=in_specs, out_specs=out_specs, out_shape=out_shape,
        compiler_params=pltpu.CompilerParams(dimension_semantics=("parallel", "parallel"),
                                             vmem_limit_bytes=VMEM_LIMIT),
        name="inproj_latent" if latent else "inproj_ctx",
    )(x, mod3, nw, w_in, lbl, qw, kw, cos_t, sin_t)


def _cast_side_job(w_refs, wb_refs):
    for w_ref, wb_ref in zip(w_refs, wb_refs):
        wb_ref[...] = w_ref[...].astype(BF16)


def _side_job_specs(weights, n_steps, step_of):
    specs = [pl.BlockSpec((wt.shape[0] // n_steps, wt.shape[1]), lambda *ids: (step_of(*ids), 0))
             for wt in weights]
    shapes = [jax.ShapeDtypeStruct(wt.shape, BF16) for wt in weights]
    return specs, shapes


def _hgrn_kernel(qdf_ref, kdf_ref, kd2f_ref, qdb_ref, kdb_ref, kd2b_ref, v_ref, vt_ref,
                 decf_ref, decb_ref, ckd2f_ref, ckd2b_ref, cvt_ref, cdecf_ref, cdecb_ref,
                 g_ref, nw_ref, *rest):
    n_w = (len(rest) - 2) // 2
    w_refs, y_ref, wb_refs, o_acc = rest[:n_w], rest[n_w], rest[n_w + 1:2 * n_w + 1], rest[-1]
    _cast_side_job(w_refs, wb_refs)

    seq = v_ref.shape[1]
    ctx_len = ckd2f_ref.shape[1]
    heads = v_ref.shape[2] // HG_DIM
    tile = HG_TILE
    cpt = tile // CHUNK
    cps = LANES // CHUNK
    spt = tile // LANES
    n_tiles = seq // tile
    half = n_tiles // 2

    row_chunk = lax.broadcasted_iota(jnp.int32, (LANES, HG_DIM), 0) // CHUNK
    ti = lax.broadcasted_iota(jnp.int32, (LANES, LANES), 0)
    tj = lax.broadcasted_iota(jnp.int32, (LANES, LANES), 1)
    same = (ti // CHUNK) == (tj // CHUNK)
    mask_f = same & (tj <= ti)
    mask_b = same & (tj >= ti)

    def chunk_update(vt_slab, k_slab, n_in_slab):
        k_m = jnp.where(row_chunk == n_in_slab, k_slab, jnp.zeros_like(k_slab))
        return _dot(vt_slab, k_m)

    def ctx_state(ckd2_ref, cdec_ref, cols, reverse):
        st = jnp.zeros((HG_DIM, HG_DIM), F32)
        order = range(ctx_len // CHUNK)
        for n in (reversed(order) if reverse else order):
            slab = n // cps
            st = cdec_ref[0, n:n + 1, cols] * st + chunk_update(
                cvt_ref[0, slab, cols, :], ckd2_ref[0, slab * LANES:(slab + 1) * LANES, cols], n % cps)
        return st

    def emit(o_t, cols, rows, finalize):
        if finalize:
            o_t = o_t + o_acc[rows, cols]
            ms = jnp.mean(o_t * o_t, axis=-1, keepdims=True)
            y = o_t * lax.rsqrt(ms + EPS) * nw_ref[...] * g_ref[0, rows, cols].astype(F32)
            y_ref[0, rows, cols] = y.astype(BF16)
        else:
            o_acc[rows, cols] = o_t

    def body(it, states, finalize):
        chains = []
        for h in range(heads):
            cols = slice(h * HG_DIM, (h + 1) * HG_DIM)
            chains.append((qdf_ref, kdf_ref, decf_ref, cols, it, False, kd2f_ref))
            chains.append((qdb_ref, kdb_ref, decb_ref, cols, n_tiles - 1 - it, True, kd2b_ref))
        n_ch = len(chains)
        rows = [pl.ds(pl.multiple_of(c[4] * tile, tile), tile) for c in chains]
        orders = [list(reversed(range(cpt))) if c[5] else list(range(cpt)) for c in chains]
        qd, sc, ups = [], [], []
        for (qd_ref, kd_ref, _, cols, t, reverse, kd2_ref), r in zip(chains, rows):
            qd.append(qd_ref[0, r, cols])
            kd = kd_ref[0, r, cols]
            sc.append([_dot_nt(qd[-1][s * LANES:(s + 1) * LANES, :], kd[s * LANES:(s + 1) * LANES, :])
                       for s in range(spt)])
            kd2 = kd2_ref[0, r, cols]
            ups.append([chunk_update(vt_ref[0, t * spt + n // cps, cols, :],
                                     kd2[(n // cps) * LANES:(n // cps + 1) * LANES, :], n % cps)
                        for n in range(cpt)])
        o = []
        for c, r, s in zip(chains, rows, sc):
            v = v_ref[0, r, c[3]]
            o.append(jnp.concatenate(
                [_dot(jnp.where(mask_b if c[5] else mask_f, s[j], 0.0).astype(BF16),
                      v[j * LANES:(j + 1) * LANES, :]) for j in range(spt)], axis=0))
        starts, new_states = [], []
        for i, c in enumerate(chains):
            dec = c[2][0, pl.ds(pl.multiple_of(c[4] * cpt, cpt), cpt), c[3]]
            st = states[i]
            start = [None] * cpt
            for n in orders[i]:
                start[n] = st.T.astype(BF16)
                st = dec[n:n + 1, :] * st + ups[i][n]
            starts.append(start)
            new_states.append(st)
        for i, c in enumerate(chains):
            outs = [o[i][n * CHUNK:(n + 1) * CHUNK, :]
                    + _dot(qd[i][n * CHUNK:(n + 1) * CHUNK, :], starts[i][n]) for n in range(cpt)]
            emit(jnp.concatenate(outs, axis=0), c[3], rows[i], finalize)
        return tuple(new_states)

    states = []
    for h in range(heads):
        cols = slice(h * HG_DIM, (h + 1) * HG_DIM)
        states += [ctx_state(ckd2f_ref, cdecf_ref, cols, False), ctx_state(ckd2b_ref, cdecb_ref, cols, True)]
    states = lax.fori_loop(0, half, functools.partial(body, finalize=False), tuple(states))
    lax.fori_loop(half, n_tiles, functools.partial(body, finalize=True), states)


def _hgrn_call(hg, v, vt, dec, ckd2f, ckd2b, cvt, cdec, g, nw, weights):
    b_, s_, _ = v.shape
    l_ = ckd2f.shape[1]
    w = HG_HEADS_PER_STEP * HG_DIM
    groups = HG_HEADS // HG_HEADS_PER_STEP
    seq = lambda n: pl.BlockSpec((1, n, w), lambda b, h: (b, 0, h))
    vt_spec = lambda n: pl.BlockSpec((1, n // LANES, w, LANES), lambda b, h: (b, 0, h, 0))
    dec_f = lambda n: pl.BlockSpec((1, n // CHUNK, w), lambda b, h: (b, 0, h))
    dec_b = lambda n: pl.BlockSpec((1, n // CHUNK, w), lambda b, h: (b, 0, groups + h))
    w_specs, w_shapes = _side_job_specs(weights, b_ * groups, lambda b, h: b * groups + h)
    outs = pl.pallas_call(
        _hgrn_kernel,
        grid=(b_, groups),
        in_specs=[pl.BlockSpec((1, s_, w), lambda b, h, j=j: (b, 0, j * groups + h)) for j in range(6)]
        + [seq(s_), vt_spec(s_), dec_f(s_), dec_b(s_),
                                  seq(l_), seq(l_), vt_spec(l_), dec_f(l_), dec_b(l_),
                                  seq(s_), _const_spec((1, HG_DIM))] + w_specs,
        out_specs=[seq(s_)] + w_specs,
        out_shape=[jax.ShapeDtypeStruct((b_, s_, HG_WIDTH), BF16)] + w_shapes,
        scratch_shapes=[pltpu.VMEM((s_, w), F32)],
        compiler_params=pltpu.CompilerParams(dimension_semantics=("parallel", "parallel"),
                                             vmem_limit_bytes=VMEM_LIMIT),
        name="hgrn_scan",
    )(hg, hg, hg, hg, hg, hg, v, vt, dec, dec, ckd2f, ckd2b, cvt, cdec, cdec, g, nw, *weights)
    return outs[0], outs[1:]


def _attn_kernel(sink_ref, q_ref, kt_ref, kc_ref, v_ref, vc_ref, *rest):
    n_w = (len(rest) - 1) // 2
    y_ref = rest[n_w]
    _cast_side_job(rest[:n_w], rest[n_w + 1:])
    nb = kt_ref.shape[1]
    blk = ATT_BLOCK
    hpg = ATT_GROUP_HEADS
    n_groups = ATT_HEADS // hpg
    qry_r = lax.broadcasted_iota(jnp.int32, (blk, blk), 0)
    key_c = lax.broadcasted_iota(jnp.int32, (blk, blk), 1)
    left = lax.broadcasted_iota(jnp.int32, (blk, LANES), 1) < HEAD_DIM
    ctx_keys = [kc_ref[0, c] for c in range(kc_ref.shape[1])]
    keep = {}

    def band(u):
        qb = pl.program_id(1) * ATT_STEP_BLOCKS + u
        ids = (jnp.maximum(qb - 1, 0), qb, jnp.minimum(qb + 1, nb - 1))
        keys_t = jnp.concatenate([kt_ref[0, j] for j in ids] + ctx_keys, axis=1)
        values = jnp.concatenate([v_ref[0, pl.ds(pl.multiple_of(j * blk, blk), blk), :] for j in ids]
                                 + [vc_ref[0]], axis=0)
        values = jnp.concatenate([values, jnp.ones_like(values)], axis=1)
        return qb, keys_t, values

    def score_issue(u, g):
        if g == 0:
            keep["band", u] = band(u)
        q = jnp.concatenate([q_ref[0, u * blk:(u + 1) * blk, h * LANES:(h + 1) * LANES]
                             for h in range(g * hpg, (g + 1) * hpg)], axis=0)
        return _dot(q, keep["band", u][1])

    def score_finish(raw, u, g):
        qb = keep["band", u][0]
        mask_prev = (key_c >= qry_r) & (qb > 0)
        mask_next = (key_c <= qry_r) & (qb < nb - 1)
        probs, sink_p = [], []
        for n in range(hpg):
            s = raw[n * blk:(n + 1) * blk, :]
            sink = sink_ref[g * hpg + n] * LOG2E
            s = jnp.concatenate([jnp.where(mask_prev, s[:, 0:blk], NEG), s[:, blk:2 * blk],
                                 jnp.where(mask_next, s[:, 2 * blk:3 * blk], NEG), s[:, 3 * blk:]], axis=1)
            mx = jnp.maximum(jnp.max(s, axis=1, keepdims=True), sink)
            probs.append(jnp.exp2(s - mx).astype(BF16))
            sink_p.append(jnp.exp2(sink - mx))
        keep["p", u, g] = jnp.concatenate(probs, axis=0)
        keep["sink_p", u, g] = sink_p

    def value_issue(u, g):
        return _dot(keep["p", u, g], keep["band", u][2])

    def value_finish(raw, u, g):
        for n in range(0, hpg, 2):
            h = g * hpg + n
            even, odd = [raw[m * blk:(m + 1) * blk, 0:LANES]
                         / (raw[m * blk:(m + 1) * blk, LANES:2 * LANES] + keep["sink_p", u, g][m])
                         for m in (n, n + 1)]
            if h < ATT_HEADS // ATT_KV_HEADS:
                y = jnp.where(left, even, pltpu.roll(odd, HEAD_DIM, 1))
            else:
                y = jnp.where(left, pltpu.roll(even, HEAD_DIM, 1), odd)
            y_ref[0, u * blk:(u + 1) * blk, (h // 2) * LANES:(h // 2 + 1) * LANES] = y.astype(BF16)

    part = functools.partial
    work = [(u, g) for u in range(ATT_STEP_BLOCKS) for g in range(n_groups)]
    score = lambda u, g: (part(score_issue, u, g), part(score_finish, u=u, g=g))
    stages = [score(*w) for w in work[:ATT_DEPTH + 1]]
    for n, (u, g) in enumerate(work):
        stages.append((part(value_issue, u, g), part(value_finish, u=u, g=g)))
        if n + ATT_DEPTH + 1 < len(work):
            stages.append(score(*work[n + ATT_DEPTH + 1]))
    _emit_pipelined(stages, ATT_DEPTH)


def _attn_call(sinks, q, katt, vatt, ckatt, cvatt, weights):
    b_, s_, qw = q.shape
    l_ = cvatt.shape[1]
    rows = ATT_STEP_BLOCKS * ATT_BLOCK
    batch4 = lambda n: pl.BlockSpec((1, n, KV_WIDTH, LANES), lambda b, i: (b, 0, 0, 0))
    batch3 = lambda n: pl.BlockSpec((1, n, KV_WIDTH), lambda b, i: (b, 0, 0))
    steps = s_ // rows
    w_specs, w_shapes = _side_job_specs(weights, b_ * steps, lambda b, i: b * steps + i)
    outs = pl.pallas_call(
        _attn_kernel,
        grid=(b_, steps),
        in_specs=[pl.BlockSpec(memory_space=pltpu.SMEM),
                  pl.BlockSpec((1, rows, qw), lambda b, i: (b, i, 0)),
                  batch4(s_ // LANES), batch4(l_ // LANES), batch3(s_), batch3(l_)] + w_specs,
        out_specs=[pl.BlockSpec((1, rows, ATT_WIDTH), lambda b, i: (b, i, 0))] + w_specs,
        out_shape=[jax.ShapeDtypeStruct((b_, s_, ATT_WIDTH), BF16)] + w_shapes,
        compiler_params=pltpu.CompilerParams(dimension_semantics=("parallel", "arbitrary"),
                                             vmem_limit_bytes=VMEM_LIMIT),
        name="window_attn",
    )(sinks, q, katt, ckatt, vatt, cvatt, *weights)
    return outs[0], outs[1:]


def _merge_ffn_kernel(x_ref, yh_ref, ya_ref, gates_ref, mod_ref, nw_ref, wbh_ref, wba_ref, wo_ref,
                      wg_ref, wu_ref, wd_ref, o_ref):
    tm = x_ref.shape[1]
    n_sub = FFN_SUBTILES
    sub = tm // n_sub
    bounds = FFN_CHUNK_BOUNDS
    n_chunks = len(bounds) - 1
    g1 = mod_ref[0, :, 2 * D_MODEL:3 * D_MODEL]
    sh2 = mod_ref[0, :, 3 * D_MODEL:4 * D_MODEL]
    sc2 = mod_ref[0, :, 4 * D_MODEL:5 * D_MODEL]
    g2 = mod_ref[0, :, 5 * D_MODEL:6 * D_MODEL]
    keep = {}

    def branch_issue(r):
        rows = slice(r * sub, (r + 1) * sub)
        return _dot(yh_ref[0, rows, :], wbh_ref[...]), _dot(ya_ref[0, rows, :], wba_ref[...])

    def branch_finish(raw, r):
        rows = slice(r * sub, (r + 1) * sub)
        a, b = raw
        mixed = (gates_ref[0, rows, 0:D_MODEL].astype(F32) * a
                 + gates_ref[0, rows, D_MODEL:2 * D_MODEL].astype(F32) * b)
        keep["mixed", r] = (0.5 * mixed).astype(BF16)

    def out_issue(r):
        return _dot(keep["mixed", r], wo_ref[...])

    def out_finish(raw, r):
        x1 = x_ref[0, r * sub:(r + 1) * sub, :] + g1 * raw
        ms = jnp.mean(x1 * x1, axis=-1, keepdims=True)
        keep["x1", r] = x1
        keep["h2", r] = ((x1 * lax.rsqrt(ms + EPS) * nw_ref[...]) * (1.0 + sc2) + sh2).astype(BF16)

    def hidden_issue(r, c):
        h2 = keep["h2", r]
        cols = slice(bounds[c], bounds[c + 1])
        return _dot(h2, wg_ref[:, cols]), _dot(h2, wu_ref[:, cols])

    def hidden_finish(raw, r, c):
        gate, up = raw
        keep["act", r, c] = (gate * _sigmoid(gate) * up).astype(BF16)

    def down_issue(r, c):
        return _dot(keep["act", r, c], wd_ref[bounds[c]:bounds[c + 1], :])

    def down_finish(raw, r, c):
        dn = raw if c == 0 else keep["dn", r] + raw
        if c + 1 < n_chunks:
            keep["dn", r] = dn
        else:
            o_ref[0, r * sub:(r + 1) * sub, :] = keep["x1", r] + g2 * dn

    part = functools.partial
    stages = []
    for issue, finish in ((branch_issue, branch_finish), (out_issue, out_finish)):
        stages += [(part(issue, r), part(finish, r=r)) for r in range(n_sub)]
    for c in range(n_chunks):
        stages += [(part(hidden_issue, r, c), part(hidden_finish, r=r, c=c)) for r in range(n_sub)]
        stages += [(part(down_issue, r, c), part(down_finish, r=r, c=c)) for r in range(n_sub)]
    _emit_pipelined(stages, n_sub - 1)


def _merge_ffn_call(x, yh, ya, gates, mod3, nw, wbh, wba, wo, wg, wu, wd):
    b_, s_, _ = x.shape
    tm = TOK_TILE
    tok = lambda w: pl.BlockSpec((1, tm, w), lambda b, i: (b, i, 0))
    return pl.pallas_call(
        _merge_ffn_kernel,
        grid=(b_, s_ // tm),
        in_specs=[tok(D_MODEL), tok(HG_WIDTH), tok(ATT_WIDTH), tok(2 * D_MODEL),
                  pl.BlockSpec((1, 1, mod3.shape[2]), lambda b, i: (b, 0, 0)),
                  _const_spec((1, D_MODEL)),
                  _const_spec(wbh.shape), _const_spec(wba.shape), _const_spec(wo.shape),
                  _const_spec(wg.shape), _const_spec(wu.shape), _const_spec(wd.shape)],
        out_specs=tok(D_MODEL),
        out_shape=jax.ShapeDtypeStruct(x.shape, F32),
        compiler_params=pltpu.CompilerParams(dimension_semantics=("parallel", "parallel"),
                                             vmem_limit_bytes=VMEM_LIMIT),
        name="merge_ffn",
    )(x, yh, ya, gates, mod3, nw, wbh, wba, wo, wg, wu, wd)


def _rope_tables(n_tok):
    t = np.arange(n_tok)
    rows = (t // GRID_W).astype(np.float64)
    cols = (t % GRID_W).astype(np.float64)
    half = HEAD_DIM // 2
    inv_freq = ROPE_THETA ** (-np.arange(0, half, 2, dtype=np.float64) / half)
    d = np.arange(LANES) % HEAD_DIM
    pos = np.where((d < half)[None, :], rows[:, None], cols[:, None])
    ang = pos * inv_freq[(d % half) % (half // 2)][None, :]
    sign = np.where((d % half) < half // 2, -1.0, 1.0)[None, :]
    return jnp.asarray(np.cos(ang), F32), jnp.asarray(np.sin(ang) * sign, F32)


def kernel(x, c, ctx, c_ctx, w_ada, b_ada, norm_mix_w, norm_ffn_w, w_in, hgrn_lb_logits, hgrn_norm_w,
           q_norm_w, k_norm_w, attn_sinks, w_branch_hgrn, w_branch_attn, w_out, w_ffn_gate, w_ffn_up,
           w_ffn_down):
    b_, s_, _ = x.shape
    layer = 0
    col = np.ones((1, IN_COLS), np.float32)
    for lo, width in ((C_FF, 2 * HG_WIDTH), (C_QHG, 2 * HG_WIDTH), (C_GATES, 2 * D_MODEL)):
        col[:, lo:lo + width] = 0.5
    mod, w_in_b = _ada_call(c, c_ctx[None, :], w_ada[layer], b_ada[layer][None, :], w_in[layer],
                            jnp.asarray(col))

    lbl = hgrn_lb_logits[:, 0:2, :]
    qw = q_norm_w[layer][None, :]
    kw = k_norm_w[layer][None, :]
    nw_mix = norm_mix_w[layer][None, :]
    cos_t, sin_t = _rope_tables(s_)

    (hg, dec, vt, v, g, q, katt, vatt, gates) = _inproj_call(
        x, mod, 0, nw_mix, w_in_b, lbl, qw, kw, cos_t, sin_t, latent=True)
    l_ = ctx.shape[1]
    pack = max(1, min(INPROJ_TILE // l_, b_))
    ctx_out = _inproj_call(ctx.reshape(b_ // pack, pack * l_, D_MODEL), mod, b_, nw_mix, w_in_b, lbl, qw, kw,
                           cos_t, sin_t, latent=False)
    ckd2f, ckd2b, cdec, cvt, ckatt, cvatt = [t.reshape((b_, t.shape[1] // pack) + t.shape[2:])
                                             for t in ctx_out]

    y_hg, (w_bh, w_ba, w_o) = _hgrn_call(
        hg, v, vt, dec, ckd2f, ckd2b, cvt, cdec, g, hgrn_norm_w[layer][None, :],
        (w_branch_hgrn[layer], w_branch_attn[layer], w_out[layer]))
    y_at, (w_g, w_u, w_d) = _attn_call(attn_sinks[layer], q, katt, vatt, ckatt, cvatt,
                                       (w_ffn_gate[layer], w_ffn_up[layer], w_ffn_down[layer]))

    return _merge_ffn_call(x, y_hg, y_at, gates, mod, norm_ffn_w[layer][None, :],
                           w_bh, w_ba, w_o, w_g, w_u, w_d)
```

```python
import functools

import jax
import jax.numpy as jnp
import numpy as np
from jax import lax
from jax.experimental import pallas as pl
from jax.experimental.pallas import tpu as pltpu

F32 = jnp.float32
BF16 = jnp.bfloat16

D_MODEL = 1024
GRID_W = 64
EPS = 1e-6
HG_HEADS = 4
HG_DIM = 128
HG_WIDTH = HG_HEADS * HG_DIM
CHUNK = 32
ATT_HEADS = 8
ATT_KV_HEADS = 2
HEAD_DIM = 64
ATT_WIDTH = ATT_HEADS * HEAD_DIM
KV_WIDTH = ATT_KV_HEADS * HEAD_DIM
WINDOW = 128
ROPE_THETA = 10000.0
D_FF = 2816
CTX_COLS = 3 * HG_WIDTH + 2 * KV_WIDTH
IN_COLS = CTX_COLS + 2 * HG_WIDTH + ATT_WIDTH + 2 * D_MODEL

C_FF, C_FB, C_INP, C_K, C_V = 0, HG_WIDTH, 2 * HG_WIDTH, 3 * HG_WIDTH, 3 * HG_WIDTH + KV_WIDTH
C_QHG = CTX_COLS
C_GHG = C_QHG + HG_WIDTH
C_Q = C_GHG + HG_WIDTH
C_GATES = C_Q + ATT_WIDTH

LANES = 128
TOK_TILE = 512
FFN_SUBTILES = 2
MXU_DEPTH = 256
FFN_CHUNK_BOUNDS = (0, 6 * MXU_DEPTH, D_FF)
INPROJ_TILE = 256
CTX_TILE = 512
INPROJ_DEPTH = 2
HG_TILE = 256
HG_HEADS_PER_STEP = 4
ATT_BLOCK = 128
ATT_GROUP_HEADS = 2
ATT_DEPTH = 1
ATT_STEP_BLOCKS = 8
ADA_ROWS = 16
ADA_STEPS = 6
VMEM_LIMIT = 56 * 1024 * 1024
NEG = -1e30
LOG2E = 1.4426950408889634

assert ATT_BLOCK == WINDOW == LANES and HG_DIM == LANES and 2 * HEAD_DIM == LANES
assert ATT_GROUP_HEADS % 2 == 0 and (ATT_HEADS // ATT_KV_HEADS) % ATT_GROUP_HEADS == 0


def _dot(a, b):
    return jnp.dot(a, b, preferred_element_type=F32)


def _dot_nt(a, b):
    return lax.dot_general(a, b, (((1,), (1,)), ((), ())), preferred_element_type=F32)


def _sigmoid(x):
    return 0.5 * jnp.tanh(0.5 * x) + 0.5


def _emit_pipelined(stages, depth):
    raws = [issue() for issue, _ in stages[:depth]]
    for i, (_, finish) in enumerate(stages):
        if i + depth < len(stages):
            raws.append(stages[i + depth][0]())
        finish(raws[i])
        raws[i] = None


def _const_spec(shape):
    n = len(shape)
    return pl.BlockSpec(shape, lambda *_: (0,) * n, pipeline_mode=pl.Buffered(1))


def _ada_kernel(c_ref, cctx_ref, w_ref, b_ref, win_ref, col_ref, o_ref, winb_ref):
    rows = o_ref.shape[0]
    n_cond = c_ref.shape[0] + 1
    c = jnp.concatenate([c_ref[...], cctx_ref[...], jnp.zeros((rows - n_cond, D_MODEL), F32)], axis=0)
    s = (c * _sigmoid(c)).astype(BF16)
    mod = _dot(s, w_ref[...].astype(BF16)) + b_ref[...]
    for r in range(mod.shape[0]):
        o_ref[r] = mod[r:r + 1, :]
    winb_ref[...] = (win_ref[...] * col_ref[...]).astype(BF16)


def _ada_call(c, c_ctx, w_ada, b_ada, w_in, col):
    rows = ADA_ROWS
    n_out = w_ada.shape[1]
    steps = ADA_STEPS
    bn = n_out // steps
    bw = w_in.shape[1] // steps
    return pl.pallas_call(
        _ada_kernel,
        grid=(steps,),
        in_specs=[pl.BlockSpec(c.shape, lambda j: (0, 0)),
                  pl.BlockSpec((1, D_MODEL), lambda j: (0, 0)),
                  pl.BlockSpec((D_MODEL, bn), lambda j: (0, j)),
                  pl.BlockSpec((1, bn), lambda j: (0, j)),
                  pl.BlockSpec((D_MODEL, bw), lambda j: (0, j)),
                  pl.BlockSpec((1, bw), lambda j: (0, j))],
        out_specs=[pl.BlockSpec((rows, 1, bn), lambda j: (0, 0, j)),
                   pl.BlockSpec((D_MODEL, bw), lambda j: (0, j))],
        out_shape=[jax.ShapeDtypeStruct((rows, 1, n_out), F32),
                   jax.ShapeDtypeStruct(w_in.shape, BF16)],
        compiler_params=pltpu.CompilerParams(dimension_semantics=("arbitrary",),
                                             vmem_limit_bytes=VMEM_LIMIT),
        name="ada_mod",
    )(c, c_ctx, w_ada, b_ada, w_in, col)


def _chunk_cumsum(x, reverse):
    n, c = x.shape
    sub = 8
    r = lax.broadcasted_iota(jnp.int32, x.shape, 0) & (CHUNK - 1)
    s = 1
    while s < sub:
        if reverse:
            x = x + jnp.where(r < CHUNK - s, pltpu.roll(x, n - s, 0), 0.0)
        else:
            x = x + jnp.where(r >= s, pltpu.roll(x, s, 0), 0.0)
        s *= 2
    per = CHUNK // sub
    x4 = x.reshape(n // CHUNK, per, sub, c)
    s = 1
    while s < per:
        if reverse:
            x4 = jnp.concatenate([x4[:, :per - s] + x4[:, s:], x4[:, per - s:]], axis=1)
        else:
            x4 = jnp.concatenate([x4[:, :s], x4[:, s:] + x4[:, :per - s]], axis=1)
        s *= 2
    return x4.reshape(n, c)


def _rope(t, cos, sin_signed, lane):
    partner = jnp.where((lane & 31) < 16, pltpu.roll(t, LANES - 16, 1), pltpu.roll(t, 16, 1))
    return t * cos + partner * sin_signed


def _inproj_kernel(x_ref, mod_ref, nw_ref, w_ref, lbl_ref, qw_ref, kw_ref, cos_ref, sin_ref,
                   *out_refs, latent):
    if latent:
        hg_ref, dec_ref, vt_ref, v_ref, g_ref, q_ref, katt_ref, vatt_ref, gates_ref = out_refs
    else:
        kd2f_ref, kd2b_ref, dec_ref, vt_ref, katt_ref, vatt_ref = out_refs

    tm = x_ref.shape[1]
    nchunk = tm // CHUNK
    x = x_ref[0]
    sh = mod_ref[0, :, 0:D_MODEL]
    sc = mod_ref[0, :, D_MODEL:2 * D_MODEL]
    ms = jnp.mean(x * x, axis=-1, keepdims=True)
    h = x * lax.rsqrt(ms + EPS) * (nw_ref[...] * (1.0 + sc)) + sh
    hb = h.astype(BF16)

    def mm(a, b):
        return _dot(hb, w_ref[:, a:b])

    l0f, l1f = lbl_ref[0, 0:1, :], lbl_ref[0, 1:2, :]
    l0b, l1b = lbl_ref[1, 0:1, :], lbl_ref[1, 1:2, :]
    lb_f = 1.0 / (1.0 + jnp.exp(l1f - l0f))
    lb_b = 1.0 / (1.0 + jnp.exp(l1b - l0b))

    qw = jnp.concatenate([qw_ref[...]] * (LANES // HEAD_DIM), axis=1)
    kw = jnp.concatenate([kw_ref[...]] * (LANES // HEAD_DIM), axis=1)
    gw = HG_WIDTH // 2
    sw = D_MODEL // 2
    lane = lax.broadcasted_iota(jnp.int32, (tm, LANES), 1)
    left = lane < HEAD_DIM
    bi = lax.broadcasted_iota(jnp.int32, (2 * LANES, LANES), 0) // HEAD_DIM
    bj = lax.broadcasted_iota(jnp.int32, (2 * LANES, LANES), 1) // HEAD_DIM
    ones_blk2 = jnp.where((bi % 2) == bj, 1.0, 0.0).astype(BF16)
    keep = {}

    def qhg_finish(raw):
        keep["q_hg"] = raw * (jnp.tanh(raw) + 1.0) * (HG_DIM ** -0.5)

    def gate_finish(raw, direction, part):
        reverse = direction == 1
        lo = part * gw
        lb = (lb_f, lb_b)[direction][:, lo:lo + gw]
        c_half = 0.5 - 0.5 * lb
        ct = c_half * jnp.tanh(raw)
        f = (0.5 + 0.5 * lb) + ct
        k = c_half - ct
        cum = _chunk_cumsum(jnp.log(f), reverse)
        cum3 = cum.reshape(nchunk, CHUNK, gw)
        tot3 = cum3[:, 0:1, :] if reverse else cum3[:, CHUNK - 1:CHUNK, :]
        dec3 = jnp.exp(tot3)
        dec_ref[0, :, direction * HG_WIDTH + lo:direction * HG_WIDTH + lo + gw] = dec3.reshape(nchunk, gw)
        kd2 = (k.reshape(nchunk, CHUNK, gw) * jnp.exp(tot3 - cum3)).reshape(tm, gw).astype(BF16)
        if latent:
            base = direction * 3 * HG_WIDTH + lo
            hg_ref[0, :, base:base + gw] = (keep["q_hg"][:, lo:lo + gw] * jnp.exp(cum)).astype(BF16)
            hg_ref[0, :, base + HG_WIDTH:base + HG_WIDTH + gw] = (k * jnp.exp(-cum)).astype(BF16)
            hg_ref[0, :, base + 2 * HG_WIDTH:base + 2 * HG_WIDTH + gw] = kd2
        else:
            (kd2f_ref, kd2b_ref)[direction][0, :, lo:lo + gw] = kd2

    def mgate_finish(raw, j):
        gates_ref[0, :, j * sw:(j + 1) * sw] = (jnp.tanh(raw) + 1.0).astype(BF16)

    def v_finish(raw):
        for s in range(tm // LANES):
            vt_ref[0, s] = raw[s * LANES:(s + 1) * LANES, :].T.astype(BF16)
        if latent:
            v_ref[0] = raw.astype(BF16)

    def g_finish(raw):
        g_ref[0] = (raw * (jnp.tanh(raw) + 1.0)).astype(BF16)

    def split_sq(t):
        sq = t * t
        hi = sq.astype(BF16)
        lo = (sq - hi.astype(F32)).astype(BF16)
        return jnp.concatenate([hi, lo], axis=1)

    def kv_finish(raw):
        keep["k_raw"] = raw[:, 0:LANES]
        keep["k_split"] = split_sq(keep["k_raw"])
        vatt_ref[0] = raw[:, LANES:2 * LANES].astype(BF16)

    def knorm_finish(ss):
        kn = keep["k_raw"] * lax.rsqrt(ss * (1.0 / HEAD_DIM) + EPS) * kw
        if latent:
            kn = _rope(kn, cos_ref[...], sin_ref[...], lane)
        for s in range(tm // LANES):
            katt_ref[0, s] = kn[s * LANES:(s + 1) * LANES, :].T.astype(BF16)

    def q_finish(raw):
        keep["q_raw"] = raw
        keep["q_split"] = [split_sq(raw[:, s * LANES:(s + 1) * LANES]) for s in range(ATT_WIDTH // LANES)]

    def qnorm_finish(ss_list):
        for s, ss in enumerate(ss_list):
            qraw = keep["q_raw"][:, s * LANES:(s + 1) * LANES]
            qn = qraw * lax.rsqrt(ss * (1.0 / HEAD_DIM) + EPS) * qw
            qn = _rope(qn, cos_ref[...], sin_ref[...], lane) * (HEAD_DIM ** -0.5 * LOG2E)
            swapped = pltpu.roll(qn, HEAD_DIM, 1)
            if s < ATT_WIDTH // LANES // 2:
                even, odd = jnp.where(left, qn, 0.0), jnp.where(left, swapped, 0.0)
            else:
                even, odd = jnp.where(left, 0.0, swapped), jnp.where(left, 0.0, qn)
            q_ref[0, :, (2 * s) * LANES:(2 * s + 1) * LANES] = even.astype(BF16)
            q_ref[0, :, (2 * s + 1) * LANES:(2 * s + 2) * LANES] = odd.astype(BF16)

    def gate_stage(direction, part):
        col = (C_FF, C_FB)[direction] + part * gw
        return (lambda: mm(col, col + gw),
                functools.partial(gate_finish, direction=direction, part=part))

    def mgate_stage(j):
        c0 = C_GATES + j * sw
        return (lambda: mm(c0, c0 + sw), functools.partial(mgate_finish, j=j))

    v_stage = (lambda: mm(C_INP, C_INP + HG_WIDTH), v_finish)
    kv_stage = (lambda: mm(C_K, C_K + 2 * KV_WIDTH), kv_finish)
    knorm_stage = (lambda: _dot(keep["k_split"], ones_blk2), knorm_finish)
    if latent:
        stages = [(lambda: mm(C_QHG, C_QHG + HG_WIDTH), qhg_finish)]
        for j in range(4):
            stages += [gate_stage(j // 2, j % 2), mgate_stage(j)]
        stages += [(lambda: mm(C_Q, C_Q + ATT_WIDTH), q_finish),
                   v_stage,
                   kv_stage,
                   (lambda: [_dot(t, ones_blk2) for t in keep["q_split"]], qnorm_finish),
                   (lambda: mm(C_GHG, C_GHG + HG_WIDTH), g_finish),
                   knorm_stage]
    else:
        stages = [gate_stage(0, 0), gate_stage(0, 1), kv_stage, gate_stage(1, 0), gate_stage(1, 1),
                  knorm_stage, v_stage]
    _emit_pipelined(stages, INPROJ_DEPTH)


def _inproj_call(x, mod3, mod_row0, nw, w_in, lbl, qw, kw, cos_t, sin_t, latent):
    b_, s_, _ = x.shape
    tm = min(INPROJ_TILE if latent else CTX_TILE, s_)
    grid = (b_, s_ // tm)
    tok = lambda w: pl.BlockSpec((1, tm, w), lambda b, i: (b, i, 0))
    mod_map = (lambda b, i: (mod_row0 + b, 0, 0)) if latent else (lambda b, i: (mod_row0, 0, 0))
    ncols = IN_COLS if latent else CTX_COLS
    in_specs = [tok(D_MODEL),
                pl.BlockSpec((1, 1, mod3.shape[2]), mod_map),
                _const_spec((1, D_MODEL)),
                pl.BlockSpec((D_MODEL, ncols), lambda b, i: (0, 0), pipeline_mode=pl.Buffered(1)),
                _const_spec(lbl.shape),
                _const_spec((1, HEAD_DIM)),
                _const_spec((1, HEAD_DIM)),
                pl.BlockSpec((tm, LANES), lambda b, i: (i, 0)),
                pl.BlockSpec((tm, LANES), lambda b, i: (i, 0))]
    bf = lambda w: jax.ShapeDtypeStruct((b_, s_, w), BF16)
    dec_shape = jax.ShapeDtypeStruct((b_, s_ // CHUNK, 2 * HG_WIDTH), F32)
    dec_spec = pl.BlockSpec((1, tm // CHUNK, 2 * HG_WIDTH), lambda b, i: (b, i, 0))
    vt_shape = jax.ShapeDtypeStruct((b_, s_ // LANES, HG_WIDTH, LANES), BF16)
    vt_spec = pl.BlockSpec((1, tm // LANES, HG_WIDTH, LANES), lambda b, i: (b, i, 0, 0))
    katt_shape = jax.ShapeDtypeStruct((b_, s_ // LANES, KV_WIDTH, LANES), BF16)
    katt_spec = pl.BlockSpec((1, tm // LANES, KV_WIDTH, LANES), lambda b, i: (b, i, 0, 0))
    if latent:
        out_shape = [bf(6 * HG_WIDTH), dec_shape, vt_shape, bf(HG_WIDTH), bf(HG_WIDTH),
                                         bf(ATT_HEADS * LANES), katt_shape, bf(KV_WIDTH), bf(2 * D_MODEL)]
        out_specs = [tok(6 * HG_WIDTH), dec_spec, vt_spec, tok(HG_WIDTH), tok(HG_WIDTH),
                                          tok(ATT_HEADS * LANES), katt_spec, tok(KV_WIDTH), tok(2 * D_MODEL)]
    else:
        out_shape = [bf(HG_WIDTH)] * 2 + [dec_shape, vt_shape, katt_shape, bf(KV_WIDTH)]
        out_specs = [tok(HG_WIDTH)] * 2 + [dec_spec, vt_spec, katt_spec, tok(KV_WIDTH)]
    return pl.pallas_call(
        functools.partial(_inproj_kernel, latent=latent),
        grid=grid, in_specs=in_specs, out_specs=out_specs, out_shape=out_shape,
        compiler_params=pltpu.CompilerParams(dimension_semantics=("parallel", "parallel"),
                                             vmem_limit_bytes=VMEM_LIMIT),
        name="inproj_latent" if latent else "inproj_ctx",
    )(x, mod3, nw, w_in, lbl, qw, kw, cos_t, sin_t)


def _cast_side_job(w_refs, wb_refs):
    for w_ref, wb_ref in zip(w_refs, wb_refs):
        wb_ref[...] = w_ref[...].astype(BF16)


def _side_job_specs(weights, n_steps, step_of):
    specs = [pl.BlockSpec((wt.shape[0] // n_steps, wt.shape[1]), lambda *ids: (step_of(*ids), 0))
             for wt in weights]
    shapes = [jax.ShapeDtypeStruct(wt.shape, BF16) for wt in weights]
    return specs, shapes


def _hgrn_kernel(qdf_ref, kdf_ref, kd2f_ref, qdb_ref, kdb_ref, kd2b_ref, v_ref, vt_ref,
                 decf_ref, decb_ref, ckd2f_ref, ckd2b_ref, cvt_ref, cdecf_ref, cdecb_ref,
                 g_ref, nw_ref, *rest):
    n_w = (len(rest) - 2) // 2
    w_refs, y_ref, wb_refs, o_acc = rest[:n_w], rest[n_w], rest[n_w + 1:2 * n_w + 1], rest[-1]
    _cast_side_job(w_refs, wb_refs)

    seq = v_ref.shape[1]
    ctx_len = ckd2f_ref.shape[1]
    heads = v_ref.shape[2] // HG_DIM
    tile = HG_TILE
    cpt = tile // CHUNK
    cps = LANES // CHUNK
    spt = tile // LANES
    n_tiles = seq // tile
    half = n_tiles // 2

    row_chunk = lax.broadcasted_iota(jnp.int32, (LANES, HG_DIM), 0) // CHUNK
    ti = lax.broadcasted_iota(jnp.int32, (LANES, LANES), 0)
    tj = lax.broadcasted_iota(jnp.int32, (LANES, LANES), 1)
    same = (ti // CHUNK) == (tj // CHUNK)
    mask_f = same & (tj <= ti)
    mask_b = same & (tj >= ti)

    def chunk_update(vt_slab, k_slab, n_in_slab):
        k_m = jnp.where(row_chunk == n_in_slab, k_slab, jnp.zeros_like(k_slab))
        return _dot(vt_slab, k_m)

    def ctx_state(ckd2_ref, cdec_ref, cols, reverse):
        st = jnp.zeros((HG_DIM, HG_DIM), F32)
        order = range(ctx_len // CHUNK)
        for n in (reversed(order) if reverse else order):
            slab = n // cps
            st = cdec_ref[0, n:n + 1, cols] * st + chunk_update(
                cvt_ref[0, slab, cols, :], ckd2_ref[0, slab * LANES:(slab + 1) * LANES, cols], n % cps)
        return st

    def emit(o_t, cols, rows, finalize):
        if finalize:
            o_t = o_t + o_acc[rows, cols]
            ms = jnp.mean(o_t * o_t, axis=-1, keepdims=True)
            y = o_t * lax.rsqrt(ms + EPS) * nw_ref[...] * g_ref[0, rows, cols].astype(F32)
            y_ref[0, rows, cols] = y.astype(BF16)
        else:
            o_acc[rows, cols] = o_t

    def body(it, states, finalize):
        chains = []
        for h in range(heads):
            cols = slice(h * HG_DIM, (h + 1) * HG_DIM)
            chains.append((qdf_ref, kdf_ref, decf_ref, cols, it, False, kd2f_ref))
            chains.append((qdb_ref, kdb_ref, decb_ref, cols, n_tiles - 1 - it, True, kd2b_ref))
        n_ch = len(chains)
        rows = [pl.ds(pl.multiple_of(c[4] * tile, tile), tile) for c in chains]
        orders = [list(reversed(range(cpt))) if c[5] else list(range(cpt)) for c in chains]
        qd, sc, ups = [], [], []
        for (qd_ref, kd_ref, _, cols, t, reverse, kd2_ref), r in zip(chains, rows):
            qd.append(qd_ref[0, r, cols])
            kd = kd_ref[0, r, cols]
            sc.append([_dot_nt(qd[-1][s * LANES:(s + 1) * LANES, :], kd[s * LANES:(s + 1) * LANES, :])
                       for s in range(spt)])
            kd2 = kd2_ref[0, r, cols]
            ups.append([chunk_update(vt_ref[0, t * spt + n // cps, cols, :],
                                     kd2[(n // cps) * LANES:(n // cps + 1) * LANES, :], n % cps)
                        for n in range(cpt)])
        o = []
        for c, r, s in zip(chains, rows, sc):
            v = v_ref[0, r, c[3]]
            o.append(jnp.concatenate(
                [_dot(jnp.where(mask_b if c[5] else mask_f, s[j], 0.0).astype(BF16),
                      v[j * LANES:(j + 1) * LANES, :]) for j in range(spt)], axis=0))
        starts, new_states = [], []
        for i, c in enumerate(chains):
            dec = c[2][0, pl.ds(pl.multiple_of(c[4] * cpt, cpt), cpt), c[3]]
            st = states[i]
            start = [None] * cpt
            for n in orders[i]:
                start[n] = st.T.astype(BF16)
                st = dec[n:n + 1, :] * st + ups[i][n]
            starts.append(start)
            new_states.append(st)
        for i, c in enumerate(chains):
            outs = [o[i][n * CHUNK:(n + 1) * CHUNK, :]
                    + _dot(qd[i][n * CHUNK:(n + 1) * CHUNK, :], starts[i][n]) for n in range(cpt)]
            emit(jnp.concatenate(outs, axis=0), c[3], rows[i], finalize)
        return tuple(new_states)

    states = []
    for h in range(heads):
        cols = slice(h * HG_DIM, (h + 1) * HG_DIM)
        states += [ctx_state(ckd2f_ref, cdecf_ref, cols, False), ctx_state(ckd2b_ref, cdecb_ref, cols, True)]
    states = lax.fori_loop(0, half, functools.partial(body, finalize=False), tuple(states))
    lax.fori_loop(half, n_tiles, functools.partial(body, finalize=True), states)


def _hgrn_call(hg, v, vt, dec, ckd2f, ckd2b, cvt, cdec, g, nw, weights):
    b_, s_, _ = v.shape
    l_ = ckd2f.shape[1]
    w = HG_HEADS_PER_STEP * HG_DIM
    groups = HG_HEADS // HG_HEADS_PER_STEP
    seq = lambda n: pl.BlockSpec((1, n, w), lambda b, h: (b, 0, h))
    vt_spec = lambda n: pl.BlockSpec((1, n // LANES, w, LANES), lambda b, h: (b, 0, h, 0))
    dec_f = lambda n: pl.BlockSpec((1, n // CHUNK, w), lambda b, h: (b, 0, h))
    dec_b = lambda n: pl.BlockSpec((1, n // CHUNK, w), lambda b, h: (b, 0, groups + h))
    w_specs, w_shapes = _side_job_specs(weights, b_ * groups, lambda b, h: b * groups + h)
    outs = pl.pallas_call(
        _hgrn_kernel,
        grid=(b_, groups),
        in_specs=[pl.BlockSpec((1, s_, w), lambda b, h, j=j: (b, 0, j * groups + h)) for j in range(6)]
        + [seq(s_), vt_spec(s_), dec_f(s_), dec_b(s_),
                                  seq(l_), seq(l_), vt_spec(l_), dec_f(l_), dec_b(l_),
                                  seq(s_), _const_spec((1, HG_DIM))] + w_specs,
        out_specs=[seq(s_)] + w_specs,
        out_shape=[jax.ShapeDtypeStruct((b_, s_, HG_WIDTH), BF16)] + w_shapes,
        scratch_shapes=[pltpu.VMEM((s_, w), F32)],
        compiler_params=pltpu.CompilerParams(dimension_semantics=("parallel", "parallel"),
                                             vmem_limit_bytes=VMEM_LIMIT),
        name="hgrn_scan",
    )(hg, hg, hg, hg, hg, hg, v, vt, dec, dec, ckd2f, ckd2b, cvt, cdec, cdec, g, nw, *weights)
    return outs[0], outs[1:]


def _attn_kernel(sink_ref, q_ref, kt_ref, kc_ref, v_ref, vc_ref, *rest):
    n_w = (len(rest) - 1) // 2
    y_ref = rest[n_w]
    _cast_side_job(rest[:n_w], rest[n_w + 1:])
    nb = kt_ref.shape[1]
    blk = ATT_BLOCK
    hpg = ATT_GROUP_HEADS
    n_groups = ATT_HEADS // hpg
    qry_r = lax.broadcasted_iota(jnp.int32, (blk, blk), 0)
    key_c = lax.broadcasted_iota(jnp.int32, (blk, blk), 1)
    left = lax.broadcasted_iota(jnp.int32, (blk, LANES), 1) < HEAD_DIM
    ctx_keys = [kc_ref[0, c] for c in range(kc_ref.shape[1])]
    keep = {}

    def band(u):
        qb = pl.program_id(1) * ATT_STEP_BLOCKS + u
        ids = (jnp.maximum(qb - 1, 0), qb, jnp.minimum(qb + 1, nb - 1))
        keys_t = jnp.concatenate([kt_ref[0, j] for j in ids] + ctx_keys, axis=1)
        values = jnp.concatenate([v_ref[0, pl.ds(pl.multiple_of(j * blk, blk), blk), :] for j in ids]
                                 + [vc_ref[0]], axis=0)
        values = jnp.concatenate([values, jnp.ones_like(values)], axis=1)
        return qb, keys_t, values

    def score_issue(u, g):
        if g == 0:
            keep["band", u] = band(u)
        q = jnp.concatenate([q_ref[0, u * blk:(u + 1) * blk, h * LANES:(h + 1) * LANES]
                             for h in range(g * hpg, (g + 1) * hpg)], axis=0)
        return _dot(q, keep["band", u][1])

    def score_finish(raw, u, g):
        qb = keep["band", u][0]
        mask_prev = (key_c >= qry_r) & (qb > 0)
        mask_next = (key_c <= qry_r) & (qb < nb - 1)
        probs, sink_p = [], []
        for n in range(hpg):
            s = raw[n * blk:(n + 1) * blk, :]
            sink = sink_ref[g * hpg + n] * LOG2E
            s = jnp.concatenate([jnp.where(mask_prev, s[:, 0:blk], NEG), s[:, blk:2 * blk],
                                 jnp.where(mask_next, s[:, 2 * blk:3 * blk], NEG), s[:, 3 * blk:]], axis=1)
            mx = jnp.maximum(jnp.max(s, axis=1, keepdims=True), sink)
            probs.append(jnp.exp2(s - mx).astype(BF16))
            sink_p.append(jnp.exp2(sink - mx))
        keep["p", u, g] = jnp.concatenate(probs, axis=0)
        keep["sink_p", u, g] = sink_p

    def value_issue(u, g):
        return _dot(keep["p", u, g], keep["band", u][2])

    def value_finish(raw, u, g):
        for n in range(0, hpg, 2):
            h = g * hpg + n
            even, odd = [raw[m * blk:(m + 1) * blk, 0:LANES]
                         / (raw[m * blk:(m + 1) * blk, LANES:2 * LANES] + keep["sink_p", u, g][m])
                         for m in (n, n + 1)]
            if h < ATT_HEADS // ATT_KV_HEADS:
                y = jnp.where(left, even, pltpu.roll(odd, HEAD_DIM, 1))
            else:
                y = jnp.where(left, pltpu.roll(even, HEAD_DIM, 1), odd)
            y_ref[0, u * blk:(u + 1) * blk, (h // 2) * LANES:(h // 2 + 1) * LANES] = y.astype(BF16)

    part = functools.partial
    work = [(u, g) for u in range(ATT_STEP_BLOCKS) for g in range(n_groups)]
    score = lambda u, g: (part(score_issue, u, g), part(score_finish, u=u, g=g))
    stages = [score(*w) for w in work[:ATT_DEPTH + 1]]
    for n, (u, g) in enumerate(work):
        stages.append((part(value_issue, u, g), part(value_finish, u=u, g=g)))
        if n + ATT_DEPTH + 1 < len(work):
            stages.append(score(*work[n + ATT_DEPTH + 1]))
    _emit_pipelined(stages, ATT_DEPTH)


def _attn_call(sinks, q, katt, vatt, ckatt, cvatt, weights):
    b_, s_, qw = q.shape
    l_ = cvatt.shape[1]
    rows = ATT_STEP_BLOCKS * ATT_BLOCK
    batch4 = lambda n: pl.BlockSpec((1, n, KV_WIDTH, LANES), lambda b, i: (b, 0, 0, 0))
    batch3 = lambda n: pl.BlockSpec((1, n, KV_WIDTH), lambda b, i: (b, 0, 0))
    steps = s_ // rows
    w_specs, w_shapes = _side_job_specs(weights, b_ * steps, lambda b, i: b * steps + i)
    outs = pl.pallas_call(
        _attn_kernel,
        grid=(b_, steps),
        in_specs=[pl.BlockSpec(memory_space=pltpu.SMEM),
                  pl.BlockSpec((1, rows, qw), lambda b, i: (b, i, 0)),
                  batch4(s_ // LANES), batch4(l_ // LANES), batch3(s_), batch3(l_)] + w_specs,
        out_specs=[pl.BlockSpec((1, rows, ATT_WIDTH), lambda b, i: (b, i, 0))] + w_specs,
        out_shape=[jax.ShapeDtypeStruct((b_, s_, ATT_WIDTH), BF16)] + w_shapes,
        compiler_params=pltpu.CompilerParams(dimension_semantics=("parallel", "arbitrary"),
                                             vmem_limit_bytes=VMEM_LIMIT),
        name="window_attn",
    )(sinks, q, katt, ckatt, vatt, cvatt, *weights)
    return outs[0], outs[1:]


def _merge_ffn_kernel(x_ref, yh_ref, ya_ref, gates_ref, mod_ref, nw_ref, wbh_ref, wba_ref, wo_ref,
                      wg_ref, wu_ref, wd_ref, o_ref):
    tm = x_ref.shape[1]
    n_sub = FFN_SUBTILES
    sub = tm // n_sub
    bounds = FFN_CHUNK_BOUNDS
    n_chunks = len(bounds) - 1
    g1 = mod_ref[0, :, 2 * D_MODEL:3 * D_MODEL]
    sh2 = mod_ref[0, :, 3 * D_MODEL:4 * D_MODEL]
    sc2 = mod_ref[0, :, 4 * D_MODEL:5 * D_MODEL]
    g2 = mod_ref[0, :, 5 * D_MODEL:6 * D_MODEL]
    keep = {}

    def branch_issue(r):
        rows = slice(r * sub, (r + 1) * sub)
        return _dot(yh_ref[0, rows, :], wbh_ref[...]), _dot(ya_ref[0, rows, :], wba_ref[...])

    def branch_finish(raw, r):
        rows = slice(r * sub, (r + 1) * sub)
        a, b = raw
        mixed = (gates_ref[0, rows, 0:D_MODEL].astype(F32) * a
                 + gates_ref[0, rows, D_MODEL:2 * D_MODEL].astype(F32) * b)
        keep["mixed", r] = (0.5 * mixed).astype(BF16)

    def out_issue(r):
        return _dot(keep["mixed", r], wo_ref[...])

    def out_finish(raw, r):
        x1 = x_ref[0, r * sub:(r + 1) * sub, :] + g1 * raw
        ms = jnp.mean(x1 * x1, axis=-1, keepdims=True)
        keep["x1", r] = x1
        keep["h2", r] = ((x1 * lax.rsqrt(ms + EPS) * nw_ref[...]) * (1.0 + sc2) + sh2).astype(BF16)

    def hidden_issue(r, c):
        h2 = keep["h2", r]
        cols = slice(bounds[c], bounds[c + 1])
        return _dot(h2, wg_ref[:, cols]), _dot(h2, wu_ref[:, cols])

    def hidden_finish(raw, r, c):
        gate, up = raw
        keep["act", r, c] = (gate * _sigmoid(gate) * up).astype(BF16)

    def down_issue(r, c):
        return _dot(keep["act", r, c], wd_ref[bounds[c]:bounds[c + 1], :])

    def down_finish(raw, r, c):
        dn = raw if c == 0 else keep["dn", r] + raw
        if c + 1 < n_chunks:
            keep["dn", r] = dn
        else:
            o_ref[0, r * sub:(r + 1) * sub, :] = keep["x1", r] + g2 * dn

    part = functools.partial
    stages = []
    for issue, finish in ((branch_issue, branch_finish), (out_issue, out_finish)):
        stages += [(part(issue, r), part(finish, r=r)) for r in range(n_sub)]
    for c in range(n_chunks):
        stages += [(part(hidden_issue, r, c), part(hidden_finish, r=r, c=c)) for r in range(n_sub)]
        stages += [(part(down_issue, r, c), part(down_finish, r=r, c=c)) for r in range(n_sub)]
    _emit_pipelined(stages, n_sub - 1)


def _merge_ffn_call(x, yh, ya, gates, mod3, nw, wbh, wba, wo, wg, wu, wd):
    b_, s_, _ = x.shape
    tm = TOK_TILE
    tok = lambda w: pl.BlockSpec((1, tm, w), lambda b, i: (b, i, 0))
    return pl.pallas_call(
        _merge_ffn_kernel,
        grid=(b_, s_ // tm),
        in_specs=[tok(D_MODEL), tok(HG_WIDTH), tok(ATT_WIDTH), tok(2 * D_MODEL),
                  pl.BlockSpec((1, 1, mod3.shape[2]), lambda b, i: (b, 0, 0)),
                  _const_spec((1, D_MODEL)),
                  _const_spec(wbh.shape), _const_spec(wba.shape), _const_spec(wo.shape),
                  _const_spec(wg.shape), _const_spec(wu.shape), _const_spec(wd.shape)],
        out_specs=tok(D_MODEL),
        out_shape=jax.ShapeDtypeStruct(x.shape, F32),
        compiler_params=pltpu.CompilerParams(dimension_semantics=("parallel", "parallel"),
                                             vmem_limit_bytes=VMEM_LIMIT),
        name="merge_ffn",
    )(x, yh, ya, gates, mod3, nw, wbh, wba, wo, wg, wu, wd)


def _rope_tables(n_tok):
    t = np.arange(n_tok)
    rows = (t // GRID_W).astype(np.float64)
    cols = (t % GRID_W).astype(np.float64)
    half = HEAD_DIM // 2
    inv_freq = ROPE_THETA ** (-np.arange(0, half, 2, dtype=np.float64) / half)
    d = np.arange(LANES) % HEAD_DIM
    pos = np.where((d < half)[None, :], rows[:, None], cols[:, None])
    ang = pos * inv_freq[(d % half) % (half // 2)][None, :]
    sign = np.where((d % half) < half // 2, -1.0, 1.0)[None, :]
    return jnp.asarray(np.cos(ang), F32), jnp.asarray(np.sin(ang) * sign, F32)


def kernel(x, c, ctx, c_ctx, w_ada, b_ada, norm_mix_w, norm_ffn_w, w_in, hgrn_lb_logits, hgrn_norm_w,
           q_norm_w, k_norm_w, attn_sinks, w_branch_hgrn, w_branch_attn, w_out, w_ffn_gate, w_ffn_up,
           w_ffn_down):
    b_, s_, _ = x.shape
    layer = 0
    col = np.ones((1, IN_COLS), np.float32)
    for lo, width in ((C_FF, 2 * HG_WIDTH), (C_QHG, 2 * HG_WIDTH), (C_GATES, 2 * D_MODEL)):
        col[:, lo:lo + width] = 0.5
    mod, w_in_b = _ada_call(c, c_ctx[None, :], w_ada[layer], b_ada[layer][None, :], w_in[layer],
                            jnp.asarray(col))

    lbl = hgrn_lb_logits[:, 0:2, :]
    qw = q_norm_w[layer][None, :]
    kw = k_norm_w[layer][None, :]
    nw_mix = norm_mix_w[layer][None, :]
    cos_t, sin_t = _rope_tables(s_)

    (hg, dec, vt, v, g, q, katt, vatt, gates) = _inproj_call(
        x, mod, 0, nw_mix, w_in_b, lbl, qw, kw, cos_t, sin_t, latent=True)
    l_ = ctx.shape[1]
    pack = max(1, min(CTX_TILE // l_, b_))
    ctx_out = _inproj_call(ctx.reshape(b_ // pack, pack * l_, D_MODEL), mod, b_, nw_mix, w_in_b, lbl, qw, kw,
                           cos_t, sin_t, latent=False)
    ckd2f, ckd2b, cdec, cvt, ckatt, cvatt = [t.reshape((b_, t.shape[1] // pack) + t.shape[2:])
                                             for t in ctx_out]

    y_hg, (w_bh, w_ba, w_o) = _hgrn_call(
        hg, v, vt, dec, ckd2f, ckd2b, cvt, cdec, g, hgrn_norm_w[layer][None, :],
        (w_branch_hgrn[layer], w_branch_attn[layer], w_out[layer]))
    y_at, (w_g, w_u, w_d) = _attn_call(attn_sinks[layer], q, katt, vatt, ckatt, cvatt,
                                       (w_ffn_gate[layer], w_ffn_up[layer], w_ffn_down[layer]))

    return _merge_ffn_call(x, y_hg, y_at, gates, mod, norm_ffn_w[layer][None, :],
                           w_bh, w_ba, w_o, w_g, w_u, w_d)
```

```python
import functools

import jax
import jax.numpy as jnp
import numpy as np
from jax import lax
from jax.experimental import pallas as pl
from jax.experimental.pallas import tpu as pltpu

F32 = jnp.float32
BF16 = jnp.bfloat16

D_MODEL = 1024
GRID_W = 64
EPS = 1e-6
HG_HEADS = 4
HG_DIM = 128
HG_WIDTH = HG_HEADS * HG_DIM
CHUNK = 32
ATT_HEADS = 8
ATT_KV_HEADS = 2
HEAD_DIM = 64
ATT_WIDTH = ATT_HEADS * HEAD_DIM
KV_WIDTH = ATT_KV_HEADS * HEAD_DIM
WINDOW = 128
ROPE_THETA = 10000.0
D_FF = 2816
CTX_COLS = 3 * HG_WIDTH + 2 * KV_WIDTH
IN_COLS = CTX_COLS + 2 * HG_WIDTH + ATT_WIDTH + 2 * D_MODEL

C_FF, C_FB, C_INP, C_K, C_V = 0, HG_WIDTH, 2 * HG_WIDTH, 3 * HG_WIDTH, 3 * HG_WIDTH + KV_WIDTH
C_QHG = CTX_COLS
C_GHG = C_QHG + HG_WIDTH
C_Q = C_GHG + HG_WIDTH
C_GATES = C_Q + ATT_WIDTH

LANES = 128
TOK_TILE = 512
FFN_SUBTILES = 2
MXU_DEPTH = 256
FFN_CHUNK_BOUNDS = (0, 6 * MXU_DEPTH, D_FF)
INPROJ_TILE = 256
CTX_TILE = 512
INPROJ_DEPTH = 2
HG_TILE = 256
HG_HEADS_PER_STEP = 4
ATT_BLOCK = 128
ATT_GROUP_HEADS = 2
ATT_DEPTH = 1
ATT_STEP_BLOCKS = 8
ADA_ROWS = 16
ADA_STEPS = 6
VMEM_LIMIT = 56 * 1024 * 1024
NEG = -1e30
LOG2E = 1.4426950408889634

assert ATT_BLOCK == WINDOW == LANES and HG_DIM == LANES and 2 * HEAD_DIM == LANES
assert ATT_GROUP_HEADS % 2 == 0 and (ATT_HEADS // ATT_KV_HEADS) % ATT_GROUP_HEADS == 0


def _dot(a, b):
    return jnp.dot(a, b, preferred_element_type=F32)


def _dot_nt(a, b):
    return lax.dot_general(a, b, (((1,), (1,)), ((), ())), preferred_element_type=F32)


def _sigmoid(x):
    return 0.5 * jnp.tanh(0.5 * x) + 0.5


def _emit_pipelined(stages, depth):
    raws = [issue() for issue, _ in stages[:depth]]
    for i, (_, finish) in enumerate(stages):
        if i + depth < len(stages):
            raws.append(stages[i + depth][0]())
        finish(raws[i])
        raws[i] = None


def _const_spec(shape):
    n = len(shape)
    return pl.BlockSpec(shape, lambda *_: (0,) * n, pipeline_mode=pl.Buffered(1))


def _ada_kernel(c_ref, cctx_ref, w_ref, b_ref, win_ref, col_ref, o_ref, winb_ref):
    rows = o_ref.shape[0]
    n_cond = c_ref.shape[0] + 1
    c = jnp.concatenate([c_ref[...], cctx_ref[...], jnp.zeros((rows - n_cond, D_MODEL), F32)], axis=0)
    s = (c * _sigmoid(c)).astype(BF16)
    mod = _dot(s, w_ref[...].astype(BF16)) + b_ref[...]
    for r in range(mod.shape[0]):
        o_ref[r] = mod[r:r + 1, :]
    winb_ref[...] = (win_ref[...] * col_ref[...]).astype(BF16)


def _ada_call(c, c_ctx, w_ada, b_ada, w_in, col):
    rows = ADA_ROWS
    n_out = w_ada.shape[1]
    steps = ADA_STEPS
    bn = n_out // steps
    bw = w_in.shape[1] // steps
    return pl.pallas_call(
        _ada_kernel,
        grid=(steps,),
        in_specs=[pl.BlockSpec(c.shape, lambda j: (0, 0)),
                  pl.BlockSpec((1, D_MODEL), lambda j: (0, 0)),
                  pl.BlockSpec((D_MODEL, bn), lambda j: (0, j)),
                  pl.BlockSpec((1, bn), lambda j: (0, j)),
                  pl.BlockSpec((D_MODEL, bw), lambda j: (0, j)),
                  pl.BlockSpec((1, bw), lambda j: (0, j))],
        out_specs=[pl.BlockSpec((rows, 1, bn), lambda j: (0, 0, j)),
                   pl.BlockSpec((D_MODEL, bw), lambda j: (0, j))],
        out_shape=[jax.ShapeDtypeStruct((rows, 1, n_out), F32),
                   jax.ShapeDtypeStruct(w_in.shape, BF16)],
        compiler_params=pltpu.CompilerParams(dimension_semantics=("arbitrary",),
                                             vmem_limit_bytes=VMEM_LIMIT),
        name="ada_mod",
    )(c, c_ctx, w_ada, b_ada, w_in, col)


def _chunk_cumsum(x, reverse):
    n, c = x.shape
    sub = 8
    r = lax.broadcasted_iota(jnp.int32, x.shape, 0) & (CHUNK - 1)
    s = 1
    while s < sub:
        if reverse:
            x = x + jnp.where(r < CHUNK - s, pltpu.roll(x, n - s, 0), 0.0)
        else:
            x = x + jnp.where(r >= s, pltpu.roll(x, s, 0), 0.0)
        s *= 2
    per = CHUNK // sub
    x4 = x.reshape(n // CHUNK, per, sub, c)
    s = 1
    while s < per:
        if reverse:
            x4 = jnp.concatenate([x4[:, :per - s] + x4[:, s:], x4[:, per - s:]], axis=1)
        else:
            x4 = jnp.concatenate([x4[:, :s], x4[:, s:] + x4[:, :per - s]], axis=1)
        s *= 2
    return x4.reshape(n, c)


def _rope(t, cos, sin_signed, lane):
    partner = jnp.where((lane & 31) < 16, pltpu.roll(t, LANES - 16, 1), pltpu.roll(t, 16, 1))
    return t * cos + partner * sin_signed


def _inproj_kernel(x_ref, mod_ref, nw_ref, w_ref, lbl_ref, qw_ref, kw_ref, cos_ref, sin_ref,
                   *out_refs, latent):
    if latent:
        hg_ref, dec_ref, vt_ref, v_ref, g_ref, q_ref, katt_ref, vatt_ref, gates_ref = out_refs
    else:
        kd2f_ref, kd2b_ref, dec_ref, vt_ref, katt_ref, vatt_ref = out_refs

    tm = x_ref.shape[1]
    nchunk = tm // CHUNK
    x = x_ref[0]
    sh = mod_ref[0, :, 0:D_MODEL]
    sc = mod_ref[0, :, D_MODEL:2 * D_MODEL]
    ms = jnp.mean(x * x, axis=-1, keepdims=True)
    h = x * lax.rsqrt(ms + EPS) * (nw_ref[...] * (1.0 + sc)) + sh
    hb = h.astype(BF16)

    def mm(a, b):
        return _dot(hb, w_ref[:, a:b])

    l0f, l1f = lbl_ref[0, 0:1, :], lbl_ref[0, 1:2, :]
    l0b, l1b = lbl_ref[1, 0:1, :], lbl_ref[1, 1:2, :]
    lb_f = 1.0 / (1.0 + jnp.exp(l1f - l0f))
    lb_b = 1.0 / (1.0 + jnp.exp(l1b - l0b))

    qw = jnp.concatenate([qw_ref[...]] * (LANES // HEAD_DIM), axis=1)
    kw = jnp.concatenate([kw_ref[...]] * (LANES // HEAD_DIM), axis=1)
    gw = HG_WIDTH // 2
    sw = D_MODEL // 2
    lane = lax.broadcasted_iota(jnp.int32, (tm, LANES), 1)
    left = lane < HEAD_DIM
    bi = lax.broadcasted_iota(jnp.int32, (2 * LANES, LANES), 0) // HEAD_DIM
    bj = lax.broadcasted_iota(jnp.int32, (2 * LANES, LANES), 1) // HEAD_DIM
    ones_blk2 = jnp.where((bi % 2) == bj, 1.0, 0.0).astype(BF16)
    keep = {}

    def qhg_finish(raw):
        keep["q_hg"] = raw * (jnp.tanh(raw) + 1.0) * (HG_DIM ** -0.5)

    def gate_finish(raw, direction, part):
        reverse = direction == 1
        lo = part * gw
        lb = (lb_f, lb_b)[direction][:, lo:lo + gw]
        c_half = 0.5 - 0.5 * lb
        ct = c_half * jnp.tanh(raw)
        f = (0.5 + 0.5 * lb) + ct
        k = c_half - ct
        cum = _chunk_cumsum(jnp.log(f), reverse)
        cum3 = cum.reshape(nchunk, CHUNK, gw)
        tot3 = cum3[:, 0:1, :] if reverse else cum3[:, CHUNK - 1:CHUNK, :]
        dec3 = jnp.exp(tot3)
        dec_ref[0, :, direction * HG_WIDTH + lo:direction * HG_WIDTH + lo + gw] = dec3.reshape(nchunk, gw)
        kd2 = (k.reshape(nchunk, CHUNK, gw) * jnp.exp(tot3 - cum3)).reshape(tm, gw).astype(BF16)
        if latent:
            base = direction * 3 * HG_WIDTH + lo
            hg_ref[0, :, base:base + gw] = (keep["q_hg"][:, lo:lo + gw] * jnp.exp(cum)).astype(BF16)
            hg_ref[0, :, base + HG_WIDTH:base + HG_WIDTH + gw] = (k * jnp.exp(-cum)).astype(BF16)
            hg_ref[0, :, base + 2 * HG_WIDTH:base + 2 * HG_WIDTH + gw] = kd2
        else:
            (kd2f_ref, kd2b_ref)[direction][0, :, lo:lo + gw] = kd2

    def mgate_finish(raw, j):
        gates_ref[0, :, j * sw:(j + 1) * sw] = (jnp.tanh(raw) + 1.0).astype(BF16)

    def v_finish(raw):
        for s in range(tm // LANES):
            vt_ref[0, s] = raw[s * LANES:(s + 1) * LANES, :].T.astype(BF16)
        if latent:
            v_ref[0] = raw.astype(BF16)

    def g_finish(raw):
        g_ref[0] = (raw * (jnp.tanh(raw) + 1.0)).astype(BF16)

    def split_sq(t):
        sq = t * t
        hi = sq.astype(BF16)
        lo = (sq - hi.astype(F32)).astype(BF16)
        return jnp.concatenate([hi, lo], axis=1)

    def kv_finish(raw):
        keep["k_raw"] = raw[:, 0:LANES]
        keep["k_split"] = split_sq(keep["k_raw"])
        vatt_ref[0] = raw[:, LANES:2 * LANES].astype(BF16)

    def knorm_finish(ss):
        kn = keep["k_raw"] * lax.rsqrt(ss * (1.0 / HEAD_DIM) + EPS) * kw
        if latent:
            kn = _rope(kn, cos_ref[...], sin_ref[...], lane)
        for s in range(tm // LANES):
            katt_ref[0, s] = kn[s * LANES:(s + 1) * LANES, :].T.astype(BF16)

    def q_finish(raw):
        keep["q_raw"] = raw
        keep["q_split"] = [split_sq(raw[:, s * LANES:(s + 1) * LANES]) for s in range(ATT_WIDTH // LANES)]

    def qnorm_finish(ss_list):
        for s, ss in enumerate(ss_list):
            qraw = keep["q_raw"][:, s * LANES:(s + 1) * LANES]
            qn = qraw * lax.rsqrt(ss * (1.0 / HEAD_DIM) + EPS) * qw
            qn = _rope(qn, cos_ref[...], sin_ref[...], lane) * (HEAD_DIM ** -0.5 * LOG2E)
            swapped = pltpu.roll(qn, HEAD_DIM, 1)
            if s < ATT_WIDTH // LANES // 2:
                even, odd = jnp.where(left, qn, 0.0), jnp.where(left, swapped, 0.0)
            else:
                even, odd = jnp.where(left, 0.0, swapped), jnp.where(left, 0.0, qn)
            q_ref[0, :, (2 * s) * LANES:(2 * s + 1) * LANES] = even.astype(BF16)
            q_ref[0, :, (2 * s + 1) * LANES:(2 * s + 2) * LANES] = odd.astype(BF16)

    def gate_stage(direction, part):
        col = (C_FF, C_FB)[direction] + part * gw
        return (lambda: mm(col, col + gw),
                functools.partial(gate_finish, direction=direction, part=part))

    def mgate_stage(j):
        c0 = C_GATES + j * sw
        return (lambda: mm(c0, c0 + sw), functools.partial(mgate_finish, j=j))

    v_stage = (lambda: mm(C_INP, C_INP + HG_WIDTH), v_finish)
    kv_stage = (lambda: mm(C_K, C_K + 2 * KV_WIDTH), kv_finish)
    knorm_stage = (lambda: _dot(keep["k_split"], ones_blk2), knorm_finish)
    if latent:
        stages = [(lambda: mm(C_QHG, C_QHG + HG_WIDTH), qhg_finish)]
        for j in range(4):
            stages += [gate_stage(j // 2, j % 2), mgate_stage(j)]
        stages += [(lambda: mm(C_Q, C_Q + ATT_WIDTH), q_finish),
                   v_stage,
                   kv_stage,
                   (lambda: [_dot(t, ones_blk2) for t in keep["q_split"]], qnorm_finish),
                   (lambda: mm(C_GHG, C_GHG + HG_WIDTH), g_finish),
                   knorm_stage]
    else:
        stages = [gate_stage(0, 0), gate_stage(0, 1), kv_stage, gate_stage(1, 0), gate_stage(1, 1),
                  knorm_stage, v_stage]
    _emit_pipelined(stages, INPROJ_DEPTH)


def _inproj_call(x, mod3, mod_row0, nw, w_in, lbl, qw, kw, cos_t, sin_t, latent):
    b_, s_, _ = x.shape
    tm = min(INPROJ_TILE if latent else CTX_TILE, s_)
    grid = (b_, s_ // tm)
    tok = lambda w: pl.BlockSpec((1, tm, w), lambda b, i: (b, i, 0))
    mod_map = (lambda b, i: (mod_row0 + b, 0, 0)) if latent else (lambda b, i: (mod_row0, 0, 0))
    ncols = IN_COLS if latent else CTX_COLS
    in_specs = [tok(D_MODEL),
                pl.BlockSpec((1, 1, mod3.shape[2]), mod_map),
                _const_spec((1, D_MODEL)),
                pl.BlockSpec((D_MODEL, ncols), lambda b, i: (0, 0), pipeline_mode=pl.Buffered(1)),
                _const_spec(lbl.shape),
                _const_spec((1, HEAD_DIM)),
                _const_spec((1, HEAD_DIM)),
                pl.BlockSpec((tm, LANES), lambda b, i: (i, 0)),
                pl.BlockSpec((tm, LANES), lambda b, i: (i, 0))]
    bf = lambda w: jax.ShapeDtypeStruct((b_, s_, w), BF16)
    dec_shape = jax.ShapeDtypeStruct((b_, s_ // CHUNK, 2 * HG_WIDTH), F32)
    dec_spec = pl.BlockSpec((1, tm // CHUNK, 2 * HG_WIDTH), lambda b, i: (b, i, 0))
    vt_shape = jax.ShapeDtypeStruct((b_, s_ // LANES, HG_WIDTH, LANES), BF16)
    vt_spec = pl.BlockSpec((1, tm // LANES, HG_WIDTH, LANES), lambda b, i: (b, i, 0, 0))
    katt_shape = jax.ShapeDtypeStruct((b_, s_ // LANES, KV_WIDTH, LANES), BF16)
    katt_spec = pl.BlockSpec((1, tm // LANES, KV_WIDTH, LANES), lambda b, i: (b, i, 0, 0))
    if latent:
        out_shape = [bf(6 * HG_WIDTH), dec_shape, vt_shape, bf(HG_WIDTH), bf(HG_WIDTH),
                                         bf(ATT_HEADS * LANES), katt_shape, bf(KV_WIDTH), bf(2 * D_MODEL)]
        out_specs = [tok(6 * HG_WIDTH), dec_spec, vt_spec, tok(HG_WIDTH), tok(HG_WIDTH),
                                          tok(ATT_HEADS * LANES), katt_spec, tok(KV_WIDTH), tok(2 * D_MODEL)]
    else:
        out_shape = [bf(HG_WIDTH)] * 2 + [dec_shape, vt_shape, katt_shape, bf(KV_WIDTH)]
        out_specs = [tok(HG_WIDTH)] * 2 + [dec_spec, vt_spec, katt_spec, tok(KV_WIDTH)]
    return pl.pallas_call(
        functools.partial(_inproj_kernel, latent=latent),
        grid=grid, in_specs=in_specs, out_specs=out_specs, out_shape=out_shape,
        compiler_params=pltpu.CompilerParams(dimension_semantics=("parallel", "parallel"),
                                             vmem_limit_bytes=VMEM_LIMIT),
        name="inproj_latent" if latent else "inproj_ctx",
    )(x, mod3, nw, w_in, lbl, qw, kw, cos_t, sin_t)


def _cast_side_job(w_refs, wb_refs):
    for w_ref, wb_ref in zip(w_refs, wb_refs):
        wb_ref[...] = w_ref[...].astype(BF16)


def _side_job_specs(weights, n_steps, step_of):
    specs = [pl.BlockSpec((wt.shape[0] // n_steps, wt.shape[1]), lambda *ids: (step_of(*ids), 0))
             for wt in weights]
    shapes = [jax.ShapeDtypeStruct(wt.shape, BF16) for wt in weights]
    return specs, shapes


def _hgrn_kernel(qdf_ref, kdf_ref, kd2f_ref, qdb_ref, kdb_ref, kd2b_ref, v_ref, vt_ref,
                 decf_ref, decb_ref, ckd2f_ref, ckd2b_ref, cvt_ref, cdecf_ref, cdecb_ref,
                 g_ref, nw_ref, *rest):
    n_w = (len(rest) - 2) // 2
    w_refs, y_ref, wb_refs, o_acc = rest[:n_w], rest[n_w], rest[n_w + 1:2 * n_w + 1], rest[-1]
    _cast_side_job(w_refs, wb_refs)

    seq = v_ref.shape[1]
    ctx_len = ckd2f_ref.shape[1]
    heads = v_ref.shape[2] // HG_DIM
    tile = HG_TILE
    cpt = tile // CHUNK
    cps = LANES // CHUNK
    spt = tile // LANES
    n_tiles = seq // tile
    half = n_tiles // 2

    row_chunk = lax.broadcasted_iota(jnp.int32, (LANES, HG_DIM), 0) // CHUNK
    ti = lax.broadcasted_iota(jnp.int32, (LANES, LANES), 0)
    tj = lax.broadcasted_iota(jnp.int32, (LANES, LANES), 1)
    same = (ti // CHUNK) == (tj // CHUNK)
    mask_f = same & (tj <= ti)
    mask_b = same & (tj >= ti)

    def chunk_update(vt_slab, k_slab, n_in_slab):
        k_m = jnp.where(row_chunk == n_in_slab, k_slab, jnp.zeros_like(k_slab))
        return _dot(vt_slab, k_m)

    def ctx_state(ckd2_ref, cdec_ref, cols, reverse):
        st = jnp.zeros((HG_DIM, HG_DIM), F32)
        order = range(ctx_len // CHUNK)
        for n in (reversed(order) if reverse else order):
            slab = n // cps
            st = cdec_ref[0, n:n + 1, cols] * st + chunk_update(
                cvt_ref[0, slab, cols, :], ckd2_ref[0, slab * LANES:(slab + 1) * LANES, cols], n % cps)
        return st

    def emit(o_t, cols, rows, finalize):
        if finalize:
            o_t = o_t + o_acc[rows, cols]
            ms = jnp.mean(o_t * o_t, axis=-1, keepdims=True)
            y = o_t * lax.rsqrt(ms + EPS) * nw_ref[...] * g_ref[0, rows, cols].astype(F32)
            y_ref[0, rows, cols] = y.astype(BF16)
        else:
            o_acc[rows, cols] = o_t

    def body(it, states, finalize):
        chains = []
        for h in range(heads):
            cols = slice(h * HG_DIM, (h + 1) * HG_DIM)
            chains.append((qdf_ref, kdf_ref, decf_ref, cols, it, False, kd2f_ref))
            chains.append((qdb_ref, kdb_ref, decb_ref, cols, n_tiles - 1 - it, True, kd2b_ref))
        n_ch = len(chains)
        rows = [pl.ds(pl.multiple_of(c[4] * tile, tile), tile) for c in chains]
        orders = [list(reversed(range(cpt))) if c[5] else list(range(cpt)) for c in chains]
        qd, sc, ups = [], [], []
        for (qd_ref, kd_ref, _, cols, t, reverse, kd2_ref), r in zip(chains, rows):
            qd.append(qd_ref[0, r, cols])
            kd = kd_ref[0, r, cols]
            sc.append([_dot_nt(qd[-1][s * LANES:(s + 1) * LANES, :], kd[s * LANES:(s + 1) * LANES, :])
                       for s in range(spt)])
            kd2 = kd2_ref[0, r, cols]
            ups.append([chunk_update(vt_ref[0, t * spt + n // cps, cols, :],
                                     kd2[(n // cps) * LANES:(n // cps + 1) * LANES, :], n % cps)
                        for n in range(cpt)])
        o = []
        for c, r, s in zip(chains, rows, sc):
            v = v_ref[0, r, c[3]]
            o.append(jnp.concatenate(
                [_dot(jnp.where(mask_b if c[5] else mask_f, s[j], 0.0).astype(BF16),
                      v[j * LANES:(j + 1) * LANES, :]) for j in range(spt)], axis=0))
        starts, new_states = [], []
        for i, c in enumerate(chains):
            dec = c[2][0, pl.ds(pl.multiple_of(c[4] * cpt, cpt), cpt), c[3]]
            st = states[i]
            start = [None] * cpt
            for n in orders[i]:
                start[n] = st.T.astype(BF16)
                st = dec[n:n + 1, :] * st + ups[i][n]
            starts.append(start)
            new_states.append(st)
        for i, c in enumerate(chains):
            outs = [o[i][n * CHUNK:(n + 1) * CHUNK, :]
                    + _dot(qd[i][n * CHUNK:(n + 1) * CHUNK, :], starts[i][n]) for n in range(cpt)]
            emit(jnp.concatenate(outs, axis=0), c[3], rows[i], finalize)
        return tuple(new_states)

    states = []
    for h in range(heads):
        cols = slice(h * HG_DIM, (h + 1) * HG_DIM)
        states += [ctx_state(ckd2f_ref, cdecf_ref, cols, False), ctx_state(ckd2b_ref, cdecb_ref, cols, True)]
    states = lax.fori_loop(0, half, functools.partial(body, finalize=False), tuple(states))
    lax.fori_loop(half, n_tiles, functools.partial(body, finalize=True), states)


def _hgrn_call(hg, v, vt, dec, ckd2f, ckd2b, cvt, cdec, g, nw, weights):
    b_, s_, _ = v.shape
    l_ = ckd2f.shape[1]
    w = HG_HEADS_PER_STEP * HG_DIM
    groups = HG_HEADS // HG_HEADS_PER_STEP
    seq = lambda n: pl.BlockSpec((1, n, w), lambda b, h: (b, 0, h))
    vt_spec = lambda n: pl.BlockSpec((1, n // LANES, w, LANES), lambda b, h: (b, 0, h, 0))
    dec_f = lambda n: pl.BlockSpec((1, n // CHUNK, w), lambda b, h: (b, 0, h))
    dec_b = lambda n: pl.BlockSpec((1, n // CHUNK, w), lambda b, h: (b, 0, groups + h))
    w_specs, w_shapes = _side_job_specs(weights, b_ * groups, lambda b, h: b * groups + h)
    outs = pl.pallas_call(
        _hgrn_kernel,
        grid=(b_, groups),
        in_specs=[pl.BlockSpec((1, s_, w), lambda b, h, j=j: (b, 0, j * groups + h)) for j in range(6)]
        + [seq(s_), vt_spec(s_), dec_f(s_), dec_b(s_),
                                  seq(l_), seq(l_), vt_spec(l_), dec_f(l_), dec_b(l_),
                                  seq(s_), _const_spec((1, HG_DIM))] + w_specs,
        out_specs=[seq(s_)] + w_specs,
        out_shape=[jax.ShapeDtypeStruct((b_, s_, HG_WIDTH), BF16)] + w_shapes,
        scratch_shapes=[pltpu.VMEM((s_, w), F32)],
        compiler_params=pltpu.CompilerParams(dimension_semantics=("parallel", "parallel"),
                                             vmem_limit_bytes=VMEM_LIMIT),
        name="hgrn_scan",
    )(hg, hg, hg, hg, hg, hg, v, vt, dec, dec, ckd2f, ckd2b, cvt, cdec, cdec, g, nw, *weights)
    return outs[0], outs[1:]


def _attn_kernel(sink_ref, q_ref, kt_ref, kc_ref, v_ref, vc_ref, *rest):
    n_w = (len(rest) - 1) // 2
    y_ref = rest[n_w]
    _cast_side_job(rest[:n_w], rest[n_w + 1:])
    nb = kt_ref.shape[1]
    blk = ATT_BLOCK
    hpg = ATT_GROUP_HEADS
    n_groups = ATT_HEADS // hpg
    qry_r = lax.broadcasted_iota(jnp.int32, (blk, blk), 0)
    key_c = lax.broadcasted_iota(jnp.int32, (blk, blk), 1)
    left = lax.broadcasted_iota(jnp.int32, (blk, LANES), 1) < HEAD_DIM
    ctx_keys = [kc_ref[0, c] for c in range(kc_ref.shape[1])]
    keep = {}

    def band(u):
        qb = pl.program_id(1) * ATT_STEP_BLOCKS + u
        ids = (jnp.maximum(qb - 1, 0), qb, jnp.minimum(qb + 1, nb - 1))
        keys_t = jnp.concatenate([kt_ref[0, j] for j in ids] + ctx_keys, axis=1)
        values = jnp.concatenate([v_ref[0, pl.ds(pl.multiple_of(j * blk, blk), blk), :] for j in ids]
                                 + [vc_ref[0]], axis=0)
        values = jnp.concatenate([values, jnp.ones_like(values)], axis=1)
        return qb, keys_t, values

    def score_issue(u, g):
        if g == 0:
            keep["band", u] = band(u)
        q = jnp.concatenate([q_ref[0, u * blk:(u + 1) * blk, h * LANES:(h + 1) * LANES]
                             for h in range(g * hpg, (g + 1) * hpg)], axis=0)
        return _dot(q, keep["band", u][1])

    def score_finish(raw, u, g):
        qb = keep["band", u][0]
        mask_prev = (key_c >= qry_r) & (qb > 0)
        mask_next = (key_c <= qry_r) & (qb < nb - 1)
        probs, sink_p = [], []
        for n in range(hpg):
            s = raw[n * blk:(n + 1) * blk, :]
            sink = sink_ref[g * hpg + n] * LOG2E
            s = jnp.concatenate([jnp.where(mask_prev, s[:, 0:blk], NEG), s[:, blk:2 * blk],
                                 jnp.where(mask_next, s[:, 2 * blk:3 * blk], NEG), s[:, 3 * blk:]], axis=1)
            mx = jnp.maximum(jnp.max(s, axis=1, keepdims=True), sink)
            probs.append(jnp.exp2(s - mx).astype(BF16))
            sink_p.append(jnp.exp2(sink - mx))
        keep["p", u, g] = jnp.concatenate(probs, axis=0)
        keep["sink_p", u, g] = sink_p

    def value_issue(u, g):
        return _dot(keep["p", u, g], keep["band", u][2])

    def value_finish(raw, u, g):
        for n in range(0, hpg, 2):
            h = g * hpg + n
            even, odd = [raw[m * blk:(m + 1) * blk, 0:LANES]
                         / (raw[m * blk:(m + 1) * blk, LANES:2 * LANES] + keep["sink_p", u, g][m])
                         for m in (n, n + 1)]
            if h < ATT_HEADS // ATT_KV_HEADS:
                y = jnp.where(left, even, pltpu.roll(odd, HEAD_DIM, 1))
            else:
                y = jnp.where(left, pltpu.roll(even, HEAD_DIM, 1), odd)
            y_ref[0, u * blk:(u + 1) * blk, (h // 2) * LANES:(h // 2 + 1) * LANES] = y.astype(BF16)

    part = functools.partial
    work = [(u, g) for u in range(ATT_STEP_BLOCKS) for g in range(n_groups)]
    score = lambda u, g: (part(score_issue, u, g), part(score_finish, u=u, g=g))
    stages = [score(*w) for w in work[:ATT_DEPTH + 1]]
    for n, (u, g) in enumerate(work):
        stages.append((part(value_issue, u, g), part(value_finish, u=u, g=g)))
        if n + ATT_DEPTH + 1 < len(work):
            stages.append(score(*work[n + ATT_DEPTH + 1]))
    _emit_pipelined(stages, ATT_DEPTH)


def _attn_call(sinks, q, katt, vatt, ckatt, cvatt, weights):
    b_, s_, qw = q.shape
    l_ = cvatt.shape[1]
    rows = ATT_STEP_BLOCKS * ATT_BLOCK
    batch4 = lambda n: pl.BlockSpec((1, n, KV_WIDTH, LANES), lambda b, i: (b, 0, 0, 0))
    batch3 = lambda n: pl.BlockSpec((1, n, KV_WIDTH), lambda b, i: (b, 0, 0))
    steps = s_ // rows
    w_specs, w_shapes = _side_job_specs(weights, b_ * steps, lambda b, i: b * steps + i)
    outs = pl.pallas_call(
        _attn_kernel,
        grid=(b_, steps),
        in_specs=[pl.BlockSpec(memory_space=pltpu.SMEM),
                  pl.BlockSpec((1, rows, qw), lambda b, i: (b, i, 0)),
                  batch4(s_ // LANES), batch4(l_ // LANES), batch3(s_), batch3(l_)] + w_specs,
        out_specs=[pl.BlockSpec((1, rows, ATT_WIDTH), lambda b, i: (b, i, 0))] + w_specs,
        out_shape=[jax.ShapeDtypeStruct((b_, s_, ATT_WIDTH), BF16)] + w_shapes,
        compiler_params=pltpu.CompilerParams(dimension_semantics=("parallel", "arbitrary"),
                                             vmem_limit_bytes=VMEM_LIMIT),
        name="window_attn",
    )(sinks, q, katt, ckatt, vatt, cvatt, *weights)
    return outs[0], outs[1:]


def _merge_ffn_kernel(x_ref, yh_ref, ya_ref, gates_ref, mod_ref, nw_ref, wbh_ref, wba_ref, wo_ref,
                      wg_ref, wu_ref, wd_ref, o_ref):
    tm = x_ref.shape[1]
    n_sub = FFN_SUBTILES
    sub = tm // n_sub
    bounds = FFN_CHUNK_BOUNDS
    n_chunks = len(bounds) - 1
    g1 = mod_ref[0, :, 2 * D_MODEL:3 * D_MODEL]
    sh2 = mod_ref[0, :, 3 * D_MODEL:4 * D_MODEL]
    sc2 = mod_ref[0, :, 4 * D_MODEL:5 * D_MODEL]
    g2 = mod_ref[0, :, 5 * D_MODEL:6 * D_MODEL]
    keep = {}

    def branch_issue(r):
        rows = slice(r * sub, (r + 1) * sub)
        return _dot(yh_ref[0, rows, :], wbh_ref[...]), _dot(ya_ref[0, rows, :], wba_ref[...])

    def branch_finish(raw, r):
        rows = slice(r * sub, (r + 1) * sub)
        a, b = raw
        mixed = (gates_ref[0, rows, 0:D_MODEL].astype(F32) * a
                 + gates_ref[0, rows, D_MODEL:2 * D_MODEL].astype(F32) * b)
        keep["mixed", r] = (0.5 * mixed).astype(BF16)

    def out_issue(r):
        return _dot(keep["mixed", r], wo_ref[...])

    def out_finish(raw, r):
        x1 = x_ref[0, r * sub:(r + 1) * sub, :] + g1 * raw
        ms = jnp.mean(x1 * x1, axis=-1, keepdims=True)
        keep["x1", r] = x1
        keep["h2", r] = ((x1 * lax.rsqrt(ms + EPS) * nw_ref[...]) * (1.0 + sc2) + sh2).astype(BF16)

    def hidden_issue(r, c):
        h2 = keep["h2", r]
        cols = slice(bounds[c], bounds[c + 1])
        return _dot(h2, wg_ref[:, cols]), _dot(h2, wu_ref[:, cols])

    def hidden_finish(raw, r, c):
        gate, up = raw
        keep["act", r, c] = (gate * _sigmoid(gate) * up).astype(BF16)

    def down_issue(r, c):
        return _dot(keep["act", r, c], wd_ref[bounds[c]:bounds[c + 1], :])

    def down_finish(raw, r, c):
        dn = raw if c == 0 else keep["dn", r] + raw
        if c + 1 < n_chunks:
            keep["dn", r] = dn
        else:
            o_ref[0, r * sub:(r + 1) * sub, :] = keep["x1", r] + g2 * dn

    part = functools.partial
    stages = []
    for issue, finish in ((branch_issue, branch_finish), (out_issue, out_finish)):
        stages += [(part(issue, r), part(finish, r=r)) for r in range(n_sub)]
    for c in range(n_chunks):
        stages += [(part(hidden_issue, r, c), part(hidden_finish, r=r, c=c)) for r in range(n_sub)]
        stages += [(part(down_issue, r, c), part(down_finish, r=r, c=c)) for r in range(n_sub)]
    _emit_pipelined(stages, n_sub - 1)


def _merge_ffn_call(x, yh, ya, gates, mod3, nw, wbh, wba, wo, wg, wu, wd):
    b_, s_, _ = x.shape
    tm = TOK_TILE
    tok = lambda w: pl.BlockSpec((1, tm, w), lambda b, i: (b, i, 0))
    return pl.pallas_call(
        _merge_ffn_kernel,
        grid=(b_, s_ // tm),
        in_specs=[tok(D_MODEL), tok(HG_WIDTH), tok(ATT_WIDTH), tok(2 * D_MODEL),
                  pl.BlockSpec((1, 1, mod3.shape[2]), lambda b, i: (b, 0, 0)),
                  _const_spec((1, D_MODEL)),
                  _const_spec(wbh.shape), _const_spec(wba.shape), _const_spec(wo.shape),
                  _const_spec(wg.shape), _const_spec(wu.shape), _const_spec(wd.shape)],
        out_specs=tok(D_MODEL),
        out_shape=jax.ShapeDtypeStruct(x.shape, F32),
        compiler_params=pltpu.CompilerParams(dimension_semantics=("parallel", "parallel"),
                                             vmem_limit_bytes=VMEM_LIMIT),
        name="merge_ffn",
    )(x, yh, ya, gates, mod3, nw, wbh, wba, wo, wg, wu, wd)


def _rope_tables(n_tok):
    t = np.arange(n_tok)
    rows = (t // GRID_W).astype(np.float64)
    cols = (t % GRID_W).astype(np.float64)
    half = HEAD_DIM // 2
    inv_freq = ROPE_THETA ** (-np.arange(0, half, 2, dtype=np.float64) / half)
    d = np.arange(LANES) % HEAD_DIM
    pos = np.where((d < half)[None, :], rows[:, None], cols[:, None])
    ang = pos * inv_freq[(d % half) % (half // 2)][None, :]
    sign = np.where((d % half) < half // 2, -1.0, 1.0)[None, :]
    return jnp.asarray(np.cos(ang), F32), jnp.asarray(np.sin(ang) * sign, F32)


def kernel(x, c, ctx, c_ctx, w_ada, b_ada, norm_mix_w, norm_ffn_w, w_in, hgrn_lb_logits, hgrn_norm_w,
           q_norm_w, k_norm_w, attn_sinks, w_branch_hgrn, w_branch_attn, w_out, w_ffn_gate, w_ffn_up,
           w_ffn_down):
    b_, s_, _ = x.shape
    layer = 0
    col = np.ones((1, IN_COLS), np.float32)
    for lo, width in ((C_FF, 2 * HG_WIDTH), (C_QHG, 2 * HG_WIDTH), (C_GATES, 2 * D_MODEL)):
        col[:, lo:lo + width] = 0.5
    mod, w_in_b = _ada_call(c, c_ctx[None, :], w_ada[layer], b_ada[layer][None, :], w_in[layer],
                            jnp.asarray(col))

    lbl = hgrn_lb_logits[:, 0:2, :]
    qw = q_norm_w[layer][None, :]
    kw = k_norm_w[layer][None, :]
    nw_mix = norm_mix_w[layer][None, :]
    cos_t, sin_t = _rope_tables(s_)

    (hg, dec, vt, v, g, q, katt, vatt, gates) = _inproj_call(
        x, mod, 0, nw_mix, w_in_b, lbl, qw, kw, cos_t, sin_t, latent=True)
    l_ = ctx.shape[1]
    pack = max(1, min(CTX_TILE // l_, b_))
    ctx_out = _inproj_call(ctx.reshape(b_ // pack, pack * l_, D_MODEL), mod, b_, nw_mix, w_in_b, lbl, qw, kw,
                           cos_t, sin_t, latent=False)
    ckd2f, ckd2b, cdec, cvt, ckatt, cvatt = [t.reshape((b_, t.shape[1] // pack) + t.shape[2:])
                                             for t in ctx_out]

    y_hg, _ = _hgrn_call(
        hg, v, vt, dec, ckd2f, ckd2b, cvt, cdec, g, hgrn_norm_w[layer][None, :], ())
    y_at, (w_bh, w_ba, w_o, w_g, w_u, w_d) = _attn_call(
        attn_sinks[layer], q, katt, vatt, ckatt, cvatt,
        (w_branch_hgrn[layer], w_branch_attn[layer], w_out[layer], w_ffn_gate[layer], w_ffn_up[layer],
         w_ffn_down[layer]))

    return _merge_ffn_call(x, y_hg, y_at, gates, mod, norm_ffn_w[layer][None, :],
                           w_bh, w_ba, w_o, w_g, w_u, w_d)
```

```python
import functools

import jax
import jax.numpy as jnp
import numpy as np
from jax import lax
from jax.experimental import pallas as pl
from jax.experimental.pallas import tpu as pltpu

F32 = jnp.float32
BF16 = jnp.bfloat16

D_MODEL = 1024
GRID_W = 64
EPS = 1e-6
HG_HEADS = 4
HG_DIM = 128
HG_WIDTH = HG_HEADS * HG_DIM
CHUNK = 32
ATT_HEADS = 8
ATT_KV_HEADS = 2
HEAD_DIM = 64
ATT_WIDTH = ATT_HEADS * HEAD_DIM
KV_WIDTH = ATT_KV_HEADS * HEAD_DIM
WINDOW = 128
ROPE_THETA = 10000.0
D_FF = 2816
CTX_COLS = 3 * HG_WIDTH + 2 * KV_WIDTH
IN_COLS = CTX_COLS + 2 * HG_WIDTH + ATT_WIDTH + 2 * D_MODEL

C_FF, C_FB, C_INP, C_K, C_V = 0, HG_WIDTH, 2 * HG_WIDTH, 3 * HG_WIDTH, 3 * HG_WIDTH + KV_WIDTH
C_QHG = CTX_COLS
C_GHG = C_QHG + HG_WIDTH
C_Q = C_GHG + HG_WIDTH
C_GATES = C_Q + ATT_WIDTH

LANES = 128
TOK_TILE = 512
FFN_SUBTILES = 2
MXU_DEPTH = 256
FFN_CHUNK_BOUNDS = (0, 6 * MXU_DEPTH, D_FF)
INPROJ_TILE = 256
CTX_TILE = 512
INPROJ_DEPTH = 2
HG_TILE = 256
HG_HEADS_PER_STEP = 4
ATT_BLOCK = 128
ATT_GROUP_HEADS = 2
ATT_DEPTH = 1
ATT_STEP_BLOCKS = 16
ADA_ROWS = 16
ADA_STEPS = 6
VMEM_LIMIT = 56 * 1024 * 1024
NEG = -1e30
LOG2E = 1.4426950408889634

assert ATT_BLOCK == WINDOW == LANES and HG_DIM == LANES and 2 * HEAD_DIM == LANES
assert ATT_GROUP_HEADS % 2 == 0 and (ATT_HEADS // ATT_KV_HEADS) % ATT_GROUP_HEADS == 0


def _dot(a, b):
    return jnp.dot(a, b, preferred_element_type=F32)


def _dot_nt(a, b):
    return lax.dot_general(a, b, (((1,), (1,)), ((), ())), preferred_element_type=F32)


def _sigmoid(x):
    return 0.5 * jnp.tanh(0.5 * x) + 0.5


def _emit_pipelined(stages, depth):
    raws = [issue() for issue, _ in stages[:depth]]
    for i, (_, finish) in enumerate(stages):
        if i + depth < len(stages):
            raws.append(stages[i + depth][0]())
        finish(raws[i])
        raws[i] = None


def _const_spec(shape):
    n = len(shape)
    return pl.BlockSpec(shape, lambda *_: (0,) * n, pipeline_mode=pl.Buffered(1))


def _ada_kernel(c_ref, cctx_ref, w_ref, b_ref, win_ref, col_ref, o_ref, winb_ref):
    rows = o_ref.shape[0]
    n_cond = c_ref.shape[0] + 1
    c = jnp.concatenate([c_ref[...], cctx_ref[...], jnp.zeros((rows - n_cond, D_MODEL), F32)], axis=0)
    s = (c * _sigmoid(c)).astype(BF16)
    mod = _dot(s, w_ref[...].astype(BF16)) + b_ref[...]
    for r in range(mod.shape[0]):
        o_ref[r] = mod[r:r + 1, :]
    winb_ref[...] = (win_ref[...] * col_ref[...]).astype(BF16)


def _ada_call(c, c_ctx, w_ada, b_ada, w_in, col):
    rows = ADA_ROWS
    n_out = w_ada.shape[1]
    steps = ADA_STEPS
    bn = n_out // steps
    bw = w_in.shape[1] // steps
    return pl.pallas_call(
        _ada_kernel,
        grid=(steps,),
        in_specs=[pl.BlockSpec(c.shape, lambda j: (0, 0)),
                  pl.BlockSpec((1, D_MODEL), lambda j: (0, 0)),
                  pl.BlockSpec((D_MODEL, bn), lambda j: (0, j)),
                  pl.BlockSpec((1, bn), lambda j: (0, j)),
                  pl.BlockSpec((D_MODEL, bw), lambda j: (0, j)),
                  pl.BlockSpec((1, bw), lambda j: (0, j))],
        out_specs=[pl.BlockSpec((rows, 1, bn), lambda j: (0, 0, j)),
                   pl.BlockSpec((D_MODEL, bw), lambda j: (0, j))],
        out_shape=[jax.ShapeDtypeStruct((rows, 1, n_out), F32),
                   jax.ShapeDtypeStruct(w_in.shape, BF16)],
        compiler_params=pltpu.CompilerParams(dimension_semantics=("arbitrary",),
                                             vmem_limit_bytes=VMEM_LIMIT),
        name="ada_mod",
    )(c, c_ctx, w_ada, b_ada, w_in, col)


def _chunk_cumsum(x, reverse):
    n, c = x.shape
    sub = 8
    r = lax.broadcasted_iota(jnp.int32, x.shape, 0) & (CHUNK - 1)
    s = 1
    while s < sub:
        if reverse:
            x = x + jnp.where(r < CHUNK - s, pltpu.roll(x, n - s, 0), 0.0)
        else:
            x = x + jnp.where(r >= s, pltpu.roll(x, s, 0), 0.0)
        s *= 2
    per = CHUNK // sub
    x4 = x.reshape(n // CHUNK, per, sub, c)
    s = 1
    while s < per:
        if reverse:
            x4 = jnp.concatenate([x4[:, :per - s] + x4[:, s:], x4[:, per - s:]], axis=1)
        else:
            x4 = jnp.concatenate([x4[:, :s], x4[:, s:] + x4[:, :per - s]], axis=1)
        s *= 2
    return x4.reshape(n, c)


def _rope(t, cos, sin_signed, lane):
    partner = jnp.where((lane & 31) < 16, pltpu.roll(t, LANES - 16, 1), pltpu.roll(t, 16, 1))
    return t * cos + partner * sin_signed


def _inproj_kernel(x_ref, mod_ref, nw_ref, w_ref, lbl_ref, qw_ref, kw_ref, cos_ref, sin_ref,
                   *out_refs, latent):
    if latent:
        hg_ref, dec_ref, vt_ref, v_ref, g_ref, q_ref, katt_ref, vatt_ref, gates_ref = out_refs
    else:
        kd2f_ref, kd2b_ref, dec_ref, vt_ref, katt_ref, vatt_ref = out_refs

    tm = x_ref.shape[1]
    nchunk = tm // CHUNK
    x = x_ref[0]
    sh = mod_ref[0, :, 0:D_MODEL]
    sc = mod_ref[0, :, D_MODEL:2 * D_MODEL]
    ms = jnp.mean(x * x, axis=-1, keepdims=True)
    h = x * lax.rsqrt(ms + EPS) * (nw_ref[...] * (1.0 + sc)) + sh
    hb = h.astype(BF16)

    def mm(a, b):
        return _dot(hb, w_ref[:, a:b])

    l0f, l1f = lbl_ref[0, 0:1, :], lbl_ref[0, 1:2, :]
    l0b, l1b = lbl_ref[1, 0:1, :], lbl_ref[1, 1:2, :]
    lb_f = 1.0 / (1.0 + jnp.exp(l1f - l0f))
    lb_b = 1.0 / (1.0 + jnp.exp(l1b - l0b))

    qw = jnp.concatenate([qw_ref[...]] * (LANES // HEAD_DIM), axis=1)
    kw = jnp.concatenate([kw_ref[...]] * (LANES // HEAD_DIM), axis=1)
    gw = HG_WIDTH // 2
    sw = D_MODEL // 2
    lane = lax.broadcasted_iota(jnp.int32, (tm, LANES), 1)
    left = lane < HEAD_DIM
    bi = lax.broadcasted_iota(jnp.int32, (2 * LANES, LANES), 0) // HEAD_DIM
    bj = lax.broadcasted_iota(jnp.int32, (2 * LANES, LANES), 1) // HEAD_DIM
    ones_blk2 = jnp.where((bi % 2) == bj, 1.0, 0.0).astype(BF16)
    keep = {}

    def qhg_finish(raw):
        keep["q_hg"] = raw * (jnp.tanh(raw) + 1.0) * (HG_DIM ** -0.5)

    def gate_finish(raw, direction, part):
        reverse = direction == 1
        lo = part * gw
        lb = (lb_f, lb_b)[direction][:, lo:lo + gw]
        c_half = 0.5 - 0.5 * lb
        ct = c_half * jnp.tanh(raw)
        f = (0.5 + 0.5 * lb) + ct
        k = c_half - ct
        cum = _chunk_cumsum(jnp.log(f), reverse)
        cum3 = cum.reshape(nchunk, CHUNK, gw)
        tot3 = cum3[:, 0:1, :] if reverse else cum3[:, CHUNK - 1:CHUNK, :]
        dec3 = jnp.exp(tot3)
        dec_ref[0, :, direction * HG_WIDTH + lo:direction * HG_WIDTH + lo + gw] = dec3.reshape(nchunk, gw)
        kd2 = (k.reshape(nchunk, CHUNK, gw) * jnp.exp(tot3 - cum3)).reshape(tm, gw).astype(BF16)
        if latent:
            base = direction * 3 * HG_WIDTH + lo
            hg_ref[0, :, base:base + gw] = (keep["q_hg"][:, lo:lo + gw] * jnp.exp(cum)).astype(BF16)
            hg_ref[0, :, base + HG_WIDTH:base + HG_WIDTH + gw] = (k * jnp.exp(-cum)).astype(BF16)
            hg_ref[0, :, base + 2 * HG_WIDTH:base + 2 * HG_WIDTH + gw] = kd2
        else:
            (kd2f_ref, kd2b_ref)[direction][0, :, lo:lo + gw] = kd2

    def mgate_finish(raw, j):
        gates_ref[0, :, j * sw:(j + 1) * sw] = (jnp.tanh(raw) + 1.0).astype(BF16)

    def v_finish(raw):
        for s in range(tm // LANES):
            vt_ref[0, s] = raw[s * LANES:(s + 1) * LANES, :].T.astype(BF16)
        if latent:
            v_ref[0] = raw.astype(BF16)

    def g_finish(raw):
        g_ref[0] = (raw * (jnp.tanh(raw) + 1.0)).astype(BF16)

    def split_sq(t):
        sq = t * t
        hi = sq.astype(BF16)
        lo = (sq - hi.astype(F32)).astype(BF16)
        return jnp.concatenate([hi, lo], axis=1)

    def kv_finish(raw):
        keep["k_raw"] = raw[:, 0:LANES]
        keep["k_split"] = split_sq(keep["k_raw"])
        vatt_ref[0] = raw[:, LANES:2 * LANES].astype(BF16)

    def knorm_finish(ss):
        kn = keep["k_raw"] * lax.rsqrt(ss * (1.0 / HEAD_DIM) + EPS) * kw
        if latent:
            kn = _rope(kn, cos_ref[...], sin_ref[...], lane)
        for s in range(tm // LANES):
            katt_ref[0, s] = kn[s * LANES:(s + 1) * LANES, :].T.astype(BF16)

    def q_finish(raw):
        keep["q_raw"] = raw
        keep["q_split"] = [split_sq(raw[:, s * LANES:(s + 1) * LANES]) for s in range(ATT_WIDTH // LANES)]

    def qnorm_finish(ss_list):
        for s, ss in enumerate(ss_list):
            qraw = keep["q_raw"][:, s * LANES:(s + 1) * LANES]
            qn = qraw * lax.rsqrt(ss * (1.0 / HEAD_DIM) + EPS) * qw
            qn = _rope(qn, cos_ref[...], sin_ref[...], lane) * (HEAD_DIM ** -0.5 * LOG2E)
            swapped = pltpu.roll(qn, HEAD_DIM, 1)
            if s < ATT_WIDTH // LANES // 2:
                even, odd = jnp.where(left, qn, 0.0), jnp.where(left, swapped, 0.0)
            else:
                even, odd = jnp.where(left, 0.0, swapped), jnp.where(left, 0.0, qn)
            q_ref[0, :, (2 * s) * LANES:(2 * s + 1) * LANES] = even.astype(BF16)
            q_ref[0, :, (2 * s + 1) * LANES:(2 * s + 2) * LANES] = odd.astype(BF16)

    def gate_stage(direction, part):
        col = (C_FF, C_FB)[direction] + part * gw
        return (lambda: mm(col, col + gw),
                functools.partial(gate_finish, direction=direction, part=part))

    def mgate_stage(j):
        c0 = C_GATES + j * sw
        return (lambda: mm(c0, c0 + sw), functools.partial(mgate_finish, j=j))

    v_stage = (lambda: mm(C_INP, C_INP + HG_WIDTH), v_finish)
    kv_stage = (lambda: mm(C_K, C_K + 2 * KV_WIDTH), kv_finish)
    knorm_stage = (lambda: _dot(keep["k_split"], ones_blk2), knorm_finish)
    if latent:
        stages = [(lambda: mm(C_QHG, C_QHG + HG_WIDTH), qhg_finish)]
        for j in range(4):
            stages += [gate_stage(j // 2, j % 2), mgate_stage(j)]
        stages += [(lambda: mm(C_Q, C_Q + ATT_WIDTH), q_finish),
                   v_stage,
                   kv_stage,
                   (lambda: [_dot(t, ones_blk2) for t in keep["q_split"]], qnorm_finish),
                   (lambda: mm(C_GHG, C_GHG + HG_WIDTH), g_finish),
                   knorm_stage]
    else:
        stages = [gate_stage(0, 0), gate_stage(0, 1), kv_stage, gate_stage(1, 0), gate_stage(1, 1),
                  knorm_stage, v_stage]
    _emit_pipelined(stages, INPROJ_DEPTH)


def _inproj_call(x, mod3, mod_row0, nw, w_in, lbl, qw, kw, cos_t, sin_t, latent):
    b_, s_, _ = x.shape
    tm = min(INPROJ_TILE if latent else CTX_TILE, s_)
    grid = (b_, s_ // tm)
    tok = lambda w: pl.BlockSpec((1, tm, w), lambda b, i: (b, i, 0))
    mod_map = (lambda b, i: (mod_row0 + b, 0, 0)) if latent else (lambda b, i: (mod_row0, 0, 0))
    ncols = IN_COLS if latent else CTX_COLS
    in_specs = [tok(D_MODEL),
                pl.BlockSpec((1, 1, mod3.shape[2]), mod_map),
                _const_spec((1, D_MODEL)),
                pl.BlockSpec((D_MODEL, ncols), lambda b, i: (0, 0), pipeline_mode=pl.Buffered(1)),
                _const_spec(lbl.shape),
                _const_spec((1, HEAD_DIM)),
                _const_spec((1, HEAD_DIM)),
                pl.BlockSpec((tm, LANES), lambda b, i: (i, 0)),
                pl.BlockSpec((tm, LANES), lambda b, i: (i, 0))]
    bf = lambda w: jax.ShapeDtypeStruct((b_, s_, w), BF16)
    dec_shape = jax.ShapeDtypeStruct((b_, s_ // CHUNK, 2 * HG_WIDTH), F32)
    dec_spec = pl.BlockSpec((1, tm // CHUNK, 2 * HG_WIDTH), lambda b, i: (b, i, 0))
    vt_shape = jax.ShapeDtypeStruct((b_, s_ // LANES, HG_WIDTH, LANES), BF16)
    vt_spec = pl.BlockSpec((1, tm // LANES, HG_WIDTH, LANES), lambda b, i: (b, i, 0, 0))
    katt_shape = jax.ShapeDtypeStruct((b_, s_ // LANES, KV_WIDTH, LANES), BF16)
    katt_spec = pl.BlockSpec((1, tm // LANES, KV_WIDTH, LANES), lambda b, i: (b, i, 0, 0))
    if latent:
        out_shape = [bf(6 * HG_WIDTH), dec_shape, vt_shape, bf(HG_WIDTH), bf(HG_WIDTH),
                                         bf(ATT_HEADS * LANES), katt_shape, bf(KV_WIDTH), bf(2 * D_MODEL)]
        out_specs = [tok(6 * HG_WIDTH), dec_spec, vt_spec, tok(HG_WIDTH), tok(HG_WIDTH),
                                          tok(ATT_HEADS * LANES), katt_spec, tok(KV_WIDTH), tok(2 * D_MODEL)]
    else:
        out_shape = [bf(HG_WIDTH)] * 2 + [dec_shape, vt_shape, katt_shape, bf(KV_WIDTH)]
        out_specs = [tok(HG_WIDTH)] * 2 + [dec_spec, vt_spec, katt_spec, tok(KV_WIDTH)]
    return pl.pallas_call(
        functools.partial(_inproj_kernel, latent=latent),
        grid=grid, in_specs=in_specs, out_specs=out_specs, out_shape=out_shape,
        compiler_params=pltpu.CompilerParams(dimension_semantics=("parallel", "parallel"),
                                             vmem_limit_bytes=VMEM_LIMIT),
        name="inproj_latent" if latent else "inproj_ctx",
    )(x, mod3, nw, w_in, lbl, qw, kw, cos_t, sin_t)


def _cast_side_job(w_refs, wb_refs):
    for w_ref, wb_ref in zip(w_refs, wb_refs):
        wb_ref[...] = w_ref[...].astype(BF16)


def _side_job_specs(weights, n_steps, step_of):
    specs = [pl.BlockSpec((wt.shape[0] // n_steps, wt.shape[1]), lambda *ids: (step_of(*ids), 0))
             for wt in weights]
    shapes = [jax.ShapeDtypeStruct(wt.shape, BF16) for wt in weights]
    return specs, shapes


def _hgrn_kernel(qdf_ref, kdf_ref, kd2f_ref, qdb_ref, kdb_ref, kd2b_ref, v_ref, vt_ref,
                 decf_ref, decb_ref, ckd2f_ref, ckd2b_ref, cvt_ref, cdecf_ref, cdecb_ref,
                 g_ref, nw_ref, y_ref, o_acc):
    seq = v_ref.shape[1]
    ctx_len = ckd2f_ref.shape[1]
    heads = v_ref.shape[2] // HG_DIM
    tile = HG_TILE
    cpt = tile // CHUNK
    cps = LANES // CHUNK
    spt = tile // LANES
    n_tiles = seq // tile
    half = n_tiles // 2

    row_chunk = lax.broadcasted_iota(jnp.int32, (LANES, HG_DIM), 0) // CHUNK
    ti = lax.broadcasted_iota(jnp.int32, (LANES, LANES), 0)
    tj = lax.broadcasted_iota(jnp.int32, (LANES, LANES), 1)
    same = (ti // CHUNK) == (tj // CHUNK)
    mask_f = same & (tj <= ti)
    mask_b = same & (tj >= ti)

    def chunk_update(vt_slab, k_slab, n_in_slab):
        k_m = jnp.where(row_chunk == n_in_slab, k_slab, jnp.zeros_like(k_slab))
        return _dot(vt_slab, k_m)

    def ctx_state(ckd2_ref, cdec_ref, cols, reverse):
        st = jnp.zeros((HG_DIM, HG_DIM), F32)
        order = range(ctx_len // CHUNK)
        for n in (reversed(order) if reverse else order):
            slab = n // cps
            st = cdec_ref[0, n:n + 1, cols] * st + chunk_update(
                cvt_ref[0, slab, cols, :], ckd2_ref[0, slab * LANES:(slab + 1) * LANES, cols], n % cps)
        return st

    def emit(o_t, cols, rows, finalize):
        if finalize:
            o_t = o_t + o_acc[rows, cols]
            ms = jnp.mean(o_t * o_t, axis=-1, keepdims=True)
            y = o_t * lax.rsqrt(ms + EPS) * nw_ref[...] * g_ref[0, rows, cols].astype(F32)
            y_ref[0, rows, cols] = y.astype(BF16)
        else:
            o_acc[rows, cols] = o_t

    def body(it, states, finalize):
        chains = []
        for h in range(heads):
            cols = slice(h * HG_DIM, (h + 1) * HG_DIM)
            chains.append((qdf_ref, kdf_ref, decf_ref, cols, it, False, kd2f_ref))
            chains.append((qdb_ref, kdb_ref, decb_ref, cols, n_tiles - 1 - it, True, kd2b_ref))
        n_ch = len(chains)
        rows = [pl.ds(pl.multiple_of(c[4] * tile, tile), tile) for c in chains]
        orders = [list(reversed(range(cpt))) if c[5] else list(range(cpt)) for c in chains]
        qd, sc, ups = [], [], []
        for (qd_ref, kd_ref, _, cols, t, reverse, kd2_ref), r in zip(chains, rows):
            qd.append(qd_ref[0, r, cols])
            kd = kd_ref[0, r, cols]
            sc.append([_dot_nt(qd[-1][s * LANES:(s + 1) * LANES, :], kd[s * LANES:(s + 1) * LANES, :])
                       for s in range(spt)])
            kd2 = kd2_ref[0, r, cols]
            ups.append([chunk_update(vt_ref[0, t * spt + n // cps, cols, :],
                                     kd2[(n // cps) * LANES:(n // cps + 1) * LANES, :], n % cps)
                        for n in range(cpt)])
        o = []
        for c, r, s in zip(chains, rows, sc):
            v = v_ref[0, r, c[3]]
            o.append(jnp.concatenate(
                [_dot(jnp.where(mask_b if c[5] else mask_f, s[j], 0.0).astype(BF16),
                      v[j * LANES:(j + 1) * LANES, :]) for j in range(spt)], axis=0))
        starts, new_states = [], []
        for i, c in enumerate(chains):
            dec = c[2][0, pl.ds(pl.multiple_of(c[4] * cpt, cpt), cpt), c[3]]
            st = states[i]
            start = [None] * cpt
            for n in orders[i]:
                start[n] = st.T.astype(BF16)
                st = dec[n:n + 1, :] * st + ups[i][n]
            starts.append(start)
            new_states.append(st)
        for i, c in enumerate(chains):
            outs = [o[i][n * CHUNK:(n + 1) * CHUNK, :]
                    + _dot(qd[i][n * CHUNK:(n + 1) * CHUNK, :], starts[i][n]) for n in range(cpt)]
            emit(jnp.concatenate(outs, axis=0), c[3], rows[i], finalize)
        return tuple(new_states)

    states = []
    for h in range(heads):
        cols = slice(h * HG_DIM, (h + 1) * HG_DIM)
        states += [ctx_state(ckd2f_ref, cdecf_ref, cols, False), ctx_state(ckd2b_ref, cdecb_ref, cols, True)]
    states = lax.fori_loop(0, half, functools.partial(body, finalize=False), tuple(states))
    lax.fori_loop(half, n_tiles, functools.partial(body, finalize=True), states)


def _hgrn_call(hg, v, vt, dec, ckd2f, ckd2b, cvt, cdec, g, nw):
    b_, s_, _ = v.shape
    l_ = ckd2f.shape[1]
    w = HG_HEADS_PER_STEP * HG_DIM
    groups = HG_HEADS // HG_HEADS_PER_STEP
    seq = lambda n: pl.BlockSpec((1, n, w), lambda b, h: (b, 0, h))
    vt_spec = lambda n: pl.BlockSpec((1, n // LANES, w, LANES), lambda b, h: (b, 0, h, 0))
    dec_f = lambda n: pl.BlockSpec((1, n // CHUNK, w), lambda b, h: (b, 0, h))
    dec_b = lambda n: pl.BlockSpec((1, n // CHUNK, w), lambda b, h: (b, 0, groups + h))
    return pl.pallas_call(
        _hgrn_kernel,
        grid=(b_, groups),
        in_specs=[pl.BlockSpec((1, s_, w), lambda b, h, j=j: (b, 0, j * groups + h)) for j in range(6)]
        + [seq(s_), vt_spec(s_), dec_f(s_), dec_b(s_),
           seq(l_), seq(l_), vt_spec(l_), dec_f(l_), dec_b(l_),
           seq(s_), _const_spec((1, HG_DIM))],
        out_specs=seq(s_),
        out_shape=jax.ShapeDtypeStruct((b_, s_, HG_WIDTH), BF16),
        scratch_shapes=[pltpu.VMEM((s_, w), F32)],
        compiler_params=pltpu.CompilerParams(dimension_semantics=("parallel", "parallel"),
                                             vmem_limit_bytes=VMEM_LIMIT),
        name="hgrn_scan",
    )(hg, hg, hg, hg, hg, hg, v, vt, dec, dec, ckd2f, ckd2b, cvt, cdec, cdec, g, nw)


def _attn_kernel(sink_ref, q_ref, kt_ref, kc_ref, v_ref, vc_ref, *rest):
    n_w = (len(rest) - 1) // 2
    y_ref = rest[n_w]
    _cast_side_job(rest[:n_w], rest[n_w + 1:])
    nb = kt_ref.shape[1]
    blk = ATT_BLOCK
    hpg = ATT_GROUP_HEADS
    n_groups = ATT_HEADS // hpg
    qry_r = lax.broadcasted_iota(jnp.int32, (blk, blk), 0)
    key_c = lax.broadcasted_iota(jnp.int32, (blk, blk), 1)
    left = lax.broadcasted_iota(jnp.int32, (blk, LANES), 1) < HEAD_DIM
    ctx_keys = [kc_ref[0, c] for c in range(kc_ref.shape[1])]
    keep = {}

    def band(u):
        qb = pl.program_id(1) * ATT_STEP_BLOCKS + u
        ids = (jnp.maximum(qb - 1, 0), qb, jnp.minimum(qb + 1, nb - 1))
        keys_t = jnp.concatenate([kt_ref[0, j] for j in ids] + ctx_keys, axis=1)
        values = jnp.concatenate([v_ref[0, pl.ds(pl.multiple_of(j * blk, blk), blk), :] for j in ids]
                                 + [vc_ref[0]], axis=0)
        values = jnp.concatenate([values, jnp.ones_like(values)], axis=1)
        return qb, keys_t, values

    def score_issue(u, g):
        if g == 0:
            keep["band", u] = band(u)
        q = jnp.concatenate([q_ref[0, u * blk:(u + 1) * blk, h * LANES:(h + 1) * LANES]
                             for h in range(g * hpg, (g + 1) * hpg)], axis=0)
        return _dot(q, keep["band", u][1])

    def score_finish(raw, u, g):
        qb = keep["band", u][0]
        mask_prev = (key_c >= qry_r) & (qb > 0)
        mask_next = (key_c <= qry_r) & (qb < nb - 1)
        probs, sink_p = [], []
        for n in range(hpg):
            s = raw[n * blk:(n + 1) * blk, :]
            sink = sink_ref[g * hpg + n] * LOG2E
            s = jnp.concatenate([jnp.where(mask_prev, s[:, 0:blk], NEG), s[:, blk:2 * blk],
                                 jnp.where(mask_next, s[:, 2 * blk:3 * blk], NEG), s[:, 3 * blk:]], axis=1)
            mx = jnp.maximum(jnp.max(s, axis=1, keepdims=True), sink)
            probs.append(jnp.exp2(s - mx).astype(BF16))
            sink_p.append(jnp.exp2(sink - mx))
        keep["p", u, g] = jnp.concatenate(probs, axis=0)
        keep["sink_p", u, g] = sink_p

    def value_issue(u, g):
        return _dot(keep["p", u, g], keep["band", u][2])

    def value_finish(raw, u, g):
        for n in range(0, hpg, 2):
            h = g * hpg + n
            even, odd = [raw[m * blk:(m + 1) * blk, 0:LANES]
                         / (raw[m * blk:(m + 1) * blk, LANES:2 * LANES] + keep["sink_p", u, g][m])
                         for m in (n, n + 1)]
            if h < ATT_HEADS // ATT_KV_HEADS:
                y = jnp.where(left, even, pltpu.roll(odd, HEAD_DIM, 1))
            else:
                y = jnp.where(left, pltpu.roll(even, HEAD_DIM, 1), odd)
            y_ref[0, u * blk:(u + 1) * blk, (h // 2) * LANES:(h // 2 + 1) * LANES] = y.astype(BF16)

    part = functools.partial
    work = [(u, g) for u in range(ATT_STEP_BLOCKS) for g in range(n_groups)]
    score = lambda u, g: (part(score_issue, u, g), part(score_finish, u=u, g=g))
    stages = [score(*w) for w in work[:ATT_DEPTH + 1]]
    for n, (u, g) in enumerate(work):
        stages.append((part(value_issue, u, g), part(value_finish, u=u, g=g)))
        if n + ATT_DEPTH + 1 < len(work):
            stages.append(score(*work[n + ATT_DEPTH + 1]))
    _emit_pipelined(stages, ATT_DEPTH)


def _attn_call(sinks, q, katt, vatt, ckatt, cvatt, weights):
    b_, s_, qw = q.shape
    l_ = cvatt.shape[1]
    rows = ATT_STEP_BLOCKS * ATT_BLOCK
    batch4 = lambda n: pl.BlockSpec((1, n, KV_WIDTH, LANES), lambda b, i: (b, 0, 0, 0))
    batch3 = lambda n: pl.BlockSpec((1, n, KV_WIDTH), lambda b, i: (b, 0, 0))
    steps = s_ // rows
    w_specs, w_shapes = _side_job_specs(weights, b_ * steps, lambda b, i: b * steps + i)
    outs = pl.pallas_call(
        _attn_kernel,
        grid=(b_, steps),
        in_specs=[pl.BlockSpec(memory_space=pltpu.SMEM),
                  pl.BlockSpec((1, rows, qw), lambda b, i: (b, i, 0)),
                  batch4(s_ // LANES), batch4(l_ // LANES), batch3(s_), batch3(l_)] + w_specs,
        out_specs=[pl.BlockSpec((1, rows, ATT_WIDTH), lambda b, i: (b, i, 0))] + w_specs,
        out_shape=[jax.ShapeDtypeStruct((b_, s_, ATT_WIDTH), BF16)] + w_shapes,
        compiler_params=pltpu.CompilerParams(dimension_semantics=("parallel", "arbitrary"),
                                             vmem_limit_bytes=VMEM_LIMIT),
        name="window_attn",
    )(sinks, q, katt, ckatt, vatt, cvatt, *weights)
    return outs[0], outs[1:]


def _merge_ffn_kernel(x_ref, yh_ref, ya_ref, gates_ref, mod_ref, nw_ref, wbh_ref, wba_ref, wo_ref,
                      wg_ref, wu_ref, wd_ref, o_ref):
    tm = x_ref.shape[1]
    n_sub = FFN_SUBTILES
    sub = tm // n_sub
    bounds = FFN_CHUNK_BOUNDS
    n_chunks = len(bounds) - 1
    g1 = mod_ref[0, :, 2 * D_MODEL:3 * D_MODEL]
    sh2 = mod_ref[0, :, 3 * D_MODEL:4 * D_MODEL]
    sc2 = mod_ref[0, :, 4 * D_MODEL:5 * D_MODEL]
    g2 = mod_ref[0, :, 5 * D_MODEL:6 * D_MODEL]
    keep = {}

    def branch_issue(r):
        rows = slice(r * sub, (r + 1) * sub)
        return _dot(yh_ref[0, rows, :], wbh_ref[...]), _dot(ya_ref[0, rows, :], wba_ref[...])

    def branch_finish(raw, r):
        rows = slice(r * sub, (r + 1) * sub)
        a, b = raw
        mixed = (gates_ref[0, rows, 0:D_MODEL].astype(F32) * a
                 + gates_ref[0, rows, D_MODEL:2 * D_MODEL].astype(F32) * b)
        keep["mixed", r] = (0.5 * mixed).astype(BF16)

    def out_issue(r):
        return _dot(keep["mixed", r], wo_ref[...])

    def out_finish(raw, r):
        x1 = x_ref[0, r * sub:(r + 1) * sub, :] + g1 * raw
        ms = jnp.mean(x1 * x1, axis=-1, keepdims=True)
        keep["x1", r] = x1
        keep["h2", r] = ((x1 * lax.rsqrt(ms + EPS) * nw_ref[...]) * (1.0 + sc2) + sh2).astype(BF16)

    def hidden_issue(r, c):
        h2 = keep["h2", r]
        cols = slice(bounds[c], bounds[c + 1])
        return _dot(h2, wg_ref[:, cols]), _dot(h2, wu_ref[:, cols])

    def hidden_finish(raw, r, c):
        gate, up = raw
        keep["act", r, c] = (gate * _sigmoid(gate) * up).astype(BF16)

    def down_issue(r, c):
        return _dot(keep["act", r, c], wd_ref[bounds[c]:bounds[c + 1], :])

    def down_finish(raw, r, c):
        dn = raw if c == 0 else keep["dn", r] + raw
        if c + 1 < n_chunks:
            keep["dn", r] = dn
        else:
            o_ref[0, r * sub:(r + 1) * sub, :] = keep["x1", r] + g2 * dn

    part = functools.partial
    stages = []
    for issue, finish in ((branch_issue, branch_finish), (out_issue, out_finish)):
        stages += [(part(issue, r), part(finish, r=r)) for r in range(n_sub)]
    for c in range(n_chunks):
        stages += [(part(hidden_issue, r, c), part(hidden_finish, r=r, c=c)) for r in range(n_sub)]
        stages += [(part(down_issue, r, c), part(down_finish, r=r, c=c)) for r in range(n_sub)]
    _emit_pipelined(stages, n_sub - 1)


def _merge_ffn_call(x, yh, ya, gates, mod3, nw, wbh, wba, wo, wg, wu, wd):
    b_, s_, _ = x.shape
    tm = TOK_TILE
    tok = lambda w: pl.BlockSpec((1, tm, w), lambda b, i: (b, i, 0))
    return pl.pallas_call(
        _merge_ffn_kernel,
        grid=(b_, s_ // tm),
        in_specs=[tok(D_MODEL), tok(HG_WIDTH), tok(ATT_WIDTH), tok(2 * D_MODEL),
                  pl.BlockSpec((1, 1, mod3.shape[2]), lambda b, i: (b, 0, 0)),
                  _const_spec((1, D_MODEL)),
                  _const_spec(wbh.shape), _const_spec(wba.shape), _const_spec(wo.shape),
                  _const_spec(wg.shape), _const_spec(wu.shape), _const_spec(wd.shape)],
        out_specs=tok(D_MODEL),
        out_shape=jax.ShapeDtypeStruct(x.shape, F32),
        compiler_params=pltpu.CompilerParams(dimension_semantics=("parallel", "parallel"),
                                             vmem_limit_bytes=VMEM_LIMIT),
        name="merge_ffn",
    )(x, yh, ya, gates, mod3, nw, wbh, wba, wo, wg, wu, wd)


def _rope_tables(n_tok):
    t = np.arange(n_tok)
    rows = (t // GRID_W).astype(np.float64)
    cols = (t % GRID_W).astype(np.float64)
    half = HEAD_DIM // 2
    inv_freq = ROPE_THETA ** (-np.arange(0, half, 2, dtype=np.float64) / half)
    d = np.arange(LANES) % HEAD_DIM
    pos = np.where((d < half)[None, :], rows[:, None], cols[:, None])
    ang = pos * inv_freq[(d % half) % (half // 2)][None, :]
    sign = np.where((d % half) < half // 2, -1.0, 1.0)[None, :]
    return jnp.asarray(np.cos(ang), F32), jnp.asarray(np.sin(ang) * sign, F32)


def kernel(x, c, ctx, c_ctx, w_ada, b_ada, norm_mix_w, norm_ffn_w, w_in, hgrn_lb_logits, hgrn_norm_w,
           q_norm_w, k_norm_w, attn_sinks, w_branch_hgrn, w_branch_attn, w_out, w_ffn_gate, w_ffn_up,
           w_ffn_down):
    b_, s_, _ = x.shape
    layer = 0
    col = np.ones((1, IN_COLS), np.float32)
    for lo, width in ((C_FF, 2 * HG_WIDTH), (C_QHG, 2 * HG_WIDTH), (C_GATES, 2 * D_MODEL)):
        col[:, lo:lo + width] = 0.5
    mod, w_in_b = _ada_call(c, c_ctx[None, :], w_ada[layer], b_ada[layer][None, :], w_in[layer],
                            jnp.asarray(col))

    lbl = hgrn_lb_logits[:, 0:2, :]
    qw = q_norm_w[layer][None, :]
    kw = k_norm_w[layer][None, :]
    nw_mix = norm_mix_w[layer][None, :]
    cos_t, sin_t = _rope_tables(s_)

    (hg, dec, vt, v, g, q, katt, vatt, gates) = _inproj_call(
        x, mod, 0, nw_mix, w_in_b, lbl, qw, kw, cos_t, sin_t, latent=True)
    l_ = ctx.shape[1]
    pack = max(1, min(CTX_TILE // l_, b_))
    ctx_out = _inproj_call(ctx.reshape(b_ // pack, pack * l_, D_MODEL), mod, b_, nw_mix, w_in_b, lbl, qw, kw,
                           cos_t, sin_t, latent=False)
    ckd2f, ckd2b, cdec, cvt, ckatt, cvatt = [t.reshape((b_, t.shape[1] // pack) + t.shape[2:])
                                             for t in ctx_out]

    y_hg = _hgrn_call(hg, v, vt, dec, ckd2f, ckd2b, cvt, cdec, g, hgrn_norm_w[layer][None, :])
    y_at, (w_bh, w_ba, w_o, w_g, w_u, w_d) = _attn_call(
        attn_sinks[layer], q, katt, vatt, ckatt, cvatt,
        (w_branch_hgrn[layer], w_branch_attn[layer], w_out[layer], w_ffn_gate[layer], w_ffn_up[layer],
         w_ffn_down[layer]))

    return _merge_ffn_call(x, y_hg, y_at, gates, mod, norm_ffn_w[layer][None, :],
                           w_bh, w_ba, w_o, w_g, w_u, w_d)
```

```python
import functools

import jax
import jax.numpy as jnp
import numpy as np
from jax import lax
from jax.experimental import pallas as pl
from jax.experimental.pallas import tpu as pltpu

F32 = jnp.float32
BF16 = jnp.bfloat16

D_MODEL = 1024
GRID_W = 64
EPS = 1e-6
HG_HEADS = 4
HG_DIM = 128
HG_WIDTH = HG_HEADS * HG_DIM
CHUNK = 32
ATT_HEADS = 8
ATT_KV_HEADS = 2
HEAD_DIM = 64
ATT_WIDTH = ATT_HEADS * HEAD_DIM
KV_WIDTH = ATT_KV_HEADS * HEAD_DIM
WINDOW = 128
ROPE_THETA = 10000.0
D_FF = 2816
CTX_COLS = 3 * HG_WIDTH + 2 * KV_WIDTH
IN_COLS = CTX_COLS + 2 * HG_WIDTH + ATT_WIDTH + 2 * D_MODEL

C_FF, C_FB, C_INP, C_K, C_V = 0, HG_WIDTH, 2 * HG_WIDTH, 3 * HG_WIDTH, 3 * HG_WIDTH + KV_WIDTH
C_QHG = CTX_COLS
C_GHG = C_QHG + HG_WIDTH
C_Q = C_GHG + HG_WIDTH
C_GATES = C_Q + ATT_WIDTH

LANES = 128
TOK_TILE = 512
FFN_SUBTILES = 2
MXU_DEPTH = 256
FFN_CHUNK_BOUNDS = (0, 6 * MXU_DEPTH, D_FF)
INPROJ_TILE = 256
CTX_TILE = 512
INPROJ_DEPTH = 2
HG_TILE = 256
HG_HEADS_PER_STEP = 4
ATT_BLOCK = 128
ATT_GROUP_HEADS = 2
ATT_DEPTH = 1
ATT_STEP_BLOCKS = 8
ADA_ROWS = 16
ADA_STEPS = 6
VMEM_LIMIT = 56 * 1024 * 1024
NEG = -1e30
LOG2E = 1.4426950408889634

assert ATT_BLOCK == WINDOW == LANES and HG_DIM == LANES and 2 * HEAD_DIM == LANES
assert ATT_GROUP_HEADS % 2 == 0 and (ATT_HEADS // ATT_KV_HEADS) % ATT_GROUP_HEADS == 0


def _dot(a, b):
    return jnp.dot(a, b, preferred_element_type=F32)


def _dot_nt(a, b):
    return lax.dot_general(a, b, (((1,), (1,)), ((), ())), preferred_element_type=F32)


def _sigmoid(x):
    return 0.5 * jnp.tanh(0.5 * x) + 0.5


def _emit_pipelined(stages, depth):
    raws = [issue() for issue, _ in stages[:depth]]
    for i, (_, finish) in enumerate(stages):
        if i + depth < len(stages):
            raws.append(stages[i + depth][0]())
        finish(raws[i])
        raws[i] = None


def _const_spec(shape):
    n = len(shape)
    return pl.BlockSpec(shape, lambda *_: (0,) * n, pipeline_mode=pl.Buffered(1))


def _ada_kernel(c_ref, cctx_ref, w_ref, b_ref, win_ref, col_ref, o_ref, winb_ref):
    rows = o_ref.shape[0]
    n_cond = c_ref.shape[0] + 1
    c = jnp.concatenate([c_ref[...], cctx_ref[...], jnp.zeros((rows - n_cond, D_MODEL), F32)], axis=0)
    s = (c * _sigmoid(c)).astype(BF16)
    mod = _dot(s, w_ref[...].astype(BF16)) + b_ref[...]
    for r in range(mod.shape[0]):
        o_ref[r] = mod[r:r + 1, :]
    winb_ref[...] = (win_ref[...] * col_ref[...]).astype(BF16)


def _ada_call(c, c_ctx, w_ada, b_ada, w_in, col):
    rows = ADA_ROWS
    n_out = w_ada.shape[1]
    steps = ADA_STEPS
    bn = n_out // steps
    bw = w_in.shape[1] // steps
    return pl.pallas_call(
        _ada_kernel,
        grid=(steps,),
        in_specs=[pl.BlockSpec(c.shape, lambda j: (0, 0)),
                  pl.BlockSpec((1, D_MODEL), lambda j: (0, 0)),
                  pl.BlockSpec((D_MODEL, bn), lambda j: (0, j)),
                  pl.BlockSpec((1, bn), lambda j: (0, j)),
                  pl.BlockSpec((D_MODEL, bw), lambda j: (0, j)),
                  pl.BlockSpec((1, bw), lambda j: (0, j))],
        out_specs=[pl.BlockSpec((rows, 1, bn), lambda j: (0, 0, j)),
                   pl.BlockSpec((D_MODEL, bw), lambda j: (0, j))],
        out_shape=[jax.ShapeDtypeStruct((rows, 1, n_out), F32),
                   jax.ShapeDtypeStruct(w_in.shape, BF16)],
        compiler_params=pltpu.CompilerParams(dimension_semantics=("arbitrary",),
                                             vmem_limit_bytes=VMEM_LIMIT),
        name="ada_mod",
    )(c, c_ctx, w_ada, b_ada, w_in, col)


def _chunk_cumsum(x, reverse):
    n, c = x.shape
    sub = 8
    r = lax.broadcasted_iota(jnp.int32, x.shape, 0) & (CHUNK - 1)
    s = 1
    while s < sub:
        if reverse:
            x = x + jnp.where(r < CHUNK - s, pltpu.roll(x, n - s, 0), 0.0)
        else:
            x = x + jnp.where(r >= s, pltpu.roll(x, s, 0), 0.0)
        s *= 2
    per = CHUNK // sub
    x4 = x.reshape(n // CHUNK, per, sub, c)
    s = 1
    while s < per:
        if reverse:
            x4 = jnp.concatenate([x4[:, :per - s] + x4[:, s:], x4[:, per - s:]], axis=1)
        else:
            x4 = jnp.concatenate([x4[:, :s], x4[:, s:] + x4[:, :per - s]], axis=1)
        s *= 2
    return x4.reshape(n, c)


def _rope(t, cos, sin_signed, lane):
    partner = jnp.where((lane & 31) < 16, pltpu.roll(t, LANES - 16, 1), pltpu.roll(t, 16, 1))
    return t * cos + partner * sin_signed


def _inproj_kernel(x_ref, mod_ref, nw_ref, w_ref, lbl_ref, qw_ref, kw_ref, cos_ref, sin_ref,
                   *out_refs, latent):
    if latent:
        hg_ref, dec_ref, vt_ref, v_ref, g_ref, q_ref, katt_ref, vatt_ref, gates_ref = out_refs
    else:
        kd2f_ref, kd2b_ref, dec_ref, vt_ref, katt_ref, vatt_ref = out_refs

    tm = x_ref.shape[1]
    nchunk = tm // CHUNK
    x = x_ref[0]
    sh = mod_ref[0, :, 0:D_MODEL]
    sc = mod_ref[0, :, D_MODEL:2 * D_MODEL]
    ms = jnp.mean(x * x, axis=-1, keepdims=True)
    h = x * lax.rsqrt(ms + EPS) * (nw_ref[...] * (1.0 + sc)) + sh
    hb = h.astype(BF16)

    def mm(a, b):
        return _dot(hb, w_ref[:, a:b])

    l0f, l1f = lbl_ref[0, 0:1, :], lbl_ref[0, 1:2, :]
    l0b, l1b = lbl_ref[1, 0:1, :], lbl_ref[1, 1:2, :]
    lb_f = 1.0 / (1.0 + jnp.exp(l1f - l0f))
    lb_b = 1.0 / (1.0 + jnp.exp(l1b - l0b))

    qw = jnp.concatenate([qw_ref[...]] * (LANES // HEAD_DIM), axis=1)
    kw = jnp.concatenate([kw_ref[...]] * (LANES // HEAD_DIM), axis=1)
    gw = HG_WIDTH // 2
    sw = D_MODEL // 2
    lane = lax.broadcasted_iota(jnp.int32, (tm, LANES), 1)
    left = lane < HEAD_DIM
    bi = lax.broadcasted_iota(jnp.int32, (2 * LANES, LANES), 0) // HEAD_DIM
    bj = lax.broadcasted_iota(jnp.int32, (2 * LANES, LANES), 1) // HEAD_DIM
    ones_blk2 = jnp.where((bi % 2) == bj, 1.0, 0.0).astype(BF16)
    keep = {}

    def qhg_finish(raw):
        keep["q_hg"] = raw * (jnp.tanh(raw) + 1.0) * (HG_DIM ** -0.5)

    def gate_finish(raw, direction, part):
        reverse = direction == 1
        lo = part * gw
        lb = (lb_f, lb_b)[direction][:, lo:lo + gw]
        c_half = 0.5 - 0.5 * lb
        ct = c_half * jnp.tanh(raw)
        f = (0.5 + 0.5 * lb) + ct
        k = c_half - ct
        cum = _chunk_cumsum(jnp.log(f), reverse)
        cum3 = cum.reshape(nchunk, CHUNK, gw)
        tot3 = cum3[:, 0:1, :] if reverse else cum3[:, CHUNK - 1:CHUNK, :]
        dec3 = jnp.exp(tot3)
        dec_ref[0, :, direction * HG_WIDTH + lo:direction * HG_WIDTH + lo + gw] = dec3.reshape(nchunk, gw)
        kd2 = (k.reshape(nchunk, CHUNK, gw) * jnp.exp(tot3 - cum3)).reshape(tm, gw).astype(BF16)
        if latent:
            base = direction * 3 * HG_WIDTH + lo
            hg_ref[0, :, base:base + gw] = (keep["q_hg"][:, lo:lo + gw] * jnp.exp(cum)).astype(BF16)
            hg_ref[0, :, base + HG_WIDTH:base + HG_WIDTH + gw] = (k * jnp.exp(-cum)).astype(BF16)
            hg_ref[0, :, base + 2 * HG_WIDTH:base + 2 * HG_WIDTH + gw] = kd2
        else:
            (kd2f_ref, kd2b_ref)[direction][0, :, lo:lo + gw] = kd2

    def mgate_finish(raw, j):
        gates_ref[0, :, j * sw:(j + 1) * sw] = (jnp.tanh(raw) + 1.0).astype(BF16)

    def v_finish(raw):
        for s in range(tm // LANES):
            vt_ref[0, s] = raw[s * LANES:(s + 1) * LANES, :].T.astype(BF16)
        if latent:
            v_ref[0] = raw.astype(BF16)

    def g_finish(raw):
        g_ref[0] = (raw * (jnp.tanh(raw) + 1.0)).astype(BF16)

    def split_sq(t):
        sq = t * t
        hi = sq.astype(BF16)
        lo = (sq - hi.astype(F32)).astype(BF16)
        return jnp.concatenate([hi, lo], axis=1)

    def kv_finish(raw):
        keep["k_raw"] = raw[:, 0:LANES]
        keep["k_split"] = split_sq(keep["k_raw"])
        vatt_ref[0] = raw[:, LANES:2 * LANES].astype(BF16)

    def knorm_finish(ss):
        kn = keep["k_raw"] * lax.rsqrt(ss * (1.0 / HEAD_DIM) + EPS) * kw
        if latent:
            kn = _rope(kn, cos_ref[...], sin_ref[...], lane)
        for s in range(tm // LANES):
            katt_ref[0, s] = kn[s * LANES:(s + 1) * LANES, :].T.astype(BF16)

    def q_finish(raw):
        keep["q_raw"] = raw
        keep["q_split"] = [split_sq(raw[:, s * LANES:(s + 1) * LANES]) for s in range(ATT_WIDTH // LANES)]

    def qnorm_finish(ss_list):
        for s, ss in enumerate(ss_list):
            qraw = keep["q_raw"][:, s * LANES:(s + 1) * LANES]
            qn = qraw * lax.rsqrt(ss * (1.0 / HEAD_DIM) + EPS) * qw
            qn = _rope(qn, cos_ref[...], sin_ref[...], lane) * (HEAD_DIM ** -0.5 * LOG2E)
            swapped = pltpu.roll(qn, HEAD_DIM, 1)
            if s < ATT_WIDTH // LANES // 2:
                even, odd = jnp.where(left, qn, 0.0), jnp.where(left, swapped, 0.0)
            else:
                even, odd = jnp.where(left, 0.0, swapped), jnp.where(left, 0.0, qn)
            q_ref[0, :, (2 * s) * LANES:(2 * s + 1) * LANES] = even.astype(BF16)
            q_ref[0, :, (2 * s + 1) * LANES:(2 * s + 2) * LANES] = odd.astype(BF16)

    def gate_stage(direction, part):
        col = (C_FF, C_FB)[direction] + part * gw
        return (lambda: mm(col, col + gw),
                functools.partial(gate_finish, direction=direction, part=part))

    def mgate_stage(j):
        c0 = C_GATES + j * sw
        return (lambda: mm(c0, c0 + sw), functools.partial(mgate_finish, j=j))

    v_stage = (lambda: mm(C_INP, C_INP + HG_WIDTH), v_finish)
    kv_stage = (lambda: mm(C_K, C_K + 2 * KV_WIDTH), kv_finish)
    knorm_stage = (lambda: _dot(keep["k_split"], ones_blk2), knorm_finish)
    if latent:
        stages = [(lambda: mm(C_QHG, C_QHG + HG_WIDTH), qhg_finish)]
        for j in range(4):
            stages += [gate_stage(j // 2, j % 2), mgate_stage(j)]
        stages += [(lambda: mm(C_Q, C_Q + ATT_WIDTH), q_finish),
                   v_stage,
                   kv_stage,
                   (lambda: [_dot(t, ones_blk2) for t in keep["q_split"]], qnorm_finish),
                   (lambda: mm(C_GHG, C_GHG + HG_WIDTH), g_finish),
                   knorm_stage]
    else:
        stages = [gate_stage(0, 0), gate_stage(0, 1), kv_stage, gate_stage(1, 0), gate_stage(1, 1),
                  knorm_stage, v_stage]
    _emit_pipelined(stages, INPROJ_DEPTH)


def _inproj_call(x, mod3, mod_row0, nw, w_in, lbl, qw, kw, cos_t, sin_t, latent):
    b_, s_, _ = x.shape
    tm = min(INPROJ_TILE if latent else CTX_TILE, s_)
    grid = (b_, s_ // tm)
    tok = lambda w: pl.BlockSpec((1, tm, w), lambda b, i: (b, i, 0))
    mod_map = (lambda b, i: (mod_row0 + b, 0, 0)) if latent else (lambda b, i: (mod_row0, 0, 0))
    ncols = IN_COLS if latent else CTX_COLS
    in_specs = [tok(D_MODEL),
                pl.BlockSpec((1, 1, mod3.shape[2]), mod_map),
                _const_spec((1, D_MODEL)),
                pl.BlockSpec((D_MODEL, ncols), lambda b, i: (0, 0), pipeline_mode=pl.Buffered(1)),
                _const_spec(lbl.shape),
                _const_spec((1, HEAD_DIM)),
                _const_spec((1, HEAD_DIM)),
                pl.BlockSpec((tm, LANES), lambda b, i: (i, 0)),
                pl.BlockSpec((tm, LANES), lambda b, i: (i, 0))]
    bf = lambda w: jax.ShapeDtypeStruct((b_, s_, w), BF16)
    dec_shape = jax.ShapeDtypeStruct((b_, s_ // CHUNK, 2 * HG_WIDTH), F32)
    dec_spec = pl.BlockSpec((1, tm // CHUNK, 2 * HG_WIDTH), lambda b, i: (b, i, 0))
    vt_shape = jax.ShapeDtypeStruct((b_, s_ // LANES, HG_WIDTH, LANES), BF16)
    vt_spec = pl.BlockSpec((1, tm // LANES, HG_WIDTH, LANES), lambda b, i: (b, i, 0, 0))
    katt_shape = jax.ShapeDtypeStruct((b_, s_ // LANES, KV_WIDTH, LANES), BF16)
    katt_spec = pl.BlockSpec((1, tm // LANES, KV_WIDTH, LANES), lambda b, i: (b, i, 0, 0))
    if latent:
        out_shape = [bf(6 * HG_WIDTH), dec_shape, vt_shape, bf(HG_WIDTH), bf(HG_WIDTH),
                                         bf(ATT_HEADS * LANES), katt_shape, bf(KV_WIDTH), bf(2 * D_MODEL)]
        out_specs = [tok(6 * HG_WIDTH), dec_spec, vt_spec, tok(HG_WIDTH), tok(HG_WIDTH),
                                          tok(ATT_HEADS * LANES), katt_spec, tok(KV_WIDTH), tok(2 * D_MODEL)]
    else:
        out_shape = [bf(HG_WIDTH)] * 2 + [dec_shape, vt_shape, katt_shape, bf(KV_WIDTH)]
        out_specs = [tok(HG_WIDTH)] * 2 + [dec_spec, vt_spec, katt_spec, tok(KV_WIDTH)]
    return pl.pallas_call(
        functools.partial(_inproj_kernel, latent=latent),
        grid=grid, in_specs=in_specs, out_specs=out_specs, out_shape=out_shape,
        compiler_params=pltpu.CompilerParams(dimension_semantics=("parallel", "parallel"),
                                             vmem_limit_bytes=VMEM_LIMIT),
        name="inproj_latent" if latent else "inproj_ctx",
    )(x, mod3, nw, w_in, lbl, qw, kw, cos_t, sin_t)


def _cast_side_job(w_refs, wb_refs):
    for w_ref, wb_ref in zip(w_refs, wb_refs):
        wb_ref[...] = w_ref[...].astype(BF16)


def _side_job_specs(weights, n_steps, step_of):
    specs = [pl.BlockSpec((wt.shape[0] // n_steps, wt.shape[1]), lambda *ids: (step_of(*ids), 0))
             for wt in weights]
    shapes = [jax.ShapeDtypeStruct(wt.shape, BF16) for wt in weights]
    return specs, shapes


def _hgrn_kernel(qdf_ref, kdf_ref, kd2f_ref, qdb_ref, kdb_ref, kd2b_ref, v_ref, vt_ref,
                 decf_ref, decb_ref, ckd2f_ref, ckd2b_ref, cvt_ref, cdecf_ref, cdecb_ref,
                 g_ref, nw_ref, y_ref, o_acc):
    seq = v_ref.shape[1]
    ctx_len = ckd2f_ref.shape[1]
    heads = v_ref.shape[2] // HG_DIM
    tile = HG_TILE
    cpt = tile // CHUNK
    cps = LANES // CHUNK
    spt = tile // LANES
    n_tiles = seq // tile
    half = n_tiles // 2

    row_chunk = lax.broadcasted_iota(jnp.int32, (LANES, HG_DIM), 0) // CHUNK
    ti = lax.broadcasted_iota(jnp.int32, (LANES, LANES), 0)
    tj = lax.broadcasted_iota(jnp.int32, (LANES, LANES), 1)
    same = (ti // CHUNK) == (tj // CHUNK)
    mask_f = same & (tj <= ti)
    mask_b = same & (tj >= ti)

    def chunk_update(vt_slab, k_slab, n_in_slab):
        k_m = jnp.where(row_chunk == n_in_slab, k_slab, jnp.zeros_like(k_slab))
        return _dot(vt_slab, k_m)

    def ctx_state(ckd2_ref, cdec_ref, cols, reverse):
        st = jnp.zeros((HG_DIM, HG_DIM), F32)
        order = range(ctx_len // CHUNK)
        for n in (reversed(order) if reverse else order):
            slab = n // cps
            st = cdec_ref[0, n:n + 1, cols] * st + chunk_update(
                cvt_ref[0, slab, cols, :], ckd2_ref[0, slab * LANES:(slab + 1) * LANES, cols], n % cps)
        return st

    def emit(o_t, cols, rows, finalize):
        if finalize:
            o_t = o_t + o_acc[rows, cols]
            ms = jnp.mean(o_t * o_t, axis=-1, keepdims=True)
            y = o_t * lax.rsqrt(ms + EPS) * nw_ref[...] * g_ref[0, rows, cols].astype(F32)
            y_ref[0, rows, cols] = y.astype(BF16)
        else:
            o_acc[rows, cols] = o_t

    def body(it, states, finalize):
        chains = []
        for h in range(heads):
            cols = slice(h * HG_DIM, (h + 1) * HG_DIM)
            chains.append((qdf_ref, kdf_ref, decf_ref, cols, it, False, kd2f_ref))
            chains.append((qdb_ref, kdb_ref, decb_ref, cols, n_tiles - 1 - it, True, kd2b_ref))
        n_ch = len(chains)
        rows = [pl.ds(pl.multiple_of(c[4] * tile, tile), tile) for c in chains]
        orders = [list(reversed(range(cpt))) if c[5] else list(range(cpt)) for c in chains]
        qd, sc, ups = [], [], []
        for (qd_ref, kd_ref, _, cols, t, reverse, kd2_ref), r in zip(chains, rows):
            qd.append(qd_ref[0, r, cols])
            kd = kd_ref[0, r, cols]
            sc.append([_dot_nt(qd[-1][s * LANES:(s + 1) * LANES, :], kd[s * LANES:(s + 1) * LANES, :])
                       for s in range(spt)])
            kd2 = kd2_ref[0, r, cols]
            ups.append([chunk_update(vt_ref[0, t * spt + n // cps, cols, :],
                                     kd2[(n // cps) * LANES:(n // cps + 1) * LANES, :], n % cps)
                        for n in range(cpt)])
        o = []
        for c, r, s in zip(chains, rows, sc):
            v = v_ref[0, r, c[3]]
            o.append(jnp.concatenate(
                [_dot(jnp.where(mask_b if c[5] else mask_f, s[j], 0.0).astype(BF16),
                      v[j * LANES:(j + 1) * LANES, :]) for j in range(spt)], axis=0))
        starts, new_states = [], []
        for i, c in enumerate(chains):
            dec = c[2][0, pl.ds(pl.multiple_of(c[4] * cpt, cpt), cpt), c[3]]
            st = states[i]
            start = [None] * cpt
            for n in orders[i]:
                start[n] = st.T.astype(BF16)
                st = dec[n:n + 1, :] * st + ups[i][n]
            starts.append(start)
            new_states.append(st)
        for i, c in enumerate(chains):
            outs = [o[i][n * CHUNK:(n + 1) * CHUNK, :]
                    + _dot(qd[i][n * CHUNK:(n + 1) * CHUNK, :], starts[i][n]) for n in range(cpt)]
            emit(jnp.concatenate(outs, axis=0), c[3], rows[i], finalize)
        return tuple(new_states)

    states = []
    for h in range(heads):
        cols = slice(h * HG_DIM, (h + 1) * HG_DIM)
        states += [ctx_state(ckd2f_ref, cdecf_ref, cols, False), ctx_state(ckd2b_ref, cdecb_ref, cols, True)]
    states = lax.fori_loop(0, half, functools.partial(body, finalize=False), tuple(states))
    lax.fori_loop(half, n_tiles, functools.partial(body, finalize=True), states)


def _hgrn_call(hg, v, vt, dec, ckd2f, ckd2b, cvt, cdec, g, nw):
    b_, s_, _ = v.shape
    l_ = ckd2f.shape[1]
    w = HG_HEADS_PER_STEP * HG_DIM
    groups = HG_HEADS // HG_HEADS_PER_STEP
    seq = lambda n: pl.BlockSpec((1, n, w), lambda b, h: (b, 0, h))
    vt_spec = lambda n: pl.BlockSpec((1, n // LANES, w, LANES), lambda b, h: (b, 0, h, 0))
    dec_f = lambda n: pl.BlockSpec((1, n // CHUNK, w), lambda b, h: (b, 0, h))
    dec_b = lambda n: pl.BlockSpec((1, n // CHUNK, w), lambda b, h: (b, 0, groups + h))
    return pl.pallas_call(
        _hgrn_kernel,
        grid=(b_, groups),
        in_specs=[pl.BlockSpec((1, s_, w), lambda b, h, j=j: (b, 0, j * groups + h)) for j in range(6)]
        + [seq(s_), vt_spec(s_), dec_f(s_), dec_b(s_),
           seq(l_), seq(l_), vt_spec(l_), dec_f(l_), dec_b(l_),
           seq(s_), _const_spec((1, HG_DIM))],
        out_specs=seq(s_),
        out_shape=jax.ShapeDtypeStruct((b_, s_, HG_WIDTH), BF16),
        scratch_shapes=[pltpu.VMEM((s_, w), F32)],
        compiler_params=pltpu.CompilerParams(dimension_semantics=("parallel", "parallel"),
                                             vmem_limit_bytes=VMEM_LIMIT),
        name="hgrn_scan",
    )(hg, hg, hg, hg, hg, hg, v, vt, dec, dec, ckd2f, ckd2b, cvt, cdec, cdec, g, nw)


def _attn_kernel(sink_ref, q_ref, kt_ref, kc_ref, v_ref, vc_ref, *rest):
    n_w = (len(rest) - 1) // 2
    y_ref = rest[n_w]
    _cast_side_job(rest[:n_w], rest[n_w + 1:])
    nb = kt_ref.shape[1]
    blk = ATT_BLOCK
    hpg = ATT_GROUP_HEADS
    n_groups = ATT_HEADS // hpg
    qry_r = lax.broadcasted_iota(jnp.int32, (blk, blk), 0)
    key_c = lax.broadcasted_iota(jnp.int32, (blk, blk), 1)
    left = lax.broadcasted_iota(jnp.int32, (blk, LANES), 1) < HEAD_DIM
    ctx_keys = [kc_ref[0, c] for c in range(kc_ref.shape[1])]
    keep = {}

    def band(u):
        qb = pl.program_id(1) * ATT_STEP_BLOCKS + u
        ids = (jnp.maximum(qb - 1, 0), qb, jnp.minimum(qb + 1, nb - 1))
        keys_t = jnp.concatenate([kt_ref[0, j] for j in ids] + ctx_keys, axis=1)
        values = jnp.concatenate([v_ref[0, pl.ds(pl.multiple_of(j * blk, blk), blk), :] for j in ids]
                                 + [vc_ref[0]], axis=0)
        values = jnp.concatenate([values, jnp.ones_like(values)], axis=1)
        return qb, keys_t, values

    def score_issue(u, g):
        if g == 0:
            keep["band", u] = band(u)
        q = jnp.concatenate([q_ref[0, u * blk:(u + 1) * blk, h * LANES:(h + 1) * LANES]
                             for h in range(g * hpg, (g + 1) * hpg)], axis=0)
        return _dot(q, keep["band", u][1])

    def score_finish(raw, u, g):
        qb = keep["band", u][0]
        mask_prev = (key_c >= qry_r) & (qb > 0)
        mask_next = (key_c <= qry_r) & (qb < nb - 1)
        probs, sink_p = [], []
        for n in range(hpg):
            s = raw[n * blk:(n + 1) * blk, :]
            sink = sink_ref[g * hpg + n] * LOG2E
            s = jnp.concatenate([jnp.where(mask_prev, s[:, 0:blk], NEG), s[:, blk:2 * blk],
                                 jnp.where(mask_next, s[:, 2 * blk:3 * blk], NEG), s[:, 3 * blk:]], axis=1)
            mx = jnp.maximum(jnp.max(s, axis=1, keepdims=True), sink)
            probs.append(jnp.exp2(s - mx).astype(BF16))
            sink_p.append(jnp.exp2(sink - mx))
        keep["p", u, g] = jnp.concatenate(probs, axis=0)
        keep["sink_p", u, g] = sink_p

    def value_issue(u, g):
        return _dot(keep["p", u, g], keep["band", u][2])

    def value_finish(raw, u, g):
        for n in range(0, hpg, 2):
            h = g * hpg + n
            even, odd = [raw[m * blk:(m + 1) * blk, 0:LANES]
                         / (raw[m * blk:(m + 1) * blk, LANES:2 * LANES] + keep["sink_p", u, g][m])
                         for m in (n, n + 1)]
            if h < ATT_HEADS // ATT_KV_HEADS:
                y = jnp.where(left, even, pltpu.roll(odd, HEAD_DIM, 1))
            else:
                y = jnp.where(left, pltpu.roll(even, HEAD_DIM, 1), odd)
            y_ref[0, u * blk:(u + 1) * blk, (h // 2) * LANES:(h // 2 + 1) * LANES] = y.astype(BF16)

    part = functools.partial
    work = [(u, g) for u in range(ATT_STEP_BLOCKS) for g in range(n_groups)]
    score = lambda u, g: (part(score_issue, u, g), part(score_finish, u=u, g=g))
    stages = [score(*w) for w in work[:ATT_DEPTH + 1]]
    for n, (u, g) in enumerate(work):
        stages.append((part(value_issue, u, g), part(value_finish, u=u, g=g)))
        if n + ATT_DEPTH + 1 < len(work):
            stages.append(score(*work[n + ATT_DEPTH + 1]))
    _emit_pipelined(stages, ATT_DEPTH)


def _attn_call(sinks, q, katt, vatt, ckatt, cvatt, weights):
    b_, s_, qw = q.shape
    l_ = cvatt.shape[1]
    rows = ATT_STEP_BLOCKS * ATT_BLOCK
    batch4 = lambda n: pl.BlockSpec((1, n, KV_WIDTH, LANES), lambda b, i: (b, 0, 0, 0))
    batch3 = lambda n: pl.BlockSpec((1, n, KV_WIDTH), lambda b, i: (b, 0, 0))
    steps = s_ // rows
    w_specs, w_shapes = _side_job_specs(weights, b_ * steps, lambda b, i: b * steps + i)
    outs = pl.pallas_call(
        _attn_kernel,
        grid=(b_, steps),
        in_specs=[pl.BlockSpec(memory_space=pltpu.SMEM),
                  pl.BlockSpec((1, rows, qw), lambda b, i: (b, i, 0)),
                  batch4(s_ // LANES), batch4(l_ // LANES), batch3(s_), batch3(l_)] + w_specs,
        out_specs=[pl.BlockSpec((1, rows, ATT_WIDTH), lambda b, i: (b, i, 0))] + w_specs,
        out_shape=[jax.ShapeDtypeStruct((b_, s_, ATT_WIDTH), BF16)] + w_shapes,
        compiler_params=pltpu.CompilerParams(dimension_semantics=("parallel", "arbitrary"),
                                             vmem_limit_bytes=VMEM_LIMIT),
        name="window_attn",
    )(sinks, q, katt, ckatt, vatt, cvatt, *weights)
    return outs[0], outs[1:]


def _merge_ffn_kernel(x_ref, yh_ref, ya_ref, gates_ref, mod_ref, nw_ref, wbh_ref, wba_ref, wo_ref,
                      wg_ref, wu_ref, wd_ref, o_ref):
    tm = x_ref.shape[1]
    n_sub = FFN_SUBTILES
    sub = tm // n_sub
    bounds = FFN_CHUNK_BOUNDS
    n_chunks = len(bounds) - 1
    g1 = mod_ref[0, :, 2 * D_MODEL:3 * D_MODEL]
    sh2 = mod_ref[0, :, 3 * D_MODEL:4 * D_MODEL]
    sc2 = mod_ref[0, :, 4 * D_MODEL:5 * D_MODEL]
    g2 = mod_ref[0, :, 5 * D_MODEL:6 * D_MODEL]
    keep = {}

    def branch_issue(r):
        rows = slice(r * sub, (r + 1) * sub)
        return _dot(yh_ref[0, rows, :], wbh_ref[...]), _dot(ya_ref[0, rows, :], wba_ref[...])

    def branch_finish(raw, r):
        rows = slice(r * sub, (r + 1) * sub)
        a, b = raw
        mixed = (gates_ref[0, rows, 0:D_MODEL].astype(F32) * a
                 + gates_ref[0, rows, D_MODEL:2 * D_MODEL].astype(F32) * b)
        keep["mixed", r] = (0.5 * mixed).astype(BF16)

    def out_issue(r):
        return _dot(keep["mixed", r], wo_ref[...])

    def out_finish(raw, r):
        x1 = x_ref[0, r * sub:(r + 1) * sub, :] + g1 * raw
        ms = jnp.mean(x1 * x1, axis=-1, keepdims=True)
        keep["x1", r] = x1
        keep["h2", r] = ((x1 * lax.rsqrt(ms + EPS) * nw_ref[...]) * (1.0 + sc2) + sh2).astype(BF16)

    def hidden_issue(r, c):
        h2 = keep["h2", r]
        cols = slice(bounds[c], bounds[c + 1])
        return _dot(h2, wg_ref[:, cols]), _dot(h2, wu_ref[:, cols])

    def hidden_finish(raw, r, c):
        gate, up = raw
        keep["act", r, c] = (gate * _sigmoid(gate) * up).astype(BF16)

    def down_issue(r, c):
        return _dot(keep["act", r, c], wd_ref[bounds[c]:bounds[c + 1], :])

    def down_finish(raw, r, c):
        dn = raw if c == 0 else keep["dn", r] + raw
        if c + 1 < n_chunks:
            keep["dn", r] = dn
        else:
            o_ref[0, r * sub:(r + 1) * sub, :] = keep["x1", r] + g2 * dn

    part = functools.partial
    stages = []
    for issue, finish in ((branch_issue, branch_finish), (out_issue, out_finish)):
        stages += [(part(issue, r), part(finish, r=r)) for r in range(n_sub)]
    for c in range(n_chunks):
        stages += [(part(hidden_issue, r, c), part(hidden_finish, r=r, c=c)) for r in range(n_sub)]
        stages += [(part(down_issue, r, c), part(down_finish, r=r, c=c)) for r in range(n_sub)]
    _emit_pipelined(stages, n_sub - 1)


def _merge_ffn_call(x, yh, ya, gates, mod3, nw, wbh, wba, wo, wg, wu, wd):
    b_, s_, _ = x.shape
    tm = TOK_TILE
    tok = lambda w: pl.BlockSpec((1, tm, w), lambda b, i: (b, i, 0))
    return pl.pallas_call(
        _merge_ffn_kernel,
        grid=(b_, s_ // tm),
        in_specs=[tok(D_MODEL), tok(HG_WIDTH), tok(ATT_WIDTH), tok(2 * D_MODEL),
                  pl.BlockSpec((1, 1, mod3.shape[2]), lambda b, i: (b, 0, 0)),
                  _const_spec((1, D_MODEL)),
                  _const_spec(wbh.shape), _const_spec(wba.shape), _const_spec(wo.shape),
                  _const_spec(wg.shape), _const_spec(wu.shape), _const_spec(wd.shape)],
        out_specs=tok(D_MODEL),
        out_shape=jax.ShapeDtypeStruct(x.shape, F32),
        compiler_params=pltpu.CompilerParams(dimension_semantics=("parallel", "parallel"),
                                             vmem_limit_bytes=VMEM_LIMIT),
        name="merge_ffn",
    )(x, yh, ya, gates, mod3, nw, wbh, wba, wo, wg, wu, wd)


def _rope_tables(n_tok):
    t = np.arange(n_tok)
    rows = (t // GRID_W).astype(np.float64)
    cols = (t % GRID_W).astype(np.float64)
    half = HEAD_DIM // 2
    inv_freq = ROPE_THETA ** (-np.arange(0, half, 2, dtype=np.float64) / half)
    d = np.arange(LANES) % HEAD_DIM
    pos = np.where((d < half)[None, :], rows[:, None], cols[:, None])
    ang = pos * inv_freq[(d % half) % (half // 2)][None, :]
    sign = np.where((d % half) < half // 2, -1.0, 1.0)[None, :]
    return jnp.asarray(np.cos(ang), F32), jnp.asarray(np.sin(ang) * sign, F32)


def kernel(x, c, ctx, c_ctx, w_ada, b_ada, norm_mix_w, norm_ffn_w, w_in, hgrn_lb_logits, hgrn_norm_w,
           q_norm_w, k_norm_w, attn_sinks, w_branch_hgrn, w_branch_attn, w_out, w_ffn_gate, w_ffn_up,
           w_ffn_down):
    b_, s_, _ = x.shape
    layer = 0
    col = np.ones((1, IN_COLS), np.float32)
    for lo, width in ((C_FF, 2 * HG_WIDTH), (C_QHG, 2 * HG_WIDTH), (C_GATES, 2 * D_MODEL)):
        col[:, lo:lo + width] = 0.5
    mod, w_in_b = _ada_call(c, c_ctx[None, :], w_ada[layer], b_ada[layer][None, :], w_in[layer],
                            jnp.asarray(col))

    lbl = hgrn_lb_logits[:, 0:2, :]
    qw = q_norm_w[layer][None, :]
    kw = k_norm_w[layer][None, :]
    nw_mix = norm_mix_w[layer][None, :]
    cos_t, sin_t = _rope_tables(s_)

    (hg, dec, vt, v, g, q, katt, vatt, gates) = _inproj_call(
        x, mod, 0, nw_mix, w_in_b, lbl, qw, kw, cos_t, sin_t, latent=True)
    l_ = ctx.shape[1]
    pack = max(1, min(CTX_TILE // l_, b_))
    ctx_out = _inproj_call(ctx.reshape(b_ // pack, pack * l_, D_MODEL), mod, b_, nw_mix, w_in_b, lbl, qw, kw,
                           cos_t, sin_t, latent=False)
    ckd2f, ckd2b, cdec, cvt, ckatt, cvatt = [t.reshape((b_, t.shape[1] // pack) + t.shape[2:])
                                             for t in ctx_out]

    y_hg = _hgrn_call(hg, v, vt, dec, ckd2f, ckd2b, cvt, cdec, g, hgrn_norm_w[layer][None, :])
    y_at, (w_bh, w_ba, w_o, w_g, w_u, w_d) = _attn_call(
        attn_sinks[layer], q, katt, vatt, ckatt, cvatt,
        (w_branch_hgrn[layer], w_branch_attn[layer], w_out[layer], w_ffn_gate[layer], w_ffn_up[layer],
         w_ffn_down[layer]))

    return _merge_ffn_call(x, y_hg, y_at, gates, mod, norm_ffn_w[layer][None, :],
                           w_bh, w_ba, w_o, w_g, w_u, w_d)
```

```python
import functools

import jax
import jax.numpy as jnp
import numpy as np
from jax import lax
from jax.experimental import pallas as pl
from jax.experimental.pallas import tpu as pltpu

F32 = jnp.float32
BF16 = jnp.bfloat16

D_MODEL = 1024
GRID_W = 64
EPS = 1e-6
HG_HEADS = 4
HG_DIM = 128
HG_WIDTH = HG_HEADS * HG_DIM
CHUNK = 32
ATT_HEADS = 8
ATT_KV_HEADS = 2
HEAD_DIM = 64
ATT_WIDTH = ATT_HEADS * HEAD_DIM
KV_WIDTH = ATT_KV_HEADS * HEAD_DIM
WINDOW = 128
ROPE_THETA = 10000.0
D_FF = 2816
CTX_COLS = 3 * HG_WIDTH + 2 * KV_WIDTH
IN_COLS = CTX_COLS + 2 * HG_WIDTH + ATT_WIDTH + 2 * D_MODEL

C_FF, C_FB, C_INP, C_K, C_V = 0, HG_WIDTH, 2 * HG_WIDTH, 3 * HG_WIDTH, 3 * HG_WIDTH + KV_WIDTH
C_QHG = CTX_COLS
C_GHG = C_QHG + HG_WIDTH
C_Q = C_GHG + HG_WIDTH
C_GATES = C_Q + ATT_WIDTH

LANES = 128
TOK_TILE = 512
FFN_SUBTILES = 2
MXU_DEPTH = 256
FFN_CHUNK_BOUNDS = (0, 6 * MXU_DEPTH, D_FF)
INPROJ_TILE = 256
CTX_TILE = 512
INPROJ_DEPTH = 2
HG_TILE = 256
HG_HEADS_PER_STEP = 4
ATT_BLOCK = 128
ATT_GROUP_HEADS = 2
ATT_DEPTH = 1
ATT_STEP_BLOCKS = 8
ADA_ROWS = 16
ADA_STEPS = 6
VMEM_LIMIT = 56 * 1024 * 1024
NEG = -1e30
LOG2E = 1.4426950408889634

assert ATT_BLOCK == WINDOW == LANES and HG_DIM == LANES and 2 * HEAD_DIM == LANES
assert ATT_GROUP_HEADS % 2 == 0 and (ATT_HEADS // ATT_KV_HEADS) % ATT_GROUP_HEADS == 0


def _dot(a, b):
    return jnp.dot(a, b, preferred_element_type=F32)


def _dot_nt(a, b):
    return lax.dot_general(a, b, (((1,), (1,)), ((), ())), preferred_element_type=F32)


def _sigmoid(x):
    return 0.5 * jnp.tanh(0.5 * x) + 0.5


def _emit_pipelined(stages, depth):
    raws = [issue() for issue, _ in stages[:depth]]
    for i, (_, finish) in enumerate(stages):
        if i + depth < len(stages):
            raws.append(stages[i + depth][0]())
        finish(raws[i])
        raws[i] = None


def _const_spec(shape):
    n = len(shape)
    return pl.BlockSpec(shape, lambda *_: (0,) * n, pipeline_mode=pl.Buffered(1))


def _ada_kernel(c_ref, cctx_ref, w_ref, b_ref, win_ref, col_ref, o_ref, winb_ref):
    rows = o_ref.shape[0]
    n_cond = c_ref.shape[0] + 1
    c = jnp.concatenate([c_ref[...], cctx_ref[...], jnp.zeros((rows - n_cond, D_MODEL), F32)], axis=0)
    s = (c * _sigmoid(c)).astype(BF16)
    mod = _dot(s, w_ref[...].astype(BF16)) + b_ref[...]
    for r in range(mod.shape[0]):
        o_ref[r] = mod[r:r + 1, :]
    winb_ref[...] = (win_ref[...] * col_ref[...]).astype(BF16)


def _ada_call(c, c_ctx, w_ada, b_ada, w_in, col):
    rows = ADA_ROWS
    n_out = w_ada.shape[1]
    steps = ADA_STEPS
    bn = n_out // steps
    bw = w_in.shape[1] // steps
    return pl.pallas_call(
        _ada_kernel,
        grid=(steps,),
        in_specs=[pl.BlockSpec(c.shape, lambda j: (0, 0)),
                  pl.BlockSpec((1, D_MODEL), lambda j: (0, 0)),
                  pl.BlockSpec((D_MODEL, bn), lambda j: (0, j)),
                  pl.BlockSpec((1, bn), lambda j: (0, j)),
                  pl.BlockSpec((D_MODEL, bw), lambda j: (0, j)),
                  pl.BlockSpec((1, bw), lambda j: (0, j))],
        out_specs=[pl.BlockSpec((rows, 1, bn), lambda j: (0, 0, j)),
                   pl.BlockSpec((D_MODEL, bw), lambda j: (0, j))],
        out_shape=[jax.ShapeDtypeStruct((rows, 1, n_out), F32),
                   jax.ShapeDtypeStruct(w_in.shape, BF16)],
        compiler_params=pltpu.CompilerParams(dimension_semantics=("arbitrary",),
                                             vmem_limit_bytes=VMEM_LIMIT),
        name="ada_mod",
    )(c, c_ctx, w_ada, b_ada, w_in, col)


def _chunk_cumsum(x, reverse):
    n, c = x.shape
    sub = 8
    r = lax.broadcasted_iota(jnp.int32, x.shape, 0) & (CHUNK - 1)
    s = 1
    while s < sub:
        if reverse:
            x = x + jnp.where(r < CHUNK - s, pltpu.roll(x, n - s, 0), 0.0)
        else:
            x = x + jnp.where(r >= s, pltpu.roll(x, s, 0), 0.0)
        s *= 2
    per = CHUNK // sub
    x4 = x.reshape(n // CHUNK, per, sub, c)
    s = 1
    while s < per:
        if reverse:
            x4 = jnp.concatenate([x4[:, :per - s] + x4[:, s:], x4[:, per - s:]], axis=1)
        else:
            x4 = jnp.concatenate([x4[:, :s], x4[:, s:] + x4[:, :per - s]], axis=1)
        s *= 2
    return x4.reshape(n, c)


def _rope(t, cos, sin_signed, lane):
    partner = jnp.where((lane & 31) < 16, pltpu.roll(t, LANES - 16, 1), pltpu.roll(t, 16, 1))
    return t * cos + partner * sin_signed


def _inproj_kernel(x_ref, mod_ref, nw_ref, w_ref, lbl_ref, qw_ref, kw_ref, cos_ref, sin_ref,
                   *out_refs, latent):
    if latent:
        hg_ref, dec_ref, v_ref, g_ref, q_ref, katt_ref, vatt_ref, gates_ref = out_refs
    else:
        kd2f_ref, kd2b_ref, dec_ref, vt_ref, katt_ref, vatt_ref = out_refs

    tm = x_ref.shape[1]
    nchunk = tm // CHUNK
    x = x_ref[0]
    sh = mod_ref[0, :, 0:D_MODEL]
    sc = mod_ref[0, :, D_MODEL:2 * D_MODEL]
    ms = jnp.mean(x * x, axis=-1, keepdims=True)
    h = x * lax.rsqrt(ms + EPS) * (nw_ref[...] * (1.0 + sc)) + sh
    hb = h.astype(BF16)

    def mm(a, b):
        return _dot(hb, w_ref[:, a:b])

    l0f, l1f = lbl_ref[0, 0:1, :], lbl_ref[0, 1:2, :]
    l0b, l1b = lbl_ref[1, 0:1, :], lbl_ref[1, 1:2, :]
    lb_f = 1.0 / (1.0 + jnp.exp(l1f - l0f))
    lb_b = 1.0 / (1.0 + jnp.exp(l1b - l0b))

    qw = jnp.concatenate([qw_ref[...]] * (LANES // HEAD_DIM), axis=1)
    kw = jnp.concatenate([kw_ref[...]] * (LANES // HEAD_DIM), axis=1)
    gw = HG_WIDTH // 2
    sw = D_MODEL // 2
    lane = lax.broadcasted_iota(jnp.int32, (tm, LANES), 1)
    left = lane < HEAD_DIM
    bi = lax.broadcasted_iota(jnp.int32, (2 * LANES, LANES), 0) // HEAD_DIM
    bj = lax.broadcasted_iota(jnp.int32, (2 * LANES, LANES), 1) // HEAD_DIM
    ones_blk2 = jnp.where((bi % 2) == bj, 1.0, 0.0).astype(BF16)
    keep = {}

    def qhg_finish(raw):
        keep["q_hg"] = raw * (jnp.tanh(raw) + 1.0) * (HG_DIM ** -0.5)

    def gate_finish(raw, direction, part):
        reverse = direction == 1
        lo = part * gw
        lb = (lb_f, lb_b)[direction][:, lo:lo + gw]
        c_half = 0.5 - 0.5 * lb
        ct = c_half * jnp.tanh(raw)
        f = (0.5 + 0.5 * lb) + ct
        k = c_half - ct
        cum = _chunk_cumsum(jnp.log(f), reverse)
        cum3 = cum.reshape(nchunk, CHUNK, gw)
        tot3 = cum3[:, 0:1, :] if reverse else cum3[:, CHUNK - 1:CHUNK, :]
        dec3 = jnp.exp(tot3)
        dec_ref[0, :, direction * HG_WIDTH + lo:direction * HG_WIDTH + lo + gw] = dec3.reshape(nchunk, gw)
        kd2 = (k.reshape(nchunk, CHUNK, gw) * jnp.exp(tot3 - cum3)).reshape(tm, gw).astype(BF16)
        if latent:
            base = direction * 3 * HG_WIDTH + lo
            hg_ref[0, :, base:base + gw] = (keep["q_hg"][:, lo:lo + gw] * jnp.exp(cum)).astype(BF16)
            hg_ref[0, :, base + HG_WIDTH:base + HG_WIDTH + gw] = (k * jnp.exp(-cum)).astype(BF16)
            hg_ref[0, :, base + 2 * HG_WIDTH:base + 2 * HG_WIDTH + gw] = kd2
        else:
            (kd2f_ref, kd2b_ref)[direction][0, :, lo:lo + gw] = kd2

    def mgate_finish(raw, j):
        gates_ref[0, :, j * sw:(j + 1) * sw] = (jnp.tanh(raw) + 1.0).astype(BF16)

    def v_finish(raw):
        if latent:
            v_ref[0] = raw.astype(BF16)
        else:
            for s in range(tm // LANES):
                vt_ref[0, s] = raw[s * LANES:(s + 1) * LANES, :].T.astype(BF16)

    def g_finish(raw):
        g_ref[0] = (raw * (jnp.tanh(raw) + 1.0)).astype(BF16)

    def split_sq(t):
        sq = t * t
        hi = sq.astype(BF16)
        lo = (sq - hi.astype(F32)).astype(BF16)
        return jnp.concatenate([hi, lo], axis=1)

    def kv_finish(raw):
        keep["k_raw"] = raw[:, 0:LANES]
        keep["k_split"] = split_sq(keep["k_raw"])
        vatt_ref[0] = raw[:, LANES:2 * LANES].astype(BF16)

    def knorm_finish(ss):
        kn = keep["k_raw"] * lax.rsqrt(ss * (1.0 / HEAD_DIM) + EPS) * kw
        if latent:
            kn = _rope(kn, cos_ref[...], sin_ref[...], lane)
        for s in range(tm // LANES):
            katt_ref[0, s] = kn[s * LANES:(s + 1) * LANES, :].T.astype(BF16)

    def q_finish(raw):
        keep["q_raw"] = raw
        keep["q_split"] = [split_sq(raw[:, s * LANES:(s + 1) * LANES]) for s in range(ATT_WIDTH // LANES)]

    def qnorm_finish(ss_list):
        for s, ss in enumerate(ss_list):
            qraw = keep["q_raw"][:, s * LANES:(s + 1) * LANES]
            qn = qraw * lax.rsqrt(ss * (1.0 / HEAD_DIM) + EPS) * qw
            qn = _rope(qn, cos_ref[...], sin_ref[...], lane) * (HEAD_DIM ** -0.5 * LOG2E)
            swapped = pltpu.roll(qn, HEAD_DIM, 1)
            if s < ATT_WIDTH // LANES // 2:
                even, odd = jnp.where(left, qn, 0.0), jnp.where(left, swapped, 0.0)
            else:
                even, odd = jnp.where(left, 0.0, swapped), jnp.where(left, 0.0, qn)
            q_ref[0, :, (2 * s) * LANES:(2 * s + 1) * LANES] = even.astype(BF16)
            q_ref[0, :, (2 * s + 1) * LANES:(2 * s + 2) * LANES] = odd.astype(BF16)

    def gate_stage(direction, part):
        col = (C_FF, C_FB)[direction] + part * gw
        return (lambda: mm(col, col + gw),
                functools.partial(gate_finish, direction=direction, part=part))

    def mgate_stage(j):
        c0 = C_GATES + j * sw
        return (lambda: mm(c0, c0 + sw), functools.partial(mgate_finish, j=j))

    v_stage = (lambda: mm(C_INP, C_INP + HG_WIDTH), v_finish)
    kv_stage = (lambda: mm(C_K, C_K + 2 * KV_WIDTH), kv_finish)
    knorm_stage = (lambda: _dot(keep["k_split"], ones_blk2), knorm_finish)
    if latent:
        stages = [(lambda: mm(C_QHG, C_QHG + HG_WIDTH), qhg_finish)]
        for j in range(4):
            stages += [gate_stage(j // 2, j % 2), mgate_stage(j)]
        stages += [(lambda: mm(C_Q, C_Q + ATT_WIDTH), q_finish),
                   v_stage,
                   kv_stage,
                   (lambda: [_dot(t, ones_blk2) for t in keep["q_split"]], qnorm_finish),
                   (lambda: mm(C_GHG, C_GHG + HG_WIDTH), g_finish),
                   knorm_stage]
    else:
        stages = [gate_stage(0, 0), gate_stage(0, 1), kv_stage, gate_stage(1, 0), gate_stage(1, 1),
                  knorm_stage, v_stage]
    _emit_pipelined(stages, INPROJ_DEPTH)


def _inproj_call(x, mod3, mod_row0, nw, w_in, lbl, qw, kw, cos_t, sin_t, latent):
    b_, s_, _ = x.shape
    tm = min(INPROJ_TILE if latent else CTX_TILE, s_)
    grid = (b_, s_ // tm)
    tok = lambda w: pl.BlockSpec((1, tm, w), lambda b, i: (b, i, 0))
    mod_map = (lambda b, i: (mod_row0 + b, 0, 0)) if latent else (lambda b, i: (mod_row0, 0, 0))
    ncols = IN_COLS if latent else CTX_COLS
    in_specs = [tok(D_MODEL),
                pl.BlockSpec((1, 1, mod3.shape[2]), mod_map),
                _const_spec((1, D_MODEL)),
                pl.BlockSpec((D_MODEL, ncols), lambda b, i: (0, 0), pipeline_mode=pl.Buffered(1)),
                _const_spec(lbl.shape),
                _const_spec((1, HEAD_DIM)),
                _const_spec((1, HEAD_DIM)),
                pl.BlockSpec((tm, LANES), lambda b, i: (i, 0)),
                pl.BlockSpec((tm, LANES), lambda b, i: (i, 0))]
    bf = lambda w: jax.ShapeDtypeStruct((b_, s_, w), BF16)
    dec_shape = jax.ShapeDtypeStruct((b_, s_ // CHUNK, 2 * HG_WIDTH), F32)
    dec_spec = pl.BlockSpec((1, tm // CHUNK, 2 * HG_WIDTH), lambda b, i: (b, i, 0))
    vt_shape = jax.ShapeDtypeStruct((b_, s_ // LANES, HG_WIDTH, LANES), BF16)
    vt_spec = pl.BlockSpec((1, tm // LANES, HG_WIDTH, LANES), lambda b, i: (b, i, 0, 0))
    katt_shape = jax.ShapeDtypeStruct((b_, s_ // LANES, KV_WIDTH, LANES), BF16)
    katt_spec = pl.BlockSpec((1, tm // LANES, KV_WIDTH, LANES), lambda b, i: (b, i, 0, 0))
    if latent:
        out_shape = [bf(6 * HG_WIDTH), dec_shape, bf(HG_WIDTH), bf(HG_WIDTH),
                                         bf(ATT_HEADS * LANES), katt_shape, bf(KV_WIDTH), bf(2 * D_MODEL)]
        out_specs = [tok(6 * HG_WIDTH), dec_spec, tok(HG_WIDTH), tok(HG_WIDTH),
                                          tok(ATT_HEADS * LANES), katt_spec, tok(KV_WIDTH), tok(2 * D_MODEL)]
    else:
        out_shape = [bf(HG_WIDTH)] * 2 + [dec_shape, vt_shape, katt_shape, bf(KV_WIDTH)]
        out_specs = [tok(HG_WIDTH)] * 2 + [dec_spec, vt_spec, katt_spec, tok(KV_WIDTH)]
    return pl.pallas_call(
        functools.partial(_inproj_kernel, latent=latent),
        grid=grid, in_specs=in_specs, out_specs=out_specs, out_shape=out_shape,
        compiler_params=pltpu.CompilerParams(dimension_semantics=("parallel", "parallel"),
                                             vmem_limit_bytes=VMEM_LIMIT),
        name="inproj_latent" if latent else "inproj_ctx",
    )(x, mod3, nw, w_in, lbl, qw, kw, cos_t, sin_t)


def _cast_side_job(w_refs, wb_refs):
    for w_ref, wb_ref in zip(w_refs, wb_refs):
        wb_ref[...] = w_ref[...].astype(BF16)


def _side_job_specs(weights, n_steps, step_of):
    specs = [pl.BlockSpec((wt.shape[0] // n_steps, wt.shape[1]), lambda *ids: (step_of(*ids), 0))
             for wt in weights]
    shapes = [jax.ShapeDtypeStruct(wt.shape, BF16) for wt in weights]
    return specs, shapes


def _hgrn_kernel(qdf_ref, kdf_ref, kd2f_ref, qdb_ref, kdb_ref, kd2b_ref, v_ref,
                 decf_ref, decb_ref, ckd2f_ref, ckd2b_ref, cvt_ref, cdecf_ref, cdecb_ref,
                 g_ref, nw_ref, *rest):
    n_w = (len(rest) - 2) // 2
    w_refs, y_ref, wb_refs, o_acc = rest[:n_w], rest[n_w], rest[n_w + 1:2 * n_w + 1], rest[-1]
    _cast_side_job(w_refs, wb_refs)

    seq = v_ref.shape[1]
    ctx_len = ckd2f_ref.shape[1]
    heads = v_ref.shape[2] // HG_DIM
    tile = HG_TILE
    cpt = tile // CHUNK
    cps = LANES // CHUNK
    spt = tile // LANES
    n_tiles = seq // tile
    half = n_tiles // 2

    row_chunk = lax.broadcasted_iota(jnp.int32, (LANES, HG_DIM), 0) // CHUNK
    ti = lax.broadcasted_iota(jnp.int32, (LANES, LANES), 0)
    tj = lax.broadcasted_iota(jnp.int32, (LANES, LANES), 1)
    same = (ti // CHUNK) == (tj // CHUNK)
    mask_f = same & (tj <= ti)
    mask_b = same & (tj >= ti)

    def chunk_update(vt_slab, k_slab, n_in_slab):
        k_m = jnp.where(row_chunk == n_in_slab, k_slab, jnp.zeros_like(k_slab))
        return _dot(vt_slab, k_m)

    def chunk_update_tn(v_slab, k_slab, n_in_slab):
        k_m = jnp.where(row_chunk == n_in_slab, k_slab, jnp.zeros_like(k_slab))
        return lax.dot_general(v_slab, k_m, (((0,), (0,)), ((), ())), preferred_element_type=F32)

    def ctx_state(ckd2_ref, cdec_ref, cols, reverse):
        st = jnp.zeros((HG_DIM, HG_DIM), F32)
        order = range(ctx_len // CHUNK)
        for n in (reversed(order) if reverse else order):
            slab = n // cps
            st = cdec_ref[0, n:n + 1, cols] * st + chunk_update(
                cvt_ref[0, slab, cols, :], ckd2_ref[0, slab * LANES:(slab + 1) * LANES, cols], n % cps)
        return st

    def emit(o_t, cols, rows, finalize):
        if finalize:
            o_t = o_t + o_acc[rows, cols]
            ms = jnp.mean(o_t * o_t, axis=-1, keepdims=True)
            y = o_t * lax.rsqrt(ms + EPS) * nw_ref[...] * g_ref[0, rows, cols].astype(F32)
            y_ref[0, rows, cols] = y.astype(BF16)
        else:
            o_acc[rows, cols] = o_t

    def body(it, states, finalize):
        chains = []
        for h in range(heads):
            cols = slice(h * HG_DIM, (h + 1) * HG_DIM)
            chains.append((qdf_ref, kdf_ref, decf_ref, cols, it, False, kd2f_ref))
            chains.append((qdb_ref, kdb_ref, decb_ref, cols, n_tiles - 1 - it, True, kd2b_ref))
        n_ch = len(chains)
        rows = [pl.ds(pl.multiple_of(c[4] * tile, tile), tile) for c in chains]
        orders = [list(reversed(range(cpt))) if c[5] else list(range(cpt)) for c in chains]
        qd, sc, ups = [], [], []
        for (qd_ref, kd_ref, _, cols, t, reverse, kd2_ref), r in zip(chains, rows):
            qd.append(qd_ref[0, r, cols])
            kd = kd_ref[0, r, cols]
            sc.append([_dot_nt(qd[-1][s * LANES:(s + 1) * LANES, :], kd[s * LANES:(s + 1) * LANES, :])
                       for s in range(spt)])
            kd2 = kd2_ref[0, r, cols]
            v = v_ref[0, r, cols]
            ups.append([chunk_update_tn(v[(n // cps) * LANES:(n // cps + 1) * LANES, :],
                                        kd2[(n // cps) * LANES:(n // cps + 1) * LANES, :], n % cps)
                        for n in range(cpt)])
        o = []
        for c, r, s in zip(chains, rows, sc):
            v = v_ref[0, r, c[3]]
            o.append(jnp.concatenate(
                [_dot(jnp.where(mask_b if c[5] else mask_f, s[j], 0.0).astype(BF16),
                      v[j * LANES:(j + 1) * LANES, :]) for j in range(spt)], axis=0))
        starts, new_states = [], []
        for i, c in enumerate(chains):
            dec = c[2][0, pl.ds(pl.multiple_of(c[4] * cpt, cpt), cpt), c[3]]
            st = states[i]
            start = [None] * cpt
            for n in orders[i]:
                start[n] = st.T.astype(BF16)
                st = dec[n:n + 1, :] * st + ups[i][n]
            starts.append(start)
            new_states.append(st)
        for i, c in enumerate(chains):
            outs = [o[i][n * CHUNK:(n + 1) * CHUNK, :]
                    + _dot(qd[i][n * CHUNK:(n + 1) * CHUNK, :], starts[i][n]) for n in range(cpt)]
            emit(jnp.concatenate(outs, axis=0), c[3], rows[i], finalize)
        return tuple(new_states)

    states = []
    for h in range(heads):
        cols = slice(h * HG_DIM, (h + 1) * HG_DIM)
        states += [ctx_state(ckd2f_ref, cdecf_ref, cols, False), ctx_state(ckd2b_ref, cdecb_ref, cols, True)]
    states = lax.fori_loop(0, half, functools.partial(body, finalize=False), tuple(states))
    lax.fori_loop(half, n_tiles, functools.partial(body, finalize=True), states)


def _hgrn_call(hg, v, dec, ckd2f, ckd2b, cvt, cdec, g, nw, weights):
    b_, s_, _ = v.shape
    l_ = ckd2f.shape[1]
    w = HG_HEADS_PER_STEP * HG_DIM
    groups = HG_HEADS // HG_HEADS_PER_STEP
    seq = lambda n: pl.BlockSpec((1, n, w), lambda b, h: (b, 0, h))
    vt_spec = lambda n: pl.BlockSpec((1, n // LANES, w, LANES), lambda b, h: (b, 0, h, 0))
    dec_f = lambda n: pl.BlockSpec((1, n // CHUNK, w), lambda b, h: (b, 0, h))
    dec_b = lambda n: pl.BlockSpec((1, n // CHUNK, w), lambda b, h: (b, 0, groups + h))
    w_specs, w_shapes = _side_job_specs(weights, b_ * groups, lambda b, h: b * groups + h)
    outs = pl.pallas_call(
        _hgrn_kernel,
        grid=(b_, groups),
        in_specs=[pl.BlockSpec((1, s_, w), lambda b, h, j=j: (b, 0, j * groups + h)) for j in range(6)]
        + [seq(s_), dec_f(s_), dec_b(s_),
                                  seq(l_), seq(l_), vt_spec(l_), dec_f(l_), dec_b(l_),
                                  seq(s_), _const_spec((1, HG_DIM))] + w_specs,
        out_specs=[seq(s_)] + w_specs,
        out_shape=[jax.ShapeDtypeStruct((b_, s_, HG_WIDTH), BF16)] + w_shapes,
        scratch_shapes=[pltpu.VMEM((s_, w), F32)],
        compiler_params=pltpu.CompilerParams(dimension_semantics=("parallel", "parallel"),
                                             vmem_limit_bytes=VMEM_LIMIT),
        name="hgrn_scan",
    )(hg, hg, hg, hg, hg, hg, v, dec, dec, ckd2f, ckd2b, cvt, cdec, cdec, g, nw, *weights)
    return outs[0], outs[1:]


def _attn_kernel(sink_ref, q_ref, kt_ref, kc_ref, v_ref, vc_ref, *rest):
    n_w = (len(rest) - 1) // 2
    y_ref = rest[n_w]
    _cast_side_job(rest[:n_w], rest[n_w + 1:])
    nb = kt_ref.shape[1]
    blk = ATT_BLOCK
    hpg = ATT_GROUP_HEADS
    n_groups = ATT_HEADS // hpg
    qry_r = lax.broadcasted_iota(jnp.int32, (blk, blk), 0)
    key_c = lax.broadcasted_iota(jnp.int32, (blk, blk), 1)
    left = lax.broadcasted_iota(jnp.int32, (blk, LANES), 1) < HEAD_DIM
    ctx_keys = [kc_ref[0, c] for c in range(kc_ref.shape[1])]
    keep = {}

    def band(u):
        qb = pl.program_id(1) * ATT_STEP_BLOCKS + u
        ids = (jnp.maximum(qb - 1, 0), qb, jnp.minimum(qb + 1, nb - 1))
        keys_t = jnp.concatenate([kt_ref[0, j] for j in ids] + ctx_keys, axis=1)
        values = jnp.concatenate([v_ref[0, pl.ds(pl.multiple_of(j * blk, blk), blk), :] for j in ids]
                                 + [vc_ref[0]], axis=0)
        values = jnp.concatenate([values, jnp.ones_like(values)], axis=1)
        return qb, keys_t, values

    def score_issue(u, g):
        if g == 0:
            keep["band", u] = band(u)
        q = jnp.concatenate([q_ref[0, u * blk:(u + 1) * blk, h * LANES:(h + 1) * LANES]
                             for h in range(g * hpg, (g + 1) * hpg)], axis=0)
        return _dot(q, keep["band", u][1])

    def score_finish(raw, u, g):
        qb = keep["band", u][0]
        mask_prev = (key_c >= qry_r) & (qb > 0)
        mask_next = (key_c <= qry_r) & (qb < nb - 1)
        probs, sink_p = [], []
        for n in range(hpg):
            s = raw[n * blk:(n + 1) * blk, :]
            sink = sink_ref[g * hpg + n] * LOG2E
            s = jnp.concatenate([jnp.where(mask_prev, s[:, 0:blk], NEG), s[:, blk:2 * blk],
                                 jnp.where(mask_next, s[:, 2 * blk:3 * blk], NEG), s[:, 3 * blk:]], axis=1)
            mx = jnp.maximum(jnp.max(s, axis=1, keepdims=True), sink)
            probs.append(jnp.exp2(s - mx).astype(BF16))
            sink_p.append(jnp.exp2(sink - mx))
        keep["p", u, g] = jnp.concatenate(probs, axis=0)
        keep["sink_p", u, g] = sink_p

    def value_issue(u, g):
        return _dot(keep["p", u, g], keep["band", u][2])

    def value_finish(raw, u, g):
        for n in range(0, hpg, 2):
            h = g * hpg + n
            even, odd = [raw[m * blk:(m + 1) * blk, 0:LANES]
                         / (raw[m * blk:(m + 1) * blk, LANES:2 * LANES] + keep["sink_p", u, g][m])
                         for m in (n, n + 1)]
            if h < ATT_HEADS // ATT_KV_HEADS:
                y = jnp.where(left, even, pltpu.roll(odd, HEAD_DIM, 1))
            else:
                y = jnp.where(left, pltpu.roll(even, HEAD_DIM, 1), odd)
            y_ref[0, u * blk:(u + 1) * blk, (h // 2) * LANES:(h // 2 + 1) * LANES] = y.astype(BF16)

    part = functools.partial
    work = [(u, g) for u in range(ATT_STEP_BLOCKS) for g in range(n_groups)]
    score = lambda u, g: (part(score_issue, u, g), part(score_finish, u=u, g=g))
    stages = [score(*w) for w in work[:ATT_DEPTH + 1]]
    for n, (u, g) in enumerate(work):
        stages.append((part(value_issue, u, g), part(value_finish, u=u, g=g)))
        if n + ATT_DEPTH + 1 < len(work):
            stages.append(score(*work[n + ATT_DEPTH + 1]))
    _emit_pipelined(stages, ATT_DEPTH)


def _attn_call(sinks, q, katt, vatt, ckatt, cvatt, weights):
    b_, s_, qw = q.shape
    l_ = cvatt.shape[1]
    rows = ATT_STEP_BLOCKS * ATT_BLOCK
    batch4 = lambda n: pl.BlockSpec((1, n, KV_WIDTH, LANES), lambda b, i: (b, 0, 0, 0))
    batch3 = lambda n: pl.BlockSpec((1, n, KV_WIDTH), lambda b, i: (b, 0, 0))
    steps = s_ // rows
    w_specs, w_shapes = _side_job_specs(weights, b_ * steps, lambda b, i: b * steps + i)
    outs = pl.pallas_call(
        _attn_kernel,
        grid=(b_, steps),
        in_specs=[pl.BlockSpec(memory_space=pltpu.SMEM),
                  pl.BlockSpec((1, rows, qw), lambda b, i: (b, i, 0)),
                  batch4(s_ // LANES), batch4(l_ // LANES), batch3(s_), batch3(l_)] + w_specs,
        out_specs=[pl.BlockSpec((1, rows, ATT_WIDTH), lambda b, i: (b, i, 0))] + w_specs,
        out_shape=[jax.ShapeDtypeStruct((b_, s_, ATT_WIDTH), BF16)] + w_shapes,
        compiler_params=pltpu.CompilerParams(dimension_semantics=("parallel", "arbitrary"),
                                             vmem_limit_bytes=VMEM_LIMIT),
        name="window_attn",
    )(sinks, q, katt, ckatt, vatt, cvatt, *weights)
    return outs[0], outs[1:]


def _merge_ffn_kernel(x_ref, yh_ref, ya_ref, gates_ref, mod_ref, nw_ref, wbh_ref, wba_ref, wo_ref,
                      wg_ref, wu_ref, wd_ref, o_ref):
    tm = x_ref.shape[1]
    n_sub = FFN_SUBTILES
    sub = tm // n_sub
    bounds = FFN_CHUNK_BOUNDS
    n_chunks = len(bounds) - 1
    g1 = mod_ref[0, :, 2 * D_MODEL:3 * D_MODEL]
    sh2 = mod_ref[0, :, 3 * D_MODEL:4 * D_MODEL]
    sc2 = mod_ref[0, :, 4 * D_MODEL:5 * D_MODEL]
    g2 = mod_ref[0, :, 5 * D_MODEL:6 * D_MODEL]
    keep = {}

    def branch_issue(r):
        rows = slice(r * sub, (r + 1) * sub)
        return _dot(yh_ref[0, rows, :], wbh_ref[...]), _dot(ya_ref[0, rows, :], wba_ref[...])

    def branch_finish(raw, r):
        rows = slice(r * sub, (r + 1) * sub)
        a, b = raw
        mixed = (gates_ref[0, rows, 0:D_MODEL].astype(F32) * a
                 + gates_ref[0, rows, D_MODEL:2 * D_MODEL].astype(F32) * b)
        keep["mixed", r] = (0.5 * mixed).astype(BF16)

    def out_issue(r):
        return _dot(keep["mixed", r], wo_ref[...])

    def out_finish(raw, r):
        x1 = x_ref[0, r * sub:(r + 1) * sub, :] + g1 * raw
        ms = jnp.mean(x1 * x1, axis=-1, keepdims=True)
        keep["x1", r] = x1
        keep["h2", r] = ((x1 * lax.rsqrt(ms + EPS) * nw_ref[...]) * (1.0 + sc2) + sh2).astype(BF16)

    def hidden_issue(r, c):
        h2 = keep["h2", r]
        cols = slice(bounds[c], bounds[c + 1])
        return _dot(h2, wg_ref[:, cols]), _dot(h2, wu_ref[:, cols])

    def hidden_finish(raw, r, c):
        gate, up = raw
        keep["act", r, c] = (gate * _sigmoid(gate) * up).astype(BF16)

    def down_issue(r, c):
        return _dot(keep["act", r, c], wd_ref[bounds[c]:bounds[c + 1], :])

    def down_finish(raw, r, c):
        dn = raw if c == 0 else keep["dn", r] + raw
        if c + 1 < n_chunks:
            keep["dn", r] = dn
        else:
            o_ref[0, r * sub:(r + 1) * sub, :] = keep["x1", r] + g2 * dn

    part = functools.partial
    stages = []
    for issue, finish in ((branch_issue, branch_finish), (out_issue, out_finish)):
        stages += [(part(issue, r), part(finish, r=r)) for r in range(n_sub)]
    for c in range(n_chunks):
        stages += [(part(hidden_issue, r, c), part(hidden_finish, r=r, c=c)) for r in range(n_sub)]
        stages += [(part(down_issue, r, c), part(down_finish, r=r, c=c)) for r in range(n_sub)]
    _emit_pipelined(stages, n_sub - 1)


def _merge_ffn_call(x, yh, ya, gates, mod3, nw, wbh, wba, wo, wg, wu, wd):
    b_, s_, _ = x.shape
    tm = TOK_TILE
    tok = lambda w: pl.BlockSpec((1, tm, w), lambda b, i: (b, i, 0))
    return pl.pallas_call(
        _merge_ffn_kernel,
        grid=(b_, s_ // tm),
        in_specs=[tok(D_MODEL), tok(HG_WIDTH), tok(ATT_WIDTH), tok(2 * D_MODEL),
                  pl.BlockSpec((1, 1, mod3.shape[2]), lambda b, i: (b, 0, 0)),
                  _const_spec((1, D_MODEL)),
                  _const_spec(wbh.shape), _const_spec(wba.shape), _const_spec(wo.shape),
                  _const_spec(wg.shape), _const_spec(wu.shape), _const_spec(wd.shape)],
        out_specs=tok(D_MODEL),
        out_shape=jax.ShapeDtypeStruct(x.shape, F32),
        compiler_params=pltpu.CompilerParams(dimension_semantics=("parallel", "parallel"),
                                             vmem_limit_bytes=VMEM_LIMIT),
        name="merge_ffn",
    )(x, yh, ya, gates, mod3, nw, wbh, wba, wo, wg, wu, wd)


def _rope_tables(n_tok):
    t = np.arange(n_tok)
    rows = (t // GRID_W).astype(np.float64)
    cols = (t % GRID_W).astype(np.float64)
    half = HEAD_DIM // 2
    inv_freq = ROPE_THETA ** (-np.arange(0, half, 2, dtype=np.float64) / half)
    d = np.arange(LANES) % HEAD_DIM
    pos = np.where((d < half)[None, :], rows[:, None], cols[:, None])
    ang = pos * inv_freq[(d % half) % (half // 2)][None, :]
    sign = np.where((d % half) < half // 2, -1.0, 1.0)[None, :]
    return jnp.asarray(np.cos(ang), F32), jnp.asarray(np.sin(ang) * sign, F32)


def kernel(x, c, ctx, c_ctx, w_ada, b_ada, norm_mix_w, norm_ffn_w, w_in, hgrn_lb_logits, hgrn_norm_w,
           q_norm_w, k_norm_w, attn_sinks, w_branch_hgrn, w_branch_attn, w_out, w_ffn_gate, w_ffn_up,
           w_ffn_down):
    b_, s_, _ = x.shape
    layer = 0
    col = np.ones((1, IN_COLS), np.float32)
    for lo, width in ((C_FF, 2 * HG_WIDTH), (C_QHG, 2 * HG_WIDTH), (C_GATES, 2 * D_MODEL)):
        col[:, lo:lo + width] = 0.5
    mod, w_in_b = _ada_call(c, c_ctx[None, :], w_ada[layer], b_ada[layer][None, :], w_in[layer],
                            jnp.asarray(col))

    lbl = hgrn_lb_logits[:, 0:2, :]
    qw = q_norm_w[layer][None, :]
    kw = k_norm_w[layer][None, :]
    nw_mix = norm_mix_w[layer][None, :]
    cos_t, sin_t = _rope_tables(s_)

    (hg, dec, v, g, q, katt, vatt, gates) = _inproj_call(
        x, mod, 0, nw_mix, w_in_b, lbl, qw, kw, cos_t, sin_t, latent=True)
    l_ = ctx.shape[1]
    pack = max(1, min(CTX_TILE // l_, b_))
    ctx_out = _inproj_call(ctx.reshape(b_ // pack, pack * l_, D_MODEL), mod, b_, nw_mix, w_in_b, lbl, qw, kw,
                           cos_t, sin_t, latent=False)
    ckd2f, ckd2b, cdec, cvt, ckatt, cvatt = [t.reshape((b_, t.shape[1] // pack) + t.shape[2:])
                                             for t in ctx_out]

    y_hg, _ = _hgrn_call(
        hg, v, dec, ckd2f, ckd2b, cvt, cdec, g, hgrn_norm_w[layer][None, :], ())
    y_at, (w_bh, w_ba, w_o, w_g, w_u, w_d) = _attn_call(
        attn_sinks[layer], q, katt, vatt, ckatt, cvatt,
        (w_branch_hgrn[layer], w_branch_attn[layer], w_out[layer], w_ffn_gate[layer], w_ffn_up[layer],
         w_ffn_down[layer]))

    return _merge_ffn_call(x, y_hg, y_at, gates, mod, norm_ffn_w[layer][None, :],
                           w_bh, w_ba, w_o, w_g, w_u, w_d)
```

```python
import functools

import jax
import jax.numpy as jnp
import numpy as np
from jax import lax
from jax.experimental import pallas as pl
from jax.experimental.pallas import tpu as pltpu

F32 = jnp.float32
BF16 = jnp.bfloat16

D_MODEL = 1024
GRID_W = 64
EPS = 1e-6
HG_HEADS = 4
HG_DIM = 128
HG_WIDTH = HG_HEADS * HG_DIM
CHUNK = 32
ATT_HEADS = 8
ATT_KV_HEADS = 2
HEAD_DIM = 64
ATT_WIDTH = ATT_HEADS * HEAD_DIM
KV_WIDTH = ATT_KV_HEADS * HEAD_DIM
WINDOW = 128
ROPE_THETA = 10000.0
D_FF = 2816
CTX_COLS = 3 * HG_WIDTH + 2 * KV_WIDTH
IN_COLS = CTX_COLS + 2 * HG_WIDTH + ATT_WIDTH + 2 * D_MODEL

C_FF, C_FB, C_INP, C_K, C_V = 0, HG_WIDTH, 2 * HG_WIDTH, 3 * HG_WIDTH, 3 * HG_WIDTH + KV_WIDTH
C_QHG = CTX_COLS
C_GHG = C_QHG + HG_WIDTH
C_Q = C_GHG + HG_WIDTH
C_GATES = C_Q + ATT_WIDTH

LANES = 128
TOK_TILE = 512
FFN_SUBTILES = 2
MXU_DEPTH = 256
FFN_CHUNK_BOUNDS = (0, 6 * MXU_DEPTH, D_FF)
INPROJ_TILE = 256
CTX_TILE = 512
INPROJ_DEPTH = 2
HG_TILE = 256
ATT_BLOCK = 128
ATT_GROUP_HEADS = 2
ATT_DEPTH = 1
ATT_STEP_BLOCKS = 8
ADA_ROWS = 16
ADA_STEPS = 6
VMEM_LIMIT = 56 * 1024 * 1024
NEG = -1e30
LOG2E = 1.4426950408889634

assert ATT_BLOCK == WINDOW == LANES and HG_DIM == LANES and 2 * HEAD_DIM == LANES
assert ATT_GROUP_HEADS % 2 == 0 and (ATT_HEADS // ATT_KV_HEADS) % ATT_GROUP_HEADS == 0


def _dot(a, b):
    return jnp.dot(a, b, preferred_element_type=F32)


def _dot_nt(a, b):
    return lax.dot_general(a, b, (((1,), (1,)), ((), ())), preferred_element_type=F32)


def _sigmoid(x):
    return 0.5 * jnp.tanh(0.5 * x) + 0.5


def _emit_pipelined(stages, depth):
    raws = [issue() for issue, _ in stages[:depth]]
    for i, (_, finish) in enumerate(stages):
        if i + depth < len(stages):
            raws.append(stages[i + depth][0]())
        finish(raws[i])
        raws[i] = None


def _const_spec(shape):
    n = len(shape)
    return pl.BlockSpec(shape, lambda *_: (0,) * n, pipeline_mode=pl.Buffered(1))


def _ada_kernel(c_ref, cctx_ref, w_ref, b_ref, win_ref, col_ref, o_ref, winb_ref):
    rows = o_ref.shape[0]
    n_cond = c_ref.shape[0] + 1
    c = jnp.concatenate([c_ref[...], cctx_ref[...], jnp.zeros((rows - n_cond, D_MODEL), F32)], axis=0)
    s = (c * _sigmoid(c)).astype(BF16)
    mod = _dot(s, w_ref[...].astype(BF16)) + b_ref[...]
    for r in range(mod.shape[0]):
        o_ref[r] = mod[r:r + 1, :]
    winb_ref[...] = (win_ref[...] * col_ref[...]).astype(BF16)


def _ada_call(c, c_ctx, w_ada, b_ada, w_in, col):
    rows = ADA_ROWS
    n_out = w_ada.shape[1]
    steps = ADA_STEPS
    bn = n_out // steps
    bw = w_in.shape[1] // steps
    return pl.pallas_call(
        _ada_kernel,
        grid=(steps,),
        in_specs=[pl.BlockSpec(c.shape, lambda j: (0, 0)),
                  pl.BlockSpec((1, D_MODEL), lambda j: (0, 0)),
                  pl.BlockSpec((D_MODEL, bn), lambda j: (0, j)),
                  pl.BlockSpec((1, bn), lambda j: (0, j)),
                  pl.BlockSpec((D_MODEL, bw), lambda j: (0, j)),
                  pl.BlockSpec((1, bw), lambda j: (0, j))],
        out_specs=[pl.BlockSpec((rows, 1, bn), lambda j: (0, 0, j)),
                   pl.BlockSpec((D_MODEL, bw), lambda j: (0, j))],
        out_shape=[jax.ShapeDtypeStruct((rows, 1, n_out), F32),
                   jax.ShapeDtypeStruct(w_in.shape, BF16)],
        compiler_params=pltpu.CompilerParams(dimension_semantics=("arbitrary",),
                                             vmem_limit_bytes=VMEM_LIMIT),
        name="ada_mod",
    )(c, c_ctx, w_ada, b_ada, w_in, col)


def _chunk_cumsum(x, reverse):
    n, c = x.shape
    sub = 8
    r = lax.broadcasted_iota(jnp.int32, x.shape, 0) & (CHUNK - 1)
    s = 1
    while s < sub:
        if reverse:
            x = x + jnp.where(r < CHUNK - s, pltpu.roll(x, n - s, 0), 0.0)
        else:
            x = x + jnp.where(r >= s, pltpu.roll(x, s, 0), 0.0)
        s *= 2
    per = CHUNK // sub
    x4 = x.reshape(n // CHUNK, per, sub, c)
    s = 1
    while s < per:
        if reverse:
            x4 = jnp.concatenate([x4[:, :per - s] + x4[:, s:], x4[:, per - s:]], axis=1)
        else:
            x4 = jnp.concatenate([x4[:, :s], x4[:, s:] + x4[:, :per - s]], axis=1)
        s *= 2
    return x4.reshape(n, c)


def _rope(t, cos, sin_signed, lane):
    partner = jnp.where((lane & 31) < 16, pltpu.roll(t, LANES - 16, 1), pltpu.roll(t, 16, 1))
    return t * cos + partner * sin_signed


def _inproj_kernel(x_ref, mod_ref, nw_ref, w_ref, lbl_ref, qw_ref, kw_ref, cos_ref, sin_ref,
                   *out_refs, latent):
    if latent:
        hg_ref, dec_ref, vt_ref, v_ref, g_ref, q_ref, katt_ref, vatt_ref, gates_ref = out_refs
    else:
        kd2f_ref, kd2b_ref, dec_ref, vt_ref, katt_ref, vatt_ref = out_refs

    tm = x_ref.shape[1]
    nchunk = tm // CHUNK
    x = x_ref[0]
    sh = mod_ref[0, :, 0:D_MODEL]
    sc = mod_ref[0, :, D_MODEL:2 * D_MODEL]
    ms = jnp.mean(x * x, axis=-1, keepdims=True)
    h = x * lax.rsqrt(ms + EPS) * (nw_ref[...] * (1.0 + sc)) + sh
    hb = h.astype(BF16)

    def mm(a, b):
        return _dot(hb, w_ref[:, a:b])

    l0f, l1f = lbl_ref[0, 0:1, :], lbl_ref[0, 1:2, :]
    l0b, l1b = lbl_ref[1, 0:1, :], lbl_ref[1, 1:2, :]
    lb_f = 1.0 / (1.0 + jnp.exp(l1f - l0f))
    lb_b = 1.0 / (1.0 + jnp.exp(l1b - l0b))

    qw = jnp.concatenate([qw_ref[...]] * (LANES // HEAD_DIM), axis=1)
    kw = jnp.concatenate([kw_ref[...]] * (LANES // HEAD_DIM), axis=1)
    gw = HG_WIDTH // 2
    sw = D_MODEL // 2
    lane = lax.broadcasted_iota(jnp.int32, (tm, LANES), 1)
    left = lane < HEAD_DIM
    bi = lax.broadcasted_iota(jnp.int32, (2 * LANES, LANES), 0) // HEAD_DIM
    bj = lax.broadcasted_iota(jnp.int32, (2 * LANES, LANES), 1) // HEAD_DIM
    ones_blk2 = jnp.where((bi % 2) == bj, 1.0, 0.0).astype(BF16)
    keep = {}

    def qhg_finish(raw):
        keep["q_hg"] = raw * (jnp.tanh(raw) + 1.0) * (HG_DIM ** -0.5)

    def gate_finish(raw, direction, part):
        reverse = direction == 1
        lo = part * gw
        lb = (lb_f, lb_b)[direction][:, lo:lo + gw]
        c_half = 0.5 - 0.5 * lb
        ct = c_half * jnp.tanh(raw)
        f = (0.5 + 0.5 * lb) + ct
        k = c_half - ct
        cum = _chunk_cumsum(jnp.log(f), reverse)
        cum3 = cum.reshape(nchunk, CHUNK, gw)
        tot3 = cum3[:, 0:1, :] if reverse else cum3[:, CHUNK - 1:CHUNK, :]
        dec3 = jnp.exp(tot3)
        dec_ref[0, :, direction * HG_WIDTH + lo:direction * HG_WIDTH + lo + gw] = dec3.reshape(nchunk, gw)
        kd2 = (k.reshape(nchunk, CHUNK, gw) * jnp.exp(tot3 - cum3)).reshape(tm, gw).astype(BF16)
        if latent:
            base = direction * 3 * HG_WIDTH + lo
            hg_ref[0, :, base:base + gw] = (keep["q_hg"][:, lo:lo + gw] * jnp.exp(cum)).astype(BF16)
            hg_ref[0, :, base + HG_WIDTH:base + HG_WIDTH + gw] = (k * jnp.exp(-cum)).astype(BF16)
            hg_ref[0, :, base + 2 * HG_WIDTH:base + 2 * HG_WIDTH + gw] = kd2
        else:
            (kd2f_ref, kd2b_ref)[direction][0, :, lo:lo + gw] = kd2

    def mgate_finish(raw, j):
        gates_ref[0, :, j * sw:(j + 1) * sw] = (jnp.tanh(raw) + 1.0).astype(BF16)

    def v_finish(raw):
        for s in range(tm // LANES):
            vt_ref[0, s] = raw[s * LANES:(s + 1) * LANES, :].T.astype(BF16)
        if latent:
            v_ref[0] = raw.astype(BF16)

    def g_finish(raw):
        g_ref[0] = (raw * (jnp.tanh(raw) + 1.0)).astype(BF16)

    def split_sq(t):
        sq = t * t
        hi = sq.astype(BF16)
        lo = (sq - hi.astype(F32)).astype(BF16)
        return jnp.concatenate([hi, lo], axis=1)

    def kv_finish(raw):
        keep["k_raw"] = raw[:, 0:LANES]
        keep["k_split"] = split_sq(keep["k_raw"])
        vatt_ref[0] = raw[:, LANES:2 * LANES].astype(BF16)

    def knorm_finish(ss):
        kn = keep["k_raw"] * lax.rsqrt(ss * (1.0 / HEAD_DIM) + EPS) * kw
        if latent:
            kn = _rope(kn, cos_ref[...], sin_ref[...], lane)
        for s in range(tm // LANES):
            katt_ref[0, s] = kn[s * LANES:(s + 1) * LANES, :].T.astype(BF16)

    def q_finish(raw):
        keep["q_raw"] = raw
        keep["q_split"] = [split_sq(raw[:, s * LANES:(s + 1) * LANES]) for s in range(ATT_WIDTH // LANES)]

    def qnorm_finish(ss_list):
        for s, ss in enumerate(ss_list):
            qraw = keep["q_raw"][:, s * LANES:(s + 1) * LANES]
            qn = qraw * lax.rsqrt(ss * (1.0 / HEAD_DIM) + EPS) * qw
            qn = _rope(qn, cos_ref[...], sin_ref[...], lane) * (HEAD_DIM ** -0.5 * LOG2E)
            swapped = pltpu.roll(qn, HEAD_DIM, 1)
            if s < ATT_WIDTH // LANES // 2:
                even, odd = jnp.where(left, qn, 0.0), jnp.where(left, swapped, 0.0)
            else:
                even, odd = jnp.where(left, 0.0, swapped), jnp.where(left, 0.0, qn)
            q_ref[0, :, (2 * s) * LANES:(2 * s + 1) * LANES] = even.astype(BF16)
            q_ref[0, :, (2 * s + 1) * LANES:(2 * s + 2) * LANES] = odd.astype(BF16)

    def gate_stage(direction, part):
        col = (C_FF, C_FB)[direction] + part * gw
        return (lambda: mm(col, col + gw),
                functools.partial(gate_finish, direction=direction, part=part))

    def mgate_stage(j):
        c0 = C_GATES + j * sw
        return (lambda: mm(c0, c0 + sw), functools.partial(mgate_finish, j=j))

    v_stage = (lambda: mm(C_INP, C_INP + HG_WIDTH), v_finish)
    kv_stage = (lambda: mm(C_K, C_K + 2 * KV_WIDTH), kv_finish)
    knorm_stage = (lambda: _dot(keep["k_split"], ones_blk2), knorm_finish)
    if latent:
        stages = [(lambda: mm(C_QHG, C_QHG + HG_WIDTH), qhg_finish)]
        for j in range(4):
            stages += [gate_stage(j // 2, j % 2), mgate_stage(j)]
        stages += [(lambda: mm(C_Q, C_Q + ATT_WIDTH), q_finish),
                   v_stage,
                   kv_stage,
                   (lambda: [_dot(t, ones_blk2) for t in keep["q_split"]], qnorm_finish),
                   (lambda: mm(C_GHG, C_GHG + HG_WIDTH), g_finish),
                   knorm_stage]
    else:
        stages = [gate_stage(0, 0), gate_stage(0, 1), kv_stage, gate_stage(1, 0), gate_stage(1, 1),
                  knorm_stage, v_stage]
    _emit_pipelined(stages, INPROJ_DEPTH)


def _inproj_call(x, mod3, mod_row0, nw, w_in, lbl, qw, kw, cos_t, sin_t, latent):
    b_, s_, _ = x.shape
    tm = min(INPROJ_TILE if latent else CTX_TILE, s_)
    grid = (b_, s_ // tm)
    tok = lambda w: pl.BlockSpec((1, tm, w), lambda b, i: (b, i, 0))
    mod_map = (lambda b, i: (mod_row0 + b, 0, 0)) if latent else (lambda b, i: (mod_row0, 0, 0))
    ncols = IN_COLS if latent else CTX_COLS
    in_specs = [tok(D_MODEL),
                pl.BlockSpec((1, 1, mod3.shape[2]), mod_map),
                _const_spec((1, D_MODEL)),
                pl.BlockSpec((D_MODEL, ncols), lambda b, i: (0, 0), pipeline_mode=pl.Buffered(1)),
                _const_spec(lbl.shape),
                _const_spec((1, HEAD_DIM)),
                _const_spec((1, HEAD_DIM)),
                pl.BlockSpec((tm, LANES), lambda b, i: (i, 0)),
                pl.BlockSpec((tm, LANES), lambda b, i: (i, 0))]
    bf = lambda w: jax.ShapeDtypeStruct((b_, s_, w), BF16)
    dec_shape = jax.ShapeDtypeStruct((b_, s_ // CHUNK, 2 * HG_WIDTH), F32)
    dec_spec = pl.BlockSpec((1, tm // CHUNK, 2 * HG_WIDTH), lambda b, i: (b, i, 0))
    vt_shape = jax.ShapeDtypeStruct((b_, s_ // LANES, HG_WIDTH, LANES), BF16)
    vt_spec = pl.BlockSpec((1, tm // LANES, HG_WIDTH, LANES), lambda b, i: (b, i, 0, 0))
    katt_shape = jax.ShapeDtypeStruct((b_, s_ // LANES, KV_WIDTH, LANES), BF16)
    katt_spec = pl.BlockSpec((1, tm // LANES, KV_WIDTH, LANES), lambda b, i: (b, i, 0, 0))
    if latent:
        out_shape = [bf(6 * HG_WIDTH), dec_shape, vt_shape, bf(HG_WIDTH), bf(HG_WIDTH),
                                         bf(ATT_HEADS * LANES), katt_shape, bf(KV_WIDTH), bf(2 * D_MODEL)]
        out_specs = [tok(6 * HG_WIDTH), dec_spec, vt_spec, tok(HG_WIDTH), tok(HG_WIDTH),
                                          tok(ATT_HEADS * LANES), katt_spec, tok(KV_WIDTH), tok(2 * D_MODEL)]
    else:
        out_shape = [bf(HG_WIDTH)] * 2 + [dec_shape, vt_shape, katt_shape, bf(KV_WIDTH)]
        out_specs = [tok(HG_WIDTH)] * 2 + [dec_spec, vt_spec, katt_spec, tok(KV_WIDTH)]
    return pl.pallas_call(
        functools.partial(_inproj_kernel, latent=latent),
        grid=grid, in_specs=in_specs, out_specs=out_specs, out_shape=out_shape,
        compiler_params=pltpu.CompilerParams(dimension_semantics=("parallel", "parallel"),
                                             vmem_limit_bytes=VMEM_LIMIT),
        name="inproj_latent" if latent else "inproj_ctx",
    )(x, mod3, nw, w_in, lbl, qw, kw, cos_t, sin_t)


def _cast_side_job(w_refs, wb_refs):
    for w_ref, wb_ref in zip(w_refs, wb_refs):
        wb_ref[...] = w_ref[...].astype(BF16)


def _side_job_specs(weights, n_steps, step_of):
    specs = [pl.BlockSpec((wt.shape[0] // n_steps, wt.shape[1]), lambda *ids: (step_of(*ids), 0))
             for wt in weights]
    shapes = [jax.ShapeDtypeStruct(wt.shape, BF16) for wt in weights]
    return specs, shapes


def _hgrn_kernel(qdf_ref, kdf_ref, kd2f_ref, qdb_ref, kdb_ref, kd2b_ref, vf_ref, vb_ref, vtf_ref, vtb_ref,
                 decf_ref, decb_ref, ckd2f_ref, ckd2b_ref, cvt_ref, cdecf_ref, cdecb_ref,
                 g_ref, nw_ref, y_ref, o_acc, st_ref):
    seq = g_ref.shape[1]
    ctx_len = ckd2f_ref.shape[1]
    heads = g_ref.shape[2] // HG_DIM
    tile = HG_TILE
    cpt = tile // CHUNK
    cps = LANES // CHUNK
    spt = tile // LANES
    n_tiles = seq // tile
    half = n_tiles // 2
    it = pl.program_id(1)

    row_chunk = lax.broadcasted_iota(jnp.int32, (LANES, HG_DIM), 0) // CHUNK
    ti = lax.broadcasted_iota(jnp.int32, (LANES, LANES), 0)
    tj = lax.broadcasted_iota(jnp.int32, (LANES, LANES), 1)
    same = (ti // CHUNK) == (tj // CHUNK)
    mask_f = same & (tj <= ti)
    mask_b = same & (tj >= ti)

    def chunk_update(vt_slab, k_slab, n_in_slab):
        k_m = jnp.where(row_chunk == n_in_slab, k_slab, jnp.zeros_like(k_slab))
        return _dot(vt_slab, k_m)

    def ctx_state(ckd2_ref, cdec_ref, cols, reverse):
        st = jnp.zeros((HG_DIM, HG_DIM), F32)
        order = range(ctx_len // CHUNK)
        for n in (reversed(order) if reverse else order):
            slab = n // cps
            st = cdec_ref[0, n:n + 1, cols] * st + chunk_update(
                cvt_ref[0, slab, cols, :], ckd2_ref[0, slab * LANES:(slab + 1) * LANES, cols], n % cps)
        return st

    def emit(o_t, cols, rows, finalize):
        if finalize:
            o_t = o_t + o_acc[rows, cols]
            ms = jnp.mean(o_t * o_t, axis=-1, keepdims=True)
            y = o_t * lax.rsqrt(ms + EPS) * nw_ref[...] * g_ref[0, rows, cols].astype(F32)
            y_ref[0, rows, cols] = y.astype(BF16)
        else:
            o_acc[rows, cols] = o_t

    def body(finalize):
        chains = []
        for h in range(heads):
            cols = slice(h * HG_DIM, (h + 1) * HG_DIM)
            chains.append((qdf_ref, kdf_ref, decf_ref, cols, it, False, kd2f_ref, vf_ref, vtf_ref))
            chains.append((qdb_ref, kdb_ref, decb_ref, cols, n_tiles - 1 - it, True, kd2b_ref, vb_ref,
                           vtb_ref))
        rows = [pl.ds(pl.multiple_of(c[4] * tile, tile), tile) for c in chains]
        orders = [list(reversed(range(cpt))) if c[5] else list(range(cpt)) for c in chains]
        qd, sc, ups = [], [], []
        for qd_ref, kd_ref, _, cols, _, _, kd2_ref, _, vt_ref in chains:
            qd.append(qd_ref[0, :, cols])
            kd = kd_ref[0, :, cols]
            sc.append([_dot_nt(qd[-1][s * LANES:(s + 1) * LANES, :], kd[s * LANES:(s + 1) * LANES, :])
                       for s in range(spt)])
            kd2 = kd2_ref[0, :, cols]
            ups.append([chunk_update(vt_ref[0, n // cps, cols, :],
                                     kd2[(n // cps) * LANES:(n // cps + 1) * LANES, :], n % cps)
                        for n in range(cpt)])
        o = []
        for c, s in zip(chains, sc):
            v = c[7][0, :, c[3]]
            o.append(jnp.concatenate(
                [_dot(jnp.where(mask_b if c[5] else mask_f, s[j], 0.0).astype(BF16),
                      v[j * LANES:(j + 1) * LANES, :]) for j in range(spt)], axis=0))
        starts = []
        for i, c in enumerate(chains):
            dec = c[2][0, :, c[3]]
            st = st_ref[i]
            start = [None] * cpt
            for n in orders[i]:
                start[n] = st.T.astype(BF16)
                st = dec[n:n + 1, :] * st + ups[i][n]
            starts.append(start)
            st_ref[i] = st
        for i, c in enumerate(chains):
            outs = [o[i][n * CHUNK:(n + 1) * CHUNK, :]
                    + _dot(qd[i][n * CHUNK:(n + 1) * CHUNK, :], starts[i][n]) for n in range(cpt)]
            emit(jnp.concatenate(outs, axis=0), c[3], rows[i], finalize)

    @pl.when(it == 0)
    def _():
        for h in range(heads):
            cols = slice(h * HG_DIM, (h + 1) * HG_DIM)
            st_ref[2 * h] = ctx_state(ckd2f_ref, cdecf_ref, cols, False)
            st_ref[2 * h + 1] = ctx_state(ckd2b_ref, cdecb_ref, cols, True)

    pl.when(it < half)(functools.partial(body, False))
    pl.when(it >= half)(functools.partial(body, True))


def _hgrn_call(hg, v, vt, dec, ckd2f, ckd2b, cvt, cdec, g, nw):
    b_, s_, _ = v.shape
    l_ = ckd2f.shape[1]
    w = HG_WIDTH
    n_tiles = s_ // HG_TILE
    fwd = lambda b, t: t
    bwd = lambda b, t: n_tiles - 1 - t
    tile = lambda at, j=0: pl.BlockSpec((1, HG_TILE, w), lambda b, t: (b, at(b, t), j))
    vt_tile = lambda at: pl.BlockSpec((1, HG_TILE // LANES, w, LANES), lambda b, t: (b, at(b, t), 0, 0))
    dec_tile = lambda at, j: pl.BlockSpec((1, HG_TILE // CHUNK, w), lambda b, t: (b, at(b, t), j))
    whole = lambda n: pl.BlockSpec((1, n, w), lambda b, t: (b, 0, 0))
    cdec_spec = lambda j: pl.BlockSpec((1, l_ // CHUNK, w), lambda b, t: (b, 0, j))
    return pl.pallas_call(
        _hgrn_kernel,
        grid=(b_, n_tiles),
        in_specs=[tile(fwd, j) for j in range(3)] + [tile(bwd, j) for j in range(3, 6)]
        + [tile(fwd), tile(bwd), vt_tile(fwd), vt_tile(bwd), dec_tile(fwd, 0), dec_tile(bwd, 1),
           whole(l_), whole(l_), pl.BlockSpec((1, l_ // LANES, w, LANES), lambda b, t: (b, 0, 0, 0)),
           cdec_spec(0), cdec_spec(1), whole(s_), _const_spec((1, HG_DIM))],
        out_specs=whole(s_),
        out_shape=jax.ShapeDtypeStruct((b_, s_, HG_WIDTH), BF16),
        scratch_shapes=[pltpu.VMEM((s_, w), F32), pltpu.VMEM((2 * HG_HEADS, HG_DIM, HG_DIM), F32)],
        compiler_params=pltpu.CompilerParams(dimension_semantics=("parallel", "arbitrary"),
                                             vmem_limit_bytes=VMEM_LIMIT),
        name="hgrn_scan",
    )(hg, hg, hg, hg, hg, hg, v, v, vt, vt, dec, dec, ckd2f, ckd2b, cvt, cdec, cdec, g, nw)


def _attn_kernel(sink_ref, q_ref, kt_ref, kc_ref, v_ref, vc_ref, *rest):
    n_w = (len(rest) - 1) // 2
    y_ref = rest[n_w]
    _cast_side_job(rest[:n_w], rest[n_w + 1:])
    nb = kt_ref.shape[1]
    blk = ATT_BLOCK
    hpg = ATT_GROUP_HEADS
    n_groups = ATT_HEADS // hpg
    qry_r = lax.broadcasted_iota(jnp.int32, (blk, blk), 0)
    key_c = lax.broadcasted_iota(jnp.int32, (blk, blk), 1)
    left = lax.broadcasted_iota(jnp.int32, (blk, LANES), 1) < HEAD_DIM
    ctx_keys = [kc_ref[0, c] for c in range(kc_ref.shape[1])]
    keep = {}

    def band(u):
        qb = pl.program_id(1) * ATT_STEP_BLOCKS + u
        ids = (jnp.maximum(qb - 1, 0), qb, jnp.minimum(qb + 1, nb - 1))
        keys_t = jnp.concatenate([kt_ref[0, j] for j in ids] + ctx_keys, axis=1)
        values = jnp.concatenate([v_ref[0, pl.ds(pl.multiple_of(j * blk, blk), blk), :] for j in ids]
                                 + [vc_ref[0]], axis=0)
        values = jnp.concatenate([values, jnp.ones_like(values)], axis=1)
        return qb, keys_t, values

    def score_issue(u, g):
        if g == 0:
            keep["band", u] = band(u)
        q = jnp.concatenate([q_ref[0, u * blk:(u + 1) * blk, h * LANES:(h + 1) * LANES]
                             for h in range(g * hpg, (g + 1) * hpg)], axis=0)
        return _dot(q, keep["band", u][1])

    def score_finish(raw, u, g):
        qb = keep["band", u][0]
        mask_prev = (key_c >= qry_r) & (qb > 0)
        mask_next = (key_c <= qry_r) & (qb < nb - 1)
        probs, sink_p = [], []
        for n in range(hpg):
            s = raw[n * blk:(n + 1) * blk, :]
            sink = sink_ref[g * hpg + n] * LOG2E
            s = jnp.concatenate([jnp.where(mask_prev, s[:, 0:blk], NEG), s[:, blk:2 * blk],
                                 jnp.where(mask_next, s[:, 2 * blk:3 * blk], NEG), s[:, 3 * blk:]], axis=1)
            mx = jnp.maximum(jnp.max(s, axis=1, keepdims=True), sink)
            probs.append(jnp.exp2(s - mx).astype(BF16))
            sink_p.append(jnp.exp2(sink - mx))
        keep["p", u, g] = jnp.concatenate(probs, axis=0)
        keep["sink_p", u, g] = sink_p

    def value_issue(u, g):
        return _dot(keep["p", u, g], keep["band", u][2])

    def value_finish(raw, u, g):
        for n in range(0, hpg, 2):
            h = g * hpg + n
            even, odd = [raw[m * blk:(m + 1) * blk, 0:LANES]
                         / (raw[m * blk:(m + 1) * blk, LANES:2 * LANES] + keep["sink_p", u, g][m])
                         for m in (n, n + 1)]
            if h < ATT_HEADS // ATT_KV_HEADS:
                y = jnp.where(left, even, pltpu.roll(odd, HEAD_DIM, 1))
            else:
                y = jnp.where(left, pltpu.roll(even, HEAD_DIM, 1), odd)
            y_ref[0, u * blk:(u + 1) * blk, (h // 2) * LANES:(h // 2 + 1) * LANES] = y.astype(BF16)

    part = functools.partial
    work = [(u, g) for u in range(ATT_STEP_BLOCKS) for g in range(n_groups)]
    score = lambda u, g: (part(score_issue, u, g), part(score_finish, u=u, g=g))
    stages = [score(*w) for w in work[:ATT_DEPTH + 1]]
    for n, (u, g) in enumerate(work):
        stages.append((part(value_issue, u, g), part(value_finish, u=u, g=g)))
        if n + ATT_DEPTH + 1 < len(work):
            stages.append(score(*work[n + ATT_DEPTH + 1]))
    _emit_pipelined(stages, ATT_DEPTH)


def _attn_call(sinks, q, katt, vatt, ckatt, cvatt, weights):
    b_, s_, qw = q.shape
    l_ = cvatt.shape[1]
    rows = ATT_STEP_BLOCKS * ATT_BLOCK
    batch4 = lambda n: pl.BlockSpec((1, n, KV_WIDTH, LANES), lambda b, i: (b, 0, 0, 0))
    batch3 = lambda n: pl.BlockSpec((1, n, KV_WIDTH), lambda b, i: (b, 0, 0))
    steps = s_ // rows
    w_specs, w_shapes = _side_job_specs(weights, b_ * steps, lambda b, i: b * steps + i)
    outs = pl.pallas_call(
        _attn_kernel,
        grid=(b_, steps),
        in_specs=[pl.BlockSpec(memory_space=pltpu.SMEM),
                  pl.BlockSpec((1, rows, qw), lambda b, i: (b, i, 0)),
                  batch4(s_ // LANES), batch4(l_ // LANES), batch3(s_), batch3(l_)] + w_specs,
        out_specs=[pl.BlockSpec((1, rows, ATT_WIDTH), lambda b, i: (b, i, 0))] + w_specs,
        out_shape=[jax.ShapeDtypeStruct((b_, s_, ATT_WIDTH), BF16)] + w_shapes,
        compiler_params=pltpu.CompilerParams(dimension_semantics=("parallel", "arbitrary"),
                                             vmem_limit_bytes=VMEM_LIMIT),
        name="window_attn",
    )(sinks, q, katt, ckatt, vatt, cvatt, *weights)
    return outs[0], outs[1:]


def _merge_ffn_kernel(x_ref, yh_ref, ya_ref, gates_ref, mod_ref, nw_ref, wbh_ref, wba_ref, wo_ref,
                      wg_ref, wu_ref, wd_ref, o_ref):
    tm = x_ref.shape[1]
    n_sub = FFN_SUBTILES
    sub = tm // n_sub
    bounds = FFN_CHUNK_BOUNDS
    n_chunks = len(bounds) - 1
    g1 = mod_ref[0, :, 2 * D_MODEL:3 * D_MODEL]
    sh2 = mod_ref[0, :, 3 * D_MODEL:4 * D_MODEL]
    sc2 = mod_ref[0, :, 4 * D_MODEL:5 * D_MODEL]
    g2 = mod_ref[0, :, 5 * D_MODEL:6 * D_MODEL]
    keep = {}

    def branch_issue(r):
        rows = slice(r * sub, (r + 1) * sub)
        return _dot(yh_ref[0, rows, :], wbh_ref[...]), _dot(ya_ref[0, rows, :], wba_ref[...])

    def branch_finish(raw, r):
        rows = slice(r * sub, (r + 1) * sub)
        a, b = raw
        mixed = (gates_ref[0, rows, 0:D_MODEL].astype(F32) * a
                 + gates_ref[0, rows, D_MODEL:2 * D_MODEL].astype(F32) * b)
        keep["mixed", r] = (0.5 * mixed).astype(BF16)

    def out_issue(r):
        return _dot(keep["mixed", r], wo_ref[...])

    def out_finish(raw, r):
        x1 = x_ref[0, r * sub:(r + 1) * sub, :] + g1 * raw
        ms = jnp.mean(x1 * x1, axis=-1, keepdims=True)
        keep["x1", r] = x1
        keep["h2", r] = ((x1 * lax.rsqrt(ms + EPS) * nw_ref[...]) * (1.0 + sc2) + sh2).astype(BF16)

    def hidden_issue(r, c):
        h2 = keep["h2", r]
        cols = slice(bounds[c], bounds[c + 1])
        return _dot(h2, wg_ref[:, cols]), _dot(h2, wu_ref[:, cols])

    def hidden_finish(raw, r, c):
        gate, up = raw
        keep["act", r, c] = (gate * _sigmoid(gate) * up).astype(BF16)

    def down_issue(r, c):
        return _dot(keep["act", r, c], wd_ref[bounds[c]:bounds[c + 1], :])

    def down_finish(raw, r, c):
        dn = raw if c == 0 else keep["dn", r] + raw
        if c + 1 < n_chunks:
            keep["dn", r] = dn
        else:
            o_ref[0, r * sub:(r + 1) * sub, :] = keep["x1", r] + g2 * dn

    part = functools.partial
    stages = []
    for issue, finish in ((branch_issue, branch_finish), (out_issue, out_finish)):
        stages += [(part(issue, r), part(finish, r=r)) for r in range(n_sub)]
    for c in range(n_chunks):
        stages += [(part(hidden_issue, r, c), part(hidden_finish, r=r, c=c)) for r in range(n_sub)]
        stages += [(part(down_issue, r, c), part(down_finish, r=r, c=c)) for r in range(n_sub)]
    _emit_pipelined(stages, n_sub - 1)


def _merge_ffn_call(x, yh, ya, gates, mod3, nw, wbh, wba, wo, wg, wu, wd):
    b_, s_, _ = x.shape
    tm = TOK_TILE
    tok = lambda w: pl.BlockSpec((1, tm, w), lambda b, i: (b, i, 0))
    return pl.pallas_call(
        _merge_ffn_kernel,
        grid=(b_, s_ // tm),
        in_specs=[tok(D_MODEL), tok(HG_WIDTH), tok(ATT_WIDTH), tok(2 * D_MODEL),
                  pl.BlockSpec((1, 1, mod3.shape[2]), lambda b, i: (b, 0, 0)),
                  _const_spec((1, D_MODEL)),
                  _const_spec(wbh.shape), _const_spec(wba.shape), _const_spec(wo.shape),
                  _const_spec(wg.shape), _const_spec(wu.shape), _const_spec(wd.shape)],
        out_specs=tok(D_MODEL),
        out_shape=jax.ShapeDtypeStruct(x.shape, F32),
        compiler_params=pltpu.CompilerParams(dimension_semantics=("parallel", "parallel"),
                                             vmem_limit_bytes=VMEM_LIMIT),
        name="merge_ffn",
    )(x, yh, ya, gates, mod3, nw, wbh, wba, wo, wg, wu, wd)


def _rope_tables(n_tok):
    t = np.arange(n_tok)
    rows = (t // GRID_W).astype(np.float64)
    cols = (t % GRID_W).astype(np.float64)
    half = HEAD_DIM // 2
    inv_freq = ROPE_THETA ** (-np.arange(0, half, 2, dtype=np.float64) / half)
    d = np.arange(LANES) % HEAD_DIM
    pos = np.where((d < half)[None, :], rows[:, None], cols[:, None])
    ang = pos * inv_freq[(d % half) % (half // 2)][None, :]
    sign = np.where((d % half) < half // 2, -1.0, 1.0)[None, :]
    return jnp.asarray(np.cos(ang), F32), jnp.asarray(np.sin(ang) * sign, F32)


def kernel(x, c, ctx, c_ctx, w_ada, b_ada, norm_mix_w, norm_ffn_w, w_in, hgrn_lb_logits, hgrn_norm_w,
           q_norm_w, k_norm_w, attn_sinks, w_branch_hgrn, w_branch_attn, w_out, w_ffn_gate, w_ffn_up,
           w_ffn_down):
    b_, s_, _ = x.shape
    layer = 0
    col = np.ones((1, IN_COLS), np.float32)
    for lo, width in ((C_FF, 2 * HG_WIDTH), (C_QHG, 2 * HG_WIDTH), (C_GATES, 2 * D_MODEL)):
        col[:, lo:lo + width] = 0.5
    mod, w_in_b = _ada_call(c, c_ctx[None, :], w_ada[layer], b_ada[layer][None, :], w_in[layer],
                            jnp.asarray(col))

    lbl = hgrn_lb_logits[:, 0:2, :]
    qw = q_norm_w[layer][None, :]
    kw = k_norm_w[layer][None, :]
    nw_mix = norm_mix_w[layer][None, :]
    cos_t, sin_t = _rope_tables(s_)

    (hg, dec, vt, v, g, q, katt, vatt, gates) = _inproj_call(
        x, mod, 0, nw_mix, w_in_b, lbl, qw, kw, cos_t, sin_t, latent=True)
    l_ = ctx.shape[1]
    pack = max(1, min(CTX_TILE // l_, b_))
    ctx_out = _inproj_call(ctx.reshape(b_ // pack, pack * l_, D_MODEL), mod, b_, nw_mix, w_in_b, lbl, qw, kw,
                           cos_t, sin_t, latent=False)
    ckd2f, ckd2b, cdec, cvt, ckatt, cvatt = [t.reshape((b_, t.shape[1] // pack) + t.shape[2:])
                                             for t in ctx_out]

    y_hg = _hgrn_call(hg, v, vt, dec, ckd2f, ckd2b, cvt, cdec, g, hgrn_norm_w[layer][None, :])
    y_at, (w_bh, w_ba, w_o, w_g, w_u, w_d) = _attn_call(
        attn_sinks[layer], q, katt, vatt, ckatt, cvatt,
        (w_branch_hgrn[layer], w_branch_attn[layer], w_out[layer], w_ffn_gate[layer], w_ffn_up[layer],
         w_ffn_down[layer]))

    return _merge_ffn_call(x, y_hg, y_at, gates, mod, norm_ffn_w[layer][None, :],
                           w_bh, w_ba, w_o, w_g, w_u, w_d)
```

```python
import functools

import jax
import jax.numpy as jnp
import numpy as np
from jax import lax
from jax.experimental import pallas as pl
from jax.experimental.pallas import tpu as pltpu

F32 = jnp.float32
BF16 = jnp.bfloat16

D_MODEL = 1024
GRID_W = 64
EPS = 1e-6
HG_HEADS = 4
HG_DIM = 128
HG_WIDTH = HG_HEADS * HG_DIM
CHUNK = 32
ATT_HEADS = 8
ATT_KV_HEADS = 2
HEAD_DIM = 64
ATT_WIDTH = ATT_HEADS * HEAD_DIM
KV_WIDTH = ATT_KV_HEADS * HEAD_DIM
WINDOW = 128
ROPE_THETA = 10000.0
D_FF = 2816
CTX_COLS = 3 * HG_WIDTH + 2 * KV_WIDTH
IN_COLS = CTX_COLS + 2 * HG_WIDTH + ATT_WIDTH + 2 * D_MODEL

C_FF, C_FB, C_INP, C_K, C_V = 0, HG_WIDTH, 2 * HG_WIDTH, 3 * HG_WIDTH, 3 * HG_WIDTH + KV_WIDTH
C_QHG = CTX_COLS
C_GHG = C_QHG + HG_WIDTH
C_Q = C_GHG + HG_WIDTH
C_GATES = C_Q + ATT_WIDTH

LANES = 128
TOK_TILE = 512
FFN_SUBTILES = 2
MXU_DEPTH = 256
FFN_CHUNK_BOUNDS = (0, 6 * MXU_DEPTH, D_FF)
INPROJ_TILE = 256
CTX_TILE = 512
INPROJ_DEPTH = 2
HG_TILE = 512
ATT_BLOCK = 128
ATT_GROUP_HEADS = 2
ATT_DEPTH = 1
ATT_STEP_BLOCKS = 8
ADA_ROWS = 16
ADA_STEPS = 6
VMEM_LIMIT = 56 * 1024 * 1024
NEG = -1e30
LOG2E = 1.4426950408889634

assert ATT_BLOCK == WINDOW == LANES and HG_DIM == LANES and 2 * HEAD_DIM == LANES
assert ATT_GROUP_HEADS % 2 == 0 and (ATT_HEADS // ATT_KV_HEADS) % ATT_GROUP_HEADS == 0


def _dot(a, b):
    return jnp.dot(a, b, preferred_element_type=F32)


def _dot_nt(a, b):
    return lax.dot_general(a, b, (((1,), (1,)), ((), ())), preferred_element_type=F32)


def _sigmoid(x):
    return 0.5 * jnp.tanh(0.5 * x) + 0.5


def _emit_pipelined(stages, depth):
    raws = [issue() for issue, _ in stages[:depth]]
    for i, (_, finish) in enumerate(stages):
        if i + depth < len(stages):
            raws.append(stages[i + depth][0]())
        finish(raws[i])
        raws[i] = None


def _const_spec(shape):
    n = len(shape)
    return pl.BlockSpec(shape, lambda *_: (0,) * n, pipeline_mode=pl.Buffered(1))


def _ada_kernel(c_ref, cctx_ref, w_ref, b_ref, win_ref, col_ref, o_ref, winb_ref):
    rows = o_ref.shape[0]
    n_cond = c_ref.shape[0] + 1
    c = jnp.concatenate([c_ref[...], cctx_ref[...], jnp.zeros((rows - n_cond, D_MODEL), F32)], axis=0)
    s = (c * _sigmoid(c)).astype(BF16)
    mod = _dot(s, w_ref[...].astype(BF16)) + b_ref[...]
    for r in range(mod.shape[0]):
        o_ref[r] = mod[r:r + 1, :]
    winb_ref[...] = (win_ref[...] * col_ref[...]).astype(BF16)


def _ada_call(c, c_ctx, w_ada, b_ada, w_in, col):
    rows = ADA_ROWS
    n_out = w_ada.shape[1]
    steps = ADA_STEPS
    bn = n_out // steps
    bw = w_in.shape[1] // steps
    return pl.pallas_call(
        _ada_kernel,
        grid=(steps,),
        in_specs=[pl.BlockSpec(c.shape, lambda j: (0, 0)),
                  pl.BlockSpec((1, D_MODEL), lambda j: (0, 0)),
                  pl.BlockSpec((D_MODEL, bn), lambda j: (0, j)),
                  pl.BlockSpec((1, bn), lambda j: (0, j)),
                  pl.BlockSpec((D_MODEL, bw), lambda j: (0, j)),
                  pl.BlockSpec((1, bw), lambda j: (0, j))],
        out_specs=[pl.BlockSpec((rows, 1, bn), lambda j: (0, 0, j)),
                   pl.BlockSpec((D_MODEL, bw), lambda j: (0, j))],
        out_shape=[jax.ShapeDtypeStruct((rows, 1, n_out), F32),
                   jax.ShapeDtypeStruct(w_in.shape, BF16)],
        compiler_params=pltpu.CompilerParams(dimension_semantics=("arbitrary",),
                                             vmem_limit_bytes=VMEM_LIMIT),
        name="ada_mod",
    )(c, c_ctx, w_ada, b_ada, w_in, col)


def _chunk_cumsum(x, reverse):
    n, c = x.shape
    sub = 8
    r = lax.broadcasted_iota(jnp.int32, x.shape, 0) & (CHUNK - 1)
    s = 1
    while s < sub:
        if reverse:
            x = x + jnp.where(r < CHUNK - s, pltpu.roll(x, n - s, 0), 0.0)
        else:
            x = x + jnp.where(r >= s, pltpu.roll(x, s, 0), 0.0)
        s *= 2
    per = CHUNK // sub
    x4 = x.reshape(n // CHUNK, per, sub, c)
    s = 1
    while s < per:
        if reverse:
            x4 = jnp.concatenate([x4[:, :per - s] + x4[:, s:], x4[:, per - s:]], axis=1)
        else:
            x4 = jnp.concatenate([x4[:, :s], x4[:, s:] + x4[:, :per - s]], axis=1)
        s *= 2
    return x4.reshape(n, c)


def _rope(t, cos, sin_signed, lane):
    partner = jnp.where((lane & 31) < 16, pltpu.roll(t, LANES - 16, 1), pltpu.roll(t, 16, 1))
    return t * cos + partner * sin_signed


def _inproj_kernel(x_ref, mod_ref, nw_ref, w_ref, lbl_ref, qw_ref, kw_ref, cos_ref, sin_ref,
                   *out_refs, latent):
    if latent:
        hg_ref, dec_ref, vt_ref, v_ref, g_ref, q_ref, katt_ref, vatt_ref, gates_ref = out_refs
    else:
        kd2f_ref, kd2b_ref, dec_ref, vt_ref, katt_ref, vatt_ref = out_refs

    tm = x_ref.shape[1]
    nchunk = tm // CHUNK
    x = x_ref[0]
    sh = mod_ref[0, :, 0:D_MODEL]
    sc = mod_ref[0, :, D_MODEL:2 * D_MODEL]
    ms = jnp.mean(x * x, axis=-1, keepdims=True)
    h = x * lax.rsqrt(ms + EPS) * (nw_ref[...] * (1.0 + sc)) + sh
    hb = h.astype(BF16)

    def mm(a, b):
        return _dot(hb, w_ref[:, a:b])

    l0f, l1f = lbl_ref[0, 0:1, :], lbl_ref[0, 1:2, :]
    l0b, l1b = lbl_ref[1, 0:1, :], lbl_ref[1, 1:2, :]
    lb_f = 1.0 / (1.0 + jnp.exp(l1f - l0f))
    lb_b = 1.0 / (1.0 + jnp.exp(l1b - l0b))

    qw = jnp.concatenate([qw_ref[...]] * (LANES // HEAD_DIM), axis=1)
    kw = jnp.concatenate([kw_ref[...]] * (LANES // HEAD_DIM), axis=1)
    gw = HG_WIDTH // 2
    sw = D_MODEL // 2
    lane = lax.broadcasted_iota(jnp.int32, (tm, LANES), 1)
    left = lane < HEAD_DIM
    bi = lax.broadcasted_iota(jnp.int32, (2 * LANES, LANES), 0) // HEAD_DIM
    bj = lax.broadcasted_iota(jnp.int32, (2 * LANES, LANES), 1) // HEAD_DIM
    ones_blk2 = jnp.where((bi % 2) == bj, 1.0, 0.0).astype(BF16)
    keep = {}

    def qhg_finish(raw):
        keep["q_hg"] = raw * (jnp.tanh(raw) + 1.0) * (HG_DIM ** -0.5)

    def gate_finish(raw, direction, part):
        reverse = direction == 1
        lo = part * gw
        lb = (lb_f, lb_b)[direction][:, lo:lo + gw]
        c_half = 0.5 - 0.5 * lb
        ct = c_half * jnp.tanh(raw)
        f = (0.5 + 0.5 * lb) + ct
        k = c_half - ct
        cum = _chunk_cumsum(jnp.log(f), reverse)
        cum3 = cum.reshape(nchunk, CHUNK, gw)
        tot3 = cum3[:, 0:1, :] if reverse else cum3[:, CHUNK - 1:CHUNK, :]
        dec3 = jnp.exp(tot3)
        dec_ref[0, :, direction * HG_WIDTH + lo:direction * HG_WIDTH + lo + gw] = dec3.reshape(nchunk, gw)
        kd2 = (k.reshape(nchunk, CHUNK, gw) * jnp.exp(tot3 - cum3)).reshape(tm, gw).astype(BF16)
        if latent:
            base = direction * 3 * HG_WIDTH + lo
            hg_ref[0, :, base:base + gw] = (keep["q_hg"][:, lo:lo + gw] * jnp.exp(cum)).astype(BF16)
            hg_ref[0, :, base + HG_WIDTH:base + HG_WIDTH + gw] = (k * jnp.exp(-cum)).astype(BF16)
            hg_ref[0, :, base + 2 * HG_WIDTH:base + 2 * HG_WIDTH + gw] = kd2
        else:
            (kd2f_ref, kd2b_ref)[direction][0, :, lo:lo + gw] = kd2

    def mgate_finish(raw, j):
        gates_ref[0, :, j * sw:(j + 1) * sw] = (jnp.tanh(raw) + 1.0).astype(BF16)

    def v_finish(raw):
        for s in range(tm // LANES):
            vt_ref[0, s] = raw[s * LANES:(s + 1) * LANES, :].T.astype(BF16)
        if latent:
            v_ref[0] = raw.astype(BF16)

    def g_finish(raw):
        g_ref[0] = (raw * (jnp.tanh(raw) + 1.0)).astype(BF16)

    def split_sq(t):
        sq = t * t
        hi = sq.astype(BF16)
        lo = (sq - hi.astype(F32)).astype(BF16)
        return jnp.concatenate([hi, lo], axis=1)

    def kv_finish(raw):
        keep["k_raw"] = raw[:, 0:LANES]
        keep["k_split"] = split_sq(keep["k_raw"])
        vatt_ref[0] = raw[:, LANES:2 * LANES].astype(BF16)

    def knorm_finish(ss):
        kn = keep["k_raw"] * lax.rsqrt(ss * (1.0 / HEAD_DIM) + EPS) * kw
        if latent:
            kn = _rope(kn, cos_ref[...], sin_ref[...], lane)
        for s in range(tm // LANES):
            katt_ref[0, s] = kn[s * LANES:(s + 1) * LANES, :].T.astype(BF16)

    def q_finish(raw):
        keep["q_raw"] = raw
        keep["q_split"] = [split_sq(raw[:, s * LANES:(s + 1) * LANES]) for s in range(ATT_WIDTH // LANES)]

    def qnorm_finish(ss_list):
        for s, ss in enumerate(ss_list):
            qraw = keep["q_raw"][:, s * LANES:(s + 1) * LANES]
            qn = qraw * lax.rsqrt(ss * (1.0 / HEAD_DIM) + EPS) * qw
            qn = _rope(qn, cos_ref[...], sin_ref[...], lane) * (HEAD_DIM ** -0.5 * LOG2E)
            swapped = pltpu.roll(qn, HEAD_DIM, 1)
            if s < ATT_WIDTH // LANES // 2:
                even, odd = jnp.where(left, qn, 0.0), jnp.where(left, swapped, 0.0)
            else:
                even, odd = jnp.where(left, 0.0, swapped), jnp.where(left, 0.0, qn)
            q_ref[0, :, (2 * s) * LANES:(2 * s + 1) * LANES] = even.astype(BF16)
            q_ref[0, :, (2 * s + 1) * LANES:(2 * s + 2) * LANES] = odd.astype(BF16)

    def gate_stage(direction, part):
        col = (C_FF, C_FB)[direction] + part * gw
        return (lambda: mm(col, col + gw),
                functools.partial(gate_finish, direction=direction, part=part))

    def mgate_stage(j):
        c0 = C_GATES + j * sw
        return (lambda: mm(c0, c0 + sw), functools.partial(mgate_finish, j=j))

    v_stage = (lambda: mm(C_INP, C_INP + HG_WIDTH), v_finish)
    kv_stage = (lambda: mm(C_K, C_K + 2 * KV_WIDTH), kv_finish)
    knorm_stage = (lambda: _dot(keep["k_split"], ones_blk2), knorm_finish)
    if latent:
        stages = [(lambda: mm(C_QHG, C_QHG + HG_WIDTH), qhg_finish)]
        for j in range(4):
            stages += [gate_stage(j // 2, j % 2), mgate_stage(j)]
        stages += [(lambda: mm(C_Q, C_Q + ATT_WIDTH), q_finish),
                   v_stage,
                   kv_stage,
                   (lambda: [_dot(t, ones_blk2) for t in keep["q_split"]], qnorm_finish),
                   (lambda: mm(C_GHG, C_GHG + HG_WIDTH), g_finish),
                   knorm_stage]
    else:
        stages = [gate_stage(0, 0), gate_stage(0, 1), kv_stage, gate_stage(1, 0), gate_stage(1, 1),
                  knorm_stage, v_stage]
    _emit_pipelined(stages, INPROJ_DEPTH)


def _inproj_call(x, mod3, mod_row0, nw, w_in, lbl, qw, kw, cos_t, sin_t, latent):
    b_, s_, _ = x.shape
    tm = min(INPROJ_TILE if latent else CTX_TILE, s_)
    grid = (b_, s_ // tm)
    tok = lambda w: pl.BlockSpec((1, tm, w), lambda b, i: (b, i, 0))
    mod_map = (lambda b, i: (mod_row0 + b, 0, 0)) if latent else (lambda b, i: (mod_row0, 0, 0))
    ncols = IN_COLS if latent else CTX_COLS
    in_specs = [tok(D_MODEL),
                pl.BlockSpec((1, 1, mod3.shape[2]), mod_map),
                _const_spec((1, D_MODEL)),
                pl.BlockSpec((D_MODEL, ncols), lambda b, i: (0, 0), pipeline_mode=pl.Buffered(1)),
                _const_spec(lbl.shape),
                _const_spec((1, HEAD_DIM)),
                _const_spec((1, HEAD_DIM)),
                pl.BlockSpec((tm, LANES), lambda b, i: (i, 0)),
                pl.BlockSpec((tm, LANES), lambda b, i: (i, 0))]
    bf = lambda w: jax.ShapeDtypeStruct((b_, s_, w), BF16)
    dec_shape = jax.ShapeDtypeStruct((b_, s_ // CHUNK, 2 * HG_WIDTH), F32)
    dec_spec = pl.BlockSpec((1, tm // CHUNK, 2 * HG_WIDTH), lambda b, i: (b, i, 0))
    vt_shape = jax.ShapeDtypeStruct((b_, s_ // LANES, HG_WIDTH, LANES), BF16)
    vt_spec = pl.BlockSpec((1, tm // LANES, HG_WIDTH, LANES), lambda b, i: (b, i, 0, 0))
    katt_shape = jax.ShapeDtypeStruct((b_, s_ // LANES, KV_WIDTH, LANES), BF16)
    katt_spec = pl.BlockSpec((1, tm // LANES, KV_WIDTH, LANES), lambda b, i: (b, i, 0, 0))
    if latent:
        out_shape = [bf(6 * HG_WIDTH), dec_shape, vt_shape, bf(HG_WIDTH), bf(HG_WIDTH),
                                         bf(ATT_HEADS * LANES), katt_shape, bf(KV_WIDTH), bf(2 * D_MODEL)]
        out_specs = [tok(6 * HG_WIDTH), dec_spec, vt_spec, tok(HG_WIDTH), tok(HG_WIDTH),
                                          tok(ATT_HEADS * LANES), katt_spec, tok(KV_WIDTH), tok(2 * D_MODEL)]
    else:
        out_shape = [bf(HG_WIDTH)] * 2 + [dec_shape, vt_shape, katt_shape, bf(KV_WIDTH)]
        out_specs = [tok(HG_WIDTH)] * 2 + [dec_spec, vt_spec, katt_spec, tok(KV_WIDTH)]
    return pl.pallas_call(
        functools.partial(_inproj_kernel, latent=latent),
        grid=grid, in_specs=in_specs, out_specs=out_specs, out_shape=out_shape,
        compiler_params=pltpu.CompilerParams(dimension_semantics=("parallel", "parallel"),
                                             vmem_limit_bytes=VMEM_LIMIT),
        name="inproj_latent" if latent else "inproj_ctx",
    )(x, mod3, nw, w_in, lbl, qw, kw, cos_t, sin_t)


def _cast_side_job(w_refs, wb_refs):
    for w_ref, wb_ref in zip(w_refs, wb_refs):
        wb_ref[...] = w_ref[...].astype(BF16)


def _side_job_specs(weights, n_steps, step_of):
    specs = [pl.BlockSpec((wt.shape[0] // n_steps, wt.shape[1]), lambda *ids: (step_of(*ids), 0))
             for wt in weights]
    shapes = [jax.ShapeDtypeStruct(wt.shape, BF16) for wt in weights]
    return specs, shapes


def _hgrn_kernel(qdf_ref, kdf_ref, kd2f_ref, qdb_ref, kdb_ref, kd2b_ref, vf_ref, vb_ref, vtf_ref, vtb_ref,
                 decf_ref, decb_ref, ckd2f_ref, ckd2b_ref, cvt_ref, cdecf_ref, cdecb_ref,
                 g_ref, nw_ref, y_ref, o_acc, st_ref):
    seq = g_ref.shape[1]
    ctx_len = ckd2f_ref.shape[1]
    heads = g_ref.shape[2] // HG_DIM
    tile = HG_TILE
    cpt = tile // CHUNK
    cps = LANES // CHUNK
    spt = tile // LANES
    n_tiles = seq // tile
    half = n_tiles // 2
    it = pl.program_id(1)

    row_chunk = lax.broadcasted_iota(jnp.int32, (LANES, HG_DIM), 0) // CHUNK
    ti = lax.broadcasted_iota(jnp.int32, (LANES, LANES), 0)
    tj = lax.broadcasted_iota(jnp.int32, (LANES, LANES), 1)
    same = (ti // CHUNK) == (tj // CHUNK)
    mask_f = same & (tj <= ti)
    mask_b = same & (tj >= ti)

    def chunk_update(vt_slab, k_slab, n_in_slab):
        k_m = jnp.where(row_chunk == n_in_slab, k_slab, jnp.zeros_like(k_slab))
        return _dot(vt_slab, k_m)

    def ctx_state(ckd2_ref, cdec_ref, cols, reverse):
        st = jnp.zeros((HG_DIM, HG_DIM), F32)
        order = range(ctx_len // CHUNK)
        for n in (reversed(order) if reverse else order):
            slab = n // cps
            st = cdec_ref[0, n:n + 1, cols] * st + chunk_update(
                cvt_ref[0, slab, cols, :], ckd2_ref[0, slab * LANES:(slab + 1) * LANES, cols], n % cps)
        return st

    def emit(o_t, cols, rows, finalize):
        if finalize:
            o_t = o_t + o_acc[rows, cols]
            ms = jnp.mean(o_t * o_t, axis=-1, keepdims=True)
            y = o_t * lax.rsqrt(ms + EPS) * nw_ref[...] * g_ref[0, rows, cols].astype(F32)
            y_ref[0, rows, cols] = y.astype(BF16)
        else:
            o_acc[rows, cols] = o_t

    def body(finalize):
        chains = []
        for h in range(heads):
            cols = slice(h * HG_DIM, (h + 1) * HG_DIM)
            chains.append((qdf_ref, kdf_ref, decf_ref, cols, it, False, kd2f_ref, vf_ref, vtf_ref))
            chains.append((qdb_ref, kdb_ref, decb_ref, cols, n_tiles - 1 - it, True, kd2b_ref, vb_ref,
                           vtb_ref))
        rows = [pl.ds(pl.multiple_of(c[4] * tile, tile), tile) for c in chains]
        orders = [list(reversed(range(cpt))) if c[5] else list(range(cpt)) for c in chains]
        qd, sc, ups = [], [], []
        for qd_ref, kd_ref, _, cols, _, _, kd2_ref, _, vt_ref in chains:
            qd.append(qd_ref[0, :, cols])
            kd = kd_ref[0, :, cols]
            sc.append([_dot_nt(qd[-1][s * LANES:(s + 1) * LANES, :], kd[s * LANES:(s + 1) * LANES, :])
                       for s in range(spt)])
            kd2 = kd2_ref[0, :, cols]
            ups.append([chunk_update(vt_ref[0, n // cps, cols, :],
                                     kd2[(n // cps) * LANES:(n // cps + 1) * LANES, :], n % cps)
                        for n in range(cpt)])
        o = []
        for c, s in zip(chains, sc):
            v = c[7][0, :, c[3]]
            o.append(jnp.concatenate(
                [_dot(jnp.where(mask_b if c[5] else mask_f, s[j], 0.0).astype(BF16),
                      v[j * LANES:(j + 1) * LANES, :]) for j in range(spt)], axis=0))
        starts = []
        for i, c in enumerate(chains):
            dec = c[2][0, :, c[3]]
            st = st_ref[i]
            start = [None] * cpt
            for n in orders[i]:
                start[n] = st.T.astype(BF16)
                st = dec[n:n + 1, :] * st + ups[i][n]
            starts.append(start)
            st_ref[i] = st
        for i, c in enumerate(chains):
            outs = [o[i][n * CHUNK:(n + 1) * CHUNK, :]
                    + _dot(qd[i][n * CHUNK:(n + 1) * CHUNK, :], starts[i][n]) for n in range(cpt)]
            emit(jnp.concatenate(outs, axis=0), c[3], rows[i], finalize)

    @pl.when(it == 0)
    def _():
        for h in range(heads):
            cols = slice(h * HG_DIM, (h + 1) * HG_DIM)
            st_ref[2 * h] = ctx_state(ckd2f_ref, cdecf_ref, cols, False)
            st_ref[2 * h + 1] = ctx_state(ckd2b_ref, cdecb_ref, cols, True)

    pl.when(it < half)(functools.partial(body, False))
    pl.when(it >= half)(functools.partial(body, True))


def _hgrn_call(hg, v, vt, dec, ckd2f, ckd2b, cvt, cdec, g, nw):
    b_, s_, _ = v.shape
    l_ = ckd2f.shape[1]
    w = HG_WIDTH
    n_tiles = s_ // HG_TILE
    fwd = lambda b, t: t
    bwd = lambda b, t: n_tiles - 1 - t
    tile = lambda at, j=0: pl.BlockSpec((1, HG_TILE, w), lambda b, t: (b, at(b, t), j))
    vt_tile = lambda at: pl.BlockSpec((1, HG_TILE // LANES, w, LANES), lambda b, t: (b, at(b, t), 0, 0))
    dec_tile = lambda at, j: pl.BlockSpec((1, HG_TILE // CHUNK, w), lambda b, t: (b, at(b, t), j))
    whole = lambda n: pl.BlockSpec((1, n, w), lambda b, t: (b, 0, 0))
    cdec_spec = lambda j: pl.BlockSpec((1, l_ // CHUNK, w), lambda b, t: (b, 0, j))
    return pl.pallas_call(
        _hgrn_kernel,
        grid=(b_, n_tiles),
        in_specs=[tile(fwd, j) for j in range(3)] + [tile(bwd, j) for j in range(3, 6)]
        + [tile(fwd), tile(bwd), vt_tile(fwd), vt_tile(bwd), dec_tile(fwd, 0), dec_tile(bwd, 1),
           whole(l_), whole(l_), pl.BlockSpec((1, l_ // LANES, w, LANES), lambda b, t: (b, 0, 0, 0)),
           cdec_spec(0), cdec_spec(1), whole(s_), _const_spec((1, HG_DIM))],
        out_specs=whole(s_),
        out_shape=jax.ShapeDtypeStruct((b_, s_, HG_WIDTH), BF16),
        scratch_shapes=[pltpu.VMEM((s_, w), F32), pltpu.VMEM((2 * HG_HEADS, HG_DIM, HG_DIM), F32)],
        compiler_params=pltpu.CompilerParams(dimension_semantics=("parallel", "arbitrary"),
                                             vmem_limit_bytes=VMEM_LIMIT),
        name="hgrn_scan",
    )(hg, hg, hg, hg, hg, hg, v, v, vt, vt, dec, dec, ckd2f, ckd2b, cvt, cdec, cdec, g, nw)


def _attn_kernel(sink_ref, q_ref, kt_ref, kc_ref, v_ref, vc_ref, *rest):
    n_w = (len(rest) - 1) // 2
    y_ref = rest[n_w]
    _cast_side_job(rest[:n_w], rest[n_w + 1:])
    nb = kt_ref.shape[1]
    blk = ATT_BLOCK
    hpg = ATT_GROUP_HEADS
    n_groups = ATT_HEADS // hpg
    qry_r = lax.broadcasted_iota(jnp.int32, (blk, blk), 0)
    key_c = lax.broadcasted_iota(jnp.int32, (blk, blk), 1)
    left = lax.broadcasted_iota(jnp.int32, (blk, LANES), 1) < HEAD_DIM
    ctx_keys = [kc_ref[0, c] for c in range(kc_ref.shape[1])]
    keep = {}

    def band(u):
        qb = pl.program_id(1) * ATT_STEP_BLOCKS + u
        ids = (jnp.maximum(qb - 1, 0), qb, jnp.minimum(qb + 1, nb - 1))
        keys_t = jnp.concatenate([kt_ref[0, j] for j in ids] + ctx_keys, axis=1)
        values = jnp.concatenate([v_ref[0, pl.ds(pl.multiple_of(j * blk, blk), blk), :] for j in ids]
                                 + [vc_ref[0]], axis=0)
        values = jnp.concatenate([values, jnp.ones_like(values)], axis=1)
        return qb, keys_t, values

    def score_issue(u, g):
        if g == 0:
            keep["band", u] = band(u)
        q = jnp.concatenate([q_ref[0, u * blk:(u + 1) * blk, h * LANES:(h + 1) * LANES]
                             for h in range(g * hpg, (g + 1) * hpg)], axis=0)
        return _dot(q, keep["band", u][1])

    def score_finish(raw, u, g):
        qb = keep["band", u][0]
        mask_prev = (key_c >= qry_r) & (qb > 0)
        mask_next = (key_c <= qry_r) & (qb < nb - 1)
        probs, sink_p = [], []
        for n in range(hpg):
            s = raw[n * blk:(n + 1) * blk, :]
            sink = sink_ref[g * hpg + n] * LOG2E
            s = jnp.concatenate([jnp.where(mask_prev, s[:, 0:blk], NEG), s[:, blk:2 * blk],
                                 jnp.where(mask_next, s[:, 2 * blk:3 * blk], NEG), s[:, 3 * blk:]], axis=1)
            mx = jnp.maximum(jnp.max(s, axis=1, keepdims=True), sink)
            probs.append(jnp.exp2(s - mx).astype(BF16))
            sink_p.append(jnp.exp2(sink - mx))
        keep["p", u, g] = jnp.concatenate(probs, axis=0)
        keep["sink_p", u, g] = sink_p

    def value_issue(u, g):
        return _dot(keep["p", u, g], keep["band", u][2])

    def value_finish(raw, u, g):
        for n in range(0, hpg, 2):
            h = g * hpg + n
            even, odd = [raw[m * blk:(m + 1) * blk, 0:LANES]
                         / (raw[m * blk:(m + 1) * blk, LANES:2 * LANES] + keep["sink_p", u, g][m])
                         for m in (n, n + 1)]
            if h < ATT_HEADS // ATT_KV_HEADS:
                y = jnp.where(left, even, pltpu.roll(odd, HEAD_DIM, 1))
            else:
                y = jnp.where(left, pltpu.roll(even, HEAD_DIM, 1), odd)
            y_ref[0, u * blk:(u + 1) * blk, (h // 2) * LANES:(h // 2 + 1) * LANES] = y.astype(BF16)

    part = functools.partial
    work = [(u, g) for u in range(ATT_STEP_BLOCKS) for g in range(n_groups)]
    score = lambda u, g: (part(score_issue, u, g), part(score_finish, u=u, g=g))
    stages = [score(*w) for w in work[:ATT_DEPTH + 1]]
    for n, (u, g) in enumerate(work):
        stages.append((part(value_issue, u, g), part(value_finish, u=u, g=g)))
        if n + ATT_DEPTH + 1 < len(work):
            stages.append(score(*work[n + ATT_DEPTH + 1]))
    _emit_pipelined(stages, ATT_DEPTH)


def _attn_call(sinks, q, katt, vatt, ckatt, cvatt, weights):
    b_, s_, qw = q.shape
    l_ = cvatt.shape[1]
    rows = ATT_STEP_BLOCKS * ATT_BLOCK
    batch4 = lambda n: pl.BlockSpec((1, n, KV_WIDTH, LANES), lambda b, i: (b, 0, 0, 0))
    batch3 = lambda n: pl.BlockSpec((1, n, KV_WIDTH), lambda b, i: (b, 0, 0))
    steps = s_ // rows
    w_specs, w_shapes = _side_job_specs(weights, b_ * steps, lambda b, i: b * steps + i)
    outs = pl.pallas_call(
        _attn_kernel,
        grid=(b_, steps),
        in_specs=[pl.BlockSpec(memory_space=pltpu.SMEM),
                  pl.BlockSpec((1, rows, qw), lambda b, i: (b, i, 0)),
                  batch4(s_ // LANES), batch4(l_ // LANES), batch3(s_), batch3(l_)] + w_specs,
        out_specs=[pl.BlockSpec((1, rows, ATT_WIDTH), lambda b, i: (b, i, 0))] + w_specs,
        out_shape=[jax.ShapeDtypeStruct((b_, s_, ATT_WIDTH), BF16)] + w_shapes,
        compiler_params=pltpu.CompilerParams(dimension_semantics=("parallel", "arbitrary"),
                                             vmem_limit_bytes=VMEM_LIMIT),
        name="window_attn",
    )(sinks, q, katt, ckatt, vatt, cvatt, *weights)
    return outs[0], outs[1:]


def _merge_ffn_kernel(x_ref, yh_ref, ya_ref, gates_ref, mod_ref, nw_ref, wbh_ref, wba_ref, wo_ref,
                      wg_ref, wu_ref, wd_ref, o_ref):
    tm = x_ref.shape[1]
    n_sub = FFN_SUBTILES
    sub = tm // n_sub
    bounds = FFN_CHUNK_BOUNDS
    n_chunks = len(bounds) - 1
    g1 = mod_ref[0, :, 2 * D_MODEL:3 * D_MODEL]
    sh2 = mod_ref[0, :, 3 * D_MODEL:4 * D_MODEL]
    sc2 = mod_ref[0, :, 4 * D_MODEL:5 * D_MODEL]
    g2 = mod_ref[0, :, 5 * D_MODEL:6 * D_MODEL]
    keep = {}

    def branch_issue(r):
        rows = slice(r * sub, (r + 1) * sub)
        return _dot(yh_ref[0, rows, :], wbh_ref[...]), _dot(ya_ref[0, rows, :], wba_ref[...])

    def branch_finish(raw, r):
        rows = slice(r * sub, (r + 1) * sub)
        a, b = raw
        mixed = (gates_ref[0, rows, 0:D_MODEL].astype(F32) * a
                 + gates_ref[0, rows, D_MODEL:2 * D_MODEL].astype(F32) * b)
        keep["mixed", r] = (0.5 * mixed).astype(BF16)

    def out_issue(r):
        return _dot(keep["mixed", r], wo_ref[...])

    def out_finish(raw, r):
        x1 = x_ref[0, r * sub:(r + 1) * sub, :] + g1 * raw
        ms = jnp.mean(x1 * x1, axis=-1, keepdims=True)
        keep["x1", r] = x1
        keep["h2", r] = ((x1 * lax.rsqrt(ms + EPS) * nw_ref[...]) * (1.0 + sc2) + sh2).astype(BF16)

    def hidden_issue(r, c):
        h2 = keep["h2", r]
        cols = slice(bounds[c], bounds[c + 1])
        return _dot(h2, wg_ref[:, cols]), _dot(h2, wu_ref[:, cols])

    def hidden_finish(raw, r, c):
        gate, up = raw
        keep["act", r, c] = (gate * _sigmoid(gate) * up).astype(BF16)

    def down_issue(r, c):
        return _dot(keep["act", r, c], wd_ref[bounds[c]:bounds[c + 1], :])

    def down_finish(raw, r, c):
        dn = raw if c == 0 else keep["dn", r] + raw
        if c + 1 < n_chunks:
            keep["dn", r] = dn
        else:
            o_ref[0, r * sub:(r + 1) * sub, :] = keep["x1", r] + g2 * dn

    part = functools.partial
    stages = []
    for issue, finish in ((branch_issue, branch_finish), (out_issue, out_finish)):
        stages += [(part(issue, r), part(finish, r=r)) for r in range(n_sub)]
    for c in range(n_chunks):
        stages += [(part(hidden_issue, r, c), part(hidden_finish, r=r, c=c)) for r in range(n_sub)]
        stages += [(part(down_issue, r, c), part(down_finish, r=r, c=c)) for r in range(n_sub)]
    _emit_pipelined(stages, n_sub - 1)


def _merge_ffn_call(x, yh, ya, gates, mod3, nw, wbh, wba, wo, wg, wu, wd):
    b_, s_, _ = x.shape
    tm = TOK_TILE
    tok = lambda w: pl.BlockSpec((1, tm, w), lambda b, i: (b, i, 0))
    return pl.pallas_call(
        _merge_ffn_kernel,
        grid=(b_, s_ // tm),
        in_specs=[tok(D_MODEL), tok(HG_WIDTH), tok(ATT_WIDTH), tok(2 * D_MODEL),
                  pl.BlockSpec((1, 1, mod3.shape[2]), lambda b, i: (b, 0, 0)),
                  _const_spec((1, D_MODEL)),
                  _const_spec(wbh.shape), _const_spec(wba.shape), _const_spec(wo.shape),
                  _const_spec(wg.shape), _const_spec(wu.shape), _const_spec(wd.shape)],
        out_specs=tok(D_MODEL),
        out_shape=jax.ShapeDtypeStruct(x.shape, F32),
        compiler_params=pltpu.CompilerParams(dimension_semantics=("parallel", "parallel"),
                                             vmem_limit_bytes=VMEM_LIMIT),
        name="merge_ffn",
    )(x, yh, ya, gates, mod3, nw, wbh, wba, wo, wg, wu, wd)


def _rope_tables(n_tok):
    t = np.arange(n_tok)
    rows = (t // GRID_W).astype(np.float64)
    cols = (t % GRID_W).astype(np.float64)
    half = HEAD_DIM // 2
    inv_freq = ROPE_THETA ** (-np.arange(0, half, 2, dtype=np.float64) / half)
    d = np.arange(LANES) % HEAD_DIM
    pos = np.where((d < half)[None, :], rows[:, None], cols[:, None])
    ang = pos * inv_freq[(d % half) % (half // 2)][None, :]
    sign = np.where((d % half) < half // 2, -1.0, 1.0)[None, :]
    return jnp.asarray(np.cos(ang), F32), jnp.asarray(np.sin(ang) * sign, F32)


def kernel(x, c, ctx, c_ctx, w_ada, b_ada, norm_mix_w, norm_ffn_w, w_in, hgrn_lb_logits, hgrn_norm_w,
           q_norm_w, k_norm_w, attn_sinks, w_branch_hgrn, w_branch_attn, w_out, w_ffn_gate, w_ffn_up,
           w_ffn_down):
    b_, s_, _ = x.shape
    layer = 0
    col = np.ones((1, IN_COLS), np.float32)
    for lo, width in ((C_FF, 2 * HG_WIDTH), (C_QHG, 2 * HG_WIDTH), (C_GATES, 2 * D_MODEL)):
        col[:, lo:lo + width] = 0.5
    mod, w_in_b = _ada_call(c, c_ctx[None, :], w_ada[layer], b_ada[layer][None, :], w_in[layer],
                            jnp.asarray(col))

    lbl = hgrn_lb_logits[:, 0:2, :]
    qw = q_norm_w[layer][None, :]
    kw = k_norm_w[layer][None, :]
    nw_mix = norm_mix_w[layer][None, :]
    cos_t, sin_t = _rope_tables(s_)

    (hg, dec, vt, v, g, q, katt, vatt, gates) = _inproj_call(
        x, mod, 0, nw_mix, w_in_b, lbl, qw, kw, cos_t, sin_t, latent=True)
    l_ = ctx.shape[1]
    pack = max(1, min(CTX_TILE // l_, b_))
    ctx_out = _inproj_call(ctx.reshape(b_ // pack, pack * l_, D_MODEL), mod, b_, nw_mix, w_in_b, lbl, qw, kw,
                           cos_t, sin_t, latent=False)
    ckd2f, ckd2b, cdec, cvt, ckatt, cvatt = [t.reshape((b_, t.shape[1] // pack) + t.shape[2:])
                                             for t in ctx_out]

    y_hg = _hgrn_call(hg, v, vt, dec, ckd2f, ckd2b, cvt, cdec, g, hgrn_norm_w[layer][None, :])
    y_at, (w_bh, w_ba, w_o, w_g, w_u, w_d) = _attn_call(
        attn_sinks[layer], q, katt, vatt, ckatt, cvatt,
        (w_branch_hgrn[layer], w_branch_attn[layer], w_out[layer], w_ffn_gate[layer], w_ffn_up[layer],
         w_ffn_down[layer]))

    return _merge_ffn_call(x, y_hg, y_at, gates, mod, norm_ffn_w[layer][None, :],
                           w_bh, w_ba, w_o, w_g, w_u, w_d)
```
